```python
import math
import jax, jax.numpy as jnp
from jax import lax
import numpy as np

D_MODEL = 1024
BATCH = 8
SEQ = 2048
DEPTH = 1

GRID_W = 64
CTX_LEN = 256
EPS = 1e-6

SSM_WIDTH = D_MODEL // 2
SSM_GROUP = 16
SSM_GROUPS = SSM_WIDTH // SSM_GROUP
SSM_STATE = 64
DT_MIN = 1e-3
DT_MAX = 1e-1

CONV_WIDTH = D_MODEL // 2
CONV_K = 3

OFF_U = 0
OFF_CB = OFF_U + SSM_WIDTH
OFF_CC = OFF_CB + CONV_WIDTH
OFF_CV = OFF_CC + CONV_WIDTH
OFF_GA = OFF_CV + CONV_WIDTH
OFF_GB = OFF_GA + D_MODEL
IN_COLS = OFF_GB + D_MODEL

N_EXPERTS = 256
TOP_K = 8
N_EXPERT_GROUPS = 8
EXPERTS_PER_GROUP = N_EXPERTS // N_EXPERT_GROUPS
TOP_K_GROUPS = 4
EXPERT_DIM = D_MODEL // 4
SHARED_DIM = EXPERT_DIM
ROUTE_SCALE = 2.5
MOE_BLOCK = 128

kernel_name = 'hybrid_s5_shortconv_moe_dit_block'


def _rmsnorm(x, g):
    xf = x.astype(jnp.float32)
    y = xf * lax.rsqrt(jnp.mean(xf * xf, axis=-1, keepdims=True) + EPS)
    return (y * g.astype(jnp.float32)).astype(x.dtype)


def _swiglu(t, wg, wu, wd):
    return (jax.nn.silu(t @ wg) * (t @ wu)) @ wd


def _zoh(lam_re, lam_im, log_dt, b_re, b_im):
    f32 = jnp.float32
    lam_re, lam_im, log_dt, b_re, b_im = (a.astype(f32) for a in (lam_re, lam_im, log_dt, b_re, b_im))
    dt = jnp.exp(log_dt)[:, None]
    mag = jnp.exp(lam_re * dt)
    a_re = mag * jnp.cos(lam_im * dt)
    a_im = mag * jnp.sin(lam_im * dt)
    den = lam_re * lam_re + lam_im * lam_im
    k_re = ((a_re - 1.0) * lam_re + a_im * lam_im) / den
    k_im = (a_im * lam_re - (a_re - 1.0) * lam_im) / den
    bb_re = k_re[..., None] * b_re - k_im[..., None] * b_im
    bb_im = k_re[..., None] * b_im + k_im[..., None] * b_re
    return a_re, a_im, bb_re, bb_im


def _drive(u, bb_re, bb_im):
    ug = u.reshape(u.shape[0], u.shape[1], SSM_GROUPS, SSM_GROUP)
    return (jnp.einsum('blgc,gpc->blgp', ug, bb_re), jnp.einsum('blgc,gpc->blgp', ug, bb_im))


def _readout(s_re, s_im, c_re, c_im):
    y = jnp.einsum('blgp,gcp->blgc', s_re, c_re) - jnp.einsum('blgp,gcp->blgc', s_im, c_im)
    return y.reshape(y.shape[0], y.shape[1], SSM_WIDTH)


def _scan_combine(e1, e2):
    a1r, a1i, b1r, b1i = e1
    a2r, a2i, b2r, b2i = e2
    return (a2r * a1r - a2i * a1i, a2r * a1i + a2i * a1r,
            a2r * b1r - a2i * b1i + b2r, a2r * b1i + a2i * b1r + b2i)


def _complex_scan(a_re, a_im, bu_re, bu_im, s0=None):
    if s0 is not None:
        s0_re, s0_im = s0
        bu_re = bu_re.at[:, 0].add(a_re * s0_re - a_im * s0_im)
        bu_im = bu_im.at[:, 0].add(a_re * s0_im + a_im * s0_re)
    length = bu_re.shape[1]
    ar = jnp.broadcast_to(a_re, (1, length) + a_re.shape)
    ai = jnp.broadcast_to(a_im, (1, length) + a_im.shape)
    _, _, s_re, s_im = lax.associative_scan(_scan_combine, (ar, ai, bu_re, bu_im), axis=1)
    return s_re, s_im


def _s5_direction(ul, uc, lam_re, lam_im, log_dt, b_re, b_im, c_re, c_im, reverse, with_ctx):
    a_re, a_im, bb_re, bb_im = _zoh(lam_re, lam_im, log_dt, b_re, b_im)
    c_re = c_re.astype(jnp.float32)
    c_im = c_im.astype(jnp.float32)
    if reverse:
        ul = jnp.flip(ul, axis=1)
        uc = jnp.flip(uc, axis=1)
    sc_re, sc_im = _complex_scan(a_re, a_im, *_drive(uc, bb_re, bb_im))
    sl_re, sl_im = _complex_scan(a_re, a_im, *_drive(ul, bb_re, bb_im), s0=(sc_re[:, -1], sc_im[:, -1]))
    y_lat = _readout(sl_re, sl_im, c_re, c_im)
    y_ctx = _readout(sc_re, sc_im, c_re, c_im) if with_ctx else None
    if reverse:
        y_lat = jnp.flip(y_lat, axis=1)
        y_ctx = jnp.flip(y_ctx, axis=1) if with_ctx else None
    return y_lat, y_ctx


def _glu(y, w_glu):
    a, b = jnp.split(jax.nn.gelu(y) @ w_glu, 2, axis=-1)
    return a * jax.nn.sigmoid(b)


def _s5_branch(u_lat, u_ctx, lam_re, lam_im, log_dt, b_re, b_im, c_re, c_im, d, w_glu, with_ctx):
    dtype = u_lat.dtype
    ul = u_lat.astype(jnp.float32)
    uc = u_ctx.astype(jnp.float32)
    dd = d.astype(jnp.float32)
    y_lat = dd * ul
    y_ctx = dd * uc
    for direction in range(2):
        yl, yc = _s5_direction(ul, uc, lam_re[direction], lam_im[direction], log_dt[direction],
                               b_re[direction], b_im[direction], c_re[direction], c_im[direction],
                               direction == 1, with_ctx)
        y_lat = y_lat + yl
        if with_ctx:
            y_ctx = y_ctx + yc
    out_lat = _glu(y_lat.astype(dtype), w_glu)
    out_ctx = _glu(y_ctx.astype(dtype), w_glu) if with_ctx else None
    return out_lat, out_ctx


def _dwconv(z, w):
    k = w.reshape(CONV_K, 1, -1).astype(z.dtype)
    return lax.conv_general_dilated(z, k, window_strides=(1,), padding=[(CONV_K // 2, CONV_K // 2)],
                                    dimension_numbers=('NWC', 'WIO', 'NWC'),
                                    feature_group_count=z.shape[-1])


def _grid_conv(z, w):
    b, length, ch = z.shape
    rows = length // GRID_W
    return _dwconv(z.reshape(b * rows, GRID_W, ch), w).reshape(b, length, ch)


def _mixer_out(proj, y_ssm, conv_fn, w_ssm_out, conv_w, w_conv_out, w_o):
    cb = proj[..., OFF_CB:OFF_CC]
    cc = proj[..., OFF_CC:OFF_CV]
    cv = proj[..., OFF_CV:OFF_GA]
    ga = proj[..., OFF_GA:OFF_GB]
    gb = proj[..., OFF_GB:IN_COLS]
    y_conv = (cb * conv_fn(cc * cv, conv_w)) @ w_conv_out
    y_a = y_ssm @ w_ssm_out
    merged = jax.nn.sigmoid(ga) * y_a + jax.nn.sigmoid(gb) * y_conv
    return merged @ w_o


def _routed_experts(t, eidx, wts, w_gate, w_up, w_down):
    n = t.shape[0]
    nk = n * TOP_K
    e_flat = eidx.reshape(-1)
    tok_flat = jnp.repeat(jnp.arange(n, dtype=jnp.int32), TOP_K)
    w_flat = wts.reshape(-1)
    order = jnp.argsort(e_flat)
    e_s, tok_s, w_s = e_flat[order], tok_flat[order], w_flat[order]
    counts = jnp.bincount(e_flat, length=N_EXPERTS)
    starts = jnp.cumsum(counts) - counts
    padded = (counts + MOE_BLOCK - 1) // MOE_BLOCK * MOE_BLOCK
    pends = jnp.cumsum(padded)
    pstarts = pends - padded
    dest = pstarts[e_s] + jnp.arange(nk, dtype=jnp.int32) - starts[e_s]
    n_blocks = -(-nk // MOE_BLOCK) + N_EXPERTS
    cap = n_blocks * MOE_BLOCK
    slot_tok = jnp.zeros((cap,), jnp.int32).at[dest].set(tok_s)
    slot_w = jnp.zeros((cap,), t.dtype).at[dest].set(w_s)
    block_expert = jnp.clip(jnp.searchsorted(pends, jnp.arange(n_blocks, dtype=jnp.int32) * MOE_BLOCK,
                                             side='right'), 0, N_EXPERTS - 1)

    def block(args):
        e, toks, ws = args
        return _swiglu(t[toks], w_gate[e], w_up[e], w_down[e]) * ws[:, None]

    y = lax.map(block, (block_expert, slot_tok.reshape(n_blocks, MOE_BLOCK), slot_w.reshape(n_blocks, MOE_BLOCK)))
    return jax.ops.segment_sum(y.reshape(cap, D_MODEL), slot_tok, num_segments=n)


def _moe(h, w_router, router_bias, w_gate, w_up, w_down, ws_gate, ws_up, ws_down):
    shape = h.shape
    t = h.reshape(-1, D_MODEL)
    n = t.shape[0]
    scores = jax.nn.sigmoid((t @ w_router).astype(jnp.float32))
    choice = scores + router_bias.astype(jnp.float32)
    grp_score = jnp.sum(lax.top_k(choice.reshape(n, N_EXPERT_GROUPS, EXPERTS_PER_GROUP), 2)[0], axis=-1)
    _, gidx = lax.top_k(grp_score, TOP_K_GROUPS)
    gmask = jnp.zeros((n, N_EXPERT_GROUPS), bool).at[jnp.arange(n)[:, None], gidx].set(True)
    choice = jnp.where(jnp.repeat(gmask, EXPERTS_PER_GROUP, axis=1), choice, -jnp.inf)
    _, eidx = lax.top_k(choice, TOP_K)
    wts = jnp.take_along_axis(scores, eidx, axis=1)
    wts = wts / jnp.sum(wts, axis=-1, keepdims=True) * ROUTE_SCALE
    routed = _routed_experts(t, eidx, wts.astype(t.dtype), w_gate, w_up, w_down)
    shared = _swiglu(t, ws_gate, ws_up, ws_down)
    return (routed + shared).reshape(shape)


def setup_inputs(seed: int = 0) -> dict:
    key = jax.random.key(seed)
    ks = iter(jax.random.split(key, 31))

    def nrm(shape, scale):
        return scale * jax.random.normal(next(ks), shape, jnp.float32)

    G, P, Q = SSM_GROUPS, SSM_STATE, SSM_GROUP
    x = nrm((BATCH, SEQ, D_MODEL), 1.0)
    c = nrm((BATCH, D_MODEL), 1.0)
    ctx = nrm((BATCH, CTX_LEN, D_MODEL), 1.0)
    c_ctx = nrm((D_MODEL,), 1.0)
    w_mod = nrm((DEPTH, D_MODEL, 6 * D_MODEL), 0.5 * D_MODEL ** -0.5)
    b_mod = nrm((DEPTH, 6 * D_MODEL), 0.02)
    norm1_g = 1.0 + nrm((DEPTH, D_MODEL), 0.02)
    norm2_g = 1.0 + nrm((DEPTH, D_MODEL), 0.02)
    w_in = nrm((DEPTH, D_MODEL, IN_COLS), D_MODEL ** -0.5)
    ssm_lam_re = -0.5 + nrm((DEPTH, 2, G, P), 0.01)
    ssm_lam_im = jnp.pi * jnp.arange(P, dtype=jnp.float32) + nrm((DEPTH, 2, G, P), 0.01)
    ssm_log_dt = jax.random.uniform(next(ks), (DEPTH, 2, G), jnp.float32, math.log(DT_MIN), math.log(DT_MAX))
    ssm_b_re = nrm((DEPTH, 2, G, P, Q), (2 * Q) ** -0.5)
    ssm_b_im = nrm((DEPTH, 2, G, P, Q), (2 * Q) ** -0.5)
    ssm_c_re = nrm((DEPTH, 2, G, Q, P), (2 * P) ** -0.5)
    ssm_c_im = nrm((DEPTH, 2, G, Q, P), (2 * P) ** -0.5)
    ssm_d = nrm((DEPTH, SSM_WIDTH), 1.0)
    w_glu = nrm((DEPTH, SSM_WIDTH, 2 * SSM_WIDTH), SSM_WIDTH ** -0.5)
    w_ssm_out = nrm((DEPTH, SSM_WIDTH, D_MODEL), SSM_WIDTH ** -0.5)
    conv_w = nrm((DEPTH, CONV_K, CONV_WIDTH), CONV_K ** -0.5)
    w_conv_out = nrm((DEPTH, CONV_WIDTH, D_MODEL), CONV_WIDTH ** -0.5)
    w_o = nrm((DEPTH, D_MODEL, D_MODEL), D_MODEL ** -0.5)
    w_router = nrm((DEPTH, D_MODEL, N_EXPERTS), D_MODEL ** -0.5)
    router_bias = nrm((DEPTH, N_EXPERTS), 0.01)
    w_gate = nrm((DEPTH, N_EXPERTS, D_MODEL, EXPERT_DIM), D_MODEL ** -0.5)
    w_up = nrm((DEPTH, N_EXPERTS, D_MODEL, EXPERT_DIM), D_MODEL ** -0.5)
    w_down = nrm((DEPTH, N_EXPERTS, EXPERT_DIM, D_MODEL), EXPERT_DIM ** -0.5)
    ws_gate = nrm((DEPTH, D_MODEL, SHARED_DIM), D_MODEL ** -0.5)
    ws_up = nrm((DEPTH, D_MODEL, SHARED_DIM), D_MODEL ** -0.5)
    ws_down = nrm((DEPTH, SHARED_DIM, D_MODEL), SHARED_DIM ** -0.5)
    final_g = 1.0 + nrm((D_MODEL,), 0.02)
    return {'x': x, 'c': c, 'ctx': ctx, 'c_ctx': c_ctx, 'w_mod': w_mod, 'b_mod': b_mod,
            'norm1_g': norm1_g, 'norm2_g': norm2_g, 'w_in': w_in,
            'ssm_lam_re': ssm_lam_re, 'ssm_lam_im': ssm_lam_im, 'ssm_log_dt': ssm_log_dt,
            'ssm_b_re': ssm_b_re, 'ssm_b_im': ssm_b_im, 'ssm_c_re': ssm_c_re, 'ssm_c_im': ssm_c_im,
            'ssm_d': ssm_d, 'w_glu': w_glu, 'w_ssm_out': w_ssm_out, 'conv_w': conv_w,
            'w_conv_out': w_conv_out, 'w_o': w_o, 'w_router': w_router, 'router_bias': router_bias,
            'w_gate': w_gate, 'w_up': w_up, 'w_down': w_down, 'ws_gate': ws_gate, 'ws_up': ws_up,
            'ws_down': ws_down, 'final_g': final_g}


def reference(x, c, ctx, c_ctx, w_mod, b_mod, norm1_g, norm2_g, w_in, ssm_lam_re, ssm_lam_im, ssm_log_dt,
              ssm_b_re, ssm_b_im, ssm_c_re, ssm_c_im, ssm_d, w_glu, w_ssm_out, conv_w, w_conv_out, w_o,
              w_router, router_bias, w_gate, w_up, w_down, ws_gate, ws_up, ws_down, final_g):
    xl = x
    xc = ctx
    for layer in range(DEPTH):
        last = layer == DEPTH - 1
        mod = jax.nn.silu(c) @ w_mod[layer] + b_mod[layer]
        mod_c = jax.nn.silu(c_ctx) @ w_mod[layer] + b_mod[layer]
        sh1, sc1, g1, sh2, sc2, g2 = (m[:, None, :] for m in jnp.split(mod, 6, axis=-1))
        csh1, csc1, cg1, csh2, csc2, cg2 = jnp.split(mod_c, 6, axis=-1)

        h = _rmsnorm(xl, norm1_g[layer]) * (1 + sc1) + sh1
        hc = _rmsnorm(xc, norm1_g[layer]) * (1 + csc1) + csh1
        proj = h @ w_in[layer]
        proj_c = hc @ (w_in[layer][:, :SSM_WIDTH] if last else w_in[layer])

        y_ssm, y_ssm_c = _s5_branch(proj[..., OFF_U:OFF_CB], proj_c[..., OFF_U:OFF_CB],
                                    ssm_lam_re[layer], ssm_lam_im[layer], ssm_log_dt[layer],
                                    ssm_b_re[layer], ssm_b_im[layer], ssm_c_re[layer], ssm_c_im[layer],
                                    ssm_d[layer], w_glu[layer], not last)

        xl = xl + g1 * _mixer_out(proj, y_ssm, _grid_conv, w_ssm_out[layer], conv_w[layer],
                                  w_conv_out[layer], w_o[layer])
        h2 = _rmsnorm(xl, norm2_g[layer]) * (1 + sc2) + sh2
        xl = xl + g2 * _moe(h2, w_router[layer], router_bias[layer], w_gate[layer], w_up[layer], w_down[layer],
                            ws_gate[layer], ws_up[layer], ws_down[layer])

        if not last:
            xc = xc + cg1 * _mixer_out(proj_c, y_ssm_c, _dwconv, w_ssm_out[layer], conv_w[layer],
                                       w_conv_out[layer], w_o[layer])
            hc2 = _rmsnorm(xc, norm2_g[layer]) * (1 + csc2) + csh2
            xc = xc + cg2 * _moe(hc2, w_router[layer], router_bias[layer], w_gate[layer], w_up[layer],
                                 w_down[layer], ws_gate[layer], ws_up[layer], ws_down[layer])
    return _rmsnorm(xl, final_g)
```

```python
import functools
import math

import jax
import jax.numpy as jnp
from jax import lax
from jax.experimental import pallas as pl
from jax.experimental.pallas import tpu as pltpu

F32 = jnp.float32
BF16 = jnp.bfloat16
I32 = jnp.int32

EPS = 1e-6
GRID_W = 64
SSM_GROUP = 16
SSM_STATE = 64
N_EXPERTS = 256
TOP_K = 8
N_EXPERT_GROUPS = 8
EXPERTS_PER_GROUP = N_EXPERTS // N_EXPERT_GROUPS
TOP_K_GROUPS = 4
ROUTE_SCALE = 2.5

SUBLANES = 8
LANES = 128
VMEM_LIMIT_BYTES = 48 * 1024 * 1024

TOKEN_TILE = 256
SCAN_CHUNK = 128
SCAN_SLAB = 512
EXPERT_BLOCK = 256
MOVE_TILE = 128


def _dot(a, b):
    return jnp.dot(a.astype(BF16), b.astype(BF16), preferred_element_type=F32)


def _rms(xf, g):
    return xf * lax.rsqrt(jnp.mean(xf * xf, axis=-1, keepdims=True) + EPS) * g


def _params(*sem):
    return pltpu.CompilerParams(dimension_semantics=sem, vmem_limit_bytes=VMEM_LIMIT_BYTES)


def _mod_kernel(c_ref, w_ref, b_ref, o_ref):
    c = c_ref[...]
    o_ref[...] = _dot(c * jax.nn.sigmoid(c), w_ref[...]) + b_ref[...]


def _modulation(c_all, w_mod, b_mod):
    rows, d = c_all.shape
    cols = w_mod.shape[1]
    blk = 1536
    return pl.pallas_call(
        _mod_kernel,
        grid=(cols // blk,),
        in_specs=[pl.BlockSpec((rows, d), lambda j: (0, 0)),
                  pl.BlockSpec((d, blk), lambda j: (0, j)),
                  pl.BlockSpec((1, blk), lambda j: (0, j))],
        out_specs=pl.BlockSpec((rows, blk), lambda j: (0, j)),
        out_shape=jax.ShapeDtypeStruct((rows, cols), F32),
        compiler_params=_params("arbitrary"),
        name="mod",
    )(c_all, w_mod, b_mod.reshape(1, cols))


def _in_proj_kernel(x_ref, mod_ref, g_ref, w_ref, cw_ref, wco_ref, u_ref, sga_ref, gbt_ref, *, d, sw):
    x = x_ref[0]
    m = mod_ref[0]
    h = _rms(x, g_ref[...]) * (1.0 + m[:, d:2 * d]) + m[:, 0:d]
    hb = h.astype(BF16)
    u_ref[...] = jnp.dot(hb, w_ref[:, 0:sw], preferred_element_type=F32)
    cb = jnp.dot(hb, w_ref[:, sw:2 * sw], preferred_element_type=F32)
    cc = jnp.dot(hb, w_ref[:, 2 * sw:3 * sw], preferred_element_type=F32)
    cv = jnp.dot(hb, w_ref[:, 3 * sw:4 * sw], preferred_element_type=F32)
    ccv = cc * cv
    rows = ccv.shape[0]
    col = lax.broadcasted_iota(I32, ccv.shape, 0) % GRID_W
    prev = jnp.where(col == 0, 0.0, pltpu.roll(ccv, 1, axis=0))
    nxt = jnp.where(col == GRID_W - 1, 0.0, pltpu.roll(ccv, rows - 1, axis=0))
    cw = cw_ref[...]
    conv = prev * cw[0:1, :] + ccv * cw[1:2, :] + nxt * cw[2:3, :]
    y_conv = _dot(cb * conv, wco_ref[...])
    ga = jnp.dot(hb, w_ref[:, 4 * sw:4 * sw + d], preferred_element_type=F32)
    gb = jnp.dot(hb, w_ref[:, 4 * sw + d:4 * sw + 2 * d], preferred_element_type=F32)
    sga_ref[0] = jax.nn.sigmoid(ga)
    gbt_ref[0] = jax.nn.sigmoid(gb) * y_conv


def _in_proj(x, mod3, g1n, w_in_b, conv_w, w_conv_out_b):
    b, l, d = x.shape
    sw = conv_w.shape[1]
    t = TOKEN_TILE
    kern = functools.partial(_in_proj_kernel, d=d, sw=sw)
    return pl.pallas_call(
        kern,
        grid=(b, l // t),
        in_specs=[pl.BlockSpec((1, t, d), lambda i, j: (i, j, 0)),
                  pl.BlockSpec((1, 1, mod3.shape[2]), lambda i, j: (i, 0, 0)),
                  pl.BlockSpec((1, d), lambda i, j: (0, 0)),
                  pl.BlockSpec(w_in_b.shape, lambda i, j: (0, 0)),
                  pl.BlockSpec(conv_w.shape, lambda i, j: (0, 0)),
                  pl.BlockSpec(w_conv_out_b.shape, lambda i, j: (0, 0))],
        out_specs=[pl.BlockSpec((t, sw), lambda i, j: (j, i)),
                   pl.BlockSpec((1, t, d), lambda i, j: (i, j, 0)),
                   pl.BlockSpec((1, t, d), lambda i, j: (i, j, 0))],
        out_shape=[jax.ShapeDtypeStruct((l, b * sw), F32),
                   jax.ShapeDtypeStruct((b, l, d), F32),
                   jax.ShapeDtypeStruct((b, l, d), F32)],
        compiler_params=_params("arbitrary", "arbitrary"),
        name="in_proj",
    )(x, mod3, g1n, w_in_b, conv_w, w_conv_out_b)


def _ctx_proj_kernel(x_ref, mod_ref, g_ref, w_ref, u_ref, *, d):
    m = mod_ref[0]
    h = _rms(x_ref[0], g_ref[...]) * (1.0 + m[:, d:2 * d]) + m[:, 0:d]
    u_ref[...] = _dot(h, w_ref[...])


def _ctx_proj(ctx, mod3, ctx_row, g1n, w_u_b):
    b, lc, d = ctx.shape
    sw = w_u_b.shape[1]
    kern = functools.partial(_ctx_proj_kernel, d=d)
    return pl.pallas_call(
        kern,
        grid=(b,),
        in_specs=[pl.BlockSpec((1, lc, d), lambda i: (i, 0, 0)),
                  pl.BlockSpec((1, 1, mod3.shape[2]), lambda i: (ctx_row, 0, 0)),
                  pl.BlockSpec((1, d), lambda i: (0, 0)),
                  pl.BlockSpec(w_u_b.shape, lambda i: (0, 0))],
        out_specs=pl.BlockSpec((lc, sw), lambda i: (0, i)),
        out_shape=jax.ShapeDtypeStruct((lc, b * sw), F32),
        compiler_params=_params("arbitrary"),
        name="ctx_proj",
    )(ctx, mod3, g1n, w_u_b)


def _scan_kernel(uc_ref, ul_ref, are_ref, aim_ref, bre_ref, bim_ref, cre_ref, cim_ref, y_ref,
                 sre_ref, sim_ref, cre_s, cim_s, *, n_ctx, nb):
    dirn = pl.program_id(0)
    j = pl.program_id(1)
    half_c = bre_ref.shape[2]
    half_s = bre_ref.shape[3]
    n_half = bre_ref.shape[1]
    steps = SCAN_CHUNK

    @pl.when(j == 0)
    def _():
        cre_s[...] = jnp.zeros_like(cre_s)
        cim_s[...] = jnp.zeros_like(cim_s)

    u = jnp.where(j < n_ctx, uc_ref[...], ul_ref[...]).astype(BF16)
    for h in range(n_half):
        uh = u[:, h * half_c:(h + 1) * half_c]
        sre_ref[:, h * half_s:(h + 1) * half_s] = jnp.dot(uh, bre_ref[0, h], preferred_element_type=F32)
        sim_ref[:, h * half_s:(h + 1) * half_s] = jnp.dot(uh, bim_ref[0, h], preferred_element_type=F32)

    lanes = sre_ref.shape[1]
    for q in range(lanes // SCAN_SLAB):
        ls = slice(q * SCAN_SLAB, (q + 1) * SCAN_SLAB)
        a_re = jnp.broadcast_to(are_ref[0, :, ls], (nb, SCAN_SLAB))
        a_im = jnp.broadcast_to(aim_ref[0, :, ls], (nb, SCAN_SLAB))

        def body(i, carry, ls=ls, a_re=a_re, a_im=a_im):
            s_re, s_im = carry
            for k in range(SUBLANES):
                t = i * SUBLANES + k
                t = jnp.where(dirn == 0, t, steps - 1 - t)
                r0 = pl.multiple_of(t * nb, nb)
                b_re = sre_ref[pl.ds(r0, nb), ls]
                b_im = sim_ref[pl.ds(r0, nb), ls]
                n_re = a_re * s_re - a_im * s_im + b_re
                n_im = a_re * s_im + a_im * s_re + b_im
                sre_ref[pl.ds(r0, nb), ls] = n_re
                sim_ref[pl.ds(r0, nb), ls] = n_im
                s_re, s_im = n_re, n_im
            return s_re, s_im

        s_re, s_im = lax.fori_loop(0, steps // SUBLANES, body, (cre_s[:, ls], cim_s[:, ls]))
        cre_s[:, ls] = s_re
        cim_s[:, ls] = s_im

    @pl.when(j >= n_ctx)
    def _():
        half_o = cre_ref.shape[3]
        for h in range(n_half):
            s_r = sre_ref[:, h * half_s:(h + 1) * half_s].astype(BF16)
            s_i = sim_ref[:, h * half_s:(h + 1) * half_s].astype(BF16)
            y_ref[0, :, h * half_o:(h + 1) * half_o] = (
                jnp.dot(s_r, cre_ref[0, h], preferred_element_type=F32)
                + jnp.dot(s_i, cim_ref[0, h], preferred_element_type=F32))


def _s5_scan(u_ctx, u_lat, a_re, a_im, b_re, b_im, c_re, c_imn, nb):
    rows_c, sw = u_ctx.shape
    rows_l = u_lat.shape[0]
    rc = SCAN_CHUNK * nb
    n_ctx = rows_c // rc
    n_lat = rows_l // rc
    lanes = a_re.shape[2]

    def ctx_idx(d, j):
        jj = jnp.minimum(j, n_ctx - 1)
        return (jnp.where(d == 0, jj, n_ctx - 1 - jj), 0)

    def lat_idx(d, j):
        jj = jnp.maximum(j - n_ctx, 0)
        return (jnp.where(d == 0, jj, n_lat - 1 - jj), 0)

    def per_dir4(d, j):
        return (d, 0, 0, 0)

    kern = functools.partial(_scan_kernel, n_ctx=n_ctx, nb=nb)
    return pl.pallas_call(
        kern,
        grid=(2, n_ctx + n_lat),
        in_specs=[pl.BlockSpec((rc, sw), ctx_idx),
                  pl.BlockSpec((rc, sw), lat_idx),
                  pl.BlockSpec((1, 1, lanes), lambda d, j: (d, 0, 0)),
                  pl.BlockSpec((1, 1, lanes), lambda d, j: (d, 0, 0)),
                  pl.BlockSpec((1,) + b_re.shape[1:], per_dir4),
                  pl.BlockSpec((1,) + b_im.shape[1:], per_dir4),
                  pl.BlockSpec((1,) + c_re.shape[1:], per_dir4),
                  pl.BlockSpec((1,) + c_imn.shape[1:], per_dir4)],
        out_specs=pl.BlockSpec((1, rc, sw), lambda d, j: (d,) + lat_idx(d, j)),
        out_shape=jax.ShapeDtypeStruct((2, rows_l, sw), F32),
        scratch_shapes=[pltpu.VMEM((rc, lanes), F32), pltpu.VMEM((rc, lanes), F32),
                        pltpu.VMEM((nb, lanes), F32), pltpu.VMEM((nb, lanes), F32)],
        compiler_params=_params("arbitrary", "arbitrary"),
        name="s5_scan",
    )(u_ctx, u_lat, a_re, a_im, b_re, b_im, c_re, c_imn)


def _ssm_operators(lam_re, lam_im, log_dt, b_re, b_im, c_re, c_im):
    dt = jnp.exp(log_dt)[..., None]
    mag = jnp.exp(lam_re * dt)
    a_re = mag * jnp.cos(lam_im * dt)
    a_im = mag * jnp.sin(lam_im * dt)
    den = lam_re * lam_re + lam_im * lam_im
    k_re = ((a_re - 1.0) * lam_re + a_im * lam_im) / den
    k_im = (a_im * lam_re - (a_re - 1.0) * lam_im) / den
    bb_re = k_re[..., None] * b_re - k_im[..., None] * b_im
    bb_im = k_re[..., None] * b_im + k_im[..., None] * b_re
    nd, g, p, q = bb_re.shape
    halves = 2
    gh = g // halves
    eye = jnp.eye(gh, dtype=F32)

    def drive(bb):
        bbh = bb.reshape(nd, halves, gh, p, q)
        m = jnp.einsum('dhgpq,gk->dhgqkp', bbh, eye)
        return m.reshape(nd, halves, gh * q, gh * p).astype(BF16)

    def readout(cc):
        cch = cc.reshape(nd, halves, gh, q, p)
        m = jnp.einsum('dhgqp,gk->dhgpkq', cch, eye)
        return m.reshape(nd, halves, gh * p, gh * q).astype(BF16)

    return (a_re.reshape(nd, 1, g * p), a_im.reshape(nd, 1, g * p),
            drive(bb_re), drive(bb_im), readout(c_re), readout(-c_im))


def _mixer_kernel(y_ref, u_ref, sga_ref, gbt_ref, x_ref, mod_ref, d_ref, wglu_ref, wso_ref, wo_ref, o_ref,
                  *, d, sw):
    y = d_ref[...] * u_ref[...] + y_ref[0] + y_ref[1]
    v = _dot(jax.nn.gelu(y), wglu_ref[...])
    ys = v[:, 0:sw] * jax.nn.sigmoid(v[:, sw:2 * sw])
    y_a = _dot(ys, wso_ref[...])
    merged = sga_ref[0] * y_a + gbt_ref[0]
    o = _dot(merged, wo_ref[...])
    g1 = mod_ref[0][:, 2 * d:3 * d]
    o_ref[0] = x_ref[0] + g1 * o


def _mixer_out(y2, u_lat, sga, gbt, x, mod3, ssm_d, w_glu_b, w_ssm_out_b, w_o_b):
    b, l, d = x.shape
    sw = ssm_d.shape[1]
    t = TOKEN_TILE
    kern = functools.partial(_mixer_kernel, d=d, sw=sw)
    tok = pl.BlockSpec((1, t, d), lambda i, j: (i, j, 0))
    return pl.pallas_call(
        kern,
        grid=(b, l // t),
        in_specs=[pl.BlockSpec((2, t, sw), lambda i, j: (0, j, i)),
                  pl.BlockSpec((t, sw), lambda i, j: (j, i)),
                  tok, tok, tok,
                  pl.BlockSpec((1, 1, mod3.shape[2]), lambda i, j: (i, 0, 0)),
                  pl.BlockSpec((1, sw), lambda i, j: (0, 0)),
                  pl.BlockSpec(w_glu_b.shape, lambda i, j: (0, 0)),
                  pl.BlockSpec(w_ssm_out_b.shape, lambda i, j: (0, 0)),
                  pl.BlockSpec(w_o_b.shape, lambda i, j: (0, 0))],
        out_specs=tok,
        out_shape=jax.ShapeDtypeStruct((b, l, d), F32),
        compiler_params=_params("arbitrary", "arbitrary"),
        name="mixer_out",
    )(y2, u_lat, sga, gbt, x, mod3, ssm_d, w_glu_b, w_ssm_out_b, w_o_b)


def _split_bf16(a):
    hi = a.astype(BF16)
    lo = (a - hi.astype(F32)).astype(BF16)
    return hi, lo


def _route_kernel(xl_ref, mod_ref, g_ref, wrt_ref, rb_ref, wsg_ref, wsu_ref, wsd_ref,
                  h2r_ref, base_ref, eidx_ref, wts_ref, rank_ref, cnt_ref, carry_ref, *, d):
    i = pl.program_id(0)
    t = xl_ref.shape[0]
    m = mod_ref[0]
    xl = xl_ref[...]
    h2 = _rms(xl, g_ref[...]) * (1.0 + m[:, 4 * d:5 * d]) + m[:, 3 * d:4 * d]

    for c in range(d // LANES):
        h2r_ref[pl.ds(c, t, stride=SUBLANES), :] = h2[:, c * LANES:(c + 1) * LANES]

    hb = h2.astype(BF16)
    sg = jnp.dot(hb, wsg_ref[...], preferred_element_type=F32)
    su = jnp.dot(hb, wsu_ref[...], preferred_element_type=F32)
    shared = _dot(sg * jax.nn.sigmoid(sg) * su, wsd_ref[...])
    base_ref[...] = xl + m[:, 5 * d:6 * d] * shared

    h_hi, h_lo = _split_bf16(h2)
    w_hi, w_lo = _split_bf16(wrt_ref[...])
    nt = (((1,), (1,)), ((), ()))
    logits = (lax.dot_general(w_hi, h_hi, nt, preferred_element_type=F32)
              + lax.dot_general(w_hi, h_lo, nt, preferred_element_type=F32)
              + lax.dot_general(w_lo, h_hi, nt, preferred_element_type=F32))
    scores = jax.nn.sigmoid(logits)
    choice = scores + rb_ref[...]

    epg = EXPERTS_PER_GROUP
    gi = lax.broadcasted_iota(I32, (epg, t), 0)
    gs = []
    for g in range(N_EXPERT_GROUPS):
        seg = choice[g * epg:(g + 1) * epg, :]
        m1 = jnp.max(seg, axis=0, keepdims=True)
        i1 = jnp.min(jnp.where(seg == m1, gi, epg), axis=0, keepdims=True)
        m2 = jnp.max(jnp.where(gi == i1, -jnp.inf, seg), axis=0, keepdims=True)
        gs.append(m1 + m2)
    masked = []
    for g in range(N_EXPERT_GROUPS):
        beat = jnp.zeros((1, t), I32)
        for g2 in range(N_EXPERT_GROUPS):
            if g2 < g:
                beat = beat + (gs[g2] >= gs[g]).astype(I32)
            elif g2 > g:
                beat = beat + (gs[g2] > gs[g]).astype(I32)
        keep = beat < TOP_K_GROUPS
        masked.append(jnp.where(keep, choice[g * epg:(g + 1) * epg, :], -jnp.inf))
    cur = jnp.concatenate(masked, axis=0)

    ei_all = lax.broadcasted_iota(I32, (N_EXPERTS, t), 0)
    picks, raw = [], []
    onehot = jnp.zeros((N_EXPERTS, t), F32)
    for _ in range(TOP_K):
        mx = jnp.max(cur, axis=0, keepdims=True)
        ei = jnp.min(jnp.where(cur == mx, ei_all, N_EXPERTS), axis=0, keepdims=True)
        hit = ei_all == ei
        raw.append(jnp.sum(jnp.where(hit, scores, 0.0), axis=0, keepdims=True))
        cur = jnp.where(hit, -jnp.inf, cur)
        onehot = jnp.where(hit, 1.0, onehot)
        picks.append(ei)
    tot = raw[0]
    for k in range(1, TOP_K):
        tot = tot + raw[k]

    @pl.when(i == 0)
    def _():
        carry_ref[...] = jnp.zeros_like(carry_ref)

    upper = (lax.broadcasted_iota(I32, (t, t), 0) < lax.broadcasted_iota(I32, (t, t), 1)).astype(BF16)
    before = jnp.dot(onehot.astype(BF16), upper, preferred_element_type=F32) + carry_ref[:, 0:1]
    for k in range(TOP_K):
        eidx_ref[k:k + 1, :] = picks[k]
        wts_ref[k:k + 1, :] = raw[k] / tot * ROUTE_SCALE
        rk = jnp.sum(jnp.where(ei_all == picks[k], before, 0.0), axis=0, keepdims=True)
        rank_ref[k:k + 1, :] = rk.astype(I32)
    carry_ref[...] = carry_ref[...] + jnp.sum(onehot, axis=1, keepdims=True)
    cnt_ref[...] = carry_ref[...]


def _route(xl2, mod3, tiles_per_batch, g2n, w_router_t, router_bias, ws_gate_b, ws_up_b, ws_down_b):
    n, d = xl2.shape
    t = TOKEN_TILE
    e = w_router_t.shape[0]
    kern = functools.partial(_route_kernel, d=d)
    full = lambda a: pl.BlockSpec(a.shape, lambda i: (0,) * a.ndim)
    small = pl.BlockSpec((TOP_K, t), lambda i: (0, i))
    return pl.pallas_call(
        kern,
        grid=(n // t,),
        in_specs=[pl.BlockSpec((t, d), lambda i: (i, 0)),
                  pl.BlockSpec((1, 1, mod3.shape[2]), lambda i: (i // tiles_per_batch, 0, 0)),
                  full(g2n), full(w_router_t), full(router_bias),
                  full(ws_gate_b), full(ws_up_b), full(ws_down_b)],
        out_specs=[pl.BlockSpec((t * SUBLANES, LANES), lambda i: (i, 0)),
                   pl.BlockSpec((t, d), lambda i: (i, 0)),
                   small, small, small,
                   pl.BlockSpec((e, LANES), lambda i: (0, 0))],
        out_shape=[jax.ShapeDtypeStruct((n * SUBLANES, LANES), F32),
                   jax.ShapeDtypeStruct((n, d), F32),
                   jax.ShapeDtypeStruct((TOP_K, n), I32),
                   jax.ShapeDtypeStruct((TOP_K, n), F32),
                   jax.ShapeDtypeStruct((TOP_K, n), I32),
                   jax.ShapeDtypeStruct((e, LANES), F32)],
        scratch_shapes=[pltpu.VMEM((e, LANES), F32)],
        compiler_params=_params("arbitrary"),
        name="route",
    )(xl2, mod3, g2n, w_router_t, router_bias, ws_gate_b, ws_up_b, ws_down_b)


def _dest_kernel(eidx_ref, rank_ref, start_ref, o_ref):
    t = eidx_ref.shape[1]
    ei_all = lax.broadcasted_iota(I32, (N_EXPERTS, t), 0)
    st = start_ref[:, 0:1]
    for k in range(TOP_K):
        hit = ei_all == eidx_ref[k:k + 1, :]
        o_ref[k:k + 1, :] = jnp.sum(jnp.where(hit, st, 0), axis=0, keepdims=True) + rank_ref[k:k + 1, :]


def _dest_rows(eidx, rank, starts_b):
    n = eidx.shape[1]
    t = 1024
    small = pl.BlockSpec((TOP_K, t), lambda i: (0, i))
    return pl.pallas_call(
        _dest_kernel,
        grid=(n // t,),
        in_specs=[small, small, pl.BlockSpec(starts_b.shape, lambda i: (0, 0))],
        out_specs=small,
        out_shape=jax.ShapeDtypeStruct((TOP_K, n), I32),
        compiler_params=_params("arbitrary"),
        name="dest",
    )(eidx, rank, starts_b)


def _move_copy(src, dst, sem, src_row, dst_row):
    return pltpu.make_async_copy(src.at[pl.ds(pl.multiple_of(src_row * SUBLANES, SUBLANES), SUBLANES)],
                                 dst.at[pl.ds(pl.multiple_of(dst_row * SUBLANES, SUBLANES), SUBLANES)], sem)


def _dispatch_kernel(dest_ref, src_ref, dst_ref, sem):
    i = pl.program_id(0)
    tile = dest_ref.shape[1]

    def body(n, c):
        for k in range(TOP_K):
            _move_copy(src_ref, dst_ref, sem, i * tile + n, dest_ref[k, n]).start()
        return c

    lax.fori_loop(0, tile, body, 0)

    def drain(n, c):
        for k in range(TOP_K):
            _move_copy(src_ref, dst_ref, sem, 0, 0).wait()
        return c

    lax.fori_loop(0, tile, drain, 0)


def _combine_kernel(dest_ref, src_ref, dst_ref, sem, *, n_tokens):
    i = pl.program_id(0)
    tile = dest_ref.shape[1]

    def body(n, c):
        for k in range(TOP_K):
            _move_copy(src_ref, dst_ref, sem, dest_ref[k, n], k * n_tokens + i * tile + n).start()
        return c

    lax.fori_loop(0, tile, body, 0)

    def drain(n, c):
        for k in range(TOP_K):
            _move_copy(src_ref, dst_ref, sem, 0, 0).wait()
        return c

    lax.fori_loop(0, tile, drain, 0)


def _row_move(kern, name, dest, src, out_rows):
    n = dest.shape[1]
    return pl.pallas_call(
        kern,
        grid=(n // MOVE_TILE,),
        in_specs=[pl.BlockSpec((TOP_K, MOVE_TILE), lambda i: (0, i), memory_space=pltpu.SMEM),
                  pl.BlockSpec(memory_space=pl.ANY)],
        out_specs=pl.BlockSpec(memory_space=pl.ANY),
        out_shape=jax.ShapeDtypeStruct((out_rows * SUBLANES, LANES), src.dtype),
        scratch_shapes=[pltpu.SemaphoreType.DMA],
        compiler_params=pltpu.CompilerParams(dimension_semantics=("arbitrary",)),
        name=name,
    )(dest, src)


def _expert_kernel(blk_ref, exp_ref, lo_ref, hi_ref, first_ref, xs_ref, wg_ref, wu_ref, wd_ref, ys_ref, *, d):
    p = pl.program_id(0)
    rows = EXPERT_BLOCK
    x = jnp.concatenate([xs_ref[pl.ds(c, rows, stride=SUBLANES), :] for c in range(d // LANES)], axis=1)
    xb = x.astype(BF16)
    g = jnp.dot(xb, wg_ref[0].astype(BF16), preferred_element_type=F32)
    u = jnp.dot(xb, wu_ref[0].astype(BF16), preferred_element_type=F32)
    y = _dot(g * jax.nn.sigmoid(g) * u, wd_ref[0])
    r = lax.broadcasted_iota(I32, (rows, LANES), 0)
    mine = (r >= lo_ref[p]) & (r < hi_ref[p])

    @pl.when(first_ref[p] == 1)
    def _():
        ys_ref[...] = jnp.zeros_like(ys_ref)

    for c in range(d // LANES):
        sl = pl.ds(c, rows, stride=SUBLANES)
        ys_ref[sl, :] = jnp.where(mine, y[:, c * LANES:(c + 1) * LANES], ys_ref[sl, :])


def _experts(meta, xs, w_gate, w_up, w_down):
    blk, exp, lo, hi, first = meta
    n_pairs = blk.shape[0]
    e, d, f = w_gate.shape
    rows = EXPERT_BLOCK * SUBLANES
    kern = functools.partial(_expert_kernel, d=d)
    grid_spec = pltpu.PrefetchScalarGridSpec(
        num_scalar_prefetch=5,
        grid=(n_pairs,),
        in_specs=[pl.BlockSpec((rows, LANES), lambda p, b, x, l, h, fr: (b[p], 0)),
                  pl.BlockSpec((1, d, f), lambda p, b, x, l, h, fr: (x[p], 0, 0)),
                  pl.BlockSpec((1, d, f), lambda p, b, x, l, h, fr: (x[p], 0, 0)),
                  pl.BlockSpec((1, f, d), lambda p, b, x, l, h, fr: (x[p], 0, 0))],
        out_specs=pl.BlockSpec((rows, LANES), lambda p, b, x, l, h, fr: (b[p], 0)),
    )
    return pl.pallas_call(
        kern,
        grid_spec=grid_spec,
        out_shape=jax.ShapeDtypeStruct(xs.shape, F32),
        compiler_params=_params("arbitrary"),
        name="experts",
    )(blk, exp, lo, hi, first, xs, w_gate, w_up, w_down)


def _pair_metadata(counts, n_rows):
    n_blocks = n_rows // EXPERT_BLOCK
    ends = jnp.cumsum(counts)
    starts = ends - counts
    bounds = jnp.sort(jnp.concatenate([jnp.arange(n_blocks, dtype=I32) * EXPERT_BLOCK, starts.astype(I32)]))
    nxt = jnp.concatenate([bounds[1:], jnp.array([n_rows], I32)])
    blk = jnp.minimum(bounds // EXPERT_BLOCK, n_blocks - 1)
    exp = jnp.clip(jnp.searchsorted(ends, bounds, side='right'), 0, N_EXPERTS - 1).astype(I32)
    lo = bounds - blk * EXPERT_BLOCK
    hi = jnp.minimum(nxt - blk * EXPERT_BLOCK, EXPERT_BLOCK)
    first = jnp.concatenate([jnp.ones((1,), I32), (blk[1:] != blk[:-1]).astype(I32)])
    return starts, (blk.astype(I32), exp, lo.astype(I32), hi.astype(I32), first)


def _final_kernel(base_ref, yt_ref, wt_ref, mod_ref, g_ref, o_ref, *, d):
    t = base_ref.shape[0]
    w = wt_ref[...]
    cols = []
    for c in range(d // LANES):
        acc = jnp.zeros((t, LANES), F32)
        for k in range(TOP_K):
            acc = acc + w[:, k:k + 1] * yt_ref[k, pl.ds(c, t, stride=SUBLANES), :]
        cols.append(acc)
    routed = jnp.concatenate(cols, axis=1)
    g2 = mod_ref[0][:, 5 * d:6 * d]
    o_ref[...] = _rms(base_ref[...] + g2 * routed, g_ref[...])


def _final(base, ytok3, wts_t, mod3, tiles_per_batch, final_g):
    n, d = base.shape
    t = TOKEN_TILE
    kern = functools.partial(_final_kernel, d=d)
    return pl.pallas_call(
        kern,
        grid=(n // t,),
        in_specs=[pl.BlockSpec((t, d), lambda i: (i, 0)),
                  pl.BlockSpec((TOP_K, t * SUBLANES, LANES), lambda i: (0, i, 0)),
                  pl.BlockSpec((t, TOP_K), lambda i: (i, 0)),
                  pl.BlockSpec((1, 1, mod3.shape[2]), lambda i: (i // tiles_per_batch, 0, 0)),
                  pl.BlockSpec((1, d), lambda i: (0, 0))],
        out_specs=pl.BlockSpec((t, d), lambda i: (i, 0)),
        out_shape=jax.ShapeDtypeStruct((n, d), F32),
        compiler_params=_params("arbitrary"),
        name="final",
    )(base, ytok3, wts_t, mod3, final_g)


def kernel(x, c, ctx, c_ctx, w_mod, b_mod, norm1_g, norm2_g, w_in, ssm_lam_re, ssm_lam_im, ssm_log_dt, ssm_b_re, ssm_b_im, ssm_c_re, ssm_c_im, ssm_d, w_glu, w_ssm_out, conv_w, w_conv_out, w_o, w_router, router_bias, w_gate, w_up, w_down, ws_gate, ws_up, ws_down, final_g):
    b, l, d = x.shape
    n = b * l
    sw = ssm_d.shape[1]
    assert w_mod.shape[0] == 1, "single layer"
    assert b == SUBLANES and l % TOKEN_TILE == 0 and l % SCAN_CHUNK == 0 and TOKEN_TILE % GRID_W == 0

    mod_rows = 2 * SUBLANES
    c_all = jnp.zeros((mod_rows, d), F32).at[:b].set(c).at[b].set(c_ctx)
    mod = _modulation(c_all, w_mod[0].astype(BF16), b_mod[0])
    mod3 = mod.reshape(mod_rows, 1, mod.shape[1])

    w_in_b = w_in[0].astype(BF16)
    g1n = norm1_g[0].reshape(1, d)
    u_lat, sga, gbt = _in_proj(x, mod3, g1n, w_in_b, conv_w[0], w_conv_out[0].astype(BF16))
    u_ctx = _ctx_proj(ctx, mod3, b, g1n, w_in_b[:, :sw])

    ops = _ssm_operators(ssm_lam_re[0], ssm_lam_im[0], ssm_log_dt[0], ssm_b_re[0], ssm_b_im[0],
                         ssm_c_re[0], ssm_c_im[0])
    y2 = _s5_scan(u_ctx.reshape(-1, sw), u_lat.reshape(-1, sw), *ops, nb=b)
    y2 = y2.reshape(2, l, b * sw)

    xl = _mixer_out(y2, u_lat, sga, gbt, x, mod3, ssm_d[0].reshape(1, sw), w_glu[0].astype(BF16),
                    w_ssm_out[0].astype(BF16), w_o[0].astype(BF16))

    tiles_per_batch = l // TOKEN_TILE
    h2r, base, eidx, wts, rank, cnt = _route(
        xl.reshape(n, d), mod3, tiles_per_batch, norm2_g[0].reshape(1, d), w_router[0].T,
        router_bias[0].reshape(N_EXPERTS, 1), ws_gate[0].astype(BF16), ws_up[0].astype(BF16),
        ws_down[0].astype(BF16))

    counts = cnt[:, 0].astype(I32)
    n_rows = n * TOP_K
    starts, meta = _pair_metadata(counts, n_rows)
    dest = _dest_rows(eidx, rank, jnp.broadcast_to(starts.astype(I32)[:, None], (N_EXPERTS, LANES)))

    xs = _row_move(_dispatch_kernel, "dispatch", dest, h2r, n_rows)
    ys = _experts(meta, xs, w_gate[0], w_up[0], w_down[0])
    ytok = _row_move(functools.partial(_combine_kernel, n_tokens=n), "combine", dest, ys, n_rows)

    out = _final(base, ytok.reshape(TOP_K, n * SUBLANES, LANES), wts.T, mod3, tiles_per_batch,
                 final_g.reshape(1, d))
    return out.reshape(b, l, d)
```

```python
import functools
import math

import jax
import jax.numpy as jnp
from jax import lax
from jax.experimental import pallas as pl
from jax.experimental.pallas import tpu as pltpu

F32 = jnp.float32
BF16 = jnp.bfloat16
I32 = jnp.int32

EPS = 1e-6
GRID_W = 64
SSM_GROUP = 16
SSM_STATE = 64
N_EXPERTS = 256
TOP_K = 8
N_EXPERT_GROUPS = 8
EXPERTS_PER_GROUP = N_EXPERTS // N_EXPERT_GROUPS
TOP_K_GROUPS = 4
ROUTE_SCALE = 2.5

SUBLANES = 8
LANES = 128
VMEM_LIMIT_BYTES = 48 * 1024 * 1024

TOKEN_TILE = 256
SCAN_CHUNK = 128
SCAN_SLAB = 512
EXPERT_BLOCK = 256
MOVE_TILE = 256


def _dot(a, b):
    return jnp.dot(a.astype(BF16), b.astype(BF16), preferred_element_type=F32)


def _rms(xf, g):
    return xf * lax.rsqrt(jnp.mean(xf * xf, axis=-1, keepdims=True) + EPS) * g


def _params(*sem):
    return pltpu.CompilerParams(dimension_semantics=sem, vmem_limit_bytes=VMEM_LIMIT_BYTES)


def _mod_kernel(c_ref, w_ref, b_ref, o_ref):
    c = c_ref[...]
    o_ref[...] = _dot(c * jax.nn.sigmoid(c), w_ref[...]) + b_ref[...]


def _modulation(c_all, w_mod, b_mod):
    rows, d = c_all.shape
    cols = w_mod.shape[1]
    blk = 1536
    return pl.pallas_call(
        _mod_kernel,
        grid=(cols // blk,),
        in_specs=[pl.BlockSpec((rows, d), lambda j: (0, 0)),
                  pl.BlockSpec((d, blk), lambda j: (0, j)),
                  pl.BlockSpec((1, blk), lambda j: (0, j))],
        out_specs=pl.BlockSpec((rows, blk), lambda j: (0, j)),
        out_shape=jax.ShapeDtypeStruct((rows, cols), F32),
        compiler_params=_params("arbitrary"),
        name="mod",
    )(c_all, w_mod, b_mod.reshape(1, cols))


def _in_proj_kernel(x_ref, mod_ref, g_ref, w_ref, cw_ref, wco_ref, u_ref, sga_ref, gbt_ref, *, d, sw):
    x = x_ref[0]
    m = mod_ref[0]
    h = _rms(x, g_ref[...]) * (1.0 + m[:, d:2 * d]) + m[:, 0:d]
    hb = h.astype(BF16)
    u_ref[...] = jnp.dot(hb, w_ref[:, 0:sw], preferred_element_type=F32)
    cb = jnp.dot(hb, w_ref[:, sw:2 * sw], preferred_element_type=F32)
    cc = jnp.dot(hb, w_ref[:, 2 * sw:3 * sw], preferred_element_type=F32)
    cv = jnp.dot(hb, w_ref[:, 3 * sw:4 * sw], preferred_element_type=F32)
    ccv = cc * cv
    rows = ccv.shape[0]
    col = lax.broadcasted_iota(I32, ccv.shape, 0) % GRID_W
    prev = jnp.where(col == 0, 0.0, pltpu.roll(ccv, 1, axis=0))
    nxt = jnp.where(col == GRID_W - 1, 0.0, pltpu.roll(ccv, rows - 1, axis=0))
    cw = cw_ref[...]
    conv = prev * cw[0:1, :] + ccv * cw[1:2, :] + nxt * cw[2:3, :]
    y_conv = _dot(cb * conv, wco_ref[...])
    ga = jnp.dot(hb, w_ref[:, 4 * sw:4 * sw + d], preferred_element_type=F32)
    gb = jnp.dot(hb, w_ref[:, 4 * sw + d:4 * sw + 2 * d], preferred_element_type=F32)
    sga_ref[0] = jax.nn.sigmoid(ga)
    gbt_ref[0] = jax.nn.sigmoid(gb) * y_conv


def _in_proj(x, mod3, g1n, w_in_b, conv_w, w_conv_out_b):
    b, l, d = x.shape
    sw = conv_w.shape[1]
    t = TOKEN_TILE
    kern = functools.partial(_in_proj_kernel, d=d, sw=sw)
    return pl.pallas_call(
        kern,
        grid=(b, l // t),
        in_specs=[pl.BlockSpec((1, t, d), lambda i, j: (i, j, 0)),
                  pl.BlockSpec((1, 1, mod3.shape[2]), lambda i, j: (i, 0, 0)),
                  pl.BlockSpec((1, d), lambda i, j: (0, 0)),
                  pl.BlockSpec(w_in_b.shape, lambda i, j: (0, 0)),
                  pl.BlockSpec(conv_w.shape, lambda i, j: (0, 0)),
                  pl.BlockSpec(w_conv_out_b.shape, lambda i, j: (0, 0))],
        out_specs=[pl.BlockSpec((t, sw), lambda i, j: (j, i)),
                   pl.BlockSpec((1, t, d), lambda i, j: (i, j, 0)),
                   pl.BlockSpec((1, t, d), lambda i, j: (i, j, 0))],
        out_shape=[jax.ShapeDtypeStruct((l, b * sw), F32),
                   jax.ShapeDtypeStruct((b, l, d), F32),
                   jax.ShapeDtypeStruct((b, l, d), F32)],
        compiler_params=_params("arbitrary", "arbitrary"),
        name="in_proj",
    )(x, mod3, g1n, w_in_b, conv_w, w_conv_out_b)


def _ctx_proj_kernel(x_ref, mod_ref, g_ref, w_ref, u_ref, *, d):
    m = mod_ref[0]
    h = _rms(x_ref[0], g_ref[...]) * (1.0 + m[:, d:2 * d]) + m[:, 0:d]
    u_ref[...] = _dot(h, w_ref[...])


def _ctx_proj(ctx, mod3, ctx_row, g1n, w_u_b):
    b, lc, d = ctx.shape
    sw = w_u_b.shape[1]
    kern = functools.partial(_ctx_proj_kernel, d=d)
    return pl.pallas_call(
        kern,
        grid=(b,),
        in_specs=[pl.BlockSpec((1, lc, d), lambda i: (i, 0, 0)),
                  pl.BlockSpec((1, 1, mod3.shape[2]), lambda i: (ctx_row, 0, 0)),
                  pl.BlockSpec((1, d), lambda i: (0, 0)),
                  pl.BlockSpec(w_u_b.shape, lambda i: (0, 0))],
        out_specs=pl.BlockSpec((lc, sw), lambda i: (0, i)),
        out_shape=jax.ShapeDtypeStruct((lc, b * sw), F32),
        compiler_params=_params("arbitrary"),
        name="ctx_proj",
    )(ctx, mod3, g1n, w_u_b)


def _scan_kernel(uc_ref, ul_ref, are_ref, aim_ref, bre_ref, bim_ref, cre_ref, cim_ref, y_ref,
                 sre_ref, sim_ref, cre_s, cim_s, *, n_ctx, nb):
    dirn = pl.program_id(0)
    j = pl.program_id(1)
    half_c = bre_ref.shape[2]
    half_s = bre_ref.shape[3]
    n_half = bre_ref.shape[1]
    steps = SCAN_CHUNK

    @pl.when(j == 0)
    def _():
        cre_s[...] = jnp.zeros_like(cre_s)
        cim_s[...] = jnp.zeros_like(cim_s)

    u = jnp.where(j < n_ctx, uc_ref[...], ul_ref[...]).astype(BF16)
    for h in range(n_half):
        uh = u[:, h * half_c:(h + 1) * half_c]
        sre_ref[:, h * half_s:(h + 1) * half_s] = jnp.dot(uh, bre_ref[0, h], preferred_element_type=F32)
        sim_ref[:, h * half_s:(h + 1) * half_s] = jnp.dot(uh, bim_ref[0, h], preferred_element_type=F32)

    lanes = sre_ref.shape[1]
    for q in range(lanes // SCAN_SLAB):
        ls = slice(q * SCAN_SLAB, (q + 1) * SCAN_SLAB)
        a_re = jnp.broadcast_to(are_ref[0, :, ls], (nb, SCAN_SLAB))
        a_im = jnp.broadcast_to(aim_ref[0, :, ls], (nb, SCAN_SLAB))

        def body(i, carry, ls=ls, a_re=a_re, a_im=a_im):
            s_re, s_im = carry
            for k in range(SUBLANES):
                t = i * SUBLANES + k
                t = jnp.where(dirn == 0, t, steps - 1 - t)
                r0 = pl.multiple_of(t * nb, nb)
                b_re = sre_ref[pl.ds(r0, nb), ls]
                b_im = sim_ref[pl.ds(r0, nb), ls]
                n_re = a_re * s_re - a_im * s_im + b_re
                n_im = a_re * s_im + a_im * s_re + b_im
                sre_ref[pl.ds(r0, nb), ls] = n_re
                sim_ref[pl.ds(r0, nb), ls] = n_im
                s_re, s_im = n_re, n_im
            return s_re, s_im

        s_re, s_im = lax.fori_loop(0, steps // SUBLANES, body, (cre_s[:, ls], cim_s[:, ls]))
        cre_s[:, ls] = s_re
        cim_s[:, ls] = s_im

    @pl.when(j >= n_ctx)
    def _():
        half_o = cre_ref.shape[3]
        for h in range(n_half):
            s_r = sre_ref[:, h * half_s:(h + 1) * half_s].astype(BF16)
            s_i = sim_ref[:, h * half_s:(h + 1) * half_s].astype(BF16)
            y_ref[0, :, h * half_o:(h + 1) * half_o] = (
                jnp.dot(s_r, cre_ref[0, h], preferred_element_type=F32)
                + jnp.dot(s_i, cim_ref[0, h], preferred_element_type=F32))


def _s5_scan(u_ctx, u_lat, a_re, a_im, b_re, b_im, c_re, c_imn, nb):
    rows_c, sw = u_ctx.shape
    rows_l = u_lat.shape[0]
    rc = SCAN_CHUNK * nb
    n_ctx = rows_c // rc
    n_lat = rows_l // rc
    lanes = a_re.shape[2]

    def ctx_idx(d, j):
        jj = jnp.minimum(j, n_ctx - 1)
        return (jnp.where(d == 0, jj, n_ctx - 1 - jj), 0)

    def lat_idx(d, j):
        jj = jnp.maximum(j - n_ctx, 0)
        return (jnp.where(d == 0, jj, n_lat - 1 - jj), 0)

    def per_dir4(d, j):
        return (d, 0, 0, 0)

    kern = functools.partial(_scan_kernel, n_ctx=n_ctx, nb=nb)
    return pl.pallas_call(
        kern,
        grid=(2, n_ctx + n_lat),
        in_specs=[pl.BlockSpec((rc, sw), ctx_idx),
                  pl.BlockSpec((rc, sw), lat_idx),
                  pl.BlockSpec((1, 1, lanes), lambda d, j: (d, 0, 0)),
                  pl.BlockSpec((1, 1, lanes), lambda d, j: (d, 0, 0)),
                  pl.BlockSpec((1,) + b_re.shape[1:], per_dir4),
                  pl.BlockSpec((1,) + b_im.shape[1:], per_dir4),
                  pl.BlockSpec((1,) + c_re.shape[1:], per_dir4),
                  pl.BlockSpec((1,) + c_imn.shape[1:], per_dir4)],
        out_specs=pl.BlockSpec((1, rc, sw), lambda d, j: (d,) + lat_idx(d, j)),
        out_shape=jax.ShapeDtypeStruct((2, rows_l, sw), F32),
        scratch_shapes=[pltpu.VMEM((rc, lanes), F32), pltpu.VMEM((rc, lanes), F32),
                        pltpu.VMEM((nb, lanes), F32), pltpu.VMEM((nb, lanes), F32)],
        compiler_params=_params("arbitrary", "arbitrary"),
        name="s5_scan",
    )(u_ctx, u_lat, a_re, a_im, b_re, b_im, c_re, c_imn)


def _ssm_operators(lam_re, lam_im, log_dt, b_re, b_im, c_re, c_im):
    dt = jnp.exp(log_dt)[..., None]
    mag = jnp.exp(lam_re * dt)
    a_re = mag * jnp.cos(lam_im * dt)
    a_im = mag * jnp.sin(lam_im * dt)
    den = lam_re * lam_re + lam_im * lam_im
    k_re = ((a_re - 1.0) * lam_re + a_im * lam_im) / den
    k_im = (a_im * lam_re - (a_re - 1.0) * lam_im) / den
    bb_re = k_re[..., None] * b_re - k_im[..., None] * b_im
    bb_im = k_re[..., None] * b_im + k_im[..., None] * b_re
    nd, g, p, q = bb_re.shape
    halves = 2
    gh = g // halves
    eye = jnp.eye(gh, dtype=F32)

    def drive(bb):
        bbh = bb.reshape(nd, halves, gh, p, q)
        m = jnp.einsum('dhgpq,gk->dhgqkp', bbh, eye)
        return m.reshape(nd, halves, gh * q, gh * p).astype(BF16)

    def readout(cc):
        cch = cc.reshape(nd, halves, gh, q, p)
        m = jnp.einsum('dhgqp,gk->dhgpkq', cch, eye)
        return m.reshape(nd, halves, gh * p, gh * q).astype(BF16)

    return (a_re.reshape(nd, 1, g * p), a_im.reshape(nd, 1, g * p),
            drive(bb_re), drive(bb_im), readout(c_re), readout(-c_im))


def _mixer_kernel(y_ref, u_ref, sga_ref, gbt_ref, x_ref, mod_ref, d_ref, wglu_ref, wso_ref, wo_ref, o_ref,
                  *, d, sw):
    y = d_ref[...] * u_ref[...] + y_ref[0] + y_ref[1]
    v = _dot(jax.nn.gelu(y), wglu_ref[...])
    ys = v[:, 0:sw] * jax.nn.sigmoid(v[:, sw:2 * sw])
    y_a = _dot(ys, wso_ref[...])
    merged = sga_ref[0] * y_a + gbt_ref[0]
    o = _dot(merged, wo_ref[...])
    g1 = mod_ref[0][:, 2 * d:3 * d]
    o_ref[0] = x_ref[0] + g1 * o


def _mixer_out(y2, u_lat, sga, gbt, x, mod3, ssm_d, w_glu_b, w_ssm_out_b, w_o_b):
    b, l, d = x.shape
    sw = ssm_d.shape[1]
    t = TOKEN_TILE
    kern = functools.partial(_mixer_kernel, d=d, sw=sw)
    tok = pl.BlockSpec((1, t, d), lambda i, j: (i, j, 0))
    return pl.pallas_call(
        kern,
        grid=(b, l // t),
        in_specs=[pl.BlockSpec((2, t, sw), lambda i, j: (0, j, i)),
                  pl.BlockSpec((t, sw), lambda i, j: (j, i)),
                  tok, tok, tok,
                  pl.BlockSpec((1, 1, mod3.shape[2]), lambda i, j: (i, 0, 0)),
                  pl.BlockSpec((1, sw), lambda i, j: (0, 0)),
                  pl.BlockSpec(w_glu_b.shape, lambda i, j: (0, 0)),
                  pl.BlockSpec(w_ssm_out_b.shape, lambda i, j: (0, 0)),
                  pl.BlockSpec(w_o_b.shape, lambda i, j: (0, 0))],
        out_specs=tok,
        out_shape=jax.ShapeDtypeStruct((b, l, d), F32),
        compiler_params=_params("arbitrary", "arbitrary"),
        name="mixer_out",
    )(y2, u_lat, sga, gbt, x, mod3, ssm_d, w_glu_b, w_ssm_out_b, w_o_b)


def _split_bf16(a):
    hi = a.astype(BF16)
    lo = (a - hi.astype(F32)).astype(BF16)
    return hi, lo


def _route_kernel(xl_ref, mod_ref, g_ref, wrt_ref, rb_ref, wsg_ref, wsu_ref, wsd_ref,
                  h2r_ref, base_ref, eidx_ref, wts_ref, rank_ref, cnt_ref, carry_ref, *, d):
    i = pl.program_id(0)
    t = xl_ref.shape[0]
    m = mod_ref[0]
    xl = xl_ref[...]
    h2 = _rms(xl, g_ref[...]) * (1.0 + m[:, 4 * d:5 * d]) + m[:, 3 * d:4 * d]

    for c in range(d // LANES):
        h2r_ref[pl.ds(c, t, stride=SUBLANES), :] = h2[:, c * LANES:(c + 1) * LANES]

    hb = h2.astype(BF16)
    sg = jnp.dot(hb, wsg_ref[...], preferred_element_type=F32)
    su = jnp.dot(hb, wsu_ref[...], preferred_element_type=F32)
    shared = _dot(sg * jax.nn.sigmoid(sg) * su, wsd_ref[...])
    base_ref[...] = xl + m[:, 5 * d:6 * d] * shared

    h_hi, h_lo = _split_bf16(h2)
    w_hi, w_lo = _split_bf16(wrt_ref[...])
    nt = (((1,), (1,)), ((), ()))
    logits = (lax.dot_general(w_hi, h_hi, nt, preferred_element_type=F32)
              + lax.dot_general(w_hi, h_lo, nt, preferred_element_type=F32)
              + lax.dot_general(w_lo, h_hi, nt, preferred_element_type=F32))
    scores = jax.nn.sigmoid(logits)
    choice = scores + rb_ref[...]

    epg = EXPERTS_PER_GROUP
    gi = lax.broadcasted_iota(I32, (epg, t), 0)
    gs = []
    for g in range(N_EXPERT_GROUPS):
        seg = choice[g * epg:(g + 1) * epg, :]
        m1 = jnp.max(seg, axis=0, keepdims=True)
        i1 = jnp.min(jnp.where(seg == m1, gi, epg), axis=0, keepdims=True)
        m2 = jnp.max(jnp.where(gi == i1, -jnp.inf, seg), axis=0, keepdims=True)
        gs.append(m1 + m2)
    masked = []
    for g in range(N_EXPERT_GROUPS):
        beat = jnp.zeros((1, t), I32)
        for g2 in range(N_EXPERT_GROUPS):
            if g2 < g:
                beat = beat + (gs[g2] >= gs[g]).astype(I32)
            elif g2 > g:
                beat = beat + (gs[g2] > gs[g]).astype(I32)
        keep = beat < TOP_K_GROUPS
        masked.append(jnp.where(keep, choice[g * epg:(g + 1) * epg, :], -jnp.inf))
    cur = jnp.concatenate(masked, axis=0)

    ei_all = lax.broadcasted_iota(I32, (N_EXPERTS, t), 0)
    picks, raw = [], []
    onehot = jnp.zeros((N_EXPERTS, t), F32)
    for _ in range(TOP_K):
        mx = jnp.max(cur, axis=0, keepdims=True)
        ei = jnp.min(jnp.where(cur == mx, ei_all, N_EXPERTS), axis=0, keepdims=True)
        hit = ei_all == ei
        raw.append(jnp.sum(jnp.where(hit, scores, 0.0), axis=0, keepdims=True))
        cur = jnp.where(hit, -jnp.inf, cur)
        onehot = jnp.where(hit, 1.0, onehot)
        picks.append(ei)
    tot = raw[0]
    for k in range(1, TOP_K):
        tot = tot + raw[k]

    @pl.when(i == 0)
    def _():
        carry_ref[...] = jnp.zeros_like(carry_ref)

    upper = (lax.broadcasted_iota(I32, (t, t), 0) < lax.broadcasted_iota(I32, (t, t), 1)).astype(BF16)
    before = jnp.dot(onehot.astype(BF16), upper, preferred_element_type=F32) + carry_ref[:, 0:1]
    for k in range(TOP_K):
        eidx_ref[k:k + 1, :] = picks[k]
        wts_ref[k:k + 1, :] = raw[k] / tot * ROUTE_SCALE
        rk = jnp.sum(jnp.where(ei_all == picks[k], before, 0.0), axis=0, keepdims=True)
        rank_ref[k:k + 1, :] = rk.astype(I32)
    carry_ref[...] = carry_ref[...] + jnp.sum(onehot, axis=1, keepdims=True)
    cnt_ref[...] = carry_ref[...]


def _route(xl2, mod3, tiles_per_batch, g2n, w_router_t, router_bias, ws_gate_b, ws_up_b, ws_down_b):
    n, d = xl2.shape
    t = TOKEN_TILE
    e = w_router_t.shape[0]
    kern = functools.partial(_route_kernel, d=d)
    full = lambda a: pl.BlockSpec(a.shape, lambda i: (0,) * a.ndim)
    small = pl.BlockSpec((TOP_K, t), lambda i: (0, i))
    return pl.pallas_call(
        kern,
        grid=(n // t,),
        in_specs=[pl.BlockSpec((t, d), lambda i: (i, 0)),
                  pl.BlockSpec((1, 1, mod3.shape[2]), lambda i: (i // tiles_per_batch, 0, 0)),
                  full(g2n), full(w_router_t), full(router_bias),
                  full(ws_gate_b), full(ws_up_b), full(ws_down_b)],
        out_specs=[pl.BlockSpec((t * SUBLANES, LANES), lambda i: (i, 0)),
                   pl.BlockSpec((t, d), lambda i: (i, 0)),
                   small, small, small,
                   pl.BlockSpec((e, LANES), lambda i: (0, 0))],
        out_shape=[jax.ShapeDtypeStruct((n * SUBLANES, LANES), F32),
                   jax.ShapeDtypeStruct((n, d), F32),
                   jax.ShapeDtypeStruct((TOP_K, n), I32),
                   jax.ShapeDtypeStruct((TOP_K, n), F32),
                   jax.ShapeDtypeStruct((TOP_K, n), I32),
                   jax.ShapeDtypeStruct((e, LANES), F32)],
        scratch_shapes=[pltpu.VMEM((e, LANES), F32)],
        compiler_params=_params("arbitrary"),
        name="route",
    )(xl2, mod3, g2n, w_router_t, router_bias, ws_gate_b, ws_up_b, ws_down_b)


def _dest_kernel(eidx_ref, rank_ref, start_ref, o_ref):
    t = eidx_ref.shape[1]
    ei_all = lax.broadcasted_iota(I32, (N_EXPERTS, t), 0)
    st = start_ref[:, 0:1]
    for k in range(TOP_K):
        hit = ei_all == eidx_ref[k:k + 1, :]
        o_ref[k:k + 1, :] = jnp.sum(jnp.where(hit, st, 0), axis=0, keepdims=True) + rank_ref[k:k + 1, :]


def _dest_rows(eidx, rank, starts_b):
    n = eidx.shape[1]
    t = 1024
    small = pl.BlockSpec((TOP_K, t), lambda i: (0, i))
    return pl.pallas_call(
        _dest_kernel,
        grid=(n // t,),
        in_specs=[small, small, pl.BlockSpec(starts_b.shape, lambda i: (0, 0))],
        out_specs=small,
        out_shape=jax.ShapeDtypeStruct((TOP_K, n), I32),
        compiler_params=_params("arbitrary"),
        name="dest",
    )(eidx, rank, starts_b)


def _row_copy(src, dst, sem, src_row, dst_row):
    return pltpu.make_async_copy(src.at[pl.ds(pl.multiple_of(src_row * SUBLANES, SUBLANES), SUBLANES)],
                                 dst.at[pl.ds(pl.multiple_of(dst_row * SUBLANES, SUBLANES), SUBLANES)], sem)


def _dispatch_kernel(dest_ref, src_ref, dst_ref, sem):
    tile = dest_ref.shape[1]

    def issue(n, c):
        for k in range(TOP_K):
            _row_copy(src_ref, dst_ref, sem, n, dest_ref[k, n]).start()
        return c

    lax.fori_loop(0, tile, issue, 0)

    def drain(n, c):
        for k in range(TOP_K):
            _row_copy(src_ref, dst_ref, sem, n, 0).wait()
        return c

    lax.fori_loop(0, tile, drain, 0)


def _dispatch(dest, src, out_rows):
    n = dest.shape[1]
    return pl.pallas_call(
        _dispatch_kernel,
        grid=(n // MOVE_TILE,),
        in_specs=[pl.BlockSpec((TOP_K, MOVE_TILE), lambda i: (0, i), memory_space=pltpu.SMEM),
                  pl.BlockSpec((MOVE_TILE * SUBLANES, LANES), lambda i: (i, 0))],
        out_specs=pl.BlockSpec(memory_space=pl.ANY),
        out_shape=jax.ShapeDtypeStruct((out_rows * SUBLANES, LANES), src.dtype),
        scratch_shapes=[pltpu.SemaphoreType.DMA],
        compiler_params=_params("arbitrary"),
        name="dispatch",
    )(dest, src)


def _expert_kernel(blk_ref, exp_ref, lo_ref, hi_ref, first_ref, xs_ref, wg_ref, wu_ref, wd_ref, ys_ref, *, d):
    p = pl.program_id(0)
    rows = EXPERT_BLOCK
    x = jnp.concatenate([xs_ref[pl.ds(c, rows, stride=SUBLANES), :] for c in range(d // LANES)], axis=1)
    xb = x.astype(BF16)
    g = jnp.dot(xb, wg_ref[0].astype(BF16), preferred_element_type=F32)
    u = jnp.dot(xb, wu_ref[0].astype(BF16), preferred_element_type=F32)
    y = _dot(g * jax.nn.sigmoid(g) * u, wd_ref[0])
    r = lax.broadcasted_iota(I32, (rows, LANES), 0)
    mine = (r >= lo_ref[p]) & (r < hi_ref[p])

    @pl.when(first_ref[p] == 1)
    def _():
        ys_ref[...] = jnp.zeros_like(ys_ref)

    for c in range(d // LANES):
        sl = pl.ds(c, rows, stride=SUBLANES)
        ys_ref[sl, :] = jnp.where(mine, y[:, c * LANES:(c + 1) * LANES], ys_ref[sl, :])


def _experts(meta, xs, w_gate, w_up, w_down):
    blk, exp, lo, hi, first = meta
    n_pairs = blk.shape[0]
    e, d, f = w_gate.shape
    rows = EXPERT_BLOCK * SUBLANES
    kern = functools.partial(_expert_kernel, d=d)
    grid_spec = pltpu.PrefetchScalarGridSpec(
        num_scalar_prefetch=5,
        grid=(n_pairs,),
        in_specs=[pl.BlockSpec((rows, LANES), lambda p, b, x, l, h, fr: (b[p], 0)),
                  pl.BlockSpec((1, d, f), lambda p, b, x, l, h, fr: (x[p], 0, 0)),
                  pl.BlockSpec((1, d, f), lambda p, b, x, l, h, fr: (x[p], 0, 0)),
                  pl.BlockSpec((1, f, d), lambda p, b, x, l, h, fr: (x[p], 0, 0))],
        out_specs=pl.BlockSpec((rows, LANES), lambda p, b, x, l, h, fr: (b[p], 0)),
    )
    return pl.pallas_call(
        kern,
        grid_spec=grid_spec,
        out_shape=jax.ShapeDtypeStruct(xs.shape, F32),
        compiler_params=_params("arbitrary"),
        name="experts",
    )(blk, exp, lo, hi, first, xs, w_gate, w_up, w_down)


def _pair_metadata(counts, n_rows):
    n_blocks = n_rows // EXPERT_BLOCK
    ends = jnp.cumsum(counts)
    starts = ends - counts
    bounds = jnp.sort(jnp.concatenate([jnp.arange(n_blocks, dtype=I32) * EXPERT_BLOCK, starts.astype(I32)]))
    nxt = jnp.concatenate([bounds[1:], jnp.array([n_rows], I32)])
    blk = jnp.minimum(bounds // EXPERT_BLOCK, n_blocks - 1)
    exp = jnp.clip(jnp.searchsorted(ends, bounds, side='right'), 0, N_EXPERTS - 1).astype(I32)
    lo = bounds - blk * EXPERT_BLOCK
    hi = jnp.minimum(nxt - blk * EXPERT_BLOCK, EXPERT_BLOCK)
    first = jnp.concatenate([jnp.ones((1,), I32), (blk[1:] != blk[:-1]).astype(I32)])
    return starts, (blk.astype(I32), exp, lo.astype(I32), hi.astype(I32), first)


def _final_kernel(dest_ref, dnext_ref, base_ref, wt_ref, mod_ref, g_ref, ys_ref, o_ref, buf_ref, sem, *, d):
    i = pl.program_id(0)
    last = pl.num_programs(0) - 1
    t = base_ref.shape[0]
    slot = i % 2

    def gather(idx_ref, s):
        def issue(n, c):
            for k in range(TOP_K):
                _row_copy(ys_ref, buf_ref.at[s, k], sem.at[s], idx_ref[k, n], n).start()
            return c
        lax.fori_loop(0, t, issue, 0)

    @pl.when(i == 0)
    def _():
        gather(dest_ref, 0)

    @pl.when(i < last)
    def _():
        gather(dnext_ref, 1 - slot)

    def drain(n, c):
        for k in range(TOP_K):
            _row_copy(ys_ref, buf_ref.at[slot, k], sem.at[slot], 0, n).wait()
        return c

    lax.fori_loop(0, t, drain, 0)

    w = wt_ref[...]
    cols = []
    for c in range(d // LANES):
        acc = jnp.zeros((t, LANES), F32)
        for k in range(TOP_K):
            acc = acc + w[:, k:k + 1] * buf_ref[slot, k, pl.ds(c, t, stride=SUBLANES), :]
        cols.append(acc)
    routed = jnp.concatenate(cols, axis=1)
    g2 = mod_ref[0][:, 5 * d:6 * d]
    o_ref[...] = _rms(base_ref[...] + g2 * routed, g_ref[...])


def _final(dest, base, ys, wts_t, mod3, tiles_per_batch, final_g):
    n, d = base.shape
    t = TOKEN_TILE
    n_tiles = n // t
    kern = functools.partial(_final_kernel, d=d)
    return pl.pallas_call(
        kern,
        grid=(n_tiles,),
        in_specs=[pl.BlockSpec((TOP_K, t), lambda i: (0, i), memory_space=pltpu.SMEM),
                  pl.BlockSpec((TOP_K, t), lambda i: (0, jnp.minimum(i + 1, n_tiles - 1)),
                               memory_space=pltpu.SMEM),
                  pl.BlockSpec((t, d), lambda i: (i, 0)),
                  pl.BlockSpec((t, TOP_K), lambda i: (i, 0)),
                  pl.BlockSpec((1, 1, mod3.shape[2]), lambda i: (i // tiles_per_batch, 0, 0)),
                  pl.BlockSpec((1, d), lambda i: (0, 0)),
                  pl.BlockSpec(memory_space=pl.ANY)],
        out_specs=pl.BlockSpec((t, d), lambda i: (i, 0)),
        out_shape=jax.ShapeDtypeStruct((n, d), F32),
        scratch_shapes=[pltpu.VMEM((2, TOP_K, t * SUBLANES, LANES), F32),
                        pltpu.SemaphoreType.DMA((2,))],
        compiler_params=_params("arbitrary"),
        name="final",
    )(dest, dest, base, wts_t, mod3, final_g, ys)


def kernel(x, c, ctx, c_ctx, w_mod, b_mod, norm1_g, norm2_g, w_in, ssm_lam_re, ssm_lam_im, ssm_log_dt, ssm_b_re, ssm_b_im, ssm_c_re, ssm_c_im, ssm_d, w_glu, w_ssm_out, conv_w, w_conv_out, w_o, w_router, router_bias, w_gate, w_up, w_down, ws_gate, ws_up, ws_down, final_g):
    b, l, d = x.shape
    n = b * l
    sw = ssm_d.shape[1]
    assert w_mod.shape[0] == 1, "single layer"
    assert b == SUBLANES and l % TOKEN_TILE == 0 and l % SCAN_CHUNK == 0 and TOKEN_TILE % GRID_W == 0

    mod_rows = 2 * SUBLANES
    c_all = jnp.zeros((mod_rows, d), F32).at[:b].set(c).at[b].set(c_ctx)
    mod = _modulation(c_all, w_mod[0].astype(BF16), b_mod[0])
    mod3 = mod.reshape(mod_rows, 1, mod.shape[1])

    w_in_b = w_in[0].astype(BF16)
    g1n = norm1_g[0].reshape(1, d)
    u_lat, sga, gbt = _in_proj(x, mod3, g1n, w_in_b, conv_w[0], w_conv_out[0].astype(BF16))
    u_ctx = _ctx_proj(ctx, mod3, b, g1n, w_in_b[:, :sw])

    ops = _ssm_operators(ssm_lam_re[0], ssm_lam_im[0], ssm_log_dt[0], ssm_b_re[0], ssm_b_im[0],
                         ssm_c_re[0], ssm_c_im[0])
    y2 = _s5_scan(u_ctx.reshape(-1, sw), u_lat.reshape(-1, sw), *ops, nb=b)
    y2 = y2.reshape(2, l, b * sw)

    xl = _mixer_out(y2, u_lat, sga, gbt, x, mod3, ssm_d[0].reshape(1, sw), w_glu[0].astype(BF16),
                    w_ssm_out[0].astype(BF16), w_o[0].astype(BF16))

    tiles_per_batch = l // TOKEN_TILE
    h2r, base, eidx, wts, rank, cnt = _route(
        xl.reshape(n, d), mod3, tiles_per_batch, norm2_g[0].reshape(1, d), w_router[0].T,
        router_bias[0].reshape(N_EXPERTS, 1), ws_gate[0].astype(BF16), ws_up[0].astype(BF16),
        ws_down[0].astype(BF16))

    counts = cnt[:, 0].astype(I32)
    n_rows = n * TOP_K
    starts, meta = _pair_metadata(counts, n_rows)
    dest = _dest_rows(eidx, rank, jnp.broadcast_to(starts.astype(I32)[:, None], (N_EXPERTS, LANES)))

    xs = _dispatch(dest, h2r, n_rows)
    ys = _experts(meta, xs, w_gate[0], w_up[0], w_down[0])
    out = _final(dest, base, ys, wts.T, mod3, tiles_per_batch, final_g.reshape(1, d))
    return out.reshape(b, l, d)
```

```python
import functools

import jax
import jax.numpy as jnp
from jax import lax
from jax.experimental import pallas as pl
from jax.experimental.pallas import tpu as pltpu

F32 = jnp.float32
BF16 = jnp.bfloat16
I32 = jnp.int32

EPS = 1e-6
GRID_W = 64
N_EXPERTS = 256
TOP_K = 8
N_EXPERT_GROUPS = 8
EXPERTS_PER_GROUP = N_EXPERTS // N_EXPERT_GROUPS
TOP_K_GROUPS = 4
ROUTE_SCALE = 2.5

SUBLANES = 8
LANES = 128
VMEM_LIMIT_BYTES = 48 * 1024 * 1024

TIME_TILE = 64
TOKEN_TILE = 256
SCAN_CHUNK = 128
SCAN_SLAB = 512
EXPERT_BLOCK = 256
MOVE_TILE = 256


def _dot(a, b):
    return jnp.dot(a.astype(BF16), b.astype(BF16), preferred_element_type=F32)


def _rms(xf, g):
    return xf * lax.rsqrt(jnp.mean(xf * xf, axis=-1, keepdims=True) + EPS) * g


def _params(*sem):
    return pltpu.CompilerParams(dimension_semantics=sem, vmem_limit_bytes=VMEM_LIMIT_BYTES)


def _full(a):
    return pl.BlockSpec(a.shape, lambda *_: (0,) * a.ndim)


def _mod_kernel(c_ref, w_ref, b_ref, o_ref):
    c = c_ref[...]
    o_ref[...] = _dot(c * jax.nn.sigmoid(c), w_ref[...]) + b_ref[...]


def _modulation(c_all, w_mod, b_mod):
    rows, d = c_all.shape
    cols = w_mod.shape[1]
    blk = 1536
    return pl.pallas_call(
        _mod_kernel,
        grid=(cols // blk,),
        in_specs=[pl.BlockSpec((rows, d), lambda j: (0, 0)),
                  pl.BlockSpec((d, blk), lambda j: (0, j)),
                  pl.BlockSpec((1, blk), lambda j: (0, j))],
        out_specs=pl.BlockSpec((rows, blk), lambda j: (0, j)),
        out_shape=jax.ShapeDtypeStruct((rows, cols), F32),
        compiler_params=_params("arbitrary"),
        name="mod",
    )(c_all, w_mod, b_mod.reshape(1, cols))


def _per_row(m3, lo, hi, tt):
    nb = m3.shape[0]
    return jnp.broadcast_to(m3[:, :, lo:hi], (nb, tt, hi - lo)).reshape(nb * tt, hi - lo)


def _to_time_major(val, nb, tt):
    c = val.shape[1]
    return pltpu.einshape("btc->tbc", val.reshape(nb, tt, c)).reshape(nb * tt, c)


def _to_batch_major(val, nb, tt):
    c = val.shape[1]
    return pltpu.einshape("tbc->btc", val.reshape(tt, nb, c)).reshape(nb * tt, c)


def _in_proj_kernel(x_ref, mod_ref, g_ref, w_ref, cw_ref, wco_ref, u_ref, sga_ref, gbt_ref, *, d, sw):
    nb, tt, _ = x_ref.shape
    rows = nb * tt
    m3 = mod_ref[...]
    x = x_ref[...].reshape(rows, d)
    h = _rms(x, g_ref[...]) * (1.0 + _per_row(m3, d, 2 * d, tt)) + _per_row(m3, 0, d, tt)
    hb = h.astype(BF16)
    u_ref[...] = _to_time_major(jnp.dot(hb, w_ref[:, 0:sw], preferred_element_type=F32), nb, tt)
    cb = jnp.dot(hb, w_ref[:, sw:2 * sw], preferred_element_type=F32)
    cc = jnp.dot(hb, w_ref[:, 2 * sw:3 * sw], preferred_element_type=F32)
    cv = jnp.dot(hb, w_ref[:, 3 * sw:4 * sw], preferred_element_type=F32)
    ccv = cc * cv
    col = lax.broadcasted_iota(I32, ccv.shape, 0) % GRID_W
    prev = jnp.where(col == 0, 0.0, pltpu.roll(ccv, 1, axis=0))
    nxt = jnp.where(col == GRID_W - 1, 0.0, pltpu.roll(ccv, rows - 1, axis=0))
    cw = cw_ref[...]
    conv = prev * cw[0:1, :] + ccv * cw[1:2, :] + nxt * cw[2:3, :]
    y_conv = _dot(cb * conv, wco_ref[...])
    ga = jnp.dot(hb, w_ref[:, 4 * sw:4 * sw + d], preferred_element_type=F32)
    gb = jnp.dot(hb, w_ref[:, 4 * sw + d:4 * sw + 2 * d], preferred_element_type=F32)
    sga_ref[...] = jax.nn.sigmoid(ga).reshape(nb, tt, d)
    gbt_ref[...] = (jax.nn.sigmoid(gb) * y_conv).reshape(nb, tt, d)


def _in_proj(x, mod3, g1n, w_in_b, conv_w, w_conv_out_b):
    b, l, d = x.shape
    sw = conv_w.shape[1]
    tt = TIME_TILE
    kern = functools.partial(_in_proj_kernel, d=d, sw=sw)
    tok = pl.BlockSpec((b, tt, d), lambda j: (0, j, 0))
    return pl.pallas_call(
        kern,
        grid=(l // tt,),
        in_specs=[tok,
                  pl.BlockSpec((b, 1, mod3.shape[2]), lambda j: (0, 0, 0)),
                  _full(g1n), _full(w_in_b), _full(conv_w), _full(w_conv_out_b)],
        out_specs=[pl.BlockSpec((tt * b, sw), lambda j: (j, 0)), tok, tok],
        out_shape=[jax.ShapeDtypeStruct((l * b, sw), F32),
                   jax.ShapeDtypeStruct((b, l, d), F32),
                   jax.ShapeDtypeStruct((b, l, d), F32)],
        compiler_params=_params("arbitrary"),
        name="in_proj",
    )(x, mod3, g1n, w_in_b, conv_w, w_conv_out_b)


def _ctx_proj_kernel(x_ref, mod_ref, g_ref, w_ref, u_ref, *, d):
    nb, tt, _ = x_ref.shape
    m = mod_ref[0]
    x = x_ref[...].reshape(nb * tt, d)
    h = _rms(x, g_ref[...]) * (1.0 + m[:, d:2 * d]) + m[:, 0:d]
    u_ref[...] = _to_time_major(_dot(h, w_ref[...]), nb, tt)


def _ctx_proj(ctx, mod3, ctx_row, g1n, w_u_b):
    b, lc, d = ctx.shape
    sw = w_u_b.shape[1]
    tt = TIME_TILE
    kern = functools.partial(_ctx_proj_kernel, d=d)
    return pl.pallas_call(
        kern,
        grid=(lc // tt,),
        in_specs=[pl.BlockSpec((b, tt, d), lambda j: (0, j, 0)),
                  pl.BlockSpec((1, 1, mod3.shape[2]), lambda j: (ctx_row, 0, 0)),
                  _full(g1n), _full(w_u_b)],
        out_specs=pl.BlockSpec((tt * b, sw), lambda j: (j, 0)),
        out_shape=jax.ShapeDtypeStruct((lc * b, sw), F32),
        compiler_params=_params("arbitrary"),
        name="ctx_proj",
    )(ctx, mod3, g1n, w_u_b)


def _scan_kernel(uc_ref, ul_ref, are_ref, aim_ref, bre_ref, bim_ref, cre_ref, cim_ref, y_ref,
                 sre_ref, sim_ref, cre_s, cim_s, *, n_ctx, nb):
    dirn = pl.program_id(0)
    j = pl.program_id(1)
    half_c = bre_ref.shape[2]
    half_s = bre_ref.shape[3]
    n_half = bre_ref.shape[1]
    steps = SCAN_CHUNK

    @pl.when(j == 0)
    def _():
        cre_s[...] = jnp.zeros_like(cre_s)
        cim_s[...] = jnp.zeros_like(cim_s)

    u = jnp.where(j < n_ctx, uc_ref[...], ul_ref[...]).astype(BF16)
    for h in range(n_half):
        uh = u[:, h * half_c:(h + 1) * half_c]
        sre_ref[:, h * half_s:(h + 1) * half_s] = jnp.dot(uh, bre_ref[0, h], preferred_element_type=F32)
        sim_ref[:, h * half_s:(h + 1) * half_s] = jnp.dot(uh, bim_ref[0, h], preferred_element_type=F32)

    lanes = sre_ref.shape[1]
    for q in range(lanes // SCAN_SLAB):
        ls = slice(q * SCAN_SLAB, (q + 1) * SCAN_SLAB)
        a_re = jnp.broadcast_to(are_ref[0, :, ls], (nb, SCAN_SLAB))
        a_im = jnp.broadcast_to(aim_ref[0, :, ls], (nb, SCAN_SLAB))

        def body(i, carry, ls=ls, a_re=a_re, a_im=a_im):
            s_re, s_im = carry
            for k in range(SUBLANES):
                t = i * SUBLANES + k
                t = jnp.where(dirn == 0, t, steps - 1 - t)
                r0 = pl.multiple_of(t * nb, nb)
                b_re = sre_ref[pl.ds(r0, nb), ls]
                b_im = sim_ref[pl.ds(r0, nb), ls]
                n_re = a_re * s_re - a_im * s_im + b_re
                n_im = a_re * s_im + a_im * s_re + b_im
                sre_ref[pl.ds(r0, nb), ls] = n_re
                sim_ref[pl.ds(r0, nb), ls] = n_im
                s_re, s_im = n_re, n_im
            return s_re, s_im

        s_re, s_im = lax.fori_loop(0, steps // SUBLANES, body, (cre_s[:, ls], cim_s[:, ls]))
        cre_s[:, ls] = s_re
        cim_s[:, ls] = s_im

    @pl.when(j >= n_ctx)
    def _():
        half_o = cre_ref.shape[3]
        for h in range(n_half):
            s_r = sre_ref[:, h * half_s:(h + 1) * half_s].astype(BF16)
            s_i = sim_ref[:, h * half_s:(h + 1) * half_s].astype(BF16)
            y_ref[0, :, h * half_o:(h + 1) * half_o] = (
                jnp.dot(s_r, cre_ref[0, h], preferred_element_type=F32)
                + jnp.dot(s_i, cim_ref[0, h], preferred_element_type=F32))


def _s5_scan(u_ctx, u_lat, a_re, a_im, b_re, b_im, c_re, c_imn, nb):
    rows_c, sw = u_ctx.shape
    rows_l = u_lat.shape[0]
    rc = SCAN_CHUNK * nb
    n_ctx = rows_c // rc
    n_lat = rows_l // rc
    lanes = a_re.shape[2]

    def ctx_idx(d, j):
        jj = jnp.minimum(j, n_ctx - 1)
        return (jnp.where(d == 0, jj, n_ctx - 1 - jj), 0)

    def lat_idx(d, j):
        jj = jnp.maximum(j - n_ctx, 0)
        return (jnp.where(d == 0, jj, n_lat - 1 - jj), 0)

    def per_dir4(d, j):
        return (d, 0, 0, 0)

    kern = functools.partial(_scan_kernel, n_ctx=n_ctx, nb=nb)
    return pl.pallas_call(
        kern,
        grid=(2, n_ctx + n_lat),
        in_specs=[pl.BlockSpec((rc, sw), ctx_idx),
                  pl.BlockSpec((rc, sw), lat_idx),
                  pl.BlockSpec((1, 1, lanes), lambda d, j: (d, 0, 0)),
                  pl.BlockSpec((1, 1, lanes), lambda d, j: (d, 0, 0)),
                  pl.BlockSpec((1,) + b_re.shape[1:], per_dir4),
                  pl.BlockSpec((1,) + b_im.shape[1:], per_dir4),
                  pl.BlockSpec((1,) + c_re.shape[1:], per_dir4),
                  pl.BlockSpec((1,) + c_imn.shape[1:], per_dir4)],
        out_specs=pl.BlockSpec((1, rc, sw), lambda d, j: (d,) + lat_idx(d, j)),
        out_shape=jax.ShapeDtypeStruct((2, rows_l, sw), F32),
        scratch_shapes=[pltpu.VMEM((rc, lanes), F32), pltpu.VMEM((rc, lanes), F32),
                        pltpu.VMEM((nb, lanes), F32), pltpu.VMEM((nb, lanes), F32)],
        compiler_params=_params("arbitrary", "arbitrary"),
        name="s5_scan",
    )(u_ctx, u_lat, a_re, a_im, b_re, b_im, c_re, c_imn)


def _ssm_operators(lam_re, lam_im, log_dt, b_re, b_im, c_re, c_im):
    dt = jnp.exp(log_dt)[..., None]
    mag = jnp.exp(lam_re * dt)
    a_re = mag * jnp.cos(lam_im * dt)
    a_im = mag * jnp.sin(lam_im * dt)
    den = lam_re * lam_re + lam_im * lam_im
    k_re = ((a_re - 1.0) * lam_re + a_im * lam_im) / den
    k_im = (a_im * lam_re - (a_re - 1.0) * lam_im) / den
    bb_re = k_re[..., None] * b_re - k_im[..., None] * b_im
    bb_im = k_re[..., None] * b_im + k_im[..., None] * b_re
    nd, g, p, q = bb_re.shape
    halves = 2
    gh = g // halves
    eye = jnp.eye(gh, dtype=F32)

    def drive(bb):
        bbh = bb.reshape(nd, halves, gh, p, q)
        m = jnp.einsum('dhgpq,gk->dhgqkp', bbh, eye)
        return m.reshape(nd, halves, gh * q, gh * p).astype(BF16)

    def readout(cc):
        cch = cc.reshape(nd, halves, gh, q, p)
        m = jnp.einsum('dhgqp,gk->dhgpkq', cch, eye)
        return m.reshape(nd, halves, gh * p, gh * q).astype(BF16)

    return (a_re.reshape(nd, 1, g * p), a_im.reshape(nd, 1, g * p),
            drive(bb_re), drive(bb_im), readout(c_re), readout(-c_im))


def _mixer_kernel(y_ref, u_ref, sga_ref, gbt_ref, x_ref, mod_ref, d_ref, wglu_ref, wso_ref, wo_ref, o_ref,
                  *, d, sw):
    nb, tt, _ = x_ref.shape
    rows = nb * tt
    y = _to_batch_major(d_ref[...] * u_ref[...] + y_ref[0] + y_ref[1], nb, tt)
    v = _dot(jax.nn.gelu(y), wglu_ref[...])
    ys = v[:, 0:sw] * jax.nn.sigmoid(v[:, sw:2 * sw])
    y_a = _dot(ys, wso_ref[...])
    merged = sga_ref[...].reshape(rows, d) * y_a + gbt_ref[...].reshape(rows, d)
    o = _dot(merged, wo_ref[...])
    g1 = _per_row(mod_ref[...], 2 * d, 3 * d, tt)
    o_ref[...] = (x_ref[...].reshape(rows, d) + g1 * o).reshape(nb, tt, d)


def _mixer_out(y2, u_lat, sga, gbt, x, mod3, ssm_d, w_glu_b, w_ssm_out_b, w_o_b):
    b, l, d = x.shape
    sw = ssm_d.shape[1]
    tt = TIME_TILE
    kern = functools.partial(_mixer_kernel, d=d, sw=sw)
    tok = pl.BlockSpec((b, tt, d), lambda j: (0, j, 0))
    return pl.pallas_call(
        kern,
        grid=(l // tt,),
        in_specs=[pl.BlockSpec((2, tt * b, sw), lambda j: (0, j, 0)),
                  pl.BlockSpec((tt * b, sw), lambda j: (j, 0)),
                  tok, tok, tok,
                  pl.BlockSpec((b, 1, mod3.shape[2]), lambda j: (0, 0, 0)),
                  _full(ssm_d), _full(w_glu_b), _full(w_ssm_out_b), _full(w_o_b)],
        out_specs=tok,
        out_shape=jax.ShapeDtypeStruct((b, l, d), F32),
        compiler_params=_params("arbitrary"),
        name="mixer_out",
    )(y2, u_lat, sga, gbt, x, mod3, ssm_d, w_glu_b, w_ssm_out_b, w_o_b)


def _split_bf16(a):
    hi = a.astype(BF16)
    lo = (a - hi.astype(F32)).astype(BF16)
    return hi, lo


def _route_kernel(xl_ref, mod_ref, g_ref, wrt_ref, rb_ref, wsg_ref, wsu_ref, wsd_ref,
                  h2_ref, base_ref, eidx_ref, wts_ref, rank_ref, cnt_ref, carry_ref, *, d):
    i = pl.program_id(0)
    t = xl_ref.shape[0]
    m = mod_ref[0]
    xl = xl_ref[...]
    h2 = _rms(xl, g_ref[...]) * (1.0 + m[:, 4 * d:5 * d]) + m[:, 3 * d:4 * d]
    h2_ref[...] = h2

    hb = h2.astype(BF16)
    sg = jnp.dot(hb, wsg_ref[...], preferred_element_type=F32)
    su = jnp.dot(hb, wsu_ref[...], preferred_element_type=F32)
    shared = _dot(sg * jax.nn.sigmoid(sg) * su, wsd_ref[...])
    base_ref[...] = xl + m[:, 5 * d:6 * d] * shared

    h_hi, h_lo = _split_bf16(h2)
    w_hi, w_lo = _split_bf16(wrt_ref[...])
    nt = (((1,), (1,)), ((), ()))
    logits = (lax.dot_general(w_hi, h_hi, nt, preferred_element_type=F32)
              + lax.dot_general(w_hi, h_lo, nt, preferred_element_type=F32)
              + lax.dot_general(w_lo, h_hi, nt, preferred_element_type=F32))
    scores = jax.nn.sigmoid(logits)
    choice = scores + rb_ref[...]

    epg = EXPERTS_PER_GROUP
    gi = lax.broadcasted_iota(I32, (epg, t), 0)
    gs = []
    for g in range(N_EXPERT_GROUPS):
        seg = choice[g * epg:(g + 1) * epg, :]
        m1 = jnp.max(seg, axis=0, keepdims=True)
        i1 = jnp.min(jnp.where(seg == m1, gi, epg), axis=0, keepdims=True)
        m2 = jnp.max(jnp.where(gi == i1, -jnp.inf, seg), axis=0, keepdims=True)
        gs.append(m1 + m2)
    masked = []
    for g in range(N_EXPERT_GROUPS):
        beat = jnp.zeros((1, t), I32)
        for g2 in range(N_EXPERT_GROUPS):
            if g2 < g:
                beat = beat + (gs[g2] >= gs[g]).astype(I32)
            elif g2 > g:
                beat = beat + (gs[g2] > gs[g]).astype(I32)
        keep = beat < TOP_K_GROUPS
        masked.append(jnp.where(keep, choice[g * epg:(g + 1) * epg, :], -jnp.inf))
    cur = jnp.concatenate(masked, axis=0)

    ei_all = lax.broadcasted_iota(I32, (N_EXPERTS, t), 0)
    picks, raw = [], []
    onehot = jnp.zeros((N_EXPERTS, t), F32)
    for _ in range(TOP_K):
        mx = jnp.max(cur, axis=0, keepdims=True)
        ei = jnp.min(jnp.where(cur == mx, ei_all, N_EXPERTS), axis=0, keepdims=True)
        hit = ei_all == ei
        raw.append(jnp.sum(jnp.where(hit, scores, 0.0), axis=0, keepdims=True))
        cur = jnp.where(hit, -jnp.inf, cur)
        onehot = jnp.where(hit, 1.0, onehot)
        picks.append(ei)
    tot = raw[0]
    for k in range(1, TOP_K):
        tot = tot + raw[k]

    @pl.when(i == 0)
    def _():
        carry_ref[...] = jnp.zeros_like(carry_ref)

    upper = (lax.broadcasted_iota(I32, (t, t), 0) < lax.broadcasted_iota(I32, (t, t), 1)).astype(BF16)
    before = jnp.dot(onehot.astype(BF16), upper, preferred_element_type=F32) + carry_ref[:, 0:1]
    for k in range(TOP_K):
        eidx_ref[k:k + 1, :] = picks[k]
        wts_ref[k:k + 1, :] = raw[k] / tot * ROUTE_SCALE
        rk = jnp.sum(jnp.where(ei_all == picks[k], before, 0.0), axis=0, keepdims=True)
        rank_ref[k:k + 1, :] = rk.astype(I32)
    carry_ref[...] = carry_ref[...] + jnp.sum(onehot, axis=1, keepdims=True)
    cnt_ref[...] = carry_ref[...]


def _route(xl2, mod3, tiles_per_batch, g2n, w_router_t, router_bias, ws_gate_b, ws_up_b, ws_down_b):
    n, d = xl2.shape
    t = TOKEN_TILE
    e = w_router_t.shape[0]
    kern = functools.partial(_route_kernel, d=d)
    tok = pl.BlockSpec((t, d), lambda i: (i, 0))
    small = pl.BlockSpec((TOP_K, t), lambda i: (0, i))
    return pl.pallas_call(
        kern,
        grid=(n // t,),
        in_specs=[tok,
                  pl.BlockSpec((1, 1, mod3.shape[2]), lambda i: (i // tiles_per_batch, 0, 0)),
                  _full(g2n), _full(w_router_t), _full(router_bias),
                  _full(ws_gate_b), _full(ws_up_b), _full(ws_down_b)],
        out_specs=[tok, tok, small, small, small,
                   pl.BlockSpec((e, LANES), lambda i: (0, 0))],
        out_shape=[jax.ShapeDtypeStruct((n, d), F32),
                   jax.ShapeDtypeStruct((n, d), F32),
                   jax.ShapeDtypeStruct((TOP_K, n), I32),
                   jax.ShapeDtypeStruct((TOP_K, n), F32),
                   jax.ShapeDtypeStruct((TOP_K, n), I32),
                   jax.ShapeDtypeStruct((e, LANES), F32)],
        scratch_shapes=[pltpu.VMEM((e, LANES), F32)],
        compiler_params=_params("arbitrary"),
        name="route",
    )(xl2, mod3, g2n, w_router_t, router_bias, ws_gate_b, ws_up_b, ws_down_b)


def _dest_kernel(eidx_ref, rank_ref, start_ref, o_ref):
    t = eidx_ref.shape[1]
    ei_all = lax.broadcasted_iota(I32, (N_EXPERTS, t), 0)
    st = start_ref[:, 0:1]
    for k in range(TOP_K):
        hit = ei_all == eidx_ref[k:k + 1, :]
        o_ref[k:k + 1, :] = jnp.sum(jnp.where(hit, st, 0), axis=0, keepdims=True) + rank_ref[k:k + 1, :]


def _dest_rows(eidx, rank, starts_b):
    n = eidx.shape[1]
    t = 1024
    small = pl.BlockSpec((TOP_K, t), lambda i: (0, i))
    return pl.pallas_call(
        _dest_kernel,
        grid=(n // t,),
        in_specs=[small, small, _full(starts_b)],
        out_specs=small,
        out_shape=jax.ShapeDtypeStruct((TOP_K, n), I32),
        compiler_params=_params("arbitrary"),
        name="dest",
    )(eidx, rank, starts_b)


def _row_copy(src, dst, sem, src_row, dst_row):
    return pltpu.make_async_copy(src.at[pl.ds(src_row, 1), :], dst.at[pl.ds(dst_row, 1), :], sem)


def _dispatch_kernel(dest_ref, src_ref, dst_ref, sem):
    tile = dest_ref.shape[1]

    def issue(n, c):
        for k in range(TOP_K):
            _row_copy(src_ref, dst_ref, sem, n, dest_ref[k, n]).start(priority=k % 2)
        return c

    lax.fori_loop(0, tile, issue, 0)

    def drain(n, c):
        for k in range(TOP_K):
            _row_copy(src_ref, dst_ref, sem, n, 0).wait()
        return c

    lax.fori_loop(0, tile, drain, 0)


def _dispatch(dest, src, out_rows):
    n, d = src.shape
    return pl.pallas_call(
        _dispatch_kernel,
        grid=(n // MOVE_TILE,),
        in_specs=[pl.BlockSpec((TOP_K, MOVE_TILE), lambda i: (0, i), memory_space=pltpu.SMEM),
                  pl.BlockSpec((MOVE_TILE, d), lambda i: (i, 0))],
        out_specs=pl.BlockSpec(memory_space=pl.ANY),
        out_shape=jax.ShapeDtypeStruct((out_rows, d), src.dtype),
        scratch_shapes=[pltpu.SemaphoreType.DMA],
        compiler_params=_params("arbitrary"),
        name="dispatch",
    )(dest, src)


def _expert_kernel(blk_ref, exp_ref, lo_ref, hi_ref, first_ref, newexp_ref, xs_ref, wg_ref, wu_ref, wd_ref,
                   ys_ref, wgb_ref, wub_ref, wdb_ref):
    p = pl.program_id(0)
    rows = xs_ref.shape[0]

    @pl.when(newexp_ref[p] == 1)
    def _():
        wgb_ref[...] = wg_ref[0].astype(BF16)
        wub_ref[...] = wu_ref[0].astype(BF16)
        wdb_ref[...] = wd_ref[0].astype(BF16)

    xb = xs_ref[...].astype(BF16)
    g = jnp.dot(xb, wgb_ref[...], preferred_element_type=F32)
    u = jnp.dot(xb, wub_ref[...], preferred_element_type=F32)
    y = jnp.dot((g * jax.nn.sigmoid(g) * u).astype(BF16), wdb_ref[...], preferred_element_type=F32)
    r = lax.broadcasted_iota(I32, (rows, 1), 0)
    mine = (r >= lo_ref[p]) & (r < hi_ref[p])

    @pl.when(first_ref[p] == 1)
    def _():
        ys_ref[...] = jnp.zeros_like(ys_ref)

    ys_ref[...] = jnp.where(mine, y, ys_ref[...])


def _experts(meta, xs, w_gate, w_up, w_down):
    blk, exp, lo, hi, first, newexp = meta
    n_pairs = blk.shape[0]
    e, d, f = w_gate.shape
    rows = EXPERT_BLOCK
    by_blk = lambda p, b, x, l, h, fr, ne: (b[p], 0)
    by_exp = lambda p, b, x, l, h, fr, ne: (x[p], 0, 0)
    grid_spec = pltpu.PrefetchScalarGridSpec(
        num_scalar_prefetch=6,
        grid=(n_pairs,),
        in_specs=[pl.BlockSpec((rows, d), by_blk),
                  pl.BlockSpec((1, d, f), by_exp),
                  pl.BlockSpec((1, d, f), by_exp),
                  pl.BlockSpec((1, f, d), by_exp)],
        out_specs=pl.BlockSpec((rows, d), by_blk),
        scratch_shapes=[pltpu.VMEM((d, f), BF16), pltpu.VMEM((d, f), BF16), pltpu.VMEM((f, d), BF16)],
    )
    return pl.pallas_call(
        _expert_kernel,
        grid_spec=grid_spec,
        out_shape=jax.ShapeDtypeStruct(xs.shape, F32),
        compiler_params=_params("arbitrary"),
        name="experts",
    )(blk, exp, lo, hi, first, newexp, xs, w_gate, w_up, w_down)


def _pair_metadata(counts, n_rows):
    n_blocks = n_rows // EXPERT_BLOCK
    ends = jnp.cumsum(counts)
    starts = ends - counts
    bounds = jnp.sort(jnp.concatenate([jnp.arange(n_blocks, dtype=I32) * EXPERT_BLOCK, starts]))
    nxt = jnp.concatenate([bounds[1:], jnp.array([n_rows], I32)])
    blk = jnp.minimum(bounds // EXPERT_BLOCK, n_blocks - 1)
    exp = jnp.minimum(jnp.sum((ends[None, :] <= bounds[:, None]).astype(I32), axis=1), N_EXPERTS - 1)
    lo = bounds - blk * EXPERT_BLOCK
    hi = jnp.minimum(nxt - blk * EXPERT_BLOCK, EXPERT_BLOCK)
    one = jnp.ones((1,), I32)
    first = jnp.concatenate([one, (blk[1:] != blk[:-1]).astype(I32)])
    newexp = jnp.concatenate([one, (exp[1:] != exp[:-1]).astype(I32)])
    return starts, (blk, exp, lo, hi, first, newexp)


def _final_kernel(dest_ref, dnext_ref, base_ref, wt_ref, mod_ref, g_ref, ys_ref, o_ref, buf_ref, sem, *, d):
    i = pl.program_id(0)
    last = pl.num_programs(0) - 1
    t = base_ref.shape[0]
    slot = i % 2

    def gather(idx_ref, s):
        def issue(n, c):
            for k in range(TOP_K):
                _row_copy(ys_ref, buf_ref.at[s, k], sem.at[s], idx_ref[k, n], n).start(priority=k % 2)
            return c
        lax.fori_loop(0, t, issue, 0)

    @pl.when(i == 0)
    def _():
        gather(dest_ref, 0)

    @pl.when(i < last)
    def _():
        gather(dnext_ref, 1 - slot)

    def drain(n, c):
        for k in range(TOP_K):
            _row_copy(ys_ref, buf_ref.at[slot, k], sem.at[slot], 0, n).wait()
        return c

    lax.fori_loop(0, t, drain, 0)

    w = wt_ref[...]
    routed = w[:, 0:1] * buf_ref[slot, 0]
    for k in range(1, TOP_K):
        routed = routed + w[:, k:k + 1] * buf_ref[slot, k]
    g2 = mod_ref[0][:, 5 * d:6 * d]
    o_ref[...] = _rms(base_ref[...] + g2 * routed, g_ref[...])


def _final(dest, base, ys, wts_t, mod3, tiles_per_batch, final_g):
    n, d = base.shape
    t = TOKEN_TILE
    n_tiles = n // t
    kern = functools.partial(_final_kernel, d=d)
    tok = pl.BlockSpec((t, d), lambda i: (i, 0))
    return pl.pallas_call(
        kern,
        grid=(n_tiles,),
        in_specs=[pl.BlockSpec((TOP_K, t), lambda i: (0, i), memory_space=pltpu.SMEM),
                  pl.BlockSpec((TOP_K, t), lambda i: (0, jnp.minimum(i + 1, n_tiles - 1)),
                               memory_space=pltpu.SMEM),
                  tok,
                  pl.BlockSpec((t, TOP_K), lambda i: (i, 0)),
                  pl.BlockSpec((1, 1, mod3.shape[2]), lambda i: (i // tiles_per_batch, 0, 0)),
                  _full(final_g),
                  pl.BlockSpec(memory_space=pl.ANY)],
        out_specs=tok,
        out_shape=jax.ShapeDtypeStruct((n, d), F32),
        scratch_shapes=[pltpu.VMEM((2, TOP_K, t, d), F32),
                        pltpu.SemaphoreType.DMA((2,))],
        compiler_params=_params("arbitrary"),
        name="final",
    )(dest, dest, base, wts_t, mod3, final_g, ys)


def kernel(x, c, ctx, c_ctx, w_mod, b_mod, norm1_g, norm2_g, w_in, ssm_lam_re, ssm_lam_im, ssm_log_dt, ssm_b_re, ssm_b_im, ssm_c_re, ssm_c_im, ssm_d, w_glu, w_ssm_out, conv_w, w_conv_out, w_o, w_router, router_bias, w_gate, w_up, w_down, ws_gate, ws_up, ws_down, final_g):
    b, l, d = x.shape
    n = b * l
    sw = ssm_d.shape[1]
    assert w_mod.shape[0] == 1, "single layer"
    assert b == SUBLANES and l % TOKEN_TILE == 0 and l % SCAN_CHUNK == 0 and TIME_TILE % GRID_W == 0

    mod_rows = 2 * SUBLANES
    c_all = jnp.concatenate([c, c_ctx[None, :], jnp.zeros((mod_rows - b - 1, d), F32)], axis=0)
    mod = _modulation(c_all, w_mod[0].astype(BF16), b_mod[0])
    mod3 = mod.reshape(mod_rows, 1, mod.shape[1])

    w_in_b = w_in[0].astype(BF16)
    g1n = norm1_g[0].reshape(1, d)
    u_lat, sga, gbt = _in_proj(x, mod3, g1n, w_in_b, conv_w[0], w_conv_out[0].astype(BF16))
    u_ctx = _ctx_proj(ctx, mod3, b, g1n, w_in_b[:, :sw])

    ops = _ssm_operators(ssm_lam_re[0], ssm_lam_im[0], ssm_log_dt[0], ssm_b_re[0], ssm_b_im[0],
                         ssm_c_re[0], ssm_c_im[0])
    y2 = _s5_scan(u_ctx, u_lat, *ops, nb=b)

    xl = _mixer_out(y2, u_lat, sga, gbt, x, mod3, ssm_d[0].reshape(1, sw), w_glu[0].astype(BF16),
                    w_ssm_out[0].astype(BF16), w_o[0].astype(BF16))

    tiles_per_batch = l // TOKEN_TILE
    h2, base, eidx, wts, rank, cnt = _route(
        xl.reshape(n, d), mod3, tiles_per_batch, norm2_g[0].reshape(1, d), w_router[0].T,
        router_bias[0].reshape(N_EXPERTS, 1), ws_gate[0].astype(BF16), ws_up[0].astype(BF16),
        ws_down[0].astype(BF16))

    counts = cnt[:, 0].astype(I32)
    n_rows = n * TOP_K
    starts, meta = _pair_metadata(counts, n_rows)
    dest = _dest_rows(eidx, rank, jnp.broadcast_to(starts[:, None], (N_EXPERTS, LANES)))

    xs = _dispatch(dest, h2, n_rows)
    ys = _experts(meta, xs, w_gate[0], w_up[0], w_down[0])
    out = _final(dest, base, ys, wts.T, mod3, tiles_per_batch, final_g.reshape(1, d))
    return out.reshape(b, l, d)
```

```python
import functools

import jax
import jax.numpy as jnp
from jax import lax
from jax.experimental import pallas as pl
from jax.experimental.pallas import tpu as pltpu

F32 = jnp.float32
BF16 = jnp.bfloat16
I32 = jnp.int32

EPS = 1e-6
GRID_W = 64
N_EXPERTS = 256
TOP_K = 8
N_EXPERT_GROUPS = 8
EXPERTS_PER_GROUP = N_EXPERTS // N_EXPERT_GROUPS
TOP_K_GROUPS = 4
ROUTE_SCALE = 2.5

SUBLANES = 8
LANES = 128
VMEM_LIMIT_BYTES = 48 * 1024 * 1024

TIME_TILE = 64
TOKEN_TILE = 256
SCAN_CHUNK = 128
SCAN_SLAB = 512
EXPERT_BLOCK = 256
MOVE_TILE = 256


def _dot(a, b):
    return jnp.dot(a.astype(BF16), b.astype(BF16), preferred_element_type=F32)


def _rms(xf, g):
    return xf * lax.rsqrt(jnp.mean(xf * xf, axis=-1, keepdims=True) + EPS) * g


def _params(*sem):
    return pltpu.CompilerParams(dimension_semantics=sem, vmem_limit_bytes=VMEM_LIMIT_BYTES)


def _full(a):
    return pl.BlockSpec(a.shape, lambda *_: (0,) * a.ndim)


def _mod_kernel(c_ref, w_ref, b_ref, o_ref):
    c = c_ref[...]
    o_ref[...] = _dot(c * jax.nn.sigmoid(c), w_ref[...]) + b_ref[...]


def _modulation(c_all, w_mod, b_mod):
    rows, d = c_all.shape
    cols = w_mod.shape[1]
    blk = 1536
    return pl.pallas_call(
        _mod_kernel,
        grid=(cols // blk,),
        in_specs=[pl.BlockSpec((rows, d), lambda j: (0, 0)),
                  pl.BlockSpec((d, blk), lambda j: (0, j)),
                  pl.BlockSpec((1, blk), lambda j: (0, j))],
        out_specs=pl.BlockSpec((rows, blk), lambda j: (0, j)),
        out_shape=jax.ShapeDtypeStruct((rows, cols), F32),
        compiler_params=_params("arbitrary"),
        name="mod",
    )(c_all, w_mod, b_mod.reshape(1, cols))


def _per_row(m3, lo, hi, tt):
    nb = m3.shape[0]
    return jnp.broadcast_to(m3[:, :, lo:hi], (nb, tt, hi - lo)).reshape(nb * tt, hi - lo)


def _to_time_major(val, nb, tt):
    c = val.shape[1]
    return pltpu.einshape("btc->tbc", val.reshape(nb, tt, c)).reshape(nb * tt, c)


def _to_batch_major(val, nb, tt):
    c = val.shape[1]
    return pltpu.einshape("tbc->btc", val.reshape(tt, nb, c)).reshape(nb * tt, c)


def _in_proj_kernel(x_ref, mod_ref, g_ref, w_ref, cw_ref, wco_ref, u_ref, sga_ref, gbt_ref, *, d, sw):
    nb, tt, _ = x_ref.shape
    rows = nb * tt
    m3 = mod_ref[...]
    x = x_ref[...].reshape(rows, d)
    h = _rms(x, g_ref[...]) * (1.0 + _per_row(m3, d, 2 * d, tt)) + _per_row(m3, 0, d, tt)
    hb = h.astype(BF16)
    u_ref[...] = _to_time_major(jnp.dot(hb, w_ref[:, 0:sw], preferred_element_type=F32), nb, tt)
    cb = jnp.dot(hb, w_ref[:, sw:2 * sw], preferred_element_type=F32)
    cc = jnp.dot(hb, w_ref[:, 2 * sw:3 * sw], preferred_element_type=F32)
    cv = jnp.dot(hb, w_ref[:, 3 * sw:4 * sw], preferred_element_type=F32)
    ccv = cc * cv
    col = lax.broadcasted_iota(I32, ccv.shape, 0) % GRID_W
    prev = jnp.where(col == 0, 0.0, pltpu.roll(ccv, 1, axis=0))
    nxt = jnp.where(col == GRID_W - 1, 0.0, pltpu.roll(ccv, rows - 1, axis=0))
    cw = cw_ref[...]
    conv = prev * cw[0:1, :] + ccv * cw[1:2, :] + nxt * cw[2:3, :]
    y_conv = _dot(cb * conv, wco_ref[...])
    ga = jnp.dot(hb, w_ref[:, 4 * sw:4 * sw + d], preferred_element_type=F32)
    gb = jnp.dot(hb, w_ref[:, 4 * sw + d:4 * sw + 2 * d], preferred_element_type=F32)
    sga_ref[...] = jax.nn.sigmoid(ga).reshape(nb, tt, d)
    gbt_ref[...] = (jax.nn.sigmoid(gb) * y_conv).reshape(nb, tt, d)


def _in_proj(x, mod3, g1n, w_in_b, conv_w, w_conv_out_b):
    b, l, d = x.shape
    sw = conv_w.shape[1]
    tt = TIME_TILE
    kern = functools.partial(_in_proj_kernel, d=d, sw=sw)
    tok = pl.BlockSpec((b, tt, d), lambda j: (0, j, 0))
    return pl.pallas_call(
        kern,
        grid=(l // tt,),
        in_specs=[tok,
                  pl.BlockSpec((b, 1, mod3.shape[2]), lambda j: (0, 0, 0)),
                  _full(g1n), _full(w_in_b), _full(conv_w), _full(w_conv_out_b)],
        out_specs=[pl.BlockSpec((tt * b, sw), lambda j: (j, 0)), tok, tok],
        out_shape=[jax.ShapeDtypeStruct((l * b, sw), F32),
                   jax.ShapeDtypeStruct((b, l, d), F32),
                   jax.ShapeDtypeStruct((b, l, d), F32)],
        compiler_params=_params("arbitrary"),
        name="in_proj",
    )(x, mod3, g1n, w_in_b, conv_w, w_conv_out_b)


def _ctx_proj_kernel(x_ref, mod_ref, g_ref, w_ref, u_ref, *, d):
    nb, tt, _ = x_ref.shape
    m = mod_ref[0]
    x = x_ref[...].reshape(nb * tt, d)
    h = _rms(x, g_ref[...]) * (1.0 + m[:, d:2 * d]) + m[:, 0:d]
    u_ref[...] = _to_time_major(_dot(h, w_ref[...]), nb, tt)


def _ctx_proj(ctx, mod3, ctx_row, g1n, w_u_b):
    b, lc, d = ctx.shape
    sw = w_u_b.shape[1]
    tt = TIME_TILE
    kern = functools.partial(_ctx_proj_kernel, d=d)
    return pl.pallas_call(
        kern,
        grid=(lc // tt,),
        in_specs=[pl.BlockSpec((b, tt, d), lambda j: (0, j, 0)),
                  pl.BlockSpec((1, 1, mod3.shape[2]), lambda j: (ctx_row, 0, 0)),
                  _full(g1n), _full(w_u_b)],
        out_specs=pl.BlockSpec((tt * b, sw), lambda j: (j, 0)),
        out_shape=jax.ShapeDtypeStruct((lc * b, sw), F32),
        compiler_params=_params("arbitrary"),
        name="ctx_proj",
    )(ctx, mod3, g1n, w_u_b)


def _scan_kernel(uc_ref, ul_ref, are_ref, aim_ref, bre_ref, bim_ref, cre_ref, cim_ref, y_ref,
                 sre_ref, sim_ref, cre_s, cim_s, *, n_ctx, nb):
    dirn = pl.program_id(0)
    j = pl.program_id(1)
    half_c = bre_ref.shape[2]
    half_s = bre_ref.shape[3]
    n_half = bre_ref.shape[1]
    steps = SCAN_CHUNK

    @pl.when(j == 0)
    def _():
        cre_s[...] = jnp.zeros_like(cre_s)
        cim_s[...] = jnp.zeros_like(cim_s)

    u = jnp.where(j < n_ctx, uc_ref[...], ul_ref[...]).astype(BF16)
    for h in range(n_half):
        uh = u[:, h * half_c:(h + 1) * half_c]
        sre_ref[:, h * half_s:(h + 1) * half_s] = jnp.dot(uh, bre_ref[0, h], preferred_element_type=F32)
        sim_ref[:, h * half_s:(h + 1) * half_s] = jnp.dot(uh, bim_ref[0, h], preferred_element_type=F32)

    lanes = sre_ref.shape[1]
    for q in range(lanes // SCAN_SLAB):
        ls = slice(q * SCAN_SLAB, (q + 1) * SCAN_SLAB)
        a_re = jnp.broadcast_to(are_ref[0, :, ls], (nb, SCAN_SLAB))
        a_im = jnp.broadcast_to(aim_ref[0, :, ls], (nb, SCAN_SLAB))

        def body(i, carry, ls=ls, a_re=a_re, a_im=a_im):
            s_re, s_im = carry
            for k in range(SUBLANES):
                t = i * SUBLANES + k
                t = jnp.where(dirn == 0, t, steps - 1 - t)
                r0 = pl.multiple_of(t * nb, nb)
                b_re = sre_ref[pl.ds(r0, nb), ls]
                b_im = sim_ref[pl.ds(r0, nb), ls]
                n_re = a_re * s_re - a_im * s_im + b_re
                n_im = a_re * s_im + a_im * s_re + b_im
                sre_ref[pl.ds(r0, nb), ls] = n_re
                sim_ref[pl.ds(r0, nb), ls] = n_im
                s_re, s_im = n_re, n_im
            return s_re, s_im

        s_re, s_im = lax.fori_loop(0, steps // SUBLANES, body, (cre_s[:, ls], cim_s[:, ls]))
        cre_s[:, ls] = s_re
        cim_s[:, ls] = s_im

    @pl.when(j >= n_ctx)
    def _():
        half_o = cre_ref.shape[3]
        for h in range(n_half):
            s_r = sre_ref[:, h * half_s:(h + 1) * half_s].astype(BF16)
            s_i = sim_ref[:, h * half_s:(h + 1) * half_s].astype(BF16)
            y_ref[0, :, h * half_o:(h + 1) * half_o] = (
                jnp.dot(s_r, cre_ref[0, h], preferred_element_type=F32)
                + jnp.dot(s_i, cim_ref[0, h], preferred_element_type=F32))


def _s5_scan(u_ctx, u_lat, a_re, a_im, b_re, b_im, c_re, c_imn, nb):
    rows_c, sw = u_ctx.shape
    rows_l = u_lat.shape[0]
    rc = SCAN_CHUNK * nb
    n_ctx = rows_c // rc
    n_lat = rows_l // rc
    lanes = a_re.shape[2]

    def ctx_idx(d, j):
        jj = jnp.minimum(j, n_ctx - 1)
        return (jnp.where(d == 0, jj, n_ctx - 1 - jj), 0)

    def lat_idx(d, j):
        jj = jnp.maximum(j - n_ctx, 0)
        return (jnp.where(d == 0, jj, n_lat - 1 - jj), 0)

    def per_dir4(d, j):
        return (d, 0, 0, 0)

    kern = functools.partial(_scan_kernel, n_ctx=n_ctx, nb=nb)
    return pl.pallas_call(
        kern,
        grid=(2, n_ctx + n_lat),
        in_specs=[pl.BlockSpec((rc, sw), ctx_idx),
                  pl.BlockSpec((rc, sw), lat_idx),
                  pl.BlockSpec((1, 1, lanes), lambda d, j: (d, 0, 0)),
                  pl.BlockSpec((1, 1, lanes), lambda d, j: (d, 0, 0)),
                  pl.BlockSpec((1,) + b_re.shape[1:], per_dir4),
                  pl.BlockSpec((1,) + b_im.shape[1:], per_dir4),
                  pl.BlockSpec((1,) + c_re.shape[1:], per_dir4),
                  pl.BlockSpec((1,) + c_imn.shape[1:], per_dir4)],
        out_specs=pl.BlockSpec((1, rc, sw), lambda d, j: (d,) + lat_idx(d, j)),
        out_shape=jax.ShapeDtypeStruct((2, rows_l, sw), F32),
        scratch_shapes=[pltpu.VMEM((rc, lanes), F32), pltpu.VMEM((rc, lanes), F32),
                        pltpu.VMEM((nb, lanes), F32), pltpu.VMEM((nb, lanes), F32)],
        compiler_params=_params("arbitrary", "arbitrary"),
        name="s5_scan",
    )(u_ctx, u_lat, a_re, a_im, b_re, b_im, c_re, c_imn)


def _ssm_operators(lam_re, lam_im, log_dt, b_re, b_im, c_re, c_im):
    dt = jnp.exp(log_dt)[..., None]
    mag = jnp.exp(lam_re * dt)
    a_re = mag * jnp.cos(lam_im * dt)
    a_im = mag * jnp.sin(lam_im * dt)
    den = lam_re * lam_re + lam_im * lam_im
    k_re = ((a_re - 1.0) * lam_re + a_im * lam_im) / den
    k_im = (a_im * lam_re - (a_re - 1.0) * lam_im) / den
    bb_re = k_re[..., None] * b_re - k_im[..., None] * b_im
    bb_im = k_re[..., None] * b_im + k_im[..., None] * b_re
    nd, g, p, q = bb_re.shape
    halves = 2
    gh = g // halves
    eye = jnp.eye(gh, dtype=F32)

    def drive(bb):
        bbh = bb.reshape(nd, halves, gh, p, q)
        m = jnp.einsum('dhgpq,gk->dhgqkp', bbh, eye)
        return m.reshape(nd, halves, gh * q, gh * p).astype(BF16)

    def readout(cc):
        cch = cc.reshape(nd, halves, gh, q, p)
        m = jnp.einsum('dhgqp,gk->dhgpkq', cch, eye)
        return m.reshape(nd, halves, gh * p, gh * q).astype(BF16)

    return (a_re.reshape(nd, 1, g * p), a_im.reshape(nd, 1, g * p),
            drive(bb_re), drive(bb_im), readout(c_re), readout(-c_im))


def _mixer_kernel(y_ref, u_ref, sga_ref, gbt_ref, x_ref, mod_ref, d_ref, wglu_ref, wso_ref, wo_ref, o_ref,
                  *, d, sw):
    nb, tt, _ = x_ref.shape
    rows = nb * tt
    y = _to_batch_major(d_ref[...] * u_ref[...] + y_ref[0] + y_ref[1], nb, tt)
    v = _dot(jax.nn.gelu(y), wglu_ref[...])
    ys = v[:, 0:sw] * jax.nn.sigmoid(v[:, sw:2 * sw])
    y_a = _dot(ys, wso_ref[...])
    merged = sga_ref[...].reshape(rows, d) * y_a + gbt_ref[...].reshape(rows, d)
    o = _dot(merged, wo_ref[...])
    g1 = _per_row(mod_ref[...], 2 * d, 3 * d, tt)
    o_ref[...] = (x_ref[...].reshape(rows, d) + g1 * o).reshape(nb, tt, d)


def _mixer_out(y2, u_lat, sga, gbt, x, mod3, ssm_d, w_glu_b, w_ssm_out_b, w_o_b):
    b, l, d = x.shape
    sw = ssm_d.shape[1]
    tt = TIME_TILE
    kern = functools.partial(_mixer_kernel, d=d, sw=sw)
    tok = pl.BlockSpec((b, tt, d), lambda j: (0, j, 0))
    return pl.pallas_call(
        kern,
        grid=(l // tt,),
        in_specs=[pl.BlockSpec((2, tt * b, sw), lambda j: (0, j, 0)),
                  pl.BlockSpec((tt * b, sw), lambda j: (j, 0)),
                  tok, tok, tok,
                  pl.BlockSpec((b, 1, mod3.shape[2]), lambda j: (0, 0, 0)),
                  _full(ssm_d), _full(w_glu_b), _full(w_ssm_out_b), _full(w_o_b)],
        out_specs=tok,
        out_shape=jax.ShapeDtypeStruct((b, l, d), F32),
        compiler_params=_params("arbitrary"),
        name="mixer_out",
    )(y2, u_lat, sga, gbt, x, mod3, ssm_d, w_glu_b, w_ssm_out_b, w_o_b)


def _split_bf16(a):
    hi = a.astype(BF16)
    lo = (a - hi.astype(F32)).astype(BF16)
    return hi, lo


def _route_kernel(xl_ref, mod_ref, g_ref, wrt_ref, rb_ref, wsg_ref, wsu_ref, wsd_ref,
                  h2_ref, base_ref, eidx_ref, wts_ref, rank_ref, cnt_ref, carry_ref, *, d):
    i = pl.program_id(0)
    t = xl_ref.shape[0]
    m = mod_ref[0]
    xl = xl_ref[...]
    h2 = _rms(xl, g_ref[...]) * (1.0 + m[:, 4 * d:5 * d]) + m[:, 3 * d:4 * d]
    h2_ref[...] = h2

    hb = h2.astype(BF16)
    sg = jnp.dot(hb, wsg_ref[...], preferred_element_type=F32)
    su = jnp.dot(hb, wsu_ref[...], preferred_element_type=F32)
    shared = _dot(sg * jax.nn.sigmoid(sg) * su, wsd_ref[...])
    base_ref[...] = xl + m[:, 5 * d:6 * d] * shared

    h_hi, h_lo = _split_bf16(h2)
    w_hi, w_lo = _split_bf16(wrt_ref[...])
    nt = (((1,), (1,)), ((), ()))
    logits = (lax.dot_general(w_hi, h_hi, nt, preferred_element_type=F32)
              + lax.dot_general(w_hi, h_lo, nt, preferred_element_type=F32)
              + lax.dot_general(w_lo, h_hi, nt, preferred_element_type=F32))
    scores = jax.nn.sigmoid(logits)
    choice = scores + rb_ref[...]

    epg = EXPERTS_PER_GROUP
    gi = lax.broadcasted_iota(I32, (epg, t), 0)
    gs = []
    for g in range(N_EXPERT_GROUPS):
        seg = choice[g * epg:(g + 1) * epg, :]
        m1 = jnp.max(seg, axis=0, keepdims=True)
        i1 = jnp.min(jnp.where(seg == m1, gi, epg), axis=0, keepdims=True)
        m2 = jnp.max(jnp.where(gi == i1, -jnp.inf, seg), axis=0, keepdims=True)
        gs.append(m1 + m2)
    masked = []
    for g in range(N_EXPERT_GROUPS):
        beat = jnp.zeros((1, t), I32)
        for g2 in range(N_EXPERT_GROUPS):
            if g2 < g:
                beat = beat + (gs[g2] >= gs[g]).astype(I32)
            elif g2 > g:
                beat = beat + (gs[g2] > gs[g]).astype(I32)
        keep = beat < TOP_K_GROUPS
        masked.append(jnp.where(keep, choice[g * epg:(g + 1) * epg, :], -jnp.inf))
    cur = jnp.concatenate(masked, axis=0)

    ei_all = lax.broadcasted_iota(I32, (N_EXPERTS, t), 0)
    picks, raw = [], []
    onehot = jnp.zeros((N_EXPERTS, t), F32)
    for _ in range(TOP_K):
        mx = jnp.max(cur, axis=0, keepdims=True)
        ei = jnp.min(jnp.where(cur == mx, ei_all, N_EXPERTS), axis=0, keepdims=True)
        hit = ei_all == ei
        raw.append(jnp.sum(jnp.where(hit, scores, 0.0), axis=0, keepdims=True))
        cur = jnp.where(hit, -jnp.inf, cur)
        onehot = jnp.where(hit, 1.0, onehot)
        picks.append(ei)
    tot = raw[0]
    for k in range(1, TOP_K):
        tot = tot + raw[k]

    @pl.when(i == 0)
    def _():
        carry_ref[...] = jnp.zeros_like(carry_ref)

    upper = (lax.broadcasted_iota(I32, (t, t), 0) < lax.broadcasted_iota(I32, (t, t), 1)).astype(BF16)
    before = jnp.dot(onehot.astype(BF16), upper, preferred_element_type=F32) + carry_ref[:, 0:1]
    for k in range(TOP_K):
        eidx_ref[k:k + 1, :] = picks[k]
        wts_ref[k:k + 1, :] = raw[k] / tot * ROUTE_SCALE
        rk = jnp.sum(jnp.where(ei_all == picks[k], before, 0.0), axis=0, keepdims=True)
        rank_ref[k:k + 1, :] = rk.astype(I32)
    carry_ref[...] = carry_ref[...] + jnp.sum(onehot, axis=1, keepdims=True)
    cnt_ref[...] = carry_ref[...]


def _route(xl2, mod3, tiles_per_batch, g2n, w_router_t, router_bias, ws_gate_b, ws_up_b, ws_down_b):
    n, d = xl2.shape
    t = TOKEN_TILE
    e = w_router_t.shape[0]
    kern = functools.partial(_route_kernel, d=d)
    tok = pl.BlockSpec((t, d), lambda i: (i, 0))
    small = pl.BlockSpec((TOP_K, t), lambda i: (0, i))
    return pl.pallas_call(
        kern,
        grid=(n // t,),
        in_specs=[tok,
                  pl.BlockSpec((1, 1, mod3.shape[2]), lambda i: (i // tiles_per_batch, 0, 0)),
                  _full(g2n), _full(w_router_t), _full(router_bias),
                  _full(ws_gate_b), _full(ws_up_b), _full(ws_down_b)],
        out_specs=[tok, tok, small, small, small,
                   pl.BlockSpec((e, LANES), lambda i: (0, 0))],
        out_shape=[jax.ShapeDtypeStruct((n, d), F32),
                   jax.ShapeDtypeStruct((n, d), F32),
                   jax.ShapeDtypeStruct((TOP_K, n), I32),
                   jax.ShapeDtypeStruct((TOP_K, n), F32),
                   jax.ShapeDtypeStruct((TOP_K, n), I32),
                   jax.ShapeDtypeStruct((e, LANES), F32)],
        scratch_shapes=[pltpu.VMEM((e, LANES), F32)],
        compiler_params=_params("arbitrary"),
        name="route",
    )(xl2, mod3, g2n, w_router_t, router_bias, ws_gate_b, ws_up_b, ws_down_b)


def _dest_kernel(eidx_ref, rank_ref, start_ref, o_ref):
    t = eidx_ref.shape[1]
    ei_all = lax.broadcasted_iota(I32, (N_EXPERTS, t), 0)
    st = start_ref[:, 0:1]
    for k in range(TOP_K):
        hit = ei_all == eidx_ref[k:k + 1, :]
        o_ref[k:k + 1, :] = jnp.sum(jnp.where(hit, st, 0), axis=0, keepdims=True) + rank_ref[k:k + 1, :]


def _dest_rows(eidx, rank, starts_b):
    n = eidx.shape[1]
    t = 1024
    small = pl.BlockSpec((TOP_K, t), lambda i: (0, i))
    return pl.pallas_call(
        _dest_kernel,
        grid=(n // t,),
        in_specs=[small, small, _full(starts_b)],
        out_specs=small,
        out_shape=jax.ShapeDtypeStruct((TOP_K, n), I32),
        compiler_params=_params("arbitrary"),
        name="dest",
    )(eidx, rank, starts_b)


def _row_copy(src, dst, sem, src_row, dst_row):
    return pltpu.make_async_copy(src.at[pl.ds(src_row, 1), :], dst.at[pl.ds(dst_row, 1), :], sem)


def _dispatch_kernel(dest_ref, src_ref, dst_ref, sem):
    tile = dest_ref.shape[1]

    def issue(n, c):
        for k in range(TOP_K):
            _row_copy(src_ref, dst_ref, sem, n, dest_ref[k, n]).start()
        return c

    lax.fori_loop(0, tile, issue, 0)

    def drain(n, c):
        for k in range(TOP_K):
            _row_copy(src_ref, dst_ref, sem, n, 0).wait()
        return c

    lax.fori_loop(0, tile, drain, 0)


def _dispatch(dest, src, out_rows):
    n, d = src.shape
    return pl.pallas_call(
        _dispatch_kernel,
        grid=(n // MOVE_TILE,),
        in_specs=[pl.BlockSpec((TOP_K, MOVE_TILE), lambda i: (0, i), memory_space=pltpu.SMEM),
                  pl.BlockSpec((MOVE_TILE, d), lambda i: (i, 0))],
        out_specs=pl.BlockSpec(memory_space=pl.ANY),
        out_shape=jax.ShapeDtypeStruct((out_rows, d), src.dtype),
        scratch_shapes=[pltpu.SemaphoreType.DMA],
        compiler_params=_params("arbitrary"),
        name="dispatch",
    )(dest, src)


def _expert_kernel(blk_ref, exp_ref, lo_ref, hi_ref, first_ref, newexp_ref, nexp_ref,
                   xs_ref, wg_hbm, wu_hbm, wd_hbm, ys_ref,
                   wgf_ref, wuf_ref, wdf_ref, wgb_ref, wub_ref, wdb_ref, sem):
    p = pl.program_id(0)
    rows = xs_ref.shape[0]

    def weight_copies(e):
        return (pltpu.make_async_copy(wg_hbm.at[e], wgf_ref, sem.at[0]),
                pltpu.make_async_copy(wu_hbm.at[e], wuf_ref, sem.at[1]),
                pltpu.make_async_copy(wd_hbm.at[e], wdf_ref, sem.at[2]))

    @pl.when(p == 0)
    def _():
        for cp in weight_copies(exp_ref[0]):
            cp.start()

    @pl.when(newexp_ref[p] == 1)
    def _():
        for cp in weight_copies(exp_ref[p]):
            cp.wait()
        wgb_ref[...] = wgf_ref[...].astype(BF16)
        wub_ref[...] = wuf_ref[...].astype(BF16)
        wdb_ref[...] = wdf_ref[...].astype(BF16)

        @pl.when(nexp_ref[p] >= 0)
        def _():
            for cp in weight_copies(nexp_ref[p]):
                cp.start()

    xb = xs_ref[...].astype(BF16)
    g = jnp.dot(xb, wgb_ref[...], preferred_element_type=F32)
    u = jnp.dot(xb, wub_ref[...], preferred_element_type=F32)
    y = jnp.dot((g * jax.nn.sigmoid(g) * u).astype(BF16), wdb_ref[...], preferred_element_type=F32)
    r = lax.broadcasted_iota(I32, (rows, 1), 0)
    mine = (r >= lo_ref[p]) & (r < hi_ref[p])

    @pl.when(first_ref[p] == 1)
    def _():
        ys_ref[...] = jnp.zeros_like(ys_ref)

    ys_ref[...] = jnp.where(mine, y, ys_ref[...])


def _experts(meta, xs, w_gate, w_up, w_down):
    n_pairs = meta[0].shape[0]
    e, d, f = w_gate.shape
    rows = EXPERT_BLOCK
    by_blk = lambda p, b, *_: (b[p], 0)
    hbm = pl.BlockSpec(memory_space=pl.ANY)
    grid_spec = pltpu.PrefetchScalarGridSpec(
        num_scalar_prefetch=len(meta),
        grid=(n_pairs,),
        in_specs=[pl.BlockSpec((rows, d), by_blk), hbm, hbm, hbm],
        out_specs=pl.BlockSpec((rows, d), by_blk),
        scratch_shapes=[pltpu.VMEM((d, f), F32), pltpu.VMEM((d, f), F32), pltpu.VMEM((f, d), F32),
                        pltpu.VMEM((d, f), BF16), pltpu.VMEM((d, f), BF16), pltpu.VMEM((f, d), BF16),
                        pltpu.SemaphoreType.DMA((3,))],
    )
    return pl.pallas_call(
        _expert_kernel,
        grid_spec=grid_spec,
        out_shape=jax.ShapeDtypeStruct(xs.shape, F32),
        compiler_params=_params("arbitrary"),
        name="experts",
    )(*meta, xs, w_gate, w_up, w_down)


def _pair_metadata(counts, n_rows):
    n_blocks = n_rows // EXPERT_BLOCK
    ends = jnp.cumsum(counts)
    starts = ends - counts
    bounds = jnp.sort(jnp.concatenate([jnp.arange(n_blocks, dtype=I32) * EXPERT_BLOCK, starts]))
    nxt = jnp.concatenate([bounds[1:], jnp.array([n_rows], I32)])
    blk = jnp.minimum(bounds // EXPERT_BLOCK, n_blocks - 1)
    exp = jnp.minimum(jnp.sum((ends[None, :] <= bounds[:, None]).astype(I32), axis=1), N_EXPERTS - 1)
    lo = bounds - blk * EXPERT_BLOCK
    hi = jnp.minimum(nxt - blk * EXPERT_BLOCK, EXPERT_BLOCK)
    one = jnp.ones((1,), I32)
    first = jnp.concatenate([one, (blk[1:] != blk[:-1]).astype(I32)])
    newexp = jnp.concatenate([one, (exp[1:] != exp[:-1]).astype(I32)])
    n_pairs = bounds.shape[0]
    change_at = jnp.where(newexp == 1, jnp.arange(n_pairs, dtype=I32), n_pairs)
    nxt_change = lax.cummin(jnp.concatenate([change_at[1:], jnp.array([n_pairs], I32)]), reverse=True)
    nexp = jnp.where(nxt_change < n_pairs, exp[jnp.minimum(nxt_change, n_pairs - 1)], -1)
    return starts, (blk, exp, lo, hi, first, newexp, nexp)


def _final_kernel(dest_ref, dnext_ref, base_ref, wt_ref, mod_ref, g_ref, ys_ref, o_ref, buf_ref, sem, *, d):
    i = pl.program_id(0)
    last = pl.num_programs(0) - 1
    t = base_ref.shape[0]
    slot = i % 2

    def gather(idx_ref, s):
        def issue(n, c):
            for k in range(TOP_K):
                _row_copy(ys_ref, buf_ref.at[s, k], sem.at[s], idx_ref[k, n], n).start(priority=k % 2)
            return c
        lax.fori_loop(0, t, issue, 0)

    @pl.when(i == 0)
    def _():
        gather(dest_ref, 0)

    @pl.when(i < last)
    def _():
        gather(dnext_ref, 1 - slot)

    def drain(n, c):
        for k in range(TOP_K):
            _row_copy(ys_ref, buf_ref.at[slot, k], sem.at[slot], 0, n).wait()
        return c

    lax.fori_loop(0, t, drain, 0)

    w = wt_ref[...]
    routed = w[:, 0:1] * buf_ref[slot, 0]
    for k in range(1, TOP_K):
        routed = routed + w[:, k:k + 1] * buf_ref[slot, k]
    g2 = mod_ref[0][:, 5 * d:6 * d]
    o_ref[...] = _rms(base_ref[...] + g2 * routed, g_ref[...])


def _final(dest, base, ys, wts_t, mod3, tiles_per_batch, final_g):
    n, d = base.shape
    t = TOKEN_TILE
    n_tiles = n // t
    kern = functools.partial(_final_kernel, d=d)
    tok = pl.BlockSpec((t, d), lambda i: (i, 0))
    return pl.pallas_call(
        kern,
        grid=(n_tiles,),
        in_specs=[pl.BlockSpec((TOP_K, t), lambda i: (0, i), memory_space=pltpu.SMEM),
                  pl.BlockSpec((TOP_K, t), lambda i: (0, jnp.minimum(i + 1, n_tiles - 1)),
                               memory_space=pltpu.SMEM),
                  tok,
                  pl.BlockSpec((t, TOP_K), lambda i: (i, 0)),
                  pl.BlockSpec((1, 1, mod3.shape[2]), lambda i: (i // tiles_per_batch, 0, 0)),
                  _full(final_g),
                  pl.BlockSpec(memory_space=pl.ANY)],
        out_specs=tok,
        out_shape=jax.ShapeDtypeStruct((n, d), F32),
        scratch_shapes=[pltpu.VMEM((2, TOP_K, t, d), F32),
                        pltpu.SemaphoreType.DMA((2,))],
        compiler_params=_params("arbitrary"),
        name="final",
    )(dest, dest, base, wts_t, mod3, final_g, ys)


def kernel(x, c, ctx, c_ctx, w_mod, b_mod, norm1_g, norm2_g, w_in, ssm_lam_re, ssm_lam_im, ssm_log_dt, ssm_b_re, ssm_b_im, ssm_c_re, ssm_c_im, ssm_d, w_glu, w_ssm_out, conv_w, w_conv_out, w_o, w_router, router_bias, w_gate, w_up, w_down, ws_gate, ws_up, ws_down, final_g):
    b, l, d = x.shape
    n = b * l
    sw = ssm_d.shape[1]
    assert w_mod.shape[0] == 1, "single layer"
    assert b == SUBLANES and l % TOKEN_TILE == 0 and l % SCAN_CHUNK == 0 and TIME_TILE % GRID_W == 0

    mod_rows = 2 * SUBLANES
    c_all = jnp.concatenate([c, c_ctx[None, :], jnp.zeros((mod_rows - b - 1, d), F32)], axis=0)
    mod = _modulation(c_all, w_mod[0].astype(BF16), b_mod[0])
    mod3 = mod.reshape(mod_rows, 1, mod.shape[1])

    w_in_b = w_in[0].astype(BF16)
    g1n = norm1_g[0].reshape(1, d)
    u_lat, sga, gbt = _in_proj(x, mod3, g1n, w_in_b, conv_w[0], w_conv_out[0].astype(BF16))
    u_ctx = _ctx_proj(ctx, mod3, b, g1n, w_in_b[:, :sw])

    ops = _ssm_operators(ssm_lam_re[0], ssm_lam_im[0], ssm_log_dt[0], ssm_b_re[0], ssm_b_im[0],
                         ssm_c_re[0], ssm_c_im[0])
    y2 = _s5_scan(u_ctx, u_lat, *ops, nb=b)

    xl = _mixer_out(y2, u_lat, sga, gbt, x, mod3, ssm_d[0].reshape(1, sw), w_glu[0].astype(BF16),
                    w_ssm_out[0].astype(BF16), w_o[0].astype(BF16))

    tiles_per_batch = l // TOKEN_TILE
    h2, base, eidx, wts, rank, cnt = _route(
        xl.reshape(n, d), mod3, tiles_per_batch, norm2_g[0].reshape(1, d), w_router[0].T,
        router_bias[0].reshape(N_EXPERTS, 1), ws_gate[0].astype(BF16), ws_up[0].astype(BF16),
        ws_down[0].astype(BF16))

    counts = cnt[:, 0].astype(I32)
    n_rows = n * TOP_K
    starts, meta = _pair_metadata(counts, n_rows)
    dest = _dest_rows(eidx, rank, jnp.broadcast_to(starts[:, None], (N_EXPERTS, LANES)))

    xs = _dispatch(dest, h2, n_rows)
    ys = _experts(meta, xs, w_gate[0], w_up[0], w_down[0])
    out = _final(dest, base, ys, wts.T, mod3, tiles_per_batch, final_g.reshape(1, d))
    return out.reshape(b, l, d)
```

```python
import functools

import jax
import jax.numpy as jnp
from jax import lax
from jax.experimental import pallas as pl
from jax.experimental.pallas import tpu as pltpu
from jax.experimental.pallas import tpu_sc as plsc

F32 = jnp.float32
BF16 = jnp.bfloat16
I32 = jnp.int32

EPS = 1e-6
GRID_W = 64
N_EXPERTS = 256
TOP_K = 8
N_EXPERT_GROUPS = 8
EXPERTS_PER_GROUP = N_EXPERTS // N_EXPERT_GROUPS
TOP_K_GROUPS = 4
ROUTE_SCALE = 2.5

SUBLANES = 8
LANES = 128
VMEM_LIMIT_BYTES = 48 * 1024 * 1024

TIME_TILE = 64
TOKEN_TILE = 256
SCAN_CHUNK = 128
SCAN_SLAB = 512
EXPERT_BLOCK = 256
SC_CHUNK = 64


def _dot(a, b):
    return jnp.dot(a.astype(BF16), b.astype(BF16), preferred_element_type=F32)


def _rms(xf, g):
    return xf * lax.rsqrt(jnp.mean(xf * xf, axis=-1, keepdims=True) + EPS) * g


def _params(*sem):
    return pltpu.CompilerParams(dimension_semantics=sem, vmem_limit_bytes=VMEM_LIMIT_BYTES)


def _full(a):
    return pl.BlockSpec(a.shape, lambda *_: (0,) * a.ndim)


def _mod_kernel(c_ref, w_ref, b_ref, o_ref):
    c = c_ref[...]
    o_ref[...] = _dot(c * jax.nn.sigmoid(c), w_ref[...]) + b_ref[...]


def _modulation(c_all, w_mod, b_mod):
    rows, d = c_all.shape
    cols = w_mod.shape[1]
    blk = 1536
    return pl.pallas_call(
        _mod_kernel,
        grid=(cols // blk,),
        in_specs=[pl.BlockSpec((rows, d), lambda j: (0, 0)),
                  pl.BlockSpec((d, blk), lambda j: (0, j)),
                  pl.BlockSpec((1, blk), lambda j: (0, j))],
        out_specs=pl.BlockSpec((rows, blk), lambda j: (0, j)),
        out_shape=jax.ShapeDtypeStruct((rows, cols), F32),
        compiler_params=_params("arbitrary"),
        name="mod",
    )(c_all, w_mod, b_mod.reshape(1, cols))


def _per_row(m3, lo, hi, tt):
    nb = m3.shape[0]
    return jnp.broadcast_to(m3[:, :, lo:hi], (nb, tt, hi - lo)).reshape(nb * tt, hi - lo)


def _to_time_major(val, nb, tt):
    c = val.shape[1]
    return pltpu.einshape("btc->tbc", val.reshape(nb, tt, c)).reshape(nb * tt, c)


def _to_batch_major(val, nb, tt):
    c = val.shape[1]
    return pltpu.einshape("tbc->btc", val.reshape(tt, nb, c)).reshape(nb * tt, c)


def _in_proj_kernel(x_ref, mod_ref, g_ref, w_ref, cw_ref, wco_ref, u_ref, sga_ref, gbt_ref, *, d, sw):
    nb, tt, _ = x_ref.shape
    rows = nb * tt
    m3 = mod_ref[...]
    x = x_ref[...].reshape(rows, d)
    h = _rms(x, g_ref[...]) * (1.0 + _per_row(m3, d, 2 * d, tt)) + _per_row(m3, 0, d, tt)
    hb = h.astype(BF16)
    u_ref[...] = _to_time_major(jnp.dot(hb, w_ref[:, 0:sw], preferred_element_type=F32), nb, tt)
    cb = jnp.dot(hb, w_ref[:, sw:2 * sw], preferred_element_type=F32)
    cc = jnp.dot(hb, w_ref[:, 2 * sw:3 * sw], preferred_element_type=F32)
    cv = jnp.dot(hb, w_ref[:, 3 * sw:4 * sw], preferred_element_type=F32)
    ccv = cc * cv
    col = lax.broadcasted_iota(I32, ccv.shape, 0) % GRID_W
    prev = jnp.where(col == 0, 0.0, pltpu.roll(ccv, 1, axis=0))
    nxt = jnp.where(col == GRID_W - 1, 0.0, pltpu.roll(ccv, rows - 1, axis=0))
    cw = cw_ref[...]
    conv = prev * cw[0:1, :] + ccv * cw[1:2, :] + nxt * cw[2:3, :]
    y_conv = _dot(cb * conv, wco_ref[...])
    ga = jnp.dot(hb, w_ref[:, 4 * sw:4 * sw + d], preferred_element_type=F32)
    gb = jnp.dot(hb, w_ref[:, 4 * sw + d:4 * sw + 2 * d], preferred_element_type=F32)
    sga_ref[...] = jax.nn.sigmoid(ga).reshape(nb, tt, d)
    gbt_ref[...] = (jax.nn.sigmoid(gb) * y_conv).reshape(nb, tt, d)


def _in_proj(x, mod3, g1n, w_in_b, conv_w, w_conv_out_b):
    b, l, d = x.shape
    sw = conv_w.shape[1]
    tt = TIME_TILE
    kern = functools.partial(_in_proj_kernel, d=d, sw=sw)
    tok = pl.BlockSpec((b, tt, d), lambda j: (0, j, 0))
    return pl.pallas_call(
        kern,
        grid=(l // tt,),
        in_specs=[tok,
                  pl.BlockSpec((b, 1, mod3.shape[2]), lambda j: (0, 0, 0)),
                  _full(g1n), _full(w_in_b), _full(conv_w), _full(w_conv_out_b)],
        out_specs=[pl.BlockSpec((tt * b, sw), lambda j: (j, 0)), tok, tok],
        out_shape=[jax.ShapeDtypeStruct((l * b, sw), F32),
                   jax.ShapeDtypeStruct((b, l, d), F32),
                   jax.ShapeDtypeStruct((b, l, d), F32)],
        compiler_params=_params("arbitrary"),
        name="in_proj",
    )(x, mod3, g1n, w_in_b, conv_w, w_conv_out_b)


def _ctx_proj_kernel(x_ref, mod_ref, g_ref, w_ref, u_ref, *, d):
    nb, tt, _ = x_ref.shape
    m = mod_ref[0]
    x = x_ref[...].reshape(nb * tt, d)
    h = _rms(x, g_ref[...]) * (1.0 + m[:, d:2 * d]) + m[:, 0:d]
    u_ref[...] = _to_time_major(_dot(h, w_ref[...]), nb, tt)


def _ctx_proj(ctx, mod3, ctx_row, g1n, w_u_b):
    b, lc, d = ctx.shape
    sw = w_u_b.shape[1]
    tt = TIME_TILE
    kern = functools.partial(_ctx_proj_kernel, d=d)
    return pl.pallas_call(
        kern,
        grid=(lc // tt,),
        in_specs=[pl.BlockSpec((b, tt, d), lambda j: (0, j, 0)),
                  pl.BlockSpec((1, 1, mod3.shape[2]), lambda j: (ctx_row, 0, 0)),
                  _full(g1n), _full(w_u_b)],
        out_specs=pl.BlockSpec((tt * b, sw), lambda j: (j, 0)),
        out_shape=jax.ShapeDtypeStruct((lc * b, sw), F32),
        compiler_params=_params("arbitrary"),
        name="ctx_proj",
    )(ctx, mod3, g1n, w_u_b)


def _scan_kernel(uc_ref, ul_ref, are_ref, aim_ref, bre_ref, bim_ref, cre_ref, cim_ref, y_ref,
                 sre_ref, sim_ref, cre_s, cim_s, *, n_ctx, nb):
    dirn = pl.program_id(0)
    j = pl.program_id(1)
    half_c = bre_ref.shape[2]
    half_s = bre_ref.shape[3]
    n_half = bre_ref.shape[1]
    steps = SCAN_CHUNK

    @pl.when(j == 0)
    def _():
        cre_s[...] = jnp.zeros_like(cre_s)
        cim_s[...] = jnp.zeros_like(cim_s)

    u = jnp.where(j < n_ctx, uc_ref[...], ul_ref[...]).astype(BF16)
    for h in range(n_half):
        uh = u[:, h * half_c:(h + 1) * half_c]
        sre_ref[:, h * half_s:(h + 1) * half_s] = jnp.dot(uh, bre_ref[0, h], preferred_element_type=F32)
        sim_ref[:, h * half_s:(h + 1) * half_s] = jnp.dot(uh, bim_ref[0, h], preferred_element_type=F32)

    lanes = sre_ref.shape[1]
    for q in range(lanes // SCAN_SLAB):
        ls = slice(q * SCAN_SLAB, (q + 1) * SCAN_SLAB)
        a_re = jnp.broadcast_to(are_ref[0, :, ls], (nb, SCAN_SLAB))
        a_im = jnp.broadcast_to(aim_ref[0, :, ls], (nb, SCAN_SLAB))

        def body(i, carry, ls=ls, a_re=a_re, a_im=a_im):
            s_re, s_im = carry
            for k in range(SUBLANES):
                t = i * SUBLANES + k
                t = jnp.where(dirn == 0, t, steps - 1 - t)
                r0 = pl.multiple_of(t * nb, nb)
                b_re = sre_ref[pl.ds(r0, nb), ls]
                b_im = sim_ref[pl.ds(r0, nb), ls]
                n_re = a_re * s_re - a_im * s_im + b_re
                n_im = a_re * s_im + a_im * s_re + b_im
                sre_ref[pl.ds(r0, nb), ls] = n_re
                sim_ref[pl.ds(r0, nb), ls] = n_im
                s_re, s_im = n_re, n_im
            return s_re, s_im

        s_re, s_im = lax.fori_loop(0, steps // SUBLANES, body, (cre_s[:, ls], cim_s[:, ls]))
        cre_s[:, ls] = s_re
        cim_s[:, ls] = s_im

    @pl.when(j >= n_ctx)
    def _():
        half_o = cre_ref.shape[3]
        for h in range(n_half):
            s_r = sre_ref[:, h * half_s:(h + 1) * half_s].astype(BF16)
            s_i = sim_ref[:, h * half_s:(h + 1) * half_s].astype(BF16)
            y_ref[0, :, h * half_o:(h + 1) * half_o] = (
                jnp.dot(s_r, cre_ref[0, h], preferred_element_type=F32)
                + jnp.dot(s_i, cim_ref[0, h], preferred_element_type=F32))


def _s5_scan(u_ctx, u_lat, a_re, a_im, b_re, b_im, c_re, c_imn, nb):
    rows_c, sw = u_ctx.shape
    rows_l = u_lat.shape[0]
    rc = SCAN_CHUNK * nb
    n_ctx = rows_c // rc
    n_lat = rows_l // rc
    lanes = a_re.shape[2]

    def ctx_idx(d, j):
        jj = jnp.minimum(j, n_ctx - 1)
        return (jnp.where(d == 0, jj, n_ctx - 1 - jj), 0)

    def lat_idx(d, j):
        jj = jnp.maximum(j - n_ctx, 0)
        return (jnp.where(d == 0, jj, n_lat - 1 - jj), 0)

    def per_dir4(d, j):
        return (d, 0, 0, 0)

    kern = functools.partial(_scan_kernel, n_ctx=n_ctx, nb=nb)
    return pl.pallas_call(
        kern,
        grid=(2, n_ctx + n_lat),
        in_specs=[pl.BlockSpec((rc, sw), ctx_idx),
                  pl.BlockSpec((rc, sw), lat_idx),
                  pl.BlockSpec((1, 1, lanes), lambda d, j: (d, 0, 0)),
                  pl.BlockSpec((1, 1, lanes), lambda d, j: (d, 0, 0)),
                  pl.BlockSpec((1,) + b_re.shape[1:], per_dir4),
                  pl.BlockSpec((1,) + b_im.shape[1:], per_dir4),
                  pl.BlockSpec((1,) + c_re.shape[1:], per_dir4),
                  pl.BlockSpec((1,) + c_imn.shape[1:], per_dir4)],
        out_specs=pl.BlockSpec((1, rc, sw), lambda d, j: (d,) + lat_idx(d, j)),
        out_shape=jax.ShapeDtypeStruct((2, rows_l, sw), F32),
        scratch_shapes=[pltpu.VMEM((rc, lanes), F32), pltpu.VMEM((rc, lanes), F32),
                        pltpu.VMEM((nb, lanes), F32), pltpu.VMEM((nb, lanes), F32)],
        compiler_params=_params("arbitrary", "arbitrary"),
        name="s5_scan",
    )(u_ctx, u_lat, a_re, a_im, b_re, b_im, c_re, c_imn)


def _ssm_operators(lam_re, lam_im, log_dt, b_re, b_im, c_re, c_im):
    dt = jnp.exp(log_dt)[..., None]
    mag = jnp.exp(lam_re * dt)
    a_re = mag * jnp.cos(lam_im * dt)
    a_im = mag * jnp.sin(lam_im * dt)
    den = lam_re * lam_re + lam_im * lam_im
    k_re = ((a_re - 1.0) * lam_re + a_im * lam_im) / den
    k_im = (a_im * lam_re - (a_re - 1.0) * lam_im) / den
    bb_re = k_re[..., None] * b_re - k_im[..., None] * b_im
    bb_im = k_re[..., None] * b_im + k_im[..., None] * b_re
    nd, g, p, q = bb_re.shape
    halves = 2
    gh = g // halves
    eye = jnp.eye(gh, dtype=F32)

    def drive(bb):
        bbh = bb.reshape(nd, halves, gh, p, q)
        m = jnp.einsum('dhgpq,gk->dhgqkp', bbh, eye)
        return m.reshape(nd, halves, gh * q, gh * p).astype(BF16)

    def readout(cc):
        cch = cc.reshape(nd, halves, gh, q, p)
        m = jnp.einsum('dhgqp,gk->dhgpkq', cch, eye)
        return m.reshape(nd, halves, gh * p, gh * q).astype(BF16)

    return (a_re.reshape(nd, 1, g * p), a_im.reshape(nd, 1, g * p),
            drive(bb_re), drive(bb_im), readout(c_re), readout(-c_im))


def _mixer_kernel(y_ref, u_ref, sga_ref, gbt_ref, x_ref, mod_ref, d_ref, wglu_ref, wso_ref, wo_ref, o_ref,
                  *, d, sw):
    nb, tt, _ = x_ref.shape
    rows = nb * tt
    y = _to_batch_major(d_ref[...] * u_ref[...] + y_ref[0] + y_ref[1], nb, tt)
    v = _dot(jax.nn.gelu(y), wglu_ref[...])
    ys = v[:, 0:sw] * jax.nn.sigmoid(v[:, sw:2 * sw])
    y_a = _dot(ys, wso_ref[...])
    merged = sga_ref[...].reshape(rows, d) * y_a + gbt_ref[...].reshape(rows, d)
    o = _dot(merged, wo_ref[...])
    g1 = _per_row(mod_ref[...], 2 * d, 3 * d, tt)
    o_ref[...] = (x_ref[...].reshape(rows, d) + g1 * o).reshape(nb, tt, d)


def _mixer_out(y2, u_lat, sga, gbt, x, mod3, ssm_d, w_glu_b, w_ssm_out_b, w_o_b):
    b, l, d = x.shape
    sw = ssm_d.shape[1]
    tt = TIME_TILE
    kern = functools.partial(_mixer_kernel, d=d, sw=sw)
    tok = pl.BlockSpec((b, tt, d), lambda j: (0, j, 0))
    return pl.pallas_call(
        kern,
        grid=(l // tt,),
        in_specs=[pl.BlockSpec((2, tt * b, sw), lambda j: (0, j, 0)),
                  pl.BlockSpec((tt * b, sw), lambda j: (j, 0)),
                  tok, tok, tok,
                  pl.BlockSpec((b, 1, mod3.shape[2]), lambda j: (0, 0, 0)),
                  _full(ssm_d), _full(w_glu_b), _full(w_ssm_out_b), _full(w_o_b)],
        out_specs=tok,
        out_shape=jax.ShapeDtypeStruct((b, l, d), F32),
        compiler_params=_params("arbitrary"),
        name="mixer_out",
    )(y2, u_lat, sga, gbt, x, mod3, ssm_d, w_glu_b, w_ssm_out_b, w_o_b)


def _split_bf16(a):
    hi = a.astype(BF16)
    lo = (a - hi.astype(F32)).astype(BF16)
    return hi, lo


def _route_kernel(xl_ref, mod_ref, g_ref, wrt_ref, rb_ref, wsg_ref, wsu_ref, wsd_ref,
                  h2_ref, base_ref, eidx_ref, wts_ref, rank_ref, cnt_ref, carry_ref, *, d):
    i = pl.program_id(0)
    t = xl_ref.shape[0]
    m = mod_ref[0]
    xl = xl_ref[...]
    h2 = _rms(xl, g_ref[...]) * (1.0 + m[:, 4 * d:5 * d]) + m[:, 3 * d:4 * d]
    for c in range(d // LANES):
        h2_ref[pl.ds(c, t, stride=SUBLANES), :] = h2[:, c * LANES:(c + 1) * LANES]

    hb = h2.astype(BF16)
    sg = jnp.dot(hb, wsg_ref[...], preferred_element_type=F32)
    su = jnp.dot(hb, wsu_ref[...], preferred_element_type=F32)
    shared = _dot(sg * jax.nn.sigmoid(sg) * su, wsd_ref[...])
    base_ref[...] = xl + m[:, 5 * d:6 * d] * shared

    h_hi, h_lo = _split_bf16(h2)
    w_hi, w_lo = _split_bf16(wrt_ref[...])
    nt = (((1,), (1,)), ((), ()))
    logits = (lax.dot_general(w_hi, h_hi, nt, preferred_element_type=F32)
              + lax.dot_general(w_hi, h_lo, nt, preferred_element_type=F32)
              + lax.dot_general(w_lo, h_hi, nt, preferred_element_type=F32))
    scores = jax.nn.sigmoid(logits)
    choice = scores + rb_ref[...]

    epg = EXPERTS_PER_GROUP
    gi = lax.broadcasted_iota(I32, (epg, t), 0)
    gs = []
    for g in range(N_EXPERT_GROUPS):
        seg = choice[g * epg:(g + 1) * epg, :]
        m1 = jnp.max(seg, axis=0, keepdims=True)
        i1 = jnp.min(jnp.where(seg == m1, gi, epg), axis=0, keepdims=True)
        m2 = jnp.max(jnp.where(gi == i1, -jnp.inf, seg), axis=0, keepdims=True)
        gs.append(m1 + m2)
    masked = []
    for g in range(N_EXPERT_GROUPS):
        beat = jnp.zeros((1, t), I32)
        for g2 in range(N_EXPERT_GROUPS):
            if g2 < g:
                beat = beat + (gs[g2] >= gs[g]).astype(I32)
            elif g2 > g:
                beat = beat + (gs[g2] > gs[g]).astype(I32)
        keep = beat < TOP_K_GROUPS
        masked.append(jnp.where(keep, choice[g * epg:(g + 1) * epg, :], -jnp.inf))
    cur = jnp.concatenate(masked, axis=0)

    ei_all = lax.broadcasted_iota(I32, (N_EXPERTS, t), 0)
    picks, raw = [], []
    onehot = jnp.zeros((N_EXPERTS, t), F32)
    for _ in range(TOP_K):
        mx = jnp.max(cur, axis=0, keepdims=True)
        ei = jnp.min(jnp.where(cur == mx, ei_all, N_EXPERTS), axis=0, keepdims=True)
        hit = ei_all == ei
        raw.append(jnp.sum(jnp.where(hit, scores, 0.0), axis=0, keepdims=True))
        cur = jnp.where(hit, -jnp.inf, cur)
        onehot = jnp.where(hit, 1.0, onehot)
        picks.append(ei)
    tot = raw[0]
    for k in range(1, TOP_K):
        tot = tot + raw[k]

    @pl.when(i == 0)
    def _():
        carry_ref[...] = jnp.zeros_like(carry_ref)

    upper = (lax.broadcasted_iota(I32, (t, t), 0) < lax.broadcasted_iota(I32, (t, t), 1)).astype(BF16)
    before = jnp.dot(onehot.astype(BF16), upper, preferred_element_type=F32) + carry_ref[:, 0:1]
    for k in range(TOP_K):
        eidx_ref[k:k + 1, :] = picks[k]
        wts_ref[k:k + 1, :] = raw[k] / tot * ROUTE_SCALE
        rk = jnp.sum(jnp.where(ei_all == picks[k], before, 0.0), axis=0, keepdims=True)
        rank_ref[k:k + 1, :] = rk.astype(I32)
    carry_ref[...] = carry_ref[...] + jnp.sum(onehot, axis=1, keepdims=True)
    cnt_ref[...] = carry_ref[...]


def _route(xl2, mod3, tiles_per_batch, g2n, w_router_t, router_bias, ws_gate_b, ws_up_b, ws_down_b):
    n, d = xl2.shape
    t = TOKEN_TILE
    e = w_router_t.shape[0]
    kern = functools.partial(_route_kernel, d=d)
    tok = pl.BlockSpec((t, d), lambda i: (i, 0))
    small = pl.BlockSpec((TOP_K, t), lambda i: (0, i))
    return pl.pallas_call(
        kern,
        grid=(n // t,),
        in_specs=[tok,
                  pl.BlockSpec((1, 1, mod3.shape[2]), lambda i: (i // tiles_per_batch, 0, 0)),
                  _full(g2n), _full(w_router_t), _full(router_bias),
                  _full(ws_gate_b), _full(ws_up_b), _full(ws_down_b)],
        out_specs=[pl.BlockSpec((t * SUBLANES, LANES), lambda i: (i, 0)), tok, small, small, small,
                   pl.BlockSpec((e, LANES), lambda i: (0, 0))],
        out_shape=[jax.ShapeDtypeStruct((n * SUBLANES, LANES), F32),
                   jax.ShapeDtypeStruct((n, d), F32),
                   jax.ShapeDtypeStruct((TOP_K, n), I32),
                   jax.ShapeDtypeStruct((TOP_K, n), F32),
                   jax.ShapeDtypeStruct((TOP_K, n), I32),
                   jax.ShapeDtypeStruct((e, LANES), F32)],
        scratch_shapes=[pltpu.VMEM((e, LANES), F32)],
        compiler_params=_params("arbitrary"),
        name="route",
    )(xl2, mod3, g2n, w_router_t, router_bias, ws_gate_b, ws_up_b, ws_down_b)


def _dest_kernel(eidx_ref, rank_ref, start_ref, o_ref):
    t = eidx_ref.shape[1]
    ei_all = lax.broadcasted_iota(I32, (N_EXPERTS, t), 0)
    st = start_ref[:, 0:1]
    for k in range(TOP_K):
        hit = ei_all == eidx_ref[k:k + 1, :]
        o_ref[k:k + 1, :] = jnp.sum(jnp.where(hit, st, 0), axis=0, keepdims=True) + rank_ref[k:k + 1, :]


def _dest_rows(eidx, rank, starts_b):
    n = eidx.shape[1]
    t = 1024
    small = pl.BlockSpec((TOP_K, t), lambda i: (0, i))
    return pl.pallas_call(
        _dest_kernel,
        grid=(n // t,),
        in_specs=[small, small, _full(starts_b)],
        out_specs=small,
        out_shape=jax.ShapeDtypeStruct((TOP_K, n), I32),
        compiler_params=_params("arbitrary"),
        name="dest",
    )(eidx, rank, starts_b)


def _sc_workers():
    info = plsc.get_sparse_core_info()
    return info.num_cores, info.num_cores * info.num_subcores


def _sc_dispatch(dest_flat, src3, out_rows):
    n = src3.shape[0]
    n_cores, n_workers = _sc_workers()
    per_worker = n // n_workers
    assert per_worker % SC_CHUNK == 0
    mesh = plsc.VectorSubcoreMesh(core_axis_name="c", subcore_axis_name="s")

    @functools.partial(
        pl.kernel, mesh=mesh,
        out_type=jax.ShapeDtypeStruct((out_rows,) + src3.shape[1:], src3.dtype),
        scratch_types=[pltpu.VMEM((SC_CHUNK,) + src3.shape[1:], src3.dtype)]
        + [pltpu.VMEM((SC_CHUNK,), I32)] * TOP_K,
        name="dispatch",
    )
    def run(dest_hbm, src_hbm, out_hbm, rows_v, *idx_v):
        worker = lax.axis_index("s") * n_cores + lax.axis_index("c")

        @pl.loop(0, per_worker // SC_CHUNK)
        def _(j):
            base = worker * per_worker + j * SC_CHUNK
            pltpu.sync_copy(src_hbm.at[pl.ds(base, SC_CHUNK)], rows_v)
            for k in range(TOP_K):
                pltpu.sync_copy(dest_hbm.at[pl.ds(k * n + base, SC_CHUNK)], idx_v[k])
            for k in range(TOP_K):
                pltpu.sync_copy(rows_v, out_hbm.at[idx_v[k]])

    return run(dest_flat, src3)


def _sc_combine(dest_flat, ys3, n):
    n_cores, n_workers = _sc_workers()
    per_worker = n // n_workers
    assert per_worker % SC_CHUNK == 0
    mesh = plsc.VectorSubcoreMesh(core_axis_name="c", subcore_axis_name="s")

    @functools.partial(
        pl.kernel, mesh=mesh,
        out_type=jax.ShapeDtypeStruct((TOP_K * n,) + ys3.shape[1:], ys3.dtype),
        scratch_types=[pltpu.VMEM((SC_CHUNK,) + ys3.shape[1:], ys3.dtype), pltpu.VMEM((SC_CHUNK,), I32)],
        name="combine",
    )
    def run(dest_hbm, ys_hbm, out_hbm, rows_v, idx_v):
        worker = lax.axis_index("s") * n_cores + lax.axis_index("c")

        @pl.loop(0, per_worker // SC_CHUNK)
        def _(j):
            base = worker * per_worker + j * SC_CHUNK
            for k in range(TOP_K):
                pltpu.sync_copy(dest_hbm.at[pl.ds(k * n + base, SC_CHUNK)], idx_v)
                pltpu.sync_copy(ys_hbm.at[idx_v], rows_v)
                pltpu.sync_copy(rows_v, out_hbm.at[pl.ds(k * n + base, SC_CHUNK)])

    return run(dest_flat, ys3)


def _expert_kernel(blk_ref, exp_ref, lo_ref, hi_ref, first_ref, newexp_ref, nexp_ref,
                   xs_ref, wg_hbm, wu_hbm, wd_hbm, ys_ref,
                   wgf_ref, wuf_ref, wdf_ref, wgb_ref, wub_ref, wdb_ref, sem):
    p = pl.program_id(0)
    rows = EXPERT_BLOCK
    n_col = xs_ref.shape[0] // rows

    def weight_copies(e):
        return (pltpu.make_async_copy(wg_hbm.at[e], wgf_ref, sem.at[0]),
                pltpu.make_async_copy(wu_hbm.at[e], wuf_ref, sem.at[1]),
                pltpu.make_async_copy(wd_hbm.at[e], wdf_ref, sem.at[2]))

    @pl.when(p == 0)
    def _():
        for cp in weight_copies(exp_ref[0]):
            cp.start()

    @pl.when(newexp_ref[p] == 1)
    def _():
        for cp in weight_copies(exp_ref[p]):
            cp.wait()
        wgb_ref[...] = wgf_ref[...].astype(BF16)
        wub_ref[...] = wuf_ref[...].astype(BF16)
        wdb_ref[...] = wdf_ref[...].astype(BF16)

        @pl.when(nexp_ref[p] >= 0)
        def _():
            for cp in weight_copies(nexp_ref[p]):
                cp.start()

    x = jnp.concatenate([xs_ref[pl.ds(c, rows, stride=SUBLANES), :] for c in range(n_col)], axis=1)
    xb = x.astype(BF16)
    g = jnp.dot(xb, wgb_ref[...], preferred_element_type=F32)
    u = jnp.dot(xb, wub_ref[...], preferred_element_type=F32)
    y = jnp.dot((g * jax.nn.sigmoid(g) * u).astype(BF16), wdb_ref[...], preferred_element_type=F32)
    r = lax.broadcasted_iota(I32, (rows, 1), 0)
    mine = (r >= lo_ref[p]) & (r < hi_ref[p])

    @pl.when(first_ref[p] == 1)
    def _():
        ys_ref[...] = jnp.zeros_like(ys_ref)

    for c in range(n_col):
        sl = pl.ds(c, rows, stride=SUBLANES)
        ys_ref[sl, :] = jnp.where(mine, y[:, c * LANES:(c + 1) * LANES], ys_ref[sl, :])


def _experts(meta, xs, w_gate, w_up, w_down):
    n_pairs = meta[0].shape[0]
    e, d, f = w_gate.shape
    rows = EXPERT_BLOCK * SUBLANES
    by_blk = lambda p, b, *_: (b[p], 0)
    hbm = pl.BlockSpec(memory_space=pl.ANY)
    grid_spec = pltpu.PrefetchScalarGridSpec(
        num_scalar_prefetch=len(meta),
        grid=(n_pairs,),
        in_specs=[pl.BlockSpec((rows, LANES), by_blk), hbm, hbm, hbm],
        out_specs=pl.BlockSpec((rows, LANES), by_blk),
        scratch_shapes=[pltpu.VMEM((d, f), F32), pltpu.VMEM((d, f), F32), pltpu.VMEM((f, d), F32),
                        pltpu.VMEM((d, f), BF16), pltpu.VMEM((d, f), BF16), pltpu.VMEM((f, d), BF16),
                        pltpu.SemaphoreType.DMA((3,))],
    )
    return pl.pallas_call(
        _expert_kernel,
        grid_spec=grid_spec,
        out_shape=jax.ShapeDtypeStruct(xs.shape, F32),
        compiler_params=_params("arbitrary"),
        name="experts",
    )(*meta, xs, w_gate, w_up, w_down)


def _pair_metadata(counts, n_rows):
    n_blocks = n_rows // EXPERT_BLOCK
    ends = jnp.cumsum(counts)
    starts = ends - counts
    bounds = jnp.sort(jnp.concatenate([jnp.arange(n_blocks, dtype=I32) * EXPERT_BLOCK, starts]))
    nxt = jnp.concatenate([bounds[1:], jnp.array([n_rows], I32)])
    blk = jnp.minimum(bounds // EXPERT_BLOCK, n_blocks - 1)
    exp = jnp.minimum(jnp.sum((ends[None, :] <= bounds[:, None]).astype(I32), axis=1), N_EXPERTS - 1)
    lo = bounds - blk * EXPERT_BLOCK
    hi = jnp.minimum(nxt - blk * EXPERT_BLOCK, EXPERT_BLOCK)
    one = jnp.ones((1,), I32)
    first = jnp.concatenate([one, (blk[1:] != blk[:-1]).astype(I32)])
    newexp = jnp.concatenate([one, (exp[1:] != exp[:-1]).astype(I32)])
    n_pairs = bounds.shape[0]
    change_at = jnp.where(newexp == 1, jnp.arange(n_pairs, dtype=I32), n_pairs)
    nxt_change = lax.cummin(jnp.concatenate([change_at[1:], jnp.array([n_pairs], I32)]), reverse=True)
    nexp = jnp.where(nxt_change < n_pairs, exp[jnp.minimum(nxt_change, n_pairs - 1)], -1)
    return starts, (blk, exp, lo, hi, first, newexp, nexp)


def _final_kernel(base_ref, yt_ref, wt_ref, mod_ref, g_ref, o_ref, *, d):
    t = base_ref.shape[0]
    w = wt_ref[...]
    cols = []
    for c in range(d // LANES):
        acc = w[:, 0:1] * yt_ref[0, pl.ds(c, t, stride=SUBLANES), :]
        for k in range(1, TOP_K):
            acc = acc + w[:, k:k + 1] * yt_ref[k, pl.ds(c, t, stride=SUBLANES), :]
        cols.append(acc)
    routed = jnp.concatenate(cols, axis=1)
    g2 = mod_ref[0][:, 5 * d:6 * d]
    o_ref[...] = _rms(base_ref[...] + g2 * routed, g_ref[...])


def _final(base, ytok3, wts_t, mod3, tiles_per_batch, final_g):
    n, d = base.shape
    t = TOKEN_TILE
    kern = functools.partial(_final_kernel, d=d)
    tok = pl.BlockSpec((t, d), lambda i: (i, 0))
    return pl.pallas_call(
        kern,
        grid=(n // t,),
        in_specs=[tok,
                  pl.BlockSpec((TOP_K, t * SUBLANES, LANES), lambda i: (0, i, 0)),
                  pl.BlockSpec((t, TOP_K), lambda i: (i, 0)),
                  pl.BlockSpec((1, 1, mod3.shape[2]), lambda i: (i // tiles_per_batch, 0, 0)),
                  _full(final_g)],
        out_specs=tok,
        out_shape=jax.ShapeDtypeStruct((n, d), F32),
        compiler_params=_params("arbitrary"),
        name="final",
    )(base, ytok3, wts_t, mod3, final_g)


def kernel(x, c, ctx, c_ctx, w_mod, b_mod, norm1_g, norm2_g, w_in, ssm_lam_re, ssm_lam_im, ssm_log_dt, ssm_b_re, ssm_b_im, ssm_c_re, ssm_c_im, ssm_d, w_glu, w_ssm_out, conv_w, w_conv_out, w_o, w_router, router_bias, w_gate, w_up, w_down, ws_gate, ws_up, ws_down, final_g):
    b, l, d = x.shape
    n = b * l
    sw = ssm_d.shape[1]
    assert w_mod.shape[0] == 1, "single layer"
    assert b == SUBLANES and l % TOKEN_TILE == 0 and l % SCAN_CHUNK == 0 and TIME_TILE % GRID_W == 0

    mod_rows = 2 * SUBLANES
    c_all = jnp.concatenate([c, c_ctx[None, :], jnp.zeros((mod_rows - b - 1, d), F32)], axis=0)
    mod = _modulation(c_all, w_mod[0].astype(BF16), b_mod[0])
    mod3 = mod.reshape(mod_rows, 1, mod.shape[1])

    w_in_b = w_in[0].astype(BF16)
    g1n = norm1_g[0].reshape(1, d)
    u_lat, sga, gbt = _in_proj(x, mod3, g1n, w_in_b, conv_w[0], w_conv_out[0].astype(BF16))
    u_ctx = _ctx_proj(ctx, mod3, b, g1n, w_in_b[:, :sw])

    ops = _ssm_operators(ssm_lam_re[0], ssm_lam_im[0], ssm_log_dt[0], ssm_b_re[0], ssm_b_im[0],
                         ssm_c_re[0], ssm_c_im[0])
    y2 = _s5_scan(u_ctx, u_lat, *ops, nb=b)

    xl = _mixer_out(y2, u_lat, sga, gbt, x, mod3, ssm_d[0].reshape(1, sw), w_glu[0].astype(BF16),
                    w_ssm_out[0].astype(BF16), w_o[0].astype(BF16))

    tiles_per_batch = l // TOKEN_TILE
    h2, base, eidx, wts, rank, cnt = _route(
        xl.reshape(n, d), mod3, tiles_per_batch, norm2_g[0].reshape(1, d), w_router[0].T,
        router_bias[0].reshape(N_EXPERTS, 1), ws_gate[0].astype(BF16), ws_up[0].astype(BF16),
        ws_down[0].astype(BF16))

    counts = cnt[:, 0].astype(I32)
    n_rows = n * TOP_K
    starts, meta = _pair_metadata(counts, n_rows)
    dest = _dest_rows(eidx, rank, jnp.broadcast_to(starts[:, None], (N_EXPERTS, LANES)))

    dest_flat = dest.reshape(TOP_K * n)
    xs = _sc_dispatch(dest_flat, h2.reshape(n, SUBLANES, LANES), n_rows)
    ys = _experts(meta, xs.reshape(n_rows * SUBLANES, LANES), w_gate[0], w_up[0], w_down[0])
    ytok = _sc_combine(dest_flat, ys.reshape(n_rows, SUBLANES, LANES), n)
    out = _final(base, ytok.reshape(TOP_K, n * SUBLANES, LANES), wts.T, mod3, tiles_per_batch,
                 final_g.reshape(1, d))
    return out.reshape(b, l, d)
```

```python
import functools

import jax
import jax.numpy as jnp
from jax import lax
from jax.experimental import pallas as pl
from jax.experimental.pallas import tpu as pltpu
from jax.experimental.pallas import tpu_sc as plsc

F32 = jnp.float32
BF16 = jnp.bfloat16
I32 = jnp.int32

EPS = 1e-6
GRID_W = 64
N_EXPERTS = 256
TOP_K = 8
N_EXPERT_GROUPS = 8
EXPERTS_PER_GROUP = N_EXPERTS // N_EXPERT_GROUPS
TOP_K_GROUPS = 4
ROUTE_SCALE = 2.5

SUBLANES = 8
LANES = 128
VMEM_LIMIT_BYTES = 48 * 1024 * 1024

TIME_TILE = 64
TOKEN_TILE = 256
SCAN_CHUNK = 128
SCAN_SLAB = 512
EXPERT_CHUNK = 256
EXPERT_RING = 4
SC_CHUNK = 64


def _dot(a, b):
    return jnp.dot(a.astype(BF16), b.astype(BF16), preferred_element_type=F32)


def _rms(xf, g):
    return xf * lax.rsqrt(jnp.mean(xf * xf, axis=-1, keepdims=True) + EPS) * g


def _params(*sem):
    return pltpu.CompilerParams(dimension_semantics=sem, vmem_limit_bytes=VMEM_LIMIT_BYTES)


def _full(a):
    return pl.BlockSpec(a.shape, lambda *_: (0,) * a.ndim)


def _mod_kernel(c_ref, w_ref, b_ref, o_ref):
    c = c_ref[...]
    o_ref[...] = _dot(c * jax.nn.sigmoid(c), w_ref[...]) + b_ref[...]


def _modulation(c_all, w_mod, b_mod):
    rows, d = c_all.shape
    cols = w_mod.shape[1]
    blk = 1536
    return pl.pallas_call(
        _mod_kernel,
        grid=(cols // blk,),
        in_specs=[pl.BlockSpec((rows, d), lambda j: (0, 0)),
                  pl.BlockSpec((d, blk), lambda j: (0, j)),
                  pl.BlockSpec((1, blk), lambda j: (0, j))],
        out_specs=pl.BlockSpec((rows, blk), lambda j: (0, j)),
        out_shape=jax.ShapeDtypeStruct((rows, cols), F32),
        compiler_params=_params("arbitrary"),
        name="mod",
    )(c_all, w_mod, b_mod.reshape(1, cols))


def _per_row(m3, lo, hi, tt):
    nb = m3.shape[0]
    return jnp.broadcast_to(m3[:, :, lo:hi], (nb, tt, hi - lo)).reshape(nb * tt, hi - lo)


def _to_time_major(val, nb, tt):
    c = val.shape[1]
    return pltpu.einshape("btc->tbc", val.reshape(nb, tt, c)).reshape(nb * tt, c)


def _to_batch_major(val, nb, tt):
    c = val.shape[1]
    return pltpu.einshape("tbc->btc", val.reshape(tt, nb, c)).reshape(nb * tt, c)


def _in_proj_kernel(x_ref, mod_ref, g_ref, w_ref, cw_ref, wco_ref, u_ref, sga_ref, gbt_ref, *, d, sw):
    nb, tt, _ = x_ref.shape
    rows = nb * tt
    m3 = mod_ref[...]
    x = x_ref[...].reshape(rows, d)
    h = _rms(x, g_ref[...]) * (1.0 + _per_row(m3, d, 2 * d, tt)) + _per_row(m3, 0, d, tt)
    hb = h.astype(BF16)
    u_ref[...] = _to_time_major(jnp.dot(hb, w_ref[:, 0:sw], preferred_element_type=F32), nb, tt)
    cb = jnp.dot(hb, w_ref[:, sw:2 * sw], preferred_element_type=F32)
    cc = jnp.dot(hb, w_ref[:, 2 * sw:3 * sw], preferred_element_type=F32)
    cv = jnp.dot(hb, w_ref[:, 3 * sw:4 * sw], preferred_element_type=F32)
    ccv = cc * cv
    col = lax.broadcasted_iota(I32, ccv.shape, 0) % GRID_W
    prev = jnp.where(col == 0, 0.0, pltpu.roll(ccv, 1, axis=0))
    nxt = jnp.where(col == GRID_W - 1, 0.0, pltpu.roll(ccv, rows - 1, axis=0))
    cw = cw_ref[...]
    conv = prev * cw[0:1, :] + ccv * cw[1:2, :] + nxt * cw[2:3, :]
    y_conv = _dot(cb * conv, wco_ref[...])
    ga = jnp.dot(hb, w_ref[:, 4 * sw:4 * sw + d], preferred_element_type=F32)
    gb = jnp.dot(hb, w_ref[:, 4 * sw + d:4 * sw + 2 * d], preferred_element_type=F32)
    sga_ref[...] = jax.nn.sigmoid(ga).reshape(nb, tt, d)
    gbt_ref[...] = (jax.nn.sigmoid(gb) * y_conv).reshape(nb, tt, d)


def _in_proj(x, mod3, g1n, w_in_b, conv_w, w_conv_out_b):
    b, l, d = x.shape
    sw = conv_w.shape[1]
    tt = TIME_TILE
    kern = functools.partial(_in_proj_kernel, d=d, sw=sw)
    tok = pl.BlockSpec((b, tt, d), lambda j: (0, j, 0))
    return pl.pallas_call(
        kern,
        grid=(l // tt,),
        in_specs=[tok,
                  pl.BlockSpec((b, 1, mod3.shape[2]), lambda j: (0, 0, 0)),
                  _full(g1n), _full(w_in_b), _full(conv_w), _full(w_conv_out_b)],
        out_specs=[pl.BlockSpec((tt * b, sw), lambda j: (j, 0)), tok, tok],
        out_shape=[jax.ShapeDtypeStruct((l * b, sw), F32),
                   jax.ShapeDtypeStruct((b, l, d), F32),
                   jax.ShapeDtypeStruct((b, l, d), F32)],
        compiler_params=_params("arbitrary"),
        name="in_proj",
    )(x, mod3, g1n, w_in_b, conv_w, w_conv_out_b)


def _ctx_proj_kernel(x_ref, mod_ref, g_ref, w_ref, u_ref, *, d):
    nb, tt, _ = x_ref.shape
    m = mod_ref[0]
    x = x_ref[...].reshape(nb * tt, d)
    h = _rms(x, g_ref[...]) * (1.0 + m[:, d:2 * d]) + m[:, 0:d]
    u_ref[...] = _to_time_major(_dot(h, w_ref[...]), nb, tt)


def _ctx_proj(ctx, mod3, ctx_row, g1n, w_u_b):
    b, lc, d = ctx.shape
    sw = w_u_b.shape[1]
    tt = TIME_TILE
    kern = functools.partial(_ctx_proj_kernel, d=d)
    return pl.pallas_call(
        kern,
        grid=(lc // tt,),
        in_specs=[pl.BlockSpec((b, tt, d), lambda j: (0, j, 0)),
                  pl.BlockSpec((1, 1, mod3.shape[2]), lambda j: (ctx_row, 0, 0)),
                  _full(g1n), _full(w_u_b)],
        out_specs=pl.BlockSpec((tt * b, sw), lambda j: (j, 0)),
        out_shape=jax.ShapeDtypeStruct((lc * b, sw), F32),
        compiler_params=_params("arbitrary"),
        name="ctx_proj",
    )(ctx, mod3, g1n, w_u_b)


def _scan_kernel(uc_ref, ul_ref, are_ref, aim_ref, bre_ref, bim_ref, cre_ref, cim_ref, y_ref,
                 sre_ref, sim_ref, cre_s, cim_s, *, n_ctx, nb):
    dirn = pl.program_id(0)
    j = pl.program_id(1)
    half_c = bre_ref.shape[2]
    half_s = bre_ref.shape[3]
    n_half = bre_ref.shape[1]
    steps = SCAN_CHUNK

    @pl.when(j == 0)
    def _():
        cre_s[...] = jnp.zeros_like(cre_s)
        cim_s[...] = jnp.zeros_like(cim_s)

    u = jnp.where(j < n_ctx, uc_ref[...], ul_ref[...]).astype(BF16)
    for h in range(n_half):
        uh = u[:, h * half_c:(h + 1) * half_c]
        sre_ref[:, h * half_s:(h + 1) * half_s] = jnp.dot(uh, bre_ref[0, h], preferred_element_type=F32)
        sim_ref[:, h * half_s:(h + 1) * half_s] = jnp.dot(uh, bim_ref[0, h], preferred_element_type=F32)

    lanes = sre_ref.shape[1]
    for q in range(lanes // SCAN_SLAB):
        ls = slice(q * SCAN_SLAB, (q + 1) * SCAN_SLAB)
        a_re = jnp.broadcast_to(are_ref[0, :, ls], (nb, SCAN_SLAB))
        a_im = jnp.broadcast_to(aim_ref[0, :, ls], (nb, SCAN_SLAB))

        def body(i, carry, ls=ls, a_re=a_re, a_im=a_im):
            s_re, s_im = carry
            for k in range(SUBLANES):
                t = i * SUBLANES + k
                t = jnp.where(dirn == 0, t, steps - 1 - t)
                r0 = pl.multiple_of(t * nb, nb)
                b_re = sre_ref[pl.ds(r0, nb), ls]
                b_im = sim_ref[pl.ds(r0, nb), ls]
                n_re = a_re * s_re - a_im * s_im + b_re
                n_im = a_re * s_im + a_im * s_re + b_im
                sre_ref[pl.ds(r0, nb), ls] = n_re
                sim_ref[pl.ds(r0, nb), ls] = n_im
                s_re, s_im = n_re, n_im
            return s_re, s_im

        s_re, s_im = lax.fori_loop(0, steps // SUBLANES, body, (cre_s[:, ls], cim_s[:, ls]))
        cre_s[:, ls] = s_re
        cim_s[:, ls] = s_im

    @pl.when(j >= n_ctx)
    def _():
        half_o = cre_ref.shape[3]
        for h in range(n_half):
            s_r = sre_ref[:, h * half_s:(h + 1) * half_s].astype(BF16)
            s_i = sim_ref[:, h * half_s:(h + 1) * half_s].astype(BF16)
            y_ref[0, :, h * half_o:(h + 1) * half_o] = (
                jnp.dot(s_r, cre_ref[0, h], preferred_element_type=F32)
                + jnp.dot(s_i, cim_ref[0, h], preferred_element_type=F32))


def _s5_scan(u_ctx, u_lat, a_re, a_im, b_re, b_im, c_re, c_imn, nb):
    rows_c, sw = u_ctx.shape
    rows_l = u_lat.shape[0]
    rc = SCAN_CHUNK * nb
    n_ctx = rows_c // rc
    n_lat = rows_l // rc
    lanes = a_re.shape[2]

    def ctx_idx(d, j):
        jj = jnp.minimum(j, n_ctx - 1)
        return (jnp.where(d == 0, jj, n_ctx - 1 - jj), 0)

    def lat_idx(d, j):
        jj = jnp.maximum(j - n_ctx, 0)
        return (jnp.where(d == 0, jj, n_lat - 1 - jj), 0)

    def per_dir4(d, j):
        return (d, 0, 0, 0)

    kern = functools.partial(_scan_kernel, n_ctx=n_ctx, nb=nb)
    return pl.pallas_call(
        kern,
        grid=(2, n_ctx + n_lat),
        in_specs=[pl.BlockSpec((rc, sw), ctx_idx),
                  pl.BlockSpec((rc, sw), lat_idx),
                  pl.BlockSpec((1, 1, lanes), lambda d, j: (d, 0, 0)),
                  pl.BlockSpec((1, 1, lanes), lambda d, j: (d, 0, 0)),
                  pl.BlockSpec((1,) + b_re.shape[1:], per_dir4),
                  pl.BlockSpec((1,) + b_im.shape[1:], per_dir4),
                  pl.BlockSpec((1,) + c_re.shape[1:], per_dir4),
                  pl.BlockSpec((1,) + c_imn.shape[1:], per_dir4)],
        out_specs=pl.BlockSpec((1, rc, sw), lambda d, j: (d,) + lat_idx(d, j)),
        out_shape=jax.ShapeDtypeStruct((2, rows_l, sw), F32),
        scratch_shapes=[pltpu.VMEM((rc, lanes), F32), pltpu.VMEM((rc, lanes), F32),
                        pltpu.VMEM((nb, lanes), F32), pltpu.VMEM((nb, lanes), F32)],
        compiler_params=_params("arbitrary", "arbitrary"),
        name="s5_scan",
    )(u_ctx, u_lat, a_re, a_im, b_re, b_im, c_re, c_imn)


def _ssm_operators(lam_re, lam_im, log_dt, b_re, b_im, c_re, c_im):
    dt = jnp.exp(log_dt)[..., None]
    mag = jnp.exp(lam_re * dt)
    a_re = mag * jnp.cos(lam_im * dt)
    a_im = mag * jnp.sin(lam_im * dt)
    den = lam_re * lam_re + lam_im * lam_im
    k_re = ((a_re - 1.0) * lam_re + a_im * lam_im) / den
    k_im = (a_im * lam_re - (a_re - 1.0) * lam_im) / den
    bb_re = k_re[..., None] * b_re - k_im[..., None] * b_im
    bb_im = k_re[..., None] * b_im + k_im[..., None] * b_re
    nd, g, p, q = bb_re.shape
    halves = 2
    gh = g // halves
    eye = jnp.eye(gh, dtype=F32)

    def drive(bb):
        bbh = bb.reshape(nd, halves, gh, p, q)
        m = jnp.einsum('dhgpq,gk->dhgqkp', bbh, eye)
        return m.reshape(nd, halves, gh * q, gh * p).astype(BF16)

    def readout(cc):
        cch = cc.reshape(nd, halves, gh, q, p)
        m = jnp.einsum('dhgqp,gk->dhgpkq', cch, eye)
        return m.reshape(nd, halves, gh * p, gh * q).astype(BF16)

    return (a_re.reshape(nd, 1, g * p), a_im.reshape(nd, 1, g * p),
            drive(bb_re), drive(bb_im), readout(c_re), readout(-c_im))


def _mixer_kernel(y_ref, u_ref, sga_ref, gbt_ref, x_ref, mod_ref, d_ref, wglu_ref, wso_ref, wo_ref, o_ref,
                  *, d, sw):
    nb, tt, _ = x_ref.shape
    rows = nb * tt
    y = _to_batch_major(d_ref[...] * u_ref[...] + y_ref[0] + y_ref[1], nb, tt)
    v = _dot(jax.nn.gelu(y), wglu_ref[...])
    ys = v[:, 0:sw] * jax.nn.sigmoid(v[:, sw:2 * sw])
    y_a = _dot(ys, wso_ref[...])
    merged = sga_ref[...].reshape(rows, d) * y_a + gbt_ref[...].reshape(rows, d)
    o = _dot(merged, wo_ref[...])
    g1 = _per_row(mod_ref[...], 2 * d, 3 * d, tt)
    o_ref[...] = (x_ref[...].reshape(rows, d) + g1 * o).reshape(nb, tt, d)


def _mixer_out(y2, u_lat, sga, gbt, x, mod3, ssm_d, w_glu_b, w_ssm_out_b, w_o_b):
    b, l, d = x.shape
    sw = ssm_d.shape[1]
    tt = TIME_TILE
    kern = functools.partial(_mixer_kernel, d=d, sw=sw)
    tok = pl.BlockSpec((b, tt, d), lambda j: (0, j, 0))
    return pl.pallas_call(
        kern,
        grid=(l // tt,),
        in_specs=[pl.BlockSpec((2, tt * b, sw), lambda j: (0, j, 0)),
                  pl.BlockSpec((tt * b, sw), lambda j: (j, 0)),
                  tok, tok, tok,
                  pl.BlockSpec((b, 1, mod3.shape[2]), lambda j: (0, 0, 0)),
                  _full(ssm_d), _full(w_glu_b), _full(w_ssm_out_b), _full(w_o_b)],
        out_specs=tok,
        out_shape=jax.ShapeDtypeStruct((b, l, d), F32),
        compiler_params=_params("arbitrary"),
        name="mixer_out",
    )(y2, u_lat, sga, gbt, x, mod3, ssm_d, w_glu_b, w_ssm_out_b, w_o_b)


def _split_bf16(a):
    hi = a.astype(BF16)
    lo = (a - hi.astype(F32)).astype(BF16)
    return hi, lo


def _route_kernel(xl_ref, mod_ref, g_ref, wrt_ref, rb_ref, wsg_ref, wsu_ref, wsd_ref,
                  h2_ref, base_ref, eidx_ref, wts_ref, rank_ref, cnt_ref, carry_ref, *, d):
    i = pl.program_id(0)
    t = xl_ref.shape[0]
    m = mod_ref[0]
    xl = xl_ref[...]
    h2 = _rms(xl, g_ref[...]) * (1.0 + m[:, 4 * d:5 * d]) + m[:, 3 * d:4 * d]
    for c in range(d // LANES):
        h2_ref[pl.ds(c, t, stride=SUBLANES), :] = h2[:, c * LANES:(c + 1) * LANES]

    hb = h2.astype(BF16)
    sg = jnp.dot(hb, wsg_ref[...], preferred_element_type=F32)
    su = jnp.dot(hb, wsu_ref[...], preferred_element_type=F32)
    shared = _dot(sg * jax.nn.sigmoid(sg) * su, wsd_ref[...])
    base_ref[...] = xl + m[:, 5 * d:6 * d] * shared

    h_hi, h_lo = _split_bf16(h2)
    w_hi, w_lo = _split_bf16(wrt_ref[...])
    nt = (((1,), (1,)), ((), ()))
    logits = (lax.dot_general(w_hi, h_hi, nt, preferred_element_type=F32)
              + lax.dot_general(w_hi, h_lo, nt, preferred_element_type=F32)
              + lax.dot_general(w_lo, h_hi, nt, preferred_element_type=F32))
    scores = jax.nn.sigmoid(logits)
    choice = scores + rb_ref[...]

    epg = EXPERTS_PER_GROUP
    gi = lax.broadcasted_iota(I32, (epg, t), 0)
    gs = []
    for g in range(N_EXPERT_GROUPS):
        seg = choice[g * epg:(g + 1) * epg, :]
        m1 = jnp.max(seg, axis=0, keepdims=True)
        i1 = jnp.min(jnp.where(seg == m1, gi, epg), axis=0, keepdims=True)
        m2 = jnp.max(jnp.where(gi == i1, -jnp.inf, seg), axis=0, keepdims=True)
        gs.append(m1 + m2)
    masked = []
    for g in range(N_EXPERT_GROUPS):
        beat = jnp.zeros((1, t), I32)
        for g2 in range(N_EXPERT_GROUPS):
            if g2 < g:
                beat = beat + (gs[g2] >= gs[g]).astype(I32)
            elif g2 > g:
                beat = beat + (gs[g2] > gs[g]).astype(I32)
        keep = beat < TOP_K_GROUPS
        masked.append(jnp.where(keep, choice[g * epg:(g + 1) * epg, :], -jnp.inf))
    cur = jnp.concatenate(masked, axis=0)

    ei_all = lax.broadcasted_iota(I32, (N_EXPERTS, t), 0)
    picks, raw = [], []
    onehot = jnp.zeros((N_EXPERTS, t), F32)
    for _ in range(TOP_K):
        mx = jnp.max(cur, axis=0, keepdims=True)
        ei = jnp.min(jnp.where(cur == mx, ei_all, N_EXPERTS), axis=0, keepdims=True)
        hit = ei_all == ei
        raw.append(jnp.sum(jnp.where(hit, scores, 0.0), axis=0, keepdims=True))
        cur = jnp.where(hit, -jnp.inf, cur)
        onehot = jnp.where(hit, 1.0, onehot)
        picks.append(ei)
    tot = raw[0]
    for k in range(1, TOP_K):
        tot = tot + raw[k]

    @pl.when(i == 0)
    def _():
        carry_ref[...] = jnp.zeros_like(carry_ref)

    upper = (lax.broadcasted_iota(I32, (t, t), 0) < lax.broadcasted_iota(I32, (t, t), 1)).astype(BF16)
    before = jnp.dot(onehot.astype(BF16), upper, preferred_element_type=F32) + carry_ref[:, 0:1]
    for k in range(TOP_K):
        eidx_ref[k:k + 1, :] = picks[k]
        wts_ref[k:k + 1, :] = raw[k] / tot * ROUTE_SCALE
        rk = jnp.sum(jnp.where(ei_all == picks[k], before, 0.0), axis=0, keepdims=True)
        rank_ref[k:k + 1, :] = rk.astype(I32)
    carry_ref[...] = carry_ref[...] + jnp.sum(onehot, axis=1, keepdims=True)
    cnt_ref[...] = carry_ref[...]


def _route(xl2, mod3, tiles_per_batch, g2n, w_router_t, router_bias, ws_gate_b, ws_up_b, ws_down_b):
    n, d = xl2.shape
    t = TOKEN_TILE
    e = w_router_t.shape[0]
    kern = functools.partial(_route_kernel, d=d)
    tok = pl.BlockSpec((t, d), lambda i: (i, 0))
    small = pl.BlockSpec((TOP_K, t), lambda i: (0, i))
    return pl.pallas_call(
        kern,
        grid=(n // t,),
        in_specs=[tok,
                  pl.BlockSpec((1, 1, mod3.shape[2]), lambda i: (i // tiles_per_batch, 0, 0)),
                  _full(g2n), _full(w_router_t), _full(router_bias),
                  _full(ws_gate_b), _full(ws_up_b), _full(ws_down_b)],
        out_specs=[pl.BlockSpec((t * SUBLANES, LANES), lambda i: (i, 0)), tok, small, small, small,
                   pl.BlockSpec((e, LANES), lambda i: (0, 0))],
        out_shape=[jax.ShapeDtypeStruct((n * SUBLANES, LANES), F32),
                   jax.ShapeDtypeStruct((n, d), F32),
                   jax.ShapeDtypeStruct((TOP_K, n), I32),
                   jax.ShapeDtypeStruct((TOP_K, n), F32),
                   jax.ShapeDtypeStruct((TOP_K, n), I32),
                   jax.ShapeDtypeStruct((e, LANES), F32)],
        scratch_shapes=[pltpu.VMEM((e, LANES), F32)],
        compiler_params=_params("arbitrary"),
        name="route",
    )(xl2, mod3, g2n, w_router_t, router_bias, ws_gate_b, ws_up_b, ws_down_b)


def _dest_kernel(eidx_ref, rank_ref, start_ref, o_ref):
    t = eidx_ref.shape[1]
    ei_all = lax.broadcasted_iota(I32, (N_EXPERTS, t), 0)
    st = start_ref[:, 0:1]
    for k in range(TOP_K):
        hit = ei_all == eidx_ref[k:k + 1, :]
        o_ref[k:k + 1, :] = jnp.sum(jnp.where(hit, st, 0), axis=0, keepdims=True) + rank_ref[k:k + 1, :]


def _dest_rows(eidx, rank, starts_b):
    n = eidx.shape[1]
    t = 1024
    small = pl.BlockSpec((TOP_K, t), lambda i: (0, i))
    return pl.pallas_call(
        _dest_kernel,
        grid=(n // t,),
        in_specs=[small, small, _full(starts_b)],
        out_specs=small,
        out_shape=jax.ShapeDtypeStruct((TOP_K, n), I32),
        compiler_params=_params("arbitrary"),
        name="dest",
    )(eidx, rank, starts_b)


def _sc_workers():
    info = plsc.get_sparse_core_info()
    return info.num_cores, info.num_cores * info.num_subcores


def _sc_dispatch(dest_flat, src3, out_rows):
    n = src3.shape[0]
    n_cores, n_workers = _sc_workers()
    per_worker = n // n_workers
    assert per_worker % SC_CHUNK == 0
    mesh = plsc.VectorSubcoreMesh(core_axis_name="c", subcore_axis_name="s")

    @functools.partial(
        pl.kernel, mesh=mesh,
        out_type=jax.ShapeDtypeStruct((out_rows,) + src3.shape[1:], src3.dtype),
        scratch_types=[pltpu.VMEM((SC_CHUNK,) + src3.shape[1:], src3.dtype)]
        + [pltpu.VMEM((SC_CHUNK,), I32)] * TOP_K,
        name="dispatch",
    )
    def run(dest_hbm, src_hbm, out_hbm, rows_v, *idx_v):
        worker = lax.axis_index("s") * n_cores + lax.axis_index("c")

        @pl.loop(0, per_worker // SC_CHUNK)
        def _(j):
            base = worker * per_worker + j * SC_CHUNK
            pltpu.sync_copy(src_hbm.at[pl.ds(base, SC_CHUNK)], rows_v)
            for k in range(TOP_K):
                pltpu.sync_copy(dest_hbm.at[pl.ds(k * n + base, SC_CHUNK)], idx_v[k])
            for k in range(TOP_K):
                pltpu.sync_copy(rows_v, out_hbm.at[idx_v[k]])

    return run(dest_flat, src3)


def _sc_combine(dest_flat, ys3, n):
    n_cores, n_workers = _sc_workers()
    per_worker = n // n_workers
    assert per_worker % SC_CHUNK == 0
    mesh = plsc.VectorSubcoreMesh(core_axis_name="c", subcore_axis_name="s")

    @functools.partial(
        pl.kernel, mesh=mesh,
        out_type=jax.ShapeDtypeStruct((TOP_K * n,) + ys3.shape[1:], ys3.dtype),
        scratch_types=[pltpu.VMEM((SC_CHUNK,) + ys3.shape[1:], ys3.dtype), pltpu.VMEM((SC_CHUNK,), I32)],
        name="combine",
    )
    def run(dest_hbm, ys_hbm, out_hbm, rows_v, idx_v):
        worker = lax.axis_index("s") * n_cores + lax.axis_index("c")

        @pl.loop(0, per_worker // SC_CHUNK)
        def _(j):
            base = worker * per_worker + j * SC_CHUNK
            for k in range(TOP_K):
                pltpu.sync_copy(dest_hbm.at[pl.ds(k * n + base, SC_CHUNK)], idx_v)
                pltpu.sync_copy(ys_hbm.at[idx_v], rows_v)
                pltpu.sync_copy(rows_v, out_hbm.at[pl.ds(k * n + base, SC_CHUNK)])

    return run(dest_flat, ys3)


def _chunk_metadata(counts, n_rows):
    ch = EXPERT_CHUNK
    n_ch = (counts + ch - 1) // ch
    cum = jnp.cumsum(n_ch)
    total = cum[-1]
    max_chunks = n_rows // ch + N_EXPERTS
    i = jnp.arange(max_chunks, dtype=I32)
    e = jnp.sum((cum[None, :] <= i[:, None]).astype(I32), axis=1)
    e_last = jnp.max(jnp.where(counts > 0, jnp.arange(N_EXPERTS, dtype=I32), 0))
    exp = jnp.where(i < total, jnp.minimum(e, N_EXPERTS - 1), e_last).astype(I32)
    newexp = jnp.concatenate([jnp.ones((1,), I32), (exp[1:] != exp[:-1]).astype(I32)])
    change_at = jnp.where(newexp == 1, i, max_chunks)
    nxt_change = lax.cummin(jnp.concatenate([change_at[1:], jnp.array([max_chunks], I32)]), reverse=True)
    nexp = jnp.where(nxt_change < max_chunks, exp[jnp.minimum(nxt_change, max_chunks - 1)], -1).astype(I32)
    starts = (cum - n_ch) * ch
    return starts.astype(I32), (exp, newexp, nexp, total.astype(I32).reshape(1))


def _expert_kernel(exp_ref, newexp_ref, nexp_ref, total_ref, xs_hbm, wg_hbm, wu_hbm, wd_hbm, ys_hbm,
                   xbuf, ybuf, wgf, wuf, wdf, wgb, wub, wdb, sem_x, sem_y, sem_w):
    ch = EXPERT_CHUNK
    nbuf = EXPERT_RING
    rows = ch * SUBLANES
    n_col = wgf.shape[0] // LANES
    total = total_ref[0]

    def x_copy(i, slot):
        r0 = pl.multiple_of(i * rows, rows)
        return pltpu.make_async_copy(xs_hbm.at[pl.ds(r0, rows)], xbuf.at[slot], sem_x.at[slot])

    def y_copy(i, slot):
        r0 = pl.multiple_of(i * rows, rows)
        return pltpu.make_async_copy(ybuf.at[slot], ys_hbm.at[pl.ds(r0, rows)], sem_y.at[slot])

    def weight_copies(e):
        return (pltpu.make_async_copy(wg_hbm.at[e], wgf, sem_w.at[0]),
                pltpu.make_async_copy(wu_hbm.at[e], wuf, sem_w.at[1]),
                pltpu.make_async_copy(wd_hbm.at[e], wdf, sem_w.at[2]))

    for cp in weight_copies(exp_ref[0]):
        cp.start()
    for b in range(nbuf - 1):
        @pl.when(b < total)
        def _(b=b):
            x_copy(b, b).start()

    def chunk(i, carry):
        slot = lax.rem(i, nbuf)
        x_copy(i, slot).wait()
        ahead = i + (nbuf - 1)

        @pl.when(ahead < total)
        def _():
            x_copy(ahead, lax.rem(ahead, nbuf)).start()

        @pl.when(newexp_ref[i] == 1)
        def _():
            for cp in weight_copies(exp_ref[i]):
                cp.wait()
            wgb[...] = wgf[...].astype(BF16)
            wub[...] = wuf[...].astype(BF16)
            wdb[...] = wdf[...].astype(BF16)

            @pl.when(nexp_ref[i] >= 0)
            def _():
                for cp in weight_copies(nexp_ref[i]):
                    cp.start()

        @pl.when(i >= nbuf)
        def _():
            y_copy(i, slot).wait()

        xr = xbuf.at[slot]
        yr = ybuf.at[slot]
        x = jnp.concatenate([xr[pl.ds(c, ch, stride=SUBLANES), :] for c in range(n_col)], axis=1).astype(BF16)
        g = jnp.dot(x, wgb[...], preferred_element_type=F32)
        u = jnp.dot(x, wub[...], preferred_element_type=F32)
        y = jnp.dot((g * jax.nn.sigmoid(g) * u).astype(BF16), wdb[...], preferred_element_type=F32)
        for c in range(n_col):
            yr[pl.ds(c, ch, stride=SUBLANES), :] = y[:, c * LANES:(c + 1) * LANES]
        y_copy(i, slot).start()
        return carry

    lax.fori_loop(0, total, chunk, 0)

    for b in range(nbuf):
        @pl.when(b < total)
        def _(b=b):
            y_copy(0, b).wait()


def _experts(meta, xs, w_gate, w_up, w_down):
    e, d, f = w_gate.shape
    hbm = pl.BlockSpec(memory_space=pl.ANY)
    rows = EXPERT_CHUNK * SUBLANES
    grid_spec = pltpu.PrefetchScalarGridSpec(
        num_scalar_prefetch=len(meta), grid=(1,),
        in_specs=[hbm, hbm, hbm, hbm], out_specs=hbm,
        scratch_shapes=[pltpu.VMEM((EXPERT_RING, rows, LANES), F32), pltpu.VMEM((EXPERT_RING, rows, LANES), F32),
                        pltpu.VMEM((d, f), F32), pltpu.VMEM((d, f), F32), pltpu.VMEM((f, d), F32),
                        pltpu.VMEM((d, f), BF16), pltpu.VMEM((d, f), BF16), pltpu.VMEM((f, d), BF16),
                        pltpu.SemaphoreType.DMA((EXPERT_RING,)), pltpu.SemaphoreType.DMA((EXPERT_RING,)),
                        pltpu.SemaphoreType.DMA((3,))])
    return pl.pallas_call(
        _expert_kernel,
        grid_spec=grid_spec,
        out_shape=jax.ShapeDtypeStruct(xs.shape, F32),
        compiler_params=_params("arbitrary"),
        name="experts",
    )(*meta, xs, w_gate, w_up, w_down)


def _final_kernel(base_ref, yt_ref, wt_ref, mod_ref, g_ref, o_ref, *, d):
    t = base_ref.shape[0]
    w = wt_ref[...]
    cols = []
    for c in range(d // LANES):
        acc = w[:, 0:1] * yt_ref[0, pl.ds(c, t, stride=SUBLANES), :]
        for k in range(1, TOP_K):
            acc = acc + w[:, k:k + 1] * yt_ref[k, pl.ds(c, t, stride=SUBLANES), :]
        cols.append(acc)
    routed = jnp.concatenate(cols, axis=1)
    g2 = mod_ref[0][:, 5 * d:6 * d]
    o_ref[...] = _rms(base_ref[...] + g2 * routed, g_ref[...])


def _final(base, ytok3, wts_t, mod3, tiles_per_batch, final_g):
    n, d = base.shape
    t = TOKEN_TILE
    kern = functools.partial(_final_kernel, d=d)
    tok = pl.BlockSpec((t, d), lambda i: (i, 0))
    return pl.pallas_call(
        kern,
        grid=(n // t,),
        in_specs=[tok,
                  pl.BlockSpec((TOP_K, t * SUBLANES, LANES), lambda i: (0, i, 0)),
                  pl.BlockSpec((t, TOP_K), lambda i: (i, 0)),
                  pl.BlockSpec((1, 1, mod3.shape[2]), lambda i: (i // tiles_per_batch, 0, 0)),
                  _full(final_g)],
        out_specs=tok,
        out_shape=jax.ShapeDtypeStruct((n, d), F32),
        compiler_params=_params("arbitrary"),
        name="final",
    )(base, ytok3, wts_t, mod3, final_g)


def kernel(x, c, ctx, c_ctx, w_mod, b_mod, norm1_g, norm2_g, w_in, ssm_lam_re, ssm_lam_im, ssm_log_dt, ssm_b_re, ssm_b_im, ssm_c_re, ssm_c_im, ssm_d, w_glu, w_ssm_out, conv_w, w_conv_out, w_o, w_router, router_bias, w_gate, w_up, w_down, ws_gate, ws_up, ws_down, final_g):
    b, l, d = x.shape
    n = b * l
    sw = ssm_d.shape[1]
    assert w_mod.shape[0] == 1, "single layer"
    assert b == SUBLANES and l % TOKEN_TILE == 0 and l % SCAN_CHUNK == 0 and TIME_TILE % GRID_W == 0

    mod_rows = 2 * SUBLANES
    c_all = jnp.concatenate([c, c_ctx[None, :], jnp.zeros((mod_rows - b - 1, d), F32)], axis=0)
    mod = _modulation(c_all, w_mod[0].astype(BF16), b_mod[0])
    mod3 = mod.reshape(mod_rows, 1, mod.shape[1])

    w_in_b = w_in[0].astype(BF16)
    g1n = norm1_g[0].reshape(1, d)
    u_lat, sga, gbt = _in_proj(x, mod3, g1n, w_in_b, conv_w[0], w_conv_out[0].astype(BF16))
    u_ctx = _ctx_proj(ctx, mod3, b, g1n, w_in_b[:, :sw])

    ops = _ssm_operators(ssm_lam_re[0], ssm_lam_im[0], ssm_log_dt[0], ssm_b_re[0], ssm_b_im[0],
                         ssm_c_re[0], ssm_c_im[0])
    y2 = _s5_scan(u_ctx, u_lat, *ops, nb=b)

    xl = _mixer_out(y2, u_lat, sga, gbt, x, mod3, ssm_d[0].reshape(1, sw), w_glu[0].astype(BF16),
                    w_ssm_out[0].astype(BF16), w_o[0].astype(BF16))

    tiles_per_batch = l // TOKEN_TILE
    h2, base, eidx, wts, rank, cnt = _route(
        xl.reshape(n, d), mod3, tiles_per_batch, norm2_g[0].reshape(1, d), w_router[0].T,
        router_bias[0].reshape(N_EXPERTS, 1), ws_gate[0].astype(BF16), ws_up[0].astype(BF16),
        ws_down[0].astype(BF16))

    counts = cnt[:, 0].astype(I32)
    n_rows = n * TOP_K
    starts, meta = _chunk_metadata(counts, n_rows)
    buf_rows = n_rows + N_EXPERTS * EXPERT_CHUNK
    dest = _dest_rows(eidx, rank, jnp.broadcast_to(starts[:, None], (N_EXPERTS, LANES)))

    dest_flat = dest.reshape(TOP_K * n)
    xs = _sc_dispatch(dest_flat, h2.reshape(n, SUBLANES, LANES), buf_rows)
    ys = _experts(meta, xs.reshape(buf_rows * SUBLANES, LANES), w_gate[0], w_up[0], w_down[0])
    ytok = _sc_combine(dest_flat, ys.reshape(buf_rows, SUBLANES, LANES), n)
    out = _final(base, ytok.reshape(TOP_K, n * SUBLANES, LANES), wts.T, mod3, tiles_per_batch,
                 final_g.reshape(1, d))
    return out.reshape(b, l, d)
```

```python
import functools

import jax
import jax.numpy as jnp
from jax import lax
from jax.experimental import pallas as pl
from jax.experimental.pallas import tpu as pltpu
from jax.experimental.pallas import tpu_sc as plsc

F32 = jnp.float32
BF16 = jnp.bfloat16
I32 = jnp.int32
U32 = jnp.uint32

EPS = 1e-6
GRID_W = 64
N_EXPERTS = 256
TOP_K = 8
N_EXPERT_GROUPS = 8
EXPERTS_PER_GROUP = N_EXPERTS // N_EXPERT_GROUPS
TOP_K_GROUPS = 4
ROUTE_SCALE = 2.5

SUBLANES = 8
LANES = 128
VMEM_LIMIT_BYTES = 48 * 1024 * 1024

TIME_TILE = 64
TOKEN_TILE = 256
SCAN_CHUNK = 128
SCAN_SLAB = 512
EXPERT_CHUNK = 256
EXPERT_RING = 4
SC_CHUNK = 128
PACK_SUBLANES = 4


def _dot(a, b):
    return jnp.dot(a.astype(BF16), b.astype(BF16), preferred_element_type=F32)


def _rms(xf, g):
    return xf * lax.rsqrt(jnp.mean(xf * xf, axis=-1, keepdims=True) + EPS) * g


def _params(*sem):
    return pltpu.CompilerParams(dimension_semantics=sem, vmem_limit_bytes=VMEM_LIMIT_BYTES)


def _full(a):
    return pl.BlockSpec(a.shape, lambda *_: (0,) * a.ndim)


def _pack_rows(v):
    half = v.shape[1] // 2
    lo = lax.bitcast_convert_type(v[:, :half].astype(BF16).astype(F32), U32) >> 16
    hi = lax.bitcast_convert_type(v[:, half:].astype(BF16).astype(F32), U32) & jnp.uint32(0xFFFF0000)
    return lo | hi


def _unpack_lo(w):
    return lax.bitcast_convert_type(w << 16, F32)


def _unpack_hi(w):
    return lax.bitcast_convert_type(w & jnp.uint32(0xFFFF0000), F32)


def _store_packed(ref, v):
    t = v.shape[0]
    w = _pack_rows(v)
    for c in range(PACK_SUBLANES):
        ref[pl.ds(c, t, stride=PACK_SUBLANES), :] = w[:, c * LANES:(c + 1) * LANES]


def _load_packed(load, t):
    ws = [load(pl.ds(c, t, stride=PACK_SUBLANES)) for c in range(PACK_SUBLANES)]
    return jnp.concatenate([_unpack_lo(w) for w in ws] + [_unpack_hi(w) for w in ws], axis=1)


def _mod_kernel(c_ref, w_ref, b_ref, o_ref):
    c = c_ref[...]
    o_ref[...] = _dot(c * jax.nn.sigmoid(c), w_ref[...]) + b_ref[...]


def _modulation(c_all, w_mod, b_mod):
    rows, d = c_all.shape
    cols = w_mod.shape[1]
    blk = 1536
    return pl.pallas_call(
        _mod_kernel,
        grid=(cols // blk,),
        in_specs=[pl.BlockSpec((rows, d), lambda j: (0, 0)),
                  pl.BlockSpec((d, blk), lambda j: (0, j)),
                  pl.BlockSpec((1, blk), lambda j: (0, j))],
        out_specs=pl.BlockSpec((rows, blk), lambda j: (0, j)),
        out_shape=jax.ShapeDtypeStruct((rows, cols), F32),
        compiler_params=_params("arbitrary"),
        name="mod",
    )(c_all, w_mod, b_mod.reshape(1, cols))


def _per_row(m3, lo, hi, tt):
    nb = m3.shape[0]
    return jnp.broadcast_to(m3[:, :, lo:hi], (nb, tt, hi - lo)).reshape(nb * tt, hi - lo)


def _to_time_major(val, nb, tt):
    c = val.shape[1]
    return pltpu.einshape("btc->tbc", val.reshape(nb, tt, c)).reshape(nb * tt, c)


def _to_batch_major(val, nb, tt):
    c = val.shape[1]
    return pltpu.einshape("tbc->btc", val.reshape(tt, nb, c)).reshape(nb * tt, c)


def _in_proj_kernel(x_ref, mod_ref, g_ref, w_ref, cw_ref, wco_ref, u_ref, sga_ref, gbt_ref, *, d, sw):
    nb, tt, _ = x_ref.shape
    rows = nb * tt
    m3 = mod_ref[...]
    x = x_ref[...].reshape(rows, d)
    h = _rms(x, g_ref[...]) * (1.0 + _per_row(m3, d, 2 * d, tt)) + _per_row(m3, 0, d, tt)
    hb = h.astype(BF16)
    u_ref[...] = _to_time_major(jnp.dot(hb, w_ref[:, 0:sw], preferred_element_type=F32), nb, tt)
    cb = jnp.dot(hb, w_ref[:, sw:2 * sw], preferred_element_type=F32)
    cc = jnp.dot(hb, w_ref[:, 2 * sw:3 * sw], preferred_element_type=F32)
    cv = jnp.dot(hb, w_ref[:, 3 * sw:4 * sw], preferred_element_type=F32)
    ccv = cc * cv
    col = lax.broadcasted_iota(I32, ccv.shape, 0) % GRID_W
    prev = jnp.where(col == 0, 0.0, pltpu.roll(ccv, 1, axis=0))
    nxt = jnp.where(col == GRID_W - 1, 0.0, pltpu.roll(ccv, rows - 1, axis=0))
    cw = cw_ref[...]
    conv = prev * cw[0:1, :] + ccv * cw[1:2, :] + nxt * cw[2:3, :]
    y_conv = _dot(cb * conv, wco_ref[...])
    ga = jnp.dot(hb, w_ref[:, 4 * sw:4 * sw + d], preferred_element_type=F32)
    gb = jnp.dot(hb, w_ref[:, 4 * sw + d:4 * sw + 2 * d], preferred_element_type=F32)
    sga_ref[...] = jax.nn.sigmoid(ga).reshape(nb, tt, d)
    gbt_ref[...] = (jax.nn.sigmoid(gb) * y_conv).reshape(nb, tt, d)


def _in_proj(x, mod3, g1n, w_in_b, conv_w, w_conv_out_b):
    b, l, d = x.shape
    sw = conv_w.shape[1]
    tt = TIME_TILE
    kern = functools.partial(_in_proj_kernel, d=d, sw=sw)
    tok = pl.BlockSpec((b, tt, d), lambda j: (0, j, 0))
    return pl.pallas_call(
        kern,
        grid=(l // tt,),
        in_specs=[tok,
                  pl.BlockSpec((b, 1, mod3.shape[2]), lambda j: (0, 0, 0)),
                  _full(g1n), _full(w_in_b), _full(conv_w), _full(w_conv_out_b)],
        out_specs=[pl.BlockSpec((tt * b, sw), lambda j: (j, 0)), tok, tok],
        out_shape=[jax.ShapeDtypeStruct((l * b, sw), F32),
                   jax.ShapeDtypeStruct((b, l, d), F32),
                   jax.ShapeDtypeStruct((b, l, d), F32)],
        compiler_params=_params("arbitrary"),
        name="in_proj",
    )(x, mod3, g1n, w_in_b, conv_w, w_conv_out_b)


def _ctx_proj_kernel(x_ref, mod_ref, g_ref, w_ref, u_ref, *, d):
    nb, tt, _ = x_ref.shape
    m = mod_ref[0]
    x = x_ref[...].reshape(nb * tt, d)
    h = _rms(x, g_ref[...]) * (1.0 + m[:, d:2 * d]) + m[:, 0:d]
    u_ref[...] = _to_time_major(_dot(h, w_ref[...]), nb, tt)


def _ctx_proj(ctx, mod3, ctx_row, g1n, w_u_b):
    b, lc, d = ctx.shape
    sw = w_u_b.shape[1]
    tt = TIME_TILE
    kern = functools.partial(_ctx_proj_kernel, d=d)
    return pl.pallas_call(
        kern,
        grid=(lc // tt,),
        in_specs=[pl.BlockSpec((b, tt, d), lambda j: (0, j, 0)),
                  pl.BlockSpec((1, 1, mod3.shape[2]), lambda j: (ctx_row, 0, 0)),
                  _full(g1n), _full(w_u_b)],
        out_specs=pl.BlockSpec((tt * b, sw), lambda j: (j, 0)),
        out_shape=jax.ShapeDtypeStruct((lc * b, sw), F32),
        compiler_params=_params("arbitrary"),
        name="ctx_proj",
    )(ctx, mod3, g1n, w_u_b)


def _scan_kernel(uc_ref, ul_ref, are_ref, aim_ref, bre_ref, bim_ref, cre_ref, cim_ref, y_ref,
                 sre_ref, sim_ref, cre_s, cim_s, *, n_ctx, nb):
    dirn = pl.program_id(0)
    j = pl.program_id(1)
    half_c = bre_ref.shape[2]
    half_s = bre_ref.shape[3]
    n_half = bre_ref.shape[1]
    steps = SCAN_CHUNK

    @pl.when(j == 0)
    def _():
        cre_s[...] = jnp.zeros_like(cre_s)
        cim_s[...] = jnp.zeros_like(cim_s)

    u = jnp.where(j < n_ctx, uc_ref[...], ul_ref[...]).astype(BF16)
    for h in range(n_half):
        uh = u[:, h * half_c:(h + 1) * half_c]
        sre_ref[:, h * half_s:(h + 1) * half_s] = jnp.dot(uh, bre_ref[0, h], preferred_element_type=F32)
        sim_ref[:, h * half_s:(h + 1) * half_s] = jnp.dot(uh, bim_ref[0, h], preferred_element_type=F32)

    lanes = sre_ref.shape[1]
    for q in range(lanes // SCAN_SLAB):
        ls = slice(q * SCAN_SLAB, (q + 1) * SCAN_SLAB)
        a_re = jnp.broadcast_to(are_ref[0, :, ls], (nb, SCAN_SLAB))
        a_im = jnp.broadcast_to(aim_ref[0, :, ls], (nb, SCAN_SLAB))

        def body(i, carry, ls=ls, a_re=a_re, a_im=a_im):
            s_re, s_im = carry
            for k in range(SUBLANES):
                t = i * SUBLANES + k
                t = jnp.where(dirn == 0, t, steps - 1 - t)
                r0 = pl.multiple_of(t * nb, nb)
                b_re = sre_ref[pl.ds(r0, nb), ls]
                b_im = sim_ref[pl.ds(r0, nb), ls]
                n_re = a_re * s_re - a_im * s_im + b_re
                n_im = a_re * s_im + a_im * s_re + b_im
                sre_ref[pl.ds(r0, nb), ls] = n_re
                sim_ref[pl.ds(r0, nb), ls] = n_im
                s_re, s_im = n_re, n_im
            return s_re, s_im

        s_re, s_im = lax.fori_loop(0, steps // SUBLANES, body, (cre_s[:, ls], cim_s[:, ls]))
        cre_s[:, ls] = s_re
        cim_s[:, ls] = s_im

    @pl.when(j >= n_ctx)
    def _():
        half_o = cre_ref.shape[3]
        for h in range(n_half):
            s_r = sre_ref[:, h * half_s:(h + 1) * half_s].astype(BF16)
            s_i = sim_ref[:, h * half_s:(h + 1) * half_s].astype(BF16)
            y_ref[0, :, h * half_o:(h + 1) * half_o] = (
                jnp.dot(s_r, cre_ref[0, h], preferred_element_type=F32)
                + jnp.dot(s_i, cim_ref[0, h], preferred_element_type=F32))


def _s5_scan(u_ctx, u_lat, a_re, a_im, b_re, b_im, c_re, c_imn, nb):
    rows_c, sw = u_ctx.shape
    rows_l = u_lat.shape[0]
    rc = SCAN_CHUNK * nb
    n_ctx = rows_c // rc
    n_lat = rows_l // rc
    lanes = a_re.shape[2]

    def ctx_idx(d, j):
        jj = jnp.minimum(j, n_ctx - 1)
        return (jnp.where(d == 0, jj, n_ctx - 1 - jj), 0)

    def lat_idx(d, j):
        jj = jnp.maximum(j - n_ctx, 0)
        return (jnp.where(d == 0, jj, n_lat - 1 - jj), 0)

    def per_dir4(d, j):
        return (d, 0, 0, 0)

    kern = functools.partial(_scan_kernel, n_ctx=n_ctx, nb=nb)
    return pl.pallas_call(
        kern,
        grid=(2, n_ctx + n_lat),
        in_specs=[pl.BlockSpec((rc, sw), ctx_idx),
                  pl.BlockSpec((rc, sw), lat_idx),
                  pl.BlockSpec((1, 1, lanes), lambda d, j: (d, 0, 0)),
                  pl.BlockSpec((1, 1, lanes), lambda d, j: (d, 0, 0)),
                  pl.BlockSpec((1,) + b_re.shape[1:], per_dir4),
                  pl.BlockSpec((1,) + b_im.shape[1:], per_dir4),
                  pl.BlockSpec((1,) + c_re.shape[1:], per_dir4),
                  pl.BlockSpec((1,) + c_imn.shape[1:], per_dir4)],
        out_specs=pl.BlockSpec((1, rc, sw), lambda d, j: (d,) + lat_idx(d, j)),
        out_shape=jax.ShapeDtypeStruct((2, rows_l, sw), F32),
        scratch_shapes=[pltpu.VMEM((rc, lanes), F32), pltpu.VMEM((rc, lanes), F32),
                        pltpu.VMEM((nb, lanes), F32), pltpu.VMEM((nb, lanes), F32)],
        compiler_params=_params("arbitrary", "arbitrary"),
        name="s5_scan",
    )(u_ctx, u_lat, a_re, a_im, b_re, b_im, c_re, c_imn)


def _ssm_operators(lam_re, lam_im, log_dt, b_re, b_im, c_re, c_im):
    dt = jnp.exp(log_dt)[..., None]
    mag = jnp.exp(lam_re * dt)
    a_re = mag * jnp.cos(lam_im * dt)
    a_im = mag * jnp.sin(lam_im * dt)
    den = lam_re * lam_re + lam_im * lam_im
    k_re = ((a_re - 1.0) * lam_re + a_im * lam_im) / den
    k_im = (a_im * lam_re - (a_re - 1.0) * lam_im) / den
    bb_re = k_re[..., None] * b_re - k_im[..., None] * b_im
    bb_im = k_re[..., None] * b_im + k_im[..., None] * b_re
    nd, g, p, q = bb_re.shape
    halves = 2
    gh = g // halves
    eye = jnp.eye(gh, dtype=F32)

    def drive(bb):
        bbh = bb.reshape(nd, halves, gh, p, q)
        m = jnp.einsum('dhgpq,gk->dhgqkp', bbh, eye)
        return m.reshape(nd, halves, gh * q, gh * p).astype(BF16)

    def readout(cc):
        cch = cc.reshape(nd, halves, gh, q, p)
        m = jnp.einsum('dhgqp,gk->dhgpkq', cch, eye)
        return m.reshape(nd, halves, gh * p, gh * q).astype(BF16)

    return (a_re.reshape(nd, 1, g * p), a_im.reshape(nd, 1, g * p),
            drive(bb_re), drive(bb_im), readout(c_re), readout(-c_im))


def _mixer_kernel(y_ref, u_ref, sga_ref, gbt_ref, x_ref, mod_ref, d_ref, wglu_ref, wso_ref, wo_ref, o_ref,
                  *, d, sw):
    nb, tt, _ = x_ref.shape
    rows = nb * tt
    y = _to_batch_major(d_ref[...] * u_ref[...] + y_ref[0] + y_ref[1], nb, tt)
    v = _dot(jax.nn.gelu(y), wglu_ref[...])
    ys = v[:, 0:sw] * jax.nn.sigmoid(v[:, sw:2 * sw])
    y_a = _dot(ys, wso_ref[...])
    merged = sga_ref[...].reshape(rows, d) * y_a + gbt_ref[...].reshape(rows, d)
    o = _dot(merged, wo_ref[...])
    g1 = _per_row(mod_ref[...], 2 * d, 3 * d, tt)
    o_ref[...] = (x_ref[...].reshape(rows, d) + g1 * o).reshape(nb, tt, d)


def _mixer_out(y2, u_lat, sga, gbt, x, mod3, ssm_d, w_glu_b, w_ssm_out_b, w_o_b):
    b, l, d = x.shape
    sw = ssm_d.shape[1]
    tt = TIME_TILE
    kern = functools.partial(_mixer_kernel, d=d, sw=sw)
    tok = pl.BlockSpec((b, tt, d), lambda j: (0, j, 0))
    return pl.pallas_call(
        kern,
        grid=(l // tt,),
        in_specs=[pl.BlockSpec((2, tt * b, sw), lambda j: (0, j, 0)),
                  pl.BlockSpec((tt * b, sw), lambda j: (j, 0)),
                  tok, tok, tok,
                  pl.BlockSpec((b, 1, mod3.shape[2]), lambda j: (0, 0, 0)),
                  _full(ssm_d), _full(w_glu_b), _full(w_ssm_out_b), _full(w_o_b)],
        out_specs=tok,
        out_shape=jax.ShapeDtypeStruct((b, l, d), F32),
        compiler_params=_params("arbitrary"),
        name="mixer_out",
    )(y2, u_lat, sga, gbt, x, mod3, ssm_d, w_glu_b, w_ssm_out_b, w_o_b)


def _split_bf16(a):
    hi = a.astype(BF16)
    lo = (a - hi.astype(F32)).astype(BF16)
    return hi, lo


def _route_kernel(xl_ref, mod_ref, g_ref, wrt_ref, rb_ref, wsg_ref, wsu_ref, wsd_ref,
                  h2_ref, base_ref, eidx_ref, wts_ref, rank_ref, cnt_ref, carry_ref, *, d):
    i = pl.program_id(0)
    t = xl_ref.shape[0]
    m = mod_ref[0]
    xl = xl_ref[...]
    h2 = _rms(xl, g_ref[...]) * (1.0 + m[:, 4 * d:5 * d]) + m[:, 3 * d:4 * d]
    _store_packed(h2_ref, h2)

    hb = h2.astype(BF16)
    sg = jnp.dot(hb, wsg_ref[...], preferred_element_type=F32)
    su = jnp.dot(hb, wsu_ref[...], preferred_element_type=F32)
    shared = _dot(sg * jax.nn.sigmoid(sg) * su, wsd_ref[...])
    base_ref[...] = xl + m[:, 5 * d:6 * d] * shared

    h_hi, h_lo = _split_bf16(h2)
    w_hi, w_lo = _split_bf16(wrt_ref[...])
    nt = (((1,), (1,)), ((), ()))
    logits = (lax.dot_general(w_hi, h_hi, nt, preferred_element_type=F32)
              + lax.dot_general(w_hi, h_lo, nt, preferred_element_type=F32)
              + lax.dot_general(w_lo, h_hi, nt, preferred_element_type=F32))
    scores = jax.nn.sigmoid(logits)
    choice = scores + rb_ref[...]

    epg = EXPERTS_PER_GROUP
    gi = lax.broadcasted_iota(I32, (epg, t), 0)
    gs = []
    for g in range(N_EXPERT_GROUPS):
        seg = choice[g * epg:(g + 1) * epg, :]
        m1 = jnp.max(seg, axis=0, keepdims=True)
        i1 = jnp.min(jnp.where(seg == m1, gi, epg), axis=0, keepdims=True)
        m2 = jnp.max(jnp.where(gi == i1, -jnp.inf, seg), axis=0, keepdims=True)
        gs.append(m1 + m2)
    masked = []
    for g in range(N_EXPERT_GROUPS):
        beat = jnp.zeros((1, t), I32)
        for g2 in range(N_EXPERT_GROUPS):
            if g2 < g:
                beat = beat + (gs[g2] >= gs[g]).astype(I32)
            elif g2 > g:
                beat = beat + (gs[g2] > gs[g]).astype(I32)
        keep = beat < TOP_K_GROUPS
        masked.append(jnp.where(keep, choice[g * epg:(g + 1) * epg, :], -jnp.inf))
    cur = jnp.concatenate(masked, axis=0)

    ei_all = lax.broadcasted_iota(I32, (N_EXPERTS, t), 0)
    picks, raw = [], []
    onehot = jnp.zeros((N_EXPERTS, t), F32)
    for _ in range(TOP_K):
        mx = jnp.max(cur, axis=0, keepdims=True)
        ei = jnp.min(jnp.where(cur == mx, ei_all, N_EXPERTS), axis=0, keepdims=True)
        hit = ei_all == ei
        raw.append(jnp.sum(jnp.where(hit, scores, 0.0), axis=0, keepdims=True))
        cur = jnp.where(hit, -jnp.inf, cur)
        onehot = jnp.where(hit, 1.0, onehot)
        picks.append(ei)
    tot = raw[0]
    for k in range(1, TOP_K):
        tot = tot + raw[k]

    @pl.when(i == 0)
    def _():
        carry_ref[...] = jnp.zeros_like(carry_ref)

    upper = (lax.broadcasted_iota(I32, (t, t), 0) < lax.broadcasted_iota(I32, (t, t), 1)).astype(BF16)
    before = jnp.dot(onehot.astype(BF16), upper, preferred_element_type=F32) + carry_ref[:, 0:1]
    for k in range(TOP_K):
        eidx_ref[k:k + 1, :] = picks[k]
        wts_ref[k:k + 1, :] = raw[k] / tot * ROUTE_SCALE
        rk = jnp.sum(jnp.where(ei_all == picks[k], before, 0.0), axis=0, keepdims=True)
        rank_ref[k:k + 1, :] = rk.astype(I32)
    carry_ref[...] = carry_ref[...] + jnp.sum(onehot, axis=1, keepdims=True)
    cnt_ref[...] = carry_ref[...]


def _route(xl2, mod3, tiles_per_batch, g2n, w_router_t, router_bias, ws_gate_b, ws_up_b, ws_down_b):
    n, d = xl2.shape
    t = TOKEN_TILE
    e = w_router_t.shape[0]
    kern = functools.partial(_route_kernel, d=d)
    tok = pl.BlockSpec((t, d), lambda i: (i, 0))
    small = pl.BlockSpec((TOP_K, t), lambda i: (0, i))
    return pl.pallas_call(
        kern,
        grid=(n // t,),
        in_specs=[tok,
                  pl.BlockSpec((1, 1, mod3.shape[2]), lambda i: (i // tiles_per_batch, 0, 0)),
                  _full(g2n), _full(w_router_t), _full(router_bias),
                  _full(ws_gate_b), _full(ws_up_b), _full(ws_down_b)],
        out_specs=[pl.BlockSpec((t * PACK_SUBLANES, LANES), lambda i: (i, 0)), tok, small, small, small,
                   pl.BlockSpec((e, LANES), lambda i: (0, 0))],
        out_shape=[jax.ShapeDtypeStruct((n * PACK_SUBLANES, LANES), U32),
                   jax.ShapeDtypeStruct((n, d), F32),
                   jax.ShapeDtypeStruct((TOP_K, n), I32),
                   jax.ShapeDtypeStruct((TOP_K, n), F32),
                   jax.ShapeDtypeStruct((TOP_K, n), I32),
                   jax.ShapeDtypeStruct((e, LANES), F32)],
        scratch_shapes=[pltpu.VMEM((e, LANES), F32)],
        compiler_params=_params("arbitrary"),
        name="route",
    )(xl2, mod3, g2n, w_router_t, router_bias, ws_gate_b, ws_up_b, ws_down_b)


def _dest_kernel(eidx_ref, rank_ref, start_ref, o_ref):
    t = eidx_ref.shape[1]
    ei_all = lax.broadcasted_iota(I32, (N_EXPERTS, t), 0)
    st = start_ref[:, 0:1]
    for k in range(TOP_K):
        hit = ei_all == eidx_ref[k:k + 1, :]
        o_ref[k:k + 1, :] = jnp.sum(jnp.where(hit, st, 0), axis=0, keepdims=True) + rank_ref[k:k + 1, :]


def _dest_rows(eidx, rank, starts_b):
    n = eidx.shape[1]
    t = 1024
    small = pl.BlockSpec((TOP_K, t), lambda i: (0, i))
    return pl.pallas_call(
        _dest_kernel,
        grid=(n // t,),
        in_specs=[small, small, _full(starts_b)],
        out_specs=small,
        out_shape=jax.ShapeDtypeStruct((TOP_K, n), I32),
        compiler_params=_params("arbitrary"),
        name="dest",
    )(eidx, rank, starts_b)


def _sc_workers():
    info = plsc.get_sparse_core_info()
    return info.num_cores, info.num_cores * info.num_subcores


def _sc_dispatch(dest_flat, src3, out_rows):
    n = src3.shape[0]
    n_cores, n_workers = _sc_workers()
    per_worker = n // n_workers
    assert per_worker % SC_CHUNK == 0
    mesh = plsc.VectorSubcoreMesh(core_axis_name="c", subcore_axis_name="s")

    @functools.partial(
        pl.kernel, mesh=mesh,
        out_type=jax.ShapeDtypeStruct((out_rows,) + src3.shape[1:], src3.dtype),
        scratch_types=[pltpu.VMEM((SC_CHUNK,) + src3.shape[1:], src3.dtype)]
        + [pltpu.VMEM((SC_CHUNK,), I32)] * TOP_K,
        name="dispatch",
    )
    def run(dest_hbm, src_hbm, out_hbm, rows_v, *idx_v):
        worker = lax.axis_index("s") * n_cores + lax.axis_index("c")

        @pl.loop(0, per_worker // SC_CHUNK)
        def _(j):
            base = worker * per_worker + j * SC_CHUNK
            pltpu.sync_copy(src_hbm.at[pl.ds(base, SC_CHUNK)], rows_v)
            for k in range(TOP_K):
                pltpu.sync_copy(dest_hbm.at[pl.ds(k * n + base, SC_CHUNK)], idx_v[k])
            for k in range(TOP_K):
                pltpu.sync_copy(rows_v, out_hbm.at[idx_v[k]])

    return run(dest_flat, src3)


def _sc_combine(dest_flat, ys3, n):
    n_cores, n_workers = _sc_workers()
    per_worker = n // n_workers
    assert per_worker % SC_CHUNK == 0
    mesh = plsc.VectorSubcoreMesh(core_axis_name="c", subcore_axis_name="s")

    @functools.partial(
        pl.kernel, mesh=mesh,
        out_type=jax.ShapeDtypeStruct((TOP_K * n,) + ys3.shape[1:], ys3.dtype),
        scratch_types=[pltpu.VMEM((SC_CHUNK,) + ys3.shape[1:], ys3.dtype), pltpu.VMEM((SC_CHUNK,), I32)],
        name="combine",
    )
    def run(dest_hbm, ys_hbm, out_hbm, rows_v, idx_v):
        worker = lax.axis_index("s") * n_cores + lax.axis_index("c")

        @pl.loop(0, per_worker // SC_CHUNK)
        def _(j):
            base = worker * per_worker + j * SC_CHUNK
            for k in range(TOP_K):
                pltpu.sync_copy(dest_hbm.at[pl.ds(k * n + base, SC_CHUNK)], idx_v)
                pltpu.sync_copy(ys_hbm.at[idx_v], rows_v)
                pltpu.sync_copy(rows_v, out_hbm.at[pl.ds(k * n + base, SC_CHUNK)])

    return run(dest_flat, ys3)


def _chunk_metadata(counts, n_rows):
    ch = EXPERT_CHUNK
    n_ch = (counts + ch - 1) // ch
    cum = jnp.cumsum(n_ch)
    total = cum[-1]
    max_chunks = n_rows // ch + N_EXPERTS
    i = jnp.arange(max_chunks, dtype=I32)
    e = jnp.sum((cum[None, :] <= i[:, None]).astype(I32), axis=1)
    e_last = jnp.max(jnp.where(counts > 0, jnp.arange(N_EXPERTS, dtype=I32), 0))
    exp = jnp.where(i < total, jnp.minimum(e, N_EXPERTS - 1), e_last).astype(I32)
    newexp = jnp.concatenate([jnp.ones((1,), I32), (exp[1:] != exp[:-1]).astype(I32)])
    change_at = jnp.where(newexp == 1, i, max_chunks)
    nxt_change = lax.cummin(jnp.concatenate([change_at[1:], jnp.array([max_chunks], I32)]), reverse=True)
    nexp = jnp.where(nxt_change < max_chunks, exp[jnp.minimum(nxt_change, max_chunks - 1)], -1).astype(I32)
    starts = (cum - n_ch) * ch
    return starts.astype(I32), (exp, newexp, nexp, total.astype(I32).reshape(1))


def _expert_kernel(exp_ref, newexp_ref, nexp_ref, total_ref, xs_hbm, wg_hbm, wu_hbm, wd_hbm, ys_hbm,
                   xbuf, ybuf, wgf, wuf, wdf, wgb, wub, wdb, sem_x, sem_y, sem_w):
    ch = EXPERT_CHUNK
    nbuf = EXPERT_RING
    rows = ch * PACK_SUBLANES
    total = total_ref[0]

    def x_copy(i, slot):
        r0 = pl.multiple_of(i * rows, rows)
        return pltpu.make_async_copy(xs_hbm.at[pl.ds(r0, rows)], xbuf.at[slot], sem_x.at[slot])

    def y_copy(i, slot):
        r0 = pl.multiple_of(i * rows, rows)
        return pltpu.make_async_copy(ybuf.at[slot], ys_hbm.at[pl.ds(r0, rows)], sem_y.at[slot])

    def weight_copies(e):
        return (pltpu.make_async_copy(wg_hbm.at[e], wgf, sem_w.at[0]),
                pltpu.make_async_copy(wu_hbm.at[e], wuf, sem_w.at[1]),
                pltpu.make_async_copy(wd_hbm.at[e], wdf, sem_w.at[2]))

    for cp in weight_copies(exp_ref[0]):
        cp.start()
    for b in range(nbuf - 1):
        @pl.when(b < total)
        def _(b=b):
            x_copy(b, b).start()

    def chunk(i, carry):
        slot = lax.rem(i, nbuf)
        x_copy(i, slot).wait()
        ahead = i + (nbuf - 1)

        @pl.when(ahead < total)
        def _():
            x_copy(ahead, lax.rem(ahead, nbuf)).start()

        @pl.when(newexp_ref[i] == 1)
        def _():
            for cp in weight_copies(exp_ref[i]):
                cp.wait()
            wgb[...] = wgf[...].astype(BF16)
            wub[...] = wuf[...].astype(BF16)
            wdb[...] = wdf[...].astype(BF16)

            @pl.when(nexp_ref[i] >= 0)
            def _():
                for cp in weight_copies(nexp_ref[i]):
                    cp.start()

        @pl.when(i >= nbuf)
        def _():
            y_copy(i, slot).wait()

        xr = xbuf.at[slot]
        yr = ybuf.at[slot]
        x = _load_packed(lambda sl: xr[sl, :], ch).astype(BF16)
        g = jnp.dot(x, wgb[...], preferred_element_type=F32)
        u = jnp.dot(x, wub[...], preferred_element_type=F32)
        y = jnp.dot((g * jax.nn.sigmoid(g) * u).astype(BF16), wdb[...], preferred_element_type=F32)
        _store_packed(yr, y)
        y_copy(i, slot).start()
        return carry

    lax.fori_loop(0, total, chunk, 0)

    for b in range(nbuf):
        @pl.when(b < total)
        def _(b=b):
            y_copy(0, b).wait()


def _experts(meta, xs, w_gate, w_up, w_down):
    e, d, f = w_gate.shape
    hbm = pl.BlockSpec(memory_space=pl.ANY)
    rows = EXPERT_CHUNK * PACK_SUBLANES
    grid_spec = pltpu.PrefetchScalarGridSpec(
        num_scalar_prefetch=len(meta), grid=(1,),
        in_specs=[hbm, hbm, hbm, hbm], out_specs=hbm,
        scratch_shapes=[pltpu.VMEM((EXPERT_RING, rows, LANES), U32), pltpu.VMEM((EXPERT_RING, rows, LANES), U32),
                        pltpu.VMEM((d, f), F32), pltpu.VMEM((d, f), F32), pltpu.VMEM((f, d), F32),
                        pltpu.VMEM((d, f), BF16), pltpu.VMEM((d, f), BF16), pltpu.VMEM((f, d), BF16),
                        pltpu.SemaphoreType.DMA((EXPERT_RING,)), pltpu.SemaphoreType.DMA((EXPERT_RING,)),
                        pltpu.SemaphoreType.DMA((3,))])
    return pl.pallas_call(
        _expert_kernel,
        grid_spec=grid_spec,
        out_shape=jax.ShapeDtypeStruct(xs.shape, xs.dtype),
        compiler_params=_params("arbitrary"),
        name="experts",
    )(*meta, xs, w_gate, w_up, w_down)


def _final_kernel(base_ref, yt_ref, wt_ref, mod_ref, g_ref, o_ref, *, d):
    t = base_ref.shape[0]
    w = wt_ref[...]
    routed = w[:, 0:1] * _load_packed(lambda sl: yt_ref[0, sl, :], t)
    for k in range(1, TOP_K):
        routed = routed + w[:, k:k + 1] * _load_packed(lambda sl, k=k: yt_ref[k, sl, :], t)
    g2 = mod_ref[0][:, 5 * d:6 * d]
    o_ref[...] = _rms(base_ref[...] + g2 * routed, g_ref[...])


def _final(base, ytok3, wts_t, mod3, tiles_per_batch, final_g):
    n, d = base.shape
    t = TOKEN_TILE
    kern = functools.partial(_final_kernel, d=d)
    tok = pl.BlockSpec((t, d), lambda i: (i, 0))
    return pl.pallas_call(
        kern,
        grid=(n // t,),
        in_specs=[tok,
                  pl.BlockSpec((TOP_K, t * PACK_SUBLANES, LANES), lambda i: (0, i, 0)),
                  pl.BlockSpec((t, TOP_K), lambda i: (i, 0)),
                  pl.BlockSpec((1, 1, mod3.shape[2]), lambda i: (i // tiles_per_batch, 0, 0)),
                  _full(final_g)],
        out_specs=tok,
        out_shape=jax.ShapeDtypeStruct((n, d), F32),
        compiler_params=_params("arbitrary"),
        name="final",
    )(base, ytok3, wts_t, mod3, final_g)


def kernel(x, c, ctx, c_ctx, w_mod, b_mod, norm1_g, norm2_g, w_in, ssm_lam_re, ssm_lam_im, ssm_log_dt, ssm_b_re, ssm_b_im, ssm_c_re, ssm_c_im, ssm_d, w_glu, w_ssm_out, conv_w, w_conv_out, w_o, w_router, router_bias, w_gate, w_up, w_down, ws_gate, ws_up, ws_down, final_g):
    b, l, d = x.shape
    n = b * l
    sw = ssm_d.shape[1]
    assert w_mod.shape[0] == 1, "single layer"
    assert b == SUBLANES and l % TOKEN_TILE == 0 and l % SCAN_CHUNK == 0 and TIME_TILE % GRID_W == 0

    mod_rows = 2 * SUBLANES
    c_all = jnp.concatenate([c, c_ctx[None, :], jnp.zeros((mod_rows - b - 1, d), F32)], axis=0)
    mod = _modulation(c_all, w_mod[0].astype(BF16), b_mod[0])
    mod3 = mod.reshape(mod_rows, 1, mod.shape[1])

    w_in_b = w_in[0].astype(BF16)
    g1n = norm1_g[0].reshape(1, d)
    u_lat, sga, gbt = _in_proj(x, mod3, g1n, w_in_b, conv_w[0], w_conv_out[0].astype(BF16))
    u_ctx = _ctx_proj(ctx, mod3, b, g1n, w_in_b[:, :sw])

    ops = _ssm_operators(ssm_lam_re[0], ssm_lam_im[0], ssm_log_dt[0], ssm_b_re[0], ssm_b_im[0],
                         ssm_c_re[0], ssm_c_im[0])
    y2 = _s5_scan(u_ctx, u_lat, *ops, nb=b)

    xl = _mixer_out(y2, u_lat, sga, gbt, x, mod3, ssm_d[0].reshape(1, sw), w_glu[0].astype(BF16),
                    w_ssm_out[0].astype(BF16), w_o[0].astype(BF16))

    tiles_per_batch = l // TOKEN_TILE
    h2, base, eidx, wts, rank, cnt = _route(
        xl.reshape(n, d), mod3, tiles_per_batch, norm2_g[0].reshape(1, d), w_router[0].T,
        router_bias[0].reshape(N_EXPERTS, 1), ws_gate[0].astype(BF16), ws_up[0].astype(BF16),
        ws_down[0].astype(BF16))

    counts = cnt[:, 0].astype(I32)
    n_rows = n * TOP_K
    starts, meta = _chunk_metadata(counts, n_rows)
    buf_rows = n_rows + N_EXPERTS * EXPERT_CHUNK
    dest = _dest_rows(eidx, rank, jnp.broadcast_to(starts[:, None], (N_EXPERTS, LANES)))

    dest_flat = dest.reshape(TOP_K * n)
    xs = _sc_dispatch(dest_flat, h2.reshape(n, PACK_SUBLANES, LANES), buf_rows)
    ys = _experts(meta, xs.reshape(buf_rows * PACK_SUBLANES, LANES), w_gate[0], w_up[0], w_down[0])
    ytok = _sc_combine(dest_flat, ys.reshape(buf_rows, PACK_SUBLANES, LANES), n)
    out = _final(base, ytok.reshape(TOP_K, n * PACK_SUBLANES, LANES), wts.T, mod3, tiles_per_batch,
                 final_g.reshape(1, d))
    return out.reshape(b, l, d)
```

```python
import functools

import jax
import jax.numpy as jnp
from jax import lax
from jax.experimental import pallas as pl
from jax.experimental.pallas import tpu as pltpu
from jax.experimental.pallas import tpu_sc as plsc

F32 = jnp.float32
BF16 = jnp.bfloat16
I32 = jnp.int32
U32 = jnp.uint32

EPS = 1e-6
GRID_W = 64
N_EXPERTS = 256
TOP_K = 8
N_EXPERT_GROUPS = 8
EXPERTS_PER_GROUP = N_EXPERTS // N_EXPERT_GROUPS
TOP_K_GROUPS = 4
ROUTE_SCALE = 2.5

SUBLANES = 8
LANES = 128
VMEM_LIMIT_BYTES = 48 * 1024 * 1024

TIME_TILE = 64
TOKEN_TILE = 256
SCAN_CHUNK = 128
SCAN_SLAB = 512
EXPERT_CHUNK = 256
EXPERT_RING = 4
SC_CHUNK = 128
COMBINE_PARTS = 2
PACK_SUBLANES = 4


def _dot(a, b):
    return jnp.dot(a.astype(BF16), b.astype(BF16), preferred_element_type=F32)


def _rms(xf, g):
    return xf * lax.rsqrt(jnp.mean(xf * xf, axis=-1, keepdims=True) + EPS) * g


def _params(*sem):
    return pltpu.CompilerParams(dimension_semantics=sem, vmem_limit_bytes=VMEM_LIMIT_BYTES)


def _full(a):
    return pl.BlockSpec(a.shape, lambda *_: (0,) * a.ndim)


def _pack_rows(v):
    half = v.shape[1] // 2
    lo = lax.bitcast_convert_type(v[:, :half].astype(BF16).astype(F32), U32) >> 16
    hi = lax.bitcast_convert_type(v[:, half:].astype(BF16).astype(F32), U32) & jnp.uint32(0xFFFF0000)
    return lo | hi


def _unpack_lo(w):
    return lax.bitcast_convert_type(w << 16, F32)


def _unpack_hi(w):
    return lax.bitcast_convert_type(w & jnp.uint32(0xFFFF0000), F32)


def _store_packed(ref, v):
    t = v.shape[0]
    w = _pack_rows(v)
    for c in range(PACK_SUBLANES):
        ref[pl.ds(c, t, stride=PACK_SUBLANES), :] = w[:, c * LANES:(c + 1) * LANES]


def _load_packed(load, t):
    ws = [load(pl.ds(c, t, stride=PACK_SUBLANES)) for c in range(PACK_SUBLANES)]
    return jnp.concatenate([_unpack_lo(w) for w in ws] + [_unpack_hi(w) for w in ws], axis=1)


def _mod_kernel(c_ref, w_ref, b_ref, o_ref):
    c = c_ref[...]
    o_ref[...] = _dot(c * jax.nn.sigmoid(c), w_ref[...]) + b_ref[...]


def _modulation(c_all, w_mod, b_mod):
    rows, d = c_all.shape
    cols = w_mod.shape[1]
    blk = 1536
    return pl.pallas_call(
        _mod_kernel,
        grid=(cols // blk,),
        in_specs=[pl.BlockSpec((rows, d), lambda j: (0, 0)),
                  pl.BlockSpec((d, blk), lambda j: (0, j)),
                  pl.BlockSpec((1, blk), lambda j: (0, j))],
        out_specs=pl.BlockSpec((rows, blk), lambda j: (0, j)),
        out_shape=jax.ShapeDtypeStruct((rows, cols), F32),
        compiler_params=_params("arbitrary"),
        name="mod",
    )(c_all, w_mod, b_mod.reshape(1, cols))


def _per_row(m3, lo, hi, tt):
    nb = m3.shape[0]
    return jnp.broadcast_to(m3[:, :, lo:hi], (nb, tt, hi - lo)).reshape(nb * tt, hi - lo)


def _to_time_major(val, nb, tt):
    c = val.shape[1]
    return pltpu.einshape("btc->tbc", val.reshape(nb, tt, c)).reshape(nb * tt, c)


def _to_batch_major(val, nb, tt):
    c = val.shape[1]
    return pltpu.einshape("tbc->btc", val.reshape(tt, nb, c)).reshape(nb * tt, c)


def _in_proj_kernel(x_ref, mod_ref, g_ref, w_ref, cw_ref, wco_ref, u_ref, sga_ref, gbt_ref, *, d, sw):
    nb, tt, _ = x_ref.shape
    rows = nb * tt
    m3 = mod_ref[...]
    x = x_ref[...].reshape(rows, d)
    h = _rms(x, g_ref[...]) * (1.0 + _per_row(m3, d, 2 * d, tt)) + _per_row(m3, 0, d, tt)
    hb = h.astype(BF16)
    u_ref[...] = _to_time_major(jnp.dot(hb, w_ref[:, 0:sw], preferred_element_type=F32), nb, tt)
    cb = jnp.dot(hb, w_ref[:, sw:2 * sw], preferred_element_type=F32)
    cc = jnp.dot(hb, w_ref[:, 2 * sw:3 * sw], preferred_element_type=F32)
    cv = jnp.dot(hb, w_ref[:, 3 * sw:4 * sw], preferred_element_type=F32)
    ccv = cc * cv
    col = lax.broadcasted_iota(I32, ccv.shape, 0) % GRID_W
    prev = jnp.where(col == 0, 0.0, pltpu.roll(ccv, 1, axis=0))
    nxt = jnp.where(col == GRID_W - 1, 0.0, pltpu.roll(ccv, rows - 1, axis=0))
    cw = cw_ref[...]
    conv = prev * cw[0:1, :] + ccv * cw[1:2, :] + nxt * cw[2:3, :]
    y_conv = _dot(cb * conv, wco_ref[...])
    ga = jnp.dot(hb, w_ref[:, 4 * sw:4 * sw + d], preferred_element_type=F32)
    gb = jnp.dot(hb, w_ref[:, 4 * sw + d:4 * sw + 2 * d], preferred_element_type=F32)
    sga_ref[...] = jax.nn.sigmoid(ga).reshape(nb, tt, d)
    gbt_ref[...] = (jax.nn.sigmoid(gb) * y_conv).reshape(nb, tt, d)


def _in_proj(x, mod3, g1n, w_in_b, conv_w, w_conv_out_b):
    b, l, d = x.shape
    sw = conv_w.shape[1]
    tt = TIME_TILE
    kern = functools.partial(_in_proj_kernel, d=d, sw=sw)
    tok = pl.BlockSpec((b, tt, d), lambda j: (0, j, 0))
    return pl.pallas_call(
        kern,
        grid=(l // tt,),
        in_specs=[tok,
                  pl.BlockSpec((b, 1, mod3.shape[2]), lambda j: (0, 0, 0)),
                  _full(g1n), _full(w_in_b), _full(conv_w), _full(w_conv_out_b)],
        out_specs=[pl.BlockSpec((tt * b, sw), lambda j: (j, 0)), tok, tok],
        out_shape=[jax.ShapeDtypeStruct((l * b, sw), F32),
                   jax.ShapeDtypeStruct((b, l, d), F32),
                   jax.ShapeDtypeStruct((b, l, d), F32)],
        compiler_params=_params("arbitrary"),
        name="in_proj",
    )(x, mod3, g1n, w_in_b, conv_w, w_conv_out_b)


def _ctx_proj_kernel(x_ref, mod_ref, g_ref, w_ref, u_ref, *, d):
    nb, tt, _ = x_ref.shape
    m = mod_ref[0]
    x = x_ref[...].reshape(nb * tt, d)
    h = _rms(x, g_ref[...]) * (1.0 + m[:, d:2 * d]) + m[:, 0:d]
    u_ref[...] = _to_time_major(_dot(h, w_ref[...]), nb, tt)


def _ctx_proj(ctx, mod3, ctx_row, g1n, w_u_b):
    b, lc, d = ctx.shape
    sw = w_u_b.shape[1]
    tt = TIME_TILE
    kern = functools.partial(_ctx_proj_kernel, d=d)
    return pl.pallas_call(
        kern,
        grid=(lc // tt,),
        in_specs=[pl.BlockSpec((b, tt, d), lambda j: (0, j, 0)),
                  pl.BlockSpec((1, 1, mod3.shape[2]), lambda j: (ctx_row, 0, 0)),
                  _full(g1n), _full(w_u_b)],
        out_specs=pl.BlockSpec((tt * b, sw), lambda j: (j, 0)),
        out_shape=jax.ShapeDtypeStruct((lc * b, sw), F32),
        compiler_params=_params("arbitrary"),
        name="ctx_proj",
    )(ctx, mod3, g1n, w_u_b)


def _scan_kernel(uc_ref, ul_ref, are_ref, aim_ref, bre_ref, bim_ref, cre_ref, cim_ref, y_ref,
                 sre_ref, sim_ref, cre_s, cim_s, *, n_ctx, nb):
    dirn = pl.program_id(0)
    j = pl.program_id(1)
    half_c = bre_ref.shape[2]
    half_s = bre_ref.shape[3]
    n_half = bre_ref.shape[1]
    steps = SCAN_CHUNK

    @pl.when(j == 0)
    def _():
        cre_s[...] = jnp.zeros_like(cre_s)
        cim_s[...] = jnp.zeros_like(cim_s)

    u = jnp.where(j < n_ctx, uc_ref[...], ul_ref[...]).astype(BF16)
    for h in range(n_half):
        uh = u[:, h * half_c:(h + 1) * half_c]
        sre_ref[:, h * half_s:(h + 1) * half_s] = jnp.dot(uh, bre_ref[0, h], preferred_element_type=F32)
        sim_ref[:, h * half_s:(h + 1) * half_s] = jnp.dot(uh, bim_ref[0, h], preferred_element_type=F32)

    lanes = sre_ref.shape[1]
    for q in range(lanes // SCAN_SLAB):
        ls = slice(q * SCAN_SLAB, (q + 1) * SCAN_SLAB)
        a_re = jnp.broadcast_to(are_ref[0, :, ls], (nb, SCAN_SLAB))
        a_im = jnp.broadcast_to(aim_ref[0, :, ls], (nb, SCAN_SLAB))

        def body(i, carry, ls=ls, a_re=a_re, a_im=a_im):
            s_re, s_im = carry
            for k in range(SUBLANES):
                t = i * SUBLANES + k
                t = jnp.where(dirn == 0, t, steps - 1 - t)
                r0 = pl.multiple_of(t * nb, nb)
                b_re = sre_ref[pl.ds(r0, nb), ls]
                b_im = sim_ref[pl.ds(r0, nb), ls]
                n_re = a_re * s_re - a_im * s_im + b_re
                n_im = a_re * s_im + a_im * s_re + b_im
                sre_ref[pl.ds(r0, nb), ls] = n_re
                sim_ref[pl.ds(r0, nb), ls] = n_im
                s_re, s_im = n_re, n_im
            return s_re, s_im

        s_re, s_im = lax.fori_loop(0, steps // SUBLANES, body, (cre_s[:, ls], cim_s[:, ls]))
        cre_s[:, ls] = s_re
        cim_s[:, ls] = s_im

    @pl.when(j >= n_ctx)
    def _():
        half_o = cre_ref.shape[3]
        for h in range(n_half):
            s_r = sre_ref[:, h * half_s:(h + 1) * half_s].astype(BF16)
            s_i = sim_ref[:, h * half_s:(h + 1) * half_s].astype(BF16)
            y_ref[0, :, h * half_o:(h + 1) * half_o] = (
                jnp.dot(s_r, cre_ref[0, h], preferred_element_type=F32)
                + jnp.dot(s_i, cim_ref[0, h], preferred_element_type=F32))


def _s5_scan(u_ctx, u_lat, a_re, a_im, b_re, b_im, c_re, c_imn, nb):
    rows_c, sw = u_ctx.shape
    rows_l = u_lat.shape[0]
    rc = SCAN_CHUNK * nb
    n_ctx = rows_c // rc
    n_lat = rows_l // rc
    lanes = a_re.shape[2]

    def ctx_idx(d, j):
        jj = jnp.minimum(j, n_ctx - 1)
        return (jnp.where(d == 0, jj, n_ctx - 1 - jj), 0)

    def lat_idx(d, j):
        jj = jnp.maximum(j - n_ctx, 0)
        return (jnp.where(d == 0, jj, n_lat - 1 - jj), 0)

    def per_dir4(d, j):
        return (d, 0, 0, 0)

    kern = functools.partial(_scan_kernel, n_ctx=n_ctx, nb=nb)
    return pl.pallas_call(
        kern,
        grid=(2, n_ctx + n_lat),
        in_specs=[pl.BlockSpec((rc, sw), ctx_idx),
                  pl.BlockSpec((rc, sw), lat_idx),
                  pl.BlockSpec((1, 1, lanes), lambda d, j: (d, 0, 0)),
                  pl.BlockSpec((1, 1, lanes), lambda d, j: (d, 0, 0)),
                  pl.BlockSpec((1,) + b_re.shape[1:], per_dir4),
                  pl.BlockSpec((1,) + b_im.shape[1:], per_dir4),
                  pl.BlockSpec((1,) + c_re.shape[1:], per_dir4),
                  pl.BlockSpec((1,) + c_imn.shape[1:], per_dir4)],
        out_specs=pl.BlockSpec((1, rc, sw), lambda d, j: (d,) + lat_idx(d, j)),
        out_shape=jax.ShapeDtypeStruct((2, rows_l, sw), F32),
        scratch_shapes=[pltpu.VMEM((rc, lanes), F32), pltpu.VMEM((rc, lanes), F32),
                        pltpu.VMEM((nb, lanes), F32), pltpu.VMEM((nb, lanes), F32)],
        compiler_params=_params("arbitrary", "arbitrary"),
        name="s5_scan",
    )(u_ctx, u_lat, a_re, a_im, b_re, b_im, c_re, c_imn)


def _ssm_operators(lam_re, lam_im, log_dt, b_re, b_im, c_re, c_im):
    dt = jnp.exp(log_dt)[..., None]
    mag = jnp.exp(lam_re * dt)
    a_re = mag * jnp.cos(lam_im * dt)
    a_im = mag * jnp.sin(lam_im * dt)
    den = lam_re * lam_re + lam_im * lam_im
    k_re = ((a_re - 1.0) * lam_re + a_im * lam_im) / den
    k_im = (a_im * lam_re - (a_re - 1.0) * lam_im) / den
    bb_re = k_re[..., None] * b_re - k_im[..., None] * b_im
    bb_im = k_re[..., None] * b_im + k_im[..., None] * b_re
    nd, g, p, q = bb_re.shape
    halves = 2
    gh = g // halves
    eye = jnp.eye(gh, dtype=F32)

    def drive(bb):
        bbh = bb.reshape(nd, halves, gh, p, q)
        m = jnp.einsum('dhgpq,gk->dhgqkp', bbh, eye)
        return m.reshape(nd, halves, gh * q, gh * p).astype(BF16)

    def readout(cc):
        cch = cc.reshape(nd, halves, gh, q, p)
        m = jnp.einsum('dhgqp,gk->dhgpkq', cch, eye)
        return m.reshape(nd, halves, gh * p, gh * q).astype(BF16)

    return (a_re.reshape(nd, 1, g * p), a_im.reshape(nd, 1, g * p),
            drive(bb_re), drive(bb_im), readout(c_re), readout(-c_im))


def _mixer_kernel(y_ref, u_ref, sga_ref, gbt_ref, x_ref, mod_ref, d_ref, wglu_ref, wso_ref, wo_ref, o_ref,
                  *, d, sw):
    nb, tt, _ = x_ref.shape
    rows = nb * tt
    y = _to_batch_major(d_ref[...] * u_ref[...] + y_ref[0] + y_ref[1], nb, tt)
    v = _dot(jax.nn.gelu(y), wglu_ref[...])
    ys = v[:, 0:sw] * jax.nn.sigmoid(v[:, sw:2 * sw])
    y_a = _dot(ys, wso_ref[...])
    merged = sga_ref[...].reshape(rows, d) * y_a + gbt_ref[...].reshape(rows, d)
    o = _dot(merged, wo_ref[...])
    g1 = _per_row(mod_ref[...], 2 * d, 3 * d, tt)
    o_ref[...] = (x_ref[...].reshape(rows, d) + g1 * o).reshape(nb, tt, d)


def _mixer_out(y2, u_lat, sga, gbt, x, mod3, ssm_d, w_glu_b, w_ssm_out_b, w_o_b):
    b, l, d = x.shape
    sw = ssm_d.shape[1]
    tt = TIME_TILE
    kern = functools.partial(_mixer_kernel, d=d, sw=sw)
    tok = pl.BlockSpec((b, tt, d), lambda j: (0, j, 0))
    return pl.pallas_call(
        kern,
        grid=(l // tt,),
        in_specs=[pl.BlockSpec((2, tt * b, sw), lambda j: (0, j, 0)),
                  pl.BlockSpec((tt * b, sw), lambda j: (j, 0)),
                  tok, tok, tok,
                  pl.BlockSpec((b, 1, mod3.shape[2]), lambda j: (0, 0, 0)),
                  _full(ssm_d), _full(w_glu_b), _full(w_ssm_out_b), _full(w_o_b)],
        out_specs=tok,
        out_shape=jax.ShapeDtypeStruct((b, l, d), F32),
        compiler_params=_params("arbitrary"),
        name="mixer_out",
    )(y2, u_lat, sga, gbt, x, mod3, ssm_d, w_glu_b, w_ssm_out_b, w_o_b)


def _split_bf16(a):
    hi = a.astype(BF16)
    lo = (a - hi.astype(F32)).astype(BF16)
    return hi, lo


def _norm2(xl, m, g, d):
    return _rms(xl, g) * (1.0 + m[:, 4 * d:5 * d]) + m[:, 3 * d:4 * d]


def _shared_kernel(xl_ref, mod_ref, g_ref, wsg_ref, wsu_ref, wsd_ref, base_ref, *, d):
    m = mod_ref[0]
    xl = xl_ref[...]
    hb = _norm2(xl, m, g_ref[...], d).astype(BF16)
    sg = jnp.dot(hb, wsg_ref[...], preferred_element_type=F32)
    su = jnp.dot(hb, wsu_ref[...], preferred_element_type=F32)
    shared = _dot(sg * jax.nn.sigmoid(sg) * su, wsd_ref[...])
    base_ref[...] = xl + m[:, 5 * d:6 * d] * shared


def _shared(xl2, mod3, tiles_per_batch, g2n, ws_gate_b, ws_up_b, ws_down_b):
    n, d = xl2.shape
    t = TOKEN_TILE
    tok = pl.BlockSpec((t, d), lambda i: (i, 0))
    return pl.pallas_call(
        functools.partial(_shared_kernel, d=d),
        grid=(n // t,),
        in_specs=[tok,
                  pl.BlockSpec((1, 1, mod3.shape[2]), lambda i: (i // tiles_per_batch, 0, 0)),
                  _full(g2n), _full(ws_gate_b), _full(ws_up_b), _full(ws_down_b)],
        out_specs=tok,
        out_shape=jax.ShapeDtypeStruct((n, d), F32),
        compiler_params=_params("arbitrary"),
        name="shared",
    )(xl2, mod3, g2n, ws_gate_b, ws_up_b, ws_down_b)


def _route_kernel(xl_ref, mod_ref, g_ref, whi_ref, wlo_ref, rb_ref,
                  h2_ref, eidx_ref, wts_ref, rank_ref, cnt_ref, carry_ref, *, d):
    i = pl.program_id(0)
    t = xl_ref.shape[0]
    h2 = _norm2(xl_ref[...], mod_ref[0], g_ref[...], d)
    _store_packed(h2_ref, h2)

    h_hi, h_lo = _split_bf16(h2)
    w_hi, w_lo = whi_ref[...], wlo_ref[...]
    nt = (((1,), (1,)), ((), ()))
    logits = (lax.dot_general(w_hi, h_hi, nt, preferred_element_type=F32)
              + lax.dot_general(w_hi, h_lo, nt, preferred_element_type=F32)
              + lax.dot_general(w_lo, h_hi, nt, preferred_element_type=F32))
    scores = jax.nn.sigmoid(logits)
    choice = scores + rb_ref[...]

    epg = EXPERTS_PER_GROUP
    gi = lax.broadcasted_iota(I32, (epg, t), 0)
    gs = []
    for g in range(N_EXPERT_GROUPS):
        seg = choice[g * epg:(g + 1) * epg, :]
        m1 = jnp.max(seg, axis=0, keepdims=True)
        i1 = jnp.min(jnp.where(seg == m1, gi, epg), axis=0, keepdims=True)
        m2 = jnp.max(jnp.where(gi == i1, -jnp.inf, seg), axis=0, keepdims=True)
        gs.append(m1 + m2)
    masked = []
    for g in range(N_EXPERT_GROUPS):
        beat = jnp.zeros((1, t), I32)
        for g2 in range(N_EXPERT_GROUPS):
            if g2 < g:
                beat = beat + (gs[g2] >= gs[g]).astype(I32)
            elif g2 > g:
                beat = beat + (gs[g2] > gs[g]).astype(I32)
        keep = beat < TOP_K_GROUPS
        masked.append(jnp.where(keep, choice[g * epg:(g + 1) * epg, :], -jnp.inf))
    cur = jnp.concatenate(masked, axis=0)

    ei_all = lax.broadcasted_iota(I32, (N_EXPERTS, t), 0)
    picks, raw = [], []
    onehot = jnp.zeros((N_EXPERTS, t), F32)
    for _ in range(TOP_K):
        mx = jnp.max(cur, axis=0, keepdims=True)
        ei = jnp.min(jnp.where(cur == mx, ei_all, N_EXPERTS), axis=0, keepdims=True)
        hit = ei_all == ei
        raw.append(jnp.sum(jnp.where(hit, scores, 0.0), axis=0, keepdims=True))
        cur = jnp.where(hit, -jnp.inf, cur)
        onehot = jnp.where(hit, 1.0, onehot)
        picks.append(ei)
    tot = raw[0]
    for k in range(1, TOP_K):
        tot = tot + raw[k]

    @pl.when(i == 0)
    def _():
        carry_ref[...] = jnp.zeros_like(carry_ref)

    upper = (lax.broadcasted_iota(I32, (t, t), 0) < lax.broadcasted_iota(I32, (t, t), 1)).astype(BF16)
    before = jnp.dot(onehot.astype(BF16), upper, preferred_element_type=F32) + carry_ref[:, 0:1]
    for k in range(TOP_K):
        eidx_ref[k:k + 1, :] = picks[k]
        wts_ref[k:k + 1, :] = raw[k] / tot * ROUTE_SCALE
        rk = jnp.sum(jnp.where(ei_all == picks[k], before, 0.0), axis=0, keepdims=True)
        rank_ref[k:k + 1, :] = rk.astype(I32)
    carry_ref[...] = carry_ref[...] + jnp.sum(onehot, axis=1, keepdims=True)
    cnt_ref[...] = carry_ref[...]


def _route(xl2, mod3, tiles_per_batch, g2n, w_router_t, router_bias):
    n, d = xl2.shape
    t = TOKEN_TILE
    e = w_router_t.shape[0]
    w_hi, w_lo = _split_bf16(w_router_t)
    kern = functools.partial(_route_kernel, d=d)
    tok = pl.BlockSpec((t, d), lambda i: (i, 0))
    small = pl.BlockSpec((TOP_K, t), lambda i: (0, i))
    return pl.pallas_call(
        kern,
        grid=(n // t,),
        in_specs=[tok,
                  pl.BlockSpec((1, 1, mod3.shape[2]), lambda i: (i // tiles_per_batch, 0, 0)),
                  _full(g2n), _full(w_hi), _full(w_lo), _full(router_bias)],
        out_specs=[pl.BlockSpec((t * PACK_SUBLANES, LANES), lambda i: (i, 0)), small, small, small,
                   pl.BlockSpec((e, LANES), lambda i: (0, 0))],
        out_shape=[jax.ShapeDtypeStruct((n * PACK_SUBLANES, LANES), U32),
                   jax.ShapeDtypeStruct((TOP_K, n), I32),
                   jax.ShapeDtypeStruct((TOP_K, n), F32),
                   jax.ShapeDtypeStruct((TOP_K, n), I32),
                   jax.ShapeDtypeStruct((e, LANES), F32)],
        scratch_shapes=[pltpu.VMEM((e, LANES), F32)],
        compiler_params=_params("arbitrary"),
        name="route",
    )(xl2, mod3, g2n, w_hi, w_lo, router_bias)


def _dest_kernel(eidx_ref, rank_ref, start_ref, o_ref):
    t = eidx_ref.shape[1]
    ei_all = lax.broadcasted_iota(I32, (N_EXPERTS, t), 0)
    st = start_ref[:, 0:1]
    for k in range(TOP_K):
        hit = ei_all == eidx_ref[k:k + 1, :]
        o_ref[k:k + 1, :] = jnp.sum(jnp.where(hit, st, 0), axis=0, keepdims=True) + rank_ref[k:k + 1, :]


def _dest_rows(eidx, rank, starts_b):
    n = eidx.shape[1]
    t = 1024
    small = pl.BlockSpec((TOP_K, t), lambda i: (0, i))
    return pl.pallas_call(
        _dest_kernel,
        grid=(n // t,),
        in_specs=[small, small, _full(starts_b)],
        out_specs=small,
        out_shape=jax.ShapeDtypeStruct((TOP_K, n), I32),
        compiler_params=_params("arbitrary"),
        name="dest",
    )(eidx, rank, starts_b)


def _sc_workers():
    info = plsc.get_sparse_core_info()
    return info.num_cores, info.num_cores * info.num_subcores


def _sc_dispatch(dest_flat, src3, out_rows):
    n = src3.shape[0]
    n_cores, n_workers = _sc_workers()
    per_worker = n // n_workers
    assert per_worker % SC_CHUNK == 0
    mesh = plsc.VectorSubcoreMesh(core_axis_name="c", subcore_axis_name="s")

    @functools.partial(
        pl.kernel, mesh=mesh,
        out_type=jax.ShapeDtypeStruct((out_rows,) + src3.shape[1:], src3.dtype),
        scratch_types=[pltpu.VMEM((SC_CHUNK,) + src3.shape[1:], src3.dtype)]
        + [pltpu.VMEM((SC_CHUNK,), I32)] * TOP_K,
        name="dispatch",
    )
    def run(dest_hbm, src_hbm, out_hbm, rows_v, *idx_v):
        worker = lax.axis_index("s") * n_cores + lax.axis_index("c")

        @pl.loop(0, per_worker // SC_CHUNK)
        def _(j):
            base = worker * per_worker + j * SC_CHUNK
            pltpu.sync_copy(src_hbm.at[pl.ds(base, SC_CHUNK)], rows_v)
            for k in range(TOP_K):
                pltpu.sync_copy(dest_hbm.at[pl.ds(k * n + base, SC_CHUNK)], idx_v[k])
            for k in range(TOP_K):
                pltpu.sync_copy(rows_v, out_hbm.at[idx_v[k]])

    return run(dest_flat, src3)


def _sc_combine(dest_flat, ys3, n):
    n_cores, n_workers = _sc_workers()
    per_worker = n // n_workers
    assert per_worker % SC_CHUNK == 0
    mesh = plsc.VectorSubcoreMesh(core_axis_name="c", subcore_axis_name="s")

    @functools.partial(
        pl.kernel, mesh=mesh,
        out_type=jax.ShapeDtypeStruct((TOP_K * n,) + ys3.shape[1:], ys3.dtype),
        scratch_types=[pltpu.VMEM((SC_CHUNK,) + ys3.shape[1:], ys3.dtype), pltpu.VMEM((SC_CHUNK,), I32)],
        name="combine",
    )
    def run(dest_hbm, ys_hbm, out_hbm, rows_v, idx_v):
        worker = lax.axis_index("s") * n_cores + lax.axis_index("c")

        @pl.loop(0, per_worker // SC_CHUNK)
        def _(j):
            base = worker * per_worker + j * SC_CHUNK
            for k in range(TOP_K):
                pltpu.sync_copy(dest_hbm.at[pl.ds(k * n + base, SC_CHUNK)], idx_v)
                pltpu.sync_copy(ys_hbm.at[idx_v], rows_v)
                pltpu.sync_copy(rows_v, out_hbm.at[pl.ds(k * n + base, SC_CHUNK)])

    return run(dest_flat, ys3)


def _chunk_metadata(counts, n_rows):
    ch = EXPERT_CHUNK
    n_ch = (counts + ch - 1) // ch
    cum = jnp.cumsum(n_ch)
    total = cum[-1]
    max_chunks = n_rows // ch + N_EXPERTS
    i = jnp.arange(max_chunks, dtype=I32)
    e = jnp.sum((cum[None, :] <= i[:, None]).astype(I32), axis=1)
    e_last = jnp.max(jnp.where(counts > 0, jnp.arange(N_EXPERTS, dtype=I32), 0))
    exp = jnp.where(i < total, jnp.minimum(e, N_EXPERTS - 1), e_last).astype(I32)
    newexp = jnp.concatenate([jnp.ones((1,), I32), (exp[1:] != exp[:-1]).astype(I32)])
    change_at = jnp.where(newexp == 1, i, max_chunks)
    nxt_change = lax.cummin(jnp.concatenate([change_at[1:], jnp.array([max_chunks], I32)]), reverse=True)
    nexp = jnp.where(nxt_change < max_chunks, exp[jnp.minimum(nxt_change, max_chunks - 1)], -1).astype(I32)
    starts = (cum - n_ch) * ch
    return starts.astype(I32), (exp, newexp, nexp, total.astype(I32).reshape(1))


def _expert_kernel(exp_ref, newexp_ref, nexp_ref, total_ref, xs_hbm, wg_hbm, wu_hbm, wd_hbm, ys_hbm,
                   xbuf, ybuf, wgf, wuf, wdf, wgb, wub, wdb, sem_x, sem_y, sem_w):
    ch = EXPERT_CHUNK
    nbuf = EXPERT_RING
    rows = ch * PACK_SUBLANES
    total = total_ref[0]

    def x_copy(i, slot):
        r0 = pl.multiple_of(i * rows, rows)
        return pltpu.make_async_copy(xs_hbm.at[pl.ds(r0, rows)], xbuf.at[slot], sem_x.at[slot])

    def y_copy(i, slot):
        r0 = pl.multiple_of(i * rows, rows)
        return pltpu.make_async_copy(ybuf.at[slot], ys_hbm.at[pl.ds(r0, rows)], sem_y.at[slot])

    def weight_copies(e):
        return (pltpu.make_async_copy(wg_hbm.at[e], wgf, sem_w.at[0]),
                pltpu.make_async_copy(wu_hbm.at[e], wuf, sem_w.at[1]),
                pltpu.make_async_copy(wd_hbm.at[e], wdf, sem_w.at[2]))

    for cp in weight_copies(exp_ref[0]):
        cp.start()
    for b in range(nbuf - 1):
        @pl.when(b < total)
        def _(b=b):
            x_copy(b, b).start()

    def chunk(i, carry):
        slot = lax.rem(i, nbuf)
        x_copy(i, slot).wait()
        ahead = i + (nbuf - 1)

        @pl.when(ahead < total)
        def _():
            x_copy(ahead, lax.rem(ahead, nbuf)).start()

        @pl.when(newexp_ref[i] == 1)
        def _():
            for cp in weight_copies(exp_ref[i]):
                cp.wait()
            wgb[...] = wgf[...].astype(BF16)
            wub[...] = wuf[...].astype(BF16)
            wdb[...] = wdf[...].astype(BF16)

            @pl.when(nexp_ref[i] >= 0)
            def _():
                for cp in weight_copies(nexp_ref[i]):
                    cp.start()

        @pl.when(i >= nbuf)
        def _():
            y_copy(i, slot).wait()

        xr = xbuf.at[slot]
        yr = ybuf.at[slot]
        x = _load_packed(lambda sl: xr[sl, :], ch).astype(BF16)
        g = jnp.dot(x, wgb[...], preferred_element_type=F32)
        u = jnp.dot(x, wub[...], preferred_element_type=F32)
        y = jnp.dot((g * jax.nn.sigmoid(g) * u).astype(BF16), wdb[...], preferred_element_type=F32)
        _store_packed(yr, y)
        y_copy(i, slot).start()
        return carry

    lax.fori_loop(0, total, chunk, 0)

    for b in range(nbuf):
        @pl.when(b < total)
        def _(b=b):
            y_copy(0, b).wait()


def _experts(meta, xs, w_gate, w_up, w_down):
    e, d, f = w_gate.shape
    hbm = pl.BlockSpec(memory_space=pl.ANY)
    rows = EXPERT_CHUNK * PACK_SUBLANES
    grid_spec = pltpu.PrefetchScalarGridSpec(
        num_scalar_prefetch=len(meta), grid=(1,),
        in_specs=[hbm, hbm, hbm, hbm], out_specs=hbm,
        scratch_shapes=[pltpu.VMEM((EXPERT_RING, rows, LANES), U32), pltpu.VMEM((EXPERT_RING, rows, LANES), U32),
                        pltpu.VMEM((d, f), F32), pltpu.VMEM((d, f), F32), pltpu.VMEM((f, d), F32),
                        pltpu.VMEM((d, f), BF16), pltpu.VMEM((d, f), BF16), pltpu.VMEM((f, d), BF16),
                        pltpu.SemaphoreType.DMA((EXPERT_RING,)), pltpu.SemaphoreType.DMA((EXPERT_RING,)),
                        pltpu.SemaphoreType.DMA((3,))])
    return pl.pallas_call(
        _expert_kernel,
        grid_spec=grid_spec,
        out_shape=jax.ShapeDtypeStruct(xs.shape, xs.dtype),
        compiler_params=_params("arbitrary"),
        name="experts",
    )(*meta, xs, w_gate, w_up, w_down)


def _final_kernel(base_ref, yt_ref, wt_ref, mod_ref, g_ref, *rest, d):
    o_ref = rest[-1]
    t = base_ref.shape[0]
    w = wt_ref[...]
    routed = w[:, 0:1] * _load_packed(lambda sl: yt_ref[0, sl, :], t)
    for k in range(1, TOP_K):
        routed = routed + w[:, k:k + 1] * _load_packed(lambda sl, k=k: yt_ref[k, sl, :], t)
    g2 = mod_ref[0][:, 5 * d:6 * d]
    o_ref[...] = _rms(base_ref[...] + g2 * routed, g_ref[...])


def _final(base, ytok3, wts_t, mod3, tiles_per_batch, final_g, part, n_parts, prev_out):
    n, d = base.shape
    t = TOKEN_TILE
    tiles = n // t // n_parts
    off = part * tiles
    kern = functools.partial(_final_kernel, d=d)
    tok = pl.BlockSpec((t, d), lambda i: (off + i, 0))
    in_specs = [tok,
                pl.BlockSpec((TOP_K, t * PACK_SUBLANES, LANES), lambda i: (0, i, 0)),
                pl.BlockSpec((t, TOP_K), lambda i: (off + i, 0)),
                pl.BlockSpec((1, 1, mod3.shape[2]), lambda i: ((off + i) // tiles_per_batch, 0, 0)),
                _full(final_g)]
    args = [base, ytok3, wts_t, mod3, final_g]
    aliases = {}
    if prev_out is not None:
        in_specs.append(pl.BlockSpec(memory_space=pl.ANY))
        args.append(prev_out)
        aliases = {len(args) - 1: 0}
    return pl.pallas_call(
        kern,
        grid=(tiles,),
        in_specs=in_specs,
        out_specs=tok,
        out_shape=jax.ShapeDtypeStruct((n, d), F32),
        input_output_aliases=aliases,
        compiler_params=_params("arbitrary"),
        name="final",
    )(*args)


def kernel(x, c, ctx, c_ctx, w_mod, b_mod, norm1_g, norm2_g, w_in, ssm_lam_re, ssm_lam_im, ssm_log_dt, ssm_b_re, ssm_b_im, ssm_c_re, ssm_c_im, ssm_d, w_glu, w_ssm_out, conv_w, w_conv_out, w_o, w_router, router_bias, w_gate, w_up, w_down, ws_gate, ws_up, ws_down, final_g):
    b, l, d = x.shape
    n = b * l
    sw = ssm_d.shape[1]
    assert w_mod.shape[0] == 1, "single layer"
    assert b == SUBLANES and l % TOKEN_TILE == 0 and l % SCAN_CHUNK == 0 and TIME_TILE % GRID_W == 0

    mod_rows = 2 * SUBLANES
    c_all = jnp.concatenate([c, c_ctx[None, :], jnp.zeros((mod_rows - b - 1, d), F32)], axis=0)
    mod = _modulation(c_all, w_mod[0].astype(BF16), b_mod[0])
    mod3 = mod.reshape(mod_rows, 1, mod.shape[1])

    w_in_b = w_in[0].astype(BF16)
    g1n = norm1_g[0].reshape(1, d)
    u_lat, sga, gbt = _in_proj(x, mod3, g1n, w_in_b, conv_w[0], w_conv_out[0].astype(BF16))
    u_ctx = _ctx_proj(ctx, mod3, b, g1n, w_in_b[:, :sw])

    ops = _ssm_operators(ssm_lam_re[0], ssm_lam_im[0], ssm_log_dt[0], ssm_b_re[0], ssm_b_im[0],
                         ssm_c_re[0], ssm_c_im[0])
    y2 = _s5_scan(u_ctx, u_lat, *ops, nb=b)

    xl = _mixer_out(y2, u_lat, sga, gbt, x, mod3, ssm_d[0].reshape(1, sw), w_glu[0].astype(BF16),
                    w_ssm_out[0].astype(BF16), w_o[0].astype(BF16))

    tiles_per_batch = l // TOKEN_TILE
    xl2 = xl.reshape(n, d)
    g2n = norm2_g[0].reshape(1, d)
    h2, eidx, wts, rank, cnt = _route(xl2, mod3, tiles_per_batch, g2n, w_router[0].T,
                                      router_bias[0].reshape(N_EXPERTS, 1))

    counts = cnt[:, 0].astype(I32)
    n_rows = n * TOP_K
    starts, meta = _chunk_metadata(counts, n_rows)
    buf_rows = n_rows + N_EXPERTS * EXPERT_CHUNK
    dest = _dest_rows(eidx, rank, jnp.broadcast_to(starts[:, None], (N_EXPERTS, LANES)))

    xs = _sc_dispatch(dest.reshape(TOP_K * n), h2.reshape(n, PACK_SUBLANES, LANES), buf_rows)
    base = _shared(xl2, mod3, tiles_per_batch, g2n, ws_gate[0].astype(BF16), ws_up[0].astype(BF16),
                   ws_down[0].astype(BF16))
    ys = _experts(meta, xs.reshape(buf_rows * PACK_SUBLANES, LANES), w_gate[0], w_up[0], w_down[0])
    ys3 = ys.reshape(buf_rows, PACK_SUBLANES, LANES)

    n_part = n // COMBINE_PARTS
    wts_t = wts.T
    out = None
    for part in range(COMBINE_PARTS):
        dest_part = dest[:, part * n_part:(part + 1) * n_part].reshape(TOP_K * n_part)
        ytok = _sc_combine(dest_part, ys3, n_part)
        out = _final(base, ytok.reshape(TOP_K, n_part * PACK_SUBLANES, LANES), wts_t, mod3, tiles_per_batch,
                     final_g.reshape(1, d), part, COMBINE_PARTS, out)
    return out.reshape(b, l, d)
```

```python
import functools

import jax
import jax.numpy as jnp
from jax import lax
from jax.experimental import pallas as pl
from jax.experimental.pallas import tpu as pltpu
from jax.experimental.pallas import tpu_sc as plsc

F32 = jnp.float32
BF16 = jnp.bfloat16
I32 = jnp.int32
U32 = jnp.uint32

EPS = 1e-6
GRID_W = 64
N_EXPERTS = 256
TOP_K = 8
N_EXPERT_GROUPS = 8
EXPERTS_PER_GROUP = N_EXPERTS // N_EXPERT_GROUPS
TOP_K_GROUPS = 4
ROUTE_SCALE = 2.5

SUBLANES = 8
LANES = 128
VMEM_LIMIT_BYTES = 48 * 1024 * 1024

TIME_TILE = 64
TOKEN_TILE = 256
SCAN_CHUNK = 128
SCAN_SLAB = 512
EXPERT_CHUNK = 256
EXPERT_RING = 4
SC_CHUNK = 128
COMBINE_PARTS = 4
PACK_SUBLANES = 4


def _dot(a, b):
    return jnp.dot(a.astype(BF16), b.astype(BF16), preferred_element_type=F32)


def _rms(xf, g):
    return xf * lax.rsqrt(jnp.mean(xf * xf, axis=-1, keepdims=True) + EPS) * g


def _params(*sem):
    return pltpu.CompilerParams(dimension_semantics=sem, vmem_limit_bytes=VMEM_LIMIT_BYTES)


def _full(a):
    return pl.BlockSpec(a.shape, lambda *_: (0,) * a.ndim)


def _pack_rows(v):
    half = v.shape[1] // 2
    lo = lax.bitcast_convert_type(v[:, :half].astype(BF16).astype(F32), U32) >> 16
    hi = lax.bitcast_convert_type(v[:, half:].astype(BF16).astype(F32), U32) & jnp.uint32(0xFFFF0000)
    return lo | hi


def _unpack_lo(w):
    return lax.bitcast_convert_type(w << 16, F32)


def _unpack_hi(w):
    return lax.bitcast_convert_type(w & jnp.uint32(0xFFFF0000), F32)


def _store_packed(ref, v):
    t = v.shape[0]
    w = _pack_rows(v)
    for c in range(PACK_SUBLANES):
        ref[pl.ds(c, t, stride=PACK_SUBLANES), :] = w[:, c * LANES:(c + 1) * LANES]


def _load_packed(load, t):
    ws = [load(pl.ds(c, t, stride=PACK_SUBLANES)) for c in range(PACK_SUBLANES)]
    return jnp.concatenate([_unpack_lo(w) for w in ws] + [_unpack_hi(w) for w in ws], axis=1)


def _mod_kernel(c_ref, w_ref, b_ref, o_ref):
    c = c_ref[...]
    o_ref[...] = _dot(c * jax.nn.sigmoid(c), w_ref[...]) + b_ref[...]


def _modulation(c_all, w_mod, b_mod):
    rows, d = c_all.shape
    cols = w_mod.shape[1]
    blk = 1536
    return pl.pallas_call(
        _mod_kernel,
        grid=(cols // blk,),
        in_specs=[pl.BlockSpec((rows, d), lambda j: (0, 0)),
                  pl.BlockSpec((d, blk), lambda j: (0, j)),
                  pl.BlockSpec((1, blk), lambda j: (0, j))],
        out_specs=pl.BlockSpec((rows, blk), lambda j: (0, j)),
        out_shape=jax.ShapeDtypeStruct((rows, cols), F32),
        compiler_params=_params("arbitrary"),
        name="mod",
    )(c_all, w_mod, b_mod.reshape(1, cols))


def _per_row(m3, lo, hi, tt):
    nb = m3.shape[0]
    return jnp.broadcast_to(m3[:, :, lo:hi], (nb, tt, hi - lo)).reshape(nb * tt, hi - lo)


def _to_time_major(val, nb, tt):
    c = val.shape[1]
    return pltpu.einshape("btc->tbc", val.reshape(nb, tt, c)).reshape(nb * tt, c)


def _to_batch_major(val, nb, tt):
    c = val.shape[1]
    return pltpu.einshape("tbc->btc", val.reshape(tt, nb, c)).reshape(nb * tt, c)


def _in_proj_kernel(x_ref, mod_ref, g_ref, w_ref, cw_ref, wco_ref, u_ref, sga_ref, gbt_ref, *, d, sw):
    nb, tt, _ = x_ref.shape
    rows = nb * tt
    m3 = mod_ref[...]
    x = x_ref[...].reshape(rows, d)
    h = _rms(x, g_ref[...]) * (1.0 + _per_row(m3, d, 2 * d, tt)) + _per_row(m3, 0, d, tt)
    hb = h.astype(BF16)
    u_ref[...] = _to_time_major(jnp.dot(hb, w_ref[:, 0:sw], preferred_element_type=F32), nb, tt)
    cb = jnp.dot(hb, w_ref[:, sw:2 * sw], preferred_element_type=F32)
    cc = jnp.dot(hb, w_ref[:, 2 * sw:3 * sw], preferred_element_type=F32)
    cv = jnp.dot(hb, w_ref[:, 3 * sw:4 * sw], preferred_element_type=F32)
    ccv = cc * cv
    col = lax.broadcasted_iota(I32, ccv.shape, 0) % GRID_W
    prev = jnp.where(col == 0, 0.0, pltpu.roll(ccv, 1, axis=0))
    nxt = jnp.where(col == GRID_W - 1, 0.0, pltpu.roll(ccv, rows - 1, axis=0))
    cw = cw_ref[...]
    conv = prev * cw[0:1, :] + ccv * cw[1:2, :] + nxt * cw[2:3, :]
    y_conv = _dot(cb * conv, wco_ref[...])
    ga = jnp.dot(hb, w_ref[:, 4 * sw:4 * sw + d], preferred_element_type=F32)
    gb = jnp.dot(hb, w_ref[:, 4 * sw + d:4 * sw + 2 * d], preferred_element_type=F32)
    sga_ref[...] = jax.nn.sigmoid(ga).reshape(nb, tt, d)
    gbt_ref[...] = (jax.nn.sigmoid(gb) * y_conv).reshape(nb, tt, d)


def _in_proj(x, mod3, g1n, w_in_b, conv_w, w_conv_out_b):
    b, l, d = x.shape
    sw = conv_w.shape[1]
    tt = TIME_TILE
    kern = functools.partial(_in_proj_kernel, d=d, sw=sw)
    tok = pl.BlockSpec((b, tt, d), lambda j: (0, j, 0))
    return pl.pallas_call(
        kern,
        grid=(l // tt,),
        in_specs=[tok,
                  pl.BlockSpec((b, 1, mod3.shape[2]), lambda j: (0, 0, 0)),
                  _full(g1n), _full(w_in_b), _full(conv_w), _full(w_conv_out_b)],
        out_specs=[pl.BlockSpec((tt * b, sw), lambda j: (j, 0)), tok, tok],
        out_shape=[jax.ShapeDtypeStruct((l * b, sw), F32),
                   jax.ShapeDtypeStruct((b, l, d), F32),
                   jax.ShapeDtypeStruct((b, l, d), F32)],
        compiler_params=_params("arbitrary"),
        name="in_proj",
    )(x, mod3, g1n, w_in_b, conv_w, w_conv_out_b)


def _ctx_proj_kernel(x_ref, mod_ref, g_ref, w_ref, u_ref, *, d):
    nb, tt, _ = x_ref.shape
    m = mod_ref[0]
    x = x_ref[...].reshape(nb * tt, d)
    h = _rms(x, g_ref[...]) * (1.0 + m[:, d:2 * d]) + m[:, 0:d]
    u_ref[...] = _to_time_major(_dot(h, w_ref[...]), nb, tt)


def _ctx_proj(ctx, mod3, ctx_row, g1n, w_u_b):
    b, lc, d = ctx.shape
    sw = w_u_b.shape[1]
    tt = TIME_TILE
    kern = functools.partial(_ctx_proj_kernel, d=d)
    return pl.pallas_call(
        kern,
        grid=(lc // tt,),
        in_specs=[pl.BlockSpec((b, tt, d), lambda j: (0, j, 0)),
                  pl.BlockSpec((1, 1, mod3.shape[2]), lambda j: (ctx_row, 0, 0)),
                  _full(g1n), _full(w_u_b)],
        out_specs=pl.BlockSpec((tt * b, sw), lambda j: (j, 0)),
        out_shape=jax.ShapeDtypeStruct((lc * b, sw), F32),
        compiler_params=_params("arbitrary"),
        name="ctx_proj",
    )(ctx, mod3, g1n, w_u_b)


def _scan_kernel(uc_ref, ul_ref, are_ref, aim_ref, bre_ref, bim_ref, cre_ref, cim_ref, y_ref,
                 sre_ref, sim_ref, cre_s, cim_s, *, n_ctx, nb):
    dirn = pl.program_id(0)
    j = pl.program_id(1)
    half_c = bre_ref.shape[2]
    half_s = bre_ref.shape[3]
    n_half = bre_ref.shape[1]
    steps = SCAN_CHUNK

    @pl.when(j == 0)
    def _():
        cre_s[...] = jnp.zeros_like(cre_s)
        cim_s[...] = jnp.zeros_like(cim_s)

    u = jnp.where(j < n_ctx, uc_ref[...], ul_ref[...]).astype(BF16)
    for h in range(n_half):
        uh = u[:, h * half_c:(h + 1) * half_c]
        sre_ref[:, h * half_s:(h + 1) * half_s] = jnp.dot(uh, bre_ref[0, h], preferred_element_type=F32)
        sim_ref[:, h * half_s:(h + 1) * half_s] = jnp.dot(uh, bim_ref[0, h], preferred_element_type=F32)

    lanes = sre_ref.shape[1]
    for q in range(lanes // SCAN_SLAB):
        ls = slice(q * SCAN_SLAB, (q + 1) * SCAN_SLAB)
        a_re = jnp.broadcast_to(are_ref[0, :, ls], (nb, SCAN_SLAB))
        a_im = jnp.broadcast_to(aim_ref[0, :, ls], (nb, SCAN_SLAB))

        def body(i, carry, ls=ls, a_re=a_re, a_im=a_im):
            s_re, s_im = carry
            for k in range(SUBLANES):
                t = i * SUBLANES + k
                t = jnp.where(dirn == 0, t, steps - 1 - t)
                r0 = pl.multiple_of(t * nb, nb)
                b_re = sre_ref[pl.ds(r0, nb), ls]
                b_im = sim_ref[pl.ds(r0, nb), ls]
                n_re = a_re * s_re - a_im * s_im + b_re
                n_im = a_re * s_im + a_im * s_re + b_im
                sre_ref[pl.ds(r0, nb), ls] = n_re
                sim_ref[pl.ds(r0, nb), ls] = n_im
                s_re, s_im = n_re, n_im
            return s_re, s_im

        s_re, s_im = lax.fori_loop(0, steps // SUBLANES, body, (cre_s[:, ls], cim_s[:, ls]))
        cre_s[:, ls] = s_re
        cim_s[:, ls] = s_im

    @pl.when(j >= n_ctx)
    def _():
        half_o = cre_ref.shape[3]
        for h in range(n_half):
            s_r = sre_ref[:, h * half_s:(h + 1) * half_s].astype(BF16)
            s_i = sim_ref[:, h * half_s:(h + 1) * half_s].astype(BF16)
            y_ref[0, :, h * half_o:(h + 1) * half_o] = (
                jnp.dot(s_r, cre_ref[0, h], preferred_element_type=F32)
                + jnp.dot(s_i, cim_ref[0, h], preferred_element_type=F32))


def _s5_scan(u_ctx, u_lat, a_re, a_im, b_re, b_im, c_re, c_imn, nb):
    rows_c, sw = u_ctx.shape
    rows_l = u_lat.shape[0]
    rc = SCAN_CHUNK * nb
    n_ctx = rows_c // rc
    n_lat = rows_l // rc
    lanes = a_re.shape[2]

    def ctx_idx(d, j):
        jj = jnp.minimum(j, n_ctx - 1)
        return (jnp.where(d == 0, jj, n_ctx - 1 - jj), 0)

    def lat_idx(d, j):
        jj = jnp.maximum(j - n_ctx, 0)
        return (jnp.where(d == 0, jj, n_lat - 1 - jj), 0)

    def per_dir4(d, j):
        return (d, 0, 0, 0)

    kern = functools.partial(_scan_kernel, n_ctx=n_ctx, nb=nb)
    return pl.pallas_call(
        kern,
        grid=(2, n_ctx + n_lat),
        in_specs=[pl.BlockSpec((rc, sw), ctx_idx),
                  pl.BlockSpec((rc, sw), lat_idx),
                  pl.BlockSpec((1, 1, lanes), lambda d, j: (d, 0, 0)),
                  pl.BlockSpec((1, 1, lanes), lambda d, j: (d, 0, 0)),
                  pl.BlockSpec((1,) + b_re.shape[1:], per_dir4),
                  pl.BlockSpec((1,) + b_im.shape[1:], per_dir4),
                  pl.BlockSpec((1,) + c_re.shape[1:], per_dir4),
                  pl.BlockSpec((1,) + c_imn.shape[1:], per_dir4)],
        out_specs=pl.BlockSpec((1, rc, sw), lambda d, j: (d,) + lat_idx(d, j)),
        out_shape=jax.ShapeDtypeStruct((2, rows_l, sw), F32),
        scratch_shapes=[pltpu.VMEM((rc, lanes), F32), pltpu.VMEM((rc, lanes), F32),
                        pltpu.VMEM((nb, lanes), F32), pltpu.VMEM((nb, lanes), F32)],
        compiler_params=_params("arbitrary", "arbitrary"),
        name="s5_scan",
    )(u_ctx, u_lat, a_re, a_im, b_re, b_im, c_re, c_imn)


def _ssm_operators(lam_re, lam_im, log_dt, b_re, b_im, c_re, c_im):
    dt = jnp.exp(log_dt)[..., None]
    mag = jnp.exp(lam_re * dt)
    a_re = mag * jnp.cos(lam_im * dt)
    a_im = mag * jnp.sin(lam_im * dt)
    den = lam_re * lam_re + lam_im * lam_im
    k_re = ((a_re - 1.0) * lam_re + a_im * lam_im) / den
    k_im = (a_im * lam_re - (a_re - 1.0) * lam_im) / den
    bb_re = k_re[..., None] * b_re - k_im[..., None] * b_im
    bb_im = k_re[..., None] * b_im + k_im[..., None] * b_re
    nd, g, p, q = bb_re.shape
    halves = 2
    gh = g // halves
    eye = jnp.eye(gh, dtype=F32)

    def drive(bb):
        bbh = bb.reshape(nd, halves, gh, p, q)
        m = jnp.einsum('dhgpq,gk->dhgqkp', bbh, eye)
        return m.reshape(nd, halves, gh * q, gh * p).astype(BF16)

    def readout(cc):
        cch = cc.reshape(nd, halves, gh, q, p)
        m = jnp.einsum('dhgqp,gk->dhgpkq', cch, eye)
        return m.reshape(nd, halves, gh * p, gh * q).astype(BF16)

    return (a_re.reshape(nd, 1, g * p), a_im.reshape(nd, 1, g * p),
            drive(bb_re), drive(bb_im), readout(c_re), readout(-c_im))


def _mixer_kernel(y_ref, u_ref, sga_ref, gbt_ref, x_ref, mod_ref, d_ref, wglu_ref, wso_ref, wo_ref, o_ref,
                  *, d, sw):
    nb, tt, _ = x_ref.shape
    rows = nb * tt
    y = _to_batch_major(d_ref[...] * u_ref[...] + y_ref[0] + y_ref[1], nb, tt)
    v = _dot(jax.nn.gelu(y), wglu_ref[...])
    ys = v[:, 0:sw] * jax.nn.sigmoid(v[:, sw:2 * sw])
    y_a = _dot(ys, wso_ref[...])
    merged = sga_ref[...].reshape(rows, d) * y_a + gbt_ref[...].reshape(rows, d)
    o = _dot(merged, wo_ref[...])
    g1 = _per_row(mod_ref[...], 2 * d, 3 * d, tt)
    o_ref[...] = (x_ref[...].reshape(rows, d) + g1 * o).reshape(nb, tt, d)


def _mixer_out(y2, u_lat, sga, gbt, x, mod3, ssm_d, w_glu_b, w_ssm_out_b, w_o_b):
    b, l, d = x.shape
    sw = ssm_d.shape[1]
    tt = TIME_TILE
    kern = functools.partial(_mixer_kernel, d=d, sw=sw)
    tok = pl.BlockSpec((b, tt, d), lambda j: (0, j, 0))
    return pl.pallas_call(
        kern,
        grid=(l // tt,),
        in_specs=[pl.BlockSpec((2, tt * b, sw), lambda j: (0, j, 0)),
                  pl.BlockSpec((tt * b, sw), lambda j: (j, 0)),
                  tok, tok, tok,
                  pl.BlockSpec((b, 1, mod3.shape[2]), lambda j: (0, 0, 0)),
                  _full(ssm_d), _full(w_glu_b), _full(w_ssm_out_b), _full(w_o_b)],
        out_specs=tok,
        out_shape=jax.ShapeDtypeStruct((b, l, d), F32),
        compiler_params=_params("arbitrary"),
        name="mixer_out",
    )(y2, u_lat, sga, gbt, x, mod3, ssm_d, w_glu_b, w_ssm_out_b, w_o_b)


def _split_bf16(a):
    hi = a.astype(BF16)
    lo = (a - hi.astype(F32)).astype(BF16)
    return hi, lo


def _norm2(xl, m, g, d):
    return _rms(xl, g) * (1.0 + m[:, 4 * d:5 * d]) + m[:, 3 * d:4 * d]


def _shared_kernel(xl_ref, mod_ref, g_ref, wsg_ref, wsu_ref, wsd_ref, base_ref, *, d):
    m = mod_ref[0]
    xl = xl_ref[...]
    hb = _norm2(xl, m, g_ref[...], d).astype(BF16)
    sg = jnp.dot(hb, wsg_ref[...], preferred_element_type=F32)
    su = jnp.dot(hb, wsu_ref[...], preferred_element_type=F32)
    shared = _dot(sg * jax.nn.sigmoid(sg) * su, wsd_ref[...])
    base_ref[...] = xl + m[:, 5 * d:6 * d] * shared


def _shared(xl2, mod3, tiles_per_batch, g2n, ws_gate_b, ws_up_b, ws_down_b):
    n, d = xl2.shape
    t = TOKEN_TILE
    tok = pl.BlockSpec((t, d), lambda i: (i, 0))
    return pl.pallas_call(
        functools.partial(_shared_kernel, d=d),
        grid=(n // t,),
        in_specs=[tok,
                  pl.BlockSpec((1, 1, mod3.shape[2]), lambda i: (i // tiles_per_batch, 0, 0)),
                  _full(g2n), _full(ws_gate_b), _full(ws_up_b), _full(ws_down_b)],
        out_specs=tok,
        out_shape=jax.ShapeDtypeStruct((n, d), F32),
        compiler_params=_params("arbitrary"),
        name="shared",
    )(xl2, mod3, g2n, ws_gate_b, ws_up_b, ws_down_b)


def _route_kernel(xl_ref, mod_ref, g_ref, whi_ref, wlo_ref, rb_ref,
                  h2_ref, eidx_ref, wts_ref, rank_ref, cnt_ref, carry_ref, *, d):
    i = pl.program_id(0)
    t = xl_ref.shape[0]
    h2 = _norm2(xl_ref[...], mod_ref[0], g_ref[...], d)
    _store_packed(h2_ref, h2)

    h_hi, h_lo = _split_bf16(h2)
    w_hi, w_lo = whi_ref[...], wlo_ref[...]
    nt = (((1,), (1,)), ((), ()))
    logits = (lax.dot_general(w_hi, h_hi, nt, preferred_element_type=F32)
              + lax.dot_general(w_hi, h_lo, nt, preferred_element_type=F32)
              + lax.dot_general(w_lo, h_hi, nt, preferred_element_type=F32))
    scores = jax.nn.sigmoid(logits)
    choice = scores + rb_ref[...]

    epg = EXPERTS_PER_GROUP
    gi = lax.broadcasted_iota(I32, (epg, t), 0)
    gs = []
    for g in range(N_EXPERT_GROUPS):
        seg = choice[g * epg:(g + 1) * epg, :]
        m1 = jnp.max(seg, axis=0, keepdims=True)
        i1 = jnp.min(jnp.where(seg == m1, gi, epg), axis=0, keepdims=True)
        m2 = jnp.max(jnp.where(gi == i1, -jnp.inf, seg), axis=0, keepdims=True)
        gs.append(m1 + m2)
    masked = []
    for g in range(N_EXPERT_GROUPS):
        beat = jnp.zeros((1, t), I32)
        for g2 in range(N_EXPERT_GROUPS):
            if g2 < g:
                beat = beat + (gs[g2] >= gs[g]).astype(I32)
            elif g2 > g:
                beat = beat + (gs[g2] > gs[g]).astype(I32)
        keep = beat < TOP_K_GROUPS
        masked.append(jnp.where(keep, choice[g * epg:(g + 1) * epg, :], -jnp.inf))
    cur = jnp.concatenate(masked, axis=0)

    ei_all = lax.broadcasted_iota(I32, (N_EXPERTS, t), 0)
    picks, raw = [], []
    onehot = jnp.zeros((N_EXPERTS, t), F32)
    for _ in range(TOP_K):
        mx = jnp.max(cur, axis=0, keepdims=True)
        ei = jnp.min(jnp.where(cur == mx, ei_all, N_EXPERTS), axis=0, keepdims=True)
        hit = ei_all == ei
        raw.append(jnp.sum(jnp.where(hit, scores, 0.0), axis=0, keepdims=True))
        cur = jnp.where(hit, -jnp.inf, cur)
        onehot = jnp.where(hit, 1.0, onehot)
        picks.append(ei)
    tot = raw[0]
    for k in range(1, TOP_K):
        tot = tot + raw[k]

    @pl.when(i == 0)
    def _():
        carry_ref[...] = jnp.zeros_like(carry_ref)

    upper = (lax.broadcasted_iota(I32, (t, t), 0) < lax.broadcasted_iota(I32, (t, t), 1)).astype(BF16)
    before = jnp.dot(onehot.astype(BF16), upper, preferred_element_type=F32) + carry_ref[:, 0:1]
    for k in range(TOP_K):
        eidx_ref[k:k + 1, :] = picks[k]
        wts_ref[k:k + 1, :] = raw[k] / tot * ROUTE_SCALE
        rk = jnp.sum(jnp.where(ei_all == picks[k], before, 0.0), axis=0, keepdims=True)
        rank_ref[k:k + 1, :] = rk.astype(I32)
    carry_ref[...] = carry_ref[...] + jnp.sum(onehot, axis=1, keepdims=True)
    cnt_ref[...] = carry_ref[...]


def _route(xl2, mod3, tiles_per_batch, g2n, w_router_t, router_bias):
    n, d = xl2.shape
    t = TOKEN_TILE
    e = w_router_t.shape[0]
    w_top = lax.bitcast_convert_type(lax.bitcast_convert_type(w_router_t, U32) & jnp.uint32(0xFFFF0000), F32)
    w_hi, w_lo = w_top.astype(BF16), (w_router_t - w_top).astype(BF16)
    kern = functools.partial(_route_kernel, d=d)
    tok = pl.BlockSpec((t, d), lambda i: (i, 0))
    small = pl.BlockSpec((TOP_K, t), lambda i: (0, i))
    return pl.pallas_call(
        kern,
        grid=(n // t,),
        in_specs=[tok,
                  pl.BlockSpec((1, 1, mod3.shape[2]), lambda i: (i // tiles_per_batch, 0, 0)),
                  _full(g2n), _full(w_hi), _full(w_lo), _full(router_bias)],
        out_specs=[pl.BlockSpec((t * PACK_SUBLANES, LANES), lambda i: (i, 0)), small, small, small,
                   pl.BlockSpec((e, LANES), lambda i: (0, 0))],
        out_shape=[jax.ShapeDtypeStruct((n * PACK_SUBLANES, LANES), U32),
                   jax.ShapeDtypeStruct((TOP_K, n), I32),
                   jax.ShapeDtypeStruct((TOP_K, n), F32),
                   jax.ShapeDtypeStruct((TOP_K, n), I32),
                   jax.ShapeDtypeStruct((e, LANES), F32)],
        scratch_shapes=[pltpu.VMEM((e, LANES), F32)],
        compiler_params=_params("arbitrary"),
        name="route",
    )(xl2, mod3, g2n, w_hi, w_lo, router_bias)


def _dest_kernel(eidx_ref, rank_ref, start_ref, o_ref):
    t = eidx_ref.shape[1]
    ei_all = lax.broadcasted_iota(I32, (N_EXPERTS, t), 0)
    st = start_ref[:, 0:1]
    for k in range(TOP_K):
        hit = ei_all == eidx_ref[k:k + 1, :]
        o_ref[k:k + 1, :] = jnp.sum(jnp.where(hit, st, 0), axis=0, keepdims=True) + rank_ref[k:k + 1, :]


def _dest_rows(eidx, rank, starts_b):
    n = eidx.shape[1]
    t = 1024
    small = pl.BlockSpec((TOP_K, t), lambda i: (0, i))
    return pl.pallas_call(
        _dest_kernel,
        grid=(n // t,),
        in_specs=[small, small, _full(starts_b)],
        out_specs=small,
        out_shape=jax.ShapeDtypeStruct((TOP_K, n), I32),
        compiler_params=_params("arbitrary"),
        name="dest",
    )(eidx, rank, starts_b)


def _sc_workers():
    info = plsc.get_sparse_core_info()
    return info.num_cores, info.num_cores * info.num_subcores


def _sc_dispatch(dest_flat, src3, out_rows):
    n = src3.shape[0]
    n_cores, n_workers = _sc_workers()
    per_worker = n // n_workers
    assert per_worker % SC_CHUNK == 0
    mesh = plsc.VectorSubcoreMesh(core_axis_name="c", subcore_axis_name="s")

    @functools.partial(
        pl.kernel, mesh=mesh,
        out_type=jax.ShapeDtypeStruct((out_rows,) + src3.shape[1:], src3.dtype),
        scratch_types=[pltpu.VMEM((SC_CHUNK,) + src3.shape[1:], src3.dtype)]
        + [pltpu.VMEM((SC_CHUNK,), I32)] * TOP_K,
        name="dispatch",
    )
    def run(dest_hbm, src_hbm, out_hbm, rows_v, *idx_v):
        worker = lax.axis_index("s") * n_cores + lax.axis_index("c")

        @pl.loop(0, per_worker // SC_CHUNK)
        def _(j):
            base = worker * per_worker + j * SC_CHUNK
            pltpu.sync_copy(src_hbm.at[pl.ds(base, SC_CHUNK)], rows_v)
            for k in range(TOP_K):
                pltpu.sync_copy(dest_hbm.at[pl.ds(k * n + base, SC_CHUNK)], idx_v[k])
            for k in range(TOP_K):
                pltpu.sync_copy(rows_v, out_hbm.at[idx_v[k]])

    return run(dest_flat, src3)


def _sc_combine(dest_flat, ys3, n):
    n_cores, n_workers = _sc_workers()
    per_worker = n // n_workers
    assert per_worker % SC_CHUNK == 0
    mesh = plsc.VectorSubcoreMesh(core_axis_name="c", subcore_axis_name="s")

    @functools.partial(
        pl.kernel, mesh=mesh,
        out_type=jax.ShapeDtypeStruct((TOP_K * n,) + ys3.shape[1:], ys3.dtype),
        scratch_types=[pltpu.VMEM((SC_CHUNK,) + ys3.shape[1:], ys3.dtype), pltpu.VMEM((SC_CHUNK,), I32)],
        name="combine",
    )
    def run(dest_hbm, ys_hbm, out_hbm, rows_v, idx_v):
        worker = lax.axis_index("s") * n_cores + lax.axis_index("c")

        @pl.loop(0, per_worker // SC_CHUNK)
        def _(j):
            base = worker * per_worker + j * SC_CHUNK
            for k in range(TOP_K):
                pltpu.sync_copy(dest_hbm.at[pl.ds(k * n + base, SC_CHUNK)], idx_v)
                pltpu.sync_copy(ys_hbm.at[idx_v], rows_v)
                pltpu.sync_copy(rows_v, out_hbm.at[pl.ds(k * n + base, SC_CHUNK)])

    return run(dest_flat, ys3)


def _chunk_metadata(counts, n_rows):
    ch = EXPERT_CHUNK
    n_ch = (counts + ch - 1) // ch
    cum = jnp.cumsum(n_ch)
    total = cum[-1]
    max_chunks = n_rows // ch + N_EXPERTS
    i = jnp.arange(max_chunks, dtype=I32)
    e = jnp.sum((cum[None, :] <= i[:, None]).astype(I32), axis=1)
    e_last = jnp.max(jnp.where(counts > 0, jnp.arange(N_EXPERTS, dtype=I32), 0))
    exp = jnp.where(i < total, jnp.minimum(e, N_EXPERTS - 1), e_last).astype(I32)
    newexp = jnp.concatenate([jnp.ones((1,), I32), (exp[1:] != exp[:-1]).astype(I32)])
    change_at = jnp.where(newexp == 1, i, max_chunks)
    nxt_change = lax.cummin(jnp.concatenate([change_at[1:], jnp.array([max_chunks], I32)]), reverse=True)
    nexp = jnp.where(nxt_change < max_chunks, exp[jnp.minimum(nxt_change, max_chunks - 1)], -1).astype(I32)
    starts = (cum - n_ch) * ch
    return starts.astype(I32), (exp, newexp, nexp, total.astype(I32).reshape(1))


def _expert_kernel(exp_ref, newexp_ref, nexp_ref, total_ref, xs_hbm, wg_hbm, wu_hbm, wd_hbm, after_hbm, ys_hbm,
                   xbuf, ybuf, wgf, wuf, wdf, wgb, wub, wdb, sem_x, sem_y, sem_w):
    del after_hbm
    ch = EXPERT_CHUNK
    nbuf = EXPERT_RING
    rows = ch * PACK_SUBLANES
    total = total_ref[0]

    def x_copy(i, slot):
        r0 = pl.multiple_of(i * rows, rows)
        return pltpu.make_async_copy(xs_hbm.at[pl.ds(r0, rows)], xbuf.at[slot], sem_x.at[slot])

    def y_copy(i, slot):
        r0 = pl.multiple_of(i * rows, rows)
        return pltpu.make_async_copy(ybuf.at[slot], ys_hbm.at[pl.ds(r0, rows)], sem_y.at[slot])

    def weight_copies(e):
        return (pltpu.make_async_copy(wg_hbm.at[e], wgf, sem_w.at[0]),
                pltpu.make_async_copy(wu_hbm.at[e], wuf, sem_w.at[1]),
                pltpu.make_async_copy(wd_hbm.at[e], wdf, sem_w.at[2]))

    for cp in weight_copies(exp_ref[0]):
        cp.start()
    for b in range(nbuf - 1):
        @pl.when(b < total)
        def _(b=b):
            x_copy(b, b).start()

    def chunk(i, carry):
        slot = lax.rem(i, nbuf)
        x_copy(i, slot).wait()
        ahead = i + (nbuf - 1)

        @pl.when(ahead < total)
        def _():
            x_copy(ahead, lax.rem(ahead, nbuf)).start()

        @pl.when(newexp_ref[i] == 1)
        def _():
            for cp in weight_copies(exp_ref[i]):
                cp.wait()
            wgb[...] = wgf[...].astype(BF16)
            wub[...] = wuf[...].astype(BF16)
            wdb[...] = wdf[...].astype(BF16)

            @pl.when(nexp_ref[i] >= 0)
            def _():
                for cp in weight_copies(nexp_ref[i]):
                    cp.start()

        @pl.when(i >= nbuf)
        def _():
            y_copy(i, slot).wait()

        xr = xbuf.at[slot]
        yr = ybuf.at[slot]
        x = _load_packed(lambda sl: xr[sl, :], ch).astype(BF16)
        g = jnp.dot(x, wgb[...], preferred_element_type=F32)
        u = jnp.dot(x, wub[...], preferred_element_type=F32)
        y = jnp.dot((g * jax.nn.sigmoid(g) * u).astype(BF16), wdb[...], preferred_element_type=F32)
        _store_packed(yr, y)
        y_copy(i, slot).start()
        return carry

    lax.fori_loop(0, total, chunk, 0)

    for b in range(nbuf):
        @pl.when(b < total)
        def _(b=b):
            y_copy(0, b).wait()


def _experts(meta, xs, w_gate, w_up, w_down, after):
    e, d, f = w_gate.shape
    hbm = pl.BlockSpec(memory_space=pl.ANY)
    rows = EXPERT_CHUNK * PACK_SUBLANES
    grid_spec = pltpu.PrefetchScalarGridSpec(
        num_scalar_prefetch=len(meta), grid=(1,),
        in_specs=[hbm, hbm, hbm, hbm, hbm], out_specs=hbm,
        scratch_shapes=[pltpu.VMEM((EXPERT_RING, rows, LANES), U32), pltpu.VMEM((EXPERT_RING, rows, LANES), U32),
                        pltpu.VMEM((d, f), F32), pltpu.VMEM((d, f), F32), pltpu.VMEM((f, d), F32),
                        pltpu.VMEM((d, f), BF16), pltpu.VMEM((d, f), BF16), pltpu.VMEM((f, d), BF16),
                        pltpu.SemaphoreType.DMA((EXPERT_RING,)), pltpu.SemaphoreType.DMA((EXPERT_RING,)),
                        pltpu.SemaphoreType.DMA((3,))])
    return pl.pallas_call(
        _expert_kernel,
        grid_spec=grid_spec,
        out_shape=jax.ShapeDtypeStruct(xs.shape, xs.dtype),
        compiler_params=_params("arbitrary"),
        name="experts",
    )(*meta, xs, w_gate, w_up, w_down, after)


def _final_kernel(base_ref, yt_ref, wt_ref, mod_ref, g_ref, *rest, d):
    o_ref = rest[-1]
    t = base_ref.shape[0]
    w = wt_ref[...]
    routed = w[:, 0:1] * _load_packed(lambda sl: yt_ref[0, sl, :], t)
    for k in range(1, TOP_K):
        routed = routed + w[:, k:k + 1] * _load_packed(lambda sl, k=k: yt_ref[k, sl, :], t)
    g2 = mod_ref[0][:, 5 * d:6 * d]
    o_ref[...] = _rms(base_ref[...] + g2 * routed, g_ref[...])


def _final(base, ytok3, wts_t, mod3, tiles_per_batch, final_g, part, n_parts, prev_out):
    n, d = base.shape
    t = TOKEN_TILE
    tiles = n // t // n_parts
    off = part * tiles
    kern = functools.partial(_final_kernel, d=d)
    tok = pl.BlockSpec((t, d), lambda i: (off + i, 0))
    in_specs = [tok,
                pl.BlockSpec((TOP_K, t * PACK_SUBLANES, LANES), lambda i: (0, i, 0)),
                pl.BlockSpec((t, TOP_K), lambda i: (off + i, 0)),
                pl.BlockSpec((1, 1, mod3.shape[2]), lambda i: ((off + i) // tiles_per_batch, 0, 0)),
                _full(final_g)]
    args = [base, ytok3, wts_t, mod3, final_g]
    aliases = {}
    if prev_out is not None:
        in_specs.append(pl.BlockSpec(memory_space=pl.ANY))
        args.append(prev_out)
        aliases = {len(args) - 1: 0}
    return pl.pallas_call(
        kern,
        grid=(tiles,),
        in_specs=in_specs,
        out_specs=tok,
        out_shape=jax.ShapeDtypeStruct((n, d), F32),
        input_output_aliases=aliases,
        compiler_params=_params("arbitrary"),
        name="final",
    )(*args)


def kernel(x, c, ctx, c_ctx, w_mod, b_mod, norm1_g, norm2_g, w_in, ssm_lam_re, ssm_lam_im, ssm_log_dt, ssm_b_re, ssm_b_im, ssm_c_re, ssm_c_im, ssm_d, w_glu, w_ssm_out, conv_w, w_conv_out, w_o, w_router, router_bias, w_gate, w_up, w_down, ws_gate, ws_up, ws_down, final_g):
    b, l, d = x.shape
    n = b * l
    sw = ssm_d.shape[1]
    assert w_mod.shape[0] == 1, "single layer"
    assert b == SUBLANES and l % TOKEN_TILE == 0 and l % SCAN_CHUNK == 0 and TIME_TILE % GRID_W == 0

    mod_rows = 2 * SUBLANES
    c_all = jnp.concatenate([c, c_ctx[None, :], jnp.zeros((mod_rows - b - 1, d), F32)], axis=0)
    mod = _modulation(c_all, w_mod[0].astype(BF16), b_mod[0])
    mod3 = mod.reshape(mod_rows, 1, mod.shape[1])

    w_in_b = w_in[0].astype(BF16)
    g1n = norm1_g[0].reshape(1, d)
    u_lat, sga, gbt = _in_proj(x, mod3, g1n, w_in_b, conv_w[0], w_conv_out[0].astype(BF16))
    u_ctx = _ctx_proj(ctx, mod3, b, g1n, w_in_b[:, :sw])

    ops = _ssm_operators(ssm_lam_re[0], ssm_lam_im[0], ssm_log_dt[0], ssm_b_re[0], ssm_b_im[0],
                         ssm_c_re[0], ssm_c_im[0])
    y2 = _s5_scan(u_ctx, u_lat, *ops, nb=b)

    xl = _mixer_out(y2, u_lat, sga, gbt, x, mod3, ssm_d[0].reshape(1, sw), w_glu[0].astype(BF16),
                    w_ssm_out[0].astype(BF16), w_o[0].astype(BF16))

    tiles_per_batch = l // TOKEN_TILE
    xl2 = xl.reshape(n, d)
    g2n = norm2_g[0].reshape(1, d)
    h2, eidx, wts, rank, cnt = _route(xl2, mod3, tiles_per_batch, g2n, w_router[0].T,
                                      router_bias[0].reshape(N_EXPERTS, 1))

    counts = cnt[:, 0].astype(I32)
    n_rows = n * TOP_K
    starts, meta = _chunk_metadata(counts, n_rows)
    buf_rows = n_rows + N_EXPERTS * EXPERT_CHUNK
    dest = _dest_rows(eidx, rank, jnp.broadcast_to(starts[:, None], (N_EXPERTS, LANES)))

    xs = _sc_dispatch(dest.reshape(TOP_K * n), h2.reshape(n, PACK_SUBLANES, LANES), buf_rows)
    base = _shared(xl2, mod3, tiles_per_batch, g2n, ws_gate[0].astype(BF16), ws_up[0].astype(BF16),
                   ws_down[0].astype(BF16))
    ys = _experts(meta, xs.reshape(buf_rows * PACK_SUBLANES, LANES), w_gate[0], w_up[0], w_down[0], base)
    ys3 = ys.reshape(buf_rows, PACK_SUBLANES, LANES)

    n_part = n // COMBINE_PARTS
    wts_t = wts.T
    out = None
    for part in range(COMBINE_PARTS):
        dest_part = dest[:, part * n_part:(part + 1) * n_part].reshape(TOP_K * n_part)
        ytok = _sc_combine(dest_part, ys3, n_part)
        out = _final(base, ytok.reshape(TOP_K, n_part * PACK_SUBLANES, LANES), wts_t, mod3, tiles_per_batch,
                     final_g.reshape(1, d), part, COMBINE_PARTS, out)
    return out.reshape(b, l, d)
```

```python
import functools

import jax
import jax.numpy as jnp
from jax import lax
from jax.experimental import pallas as pl
from jax.experimental.pallas import tpu as pltpu
from jax.experimental.pallas import tpu_sc as plsc

F32 = jnp.float32
BF16 = jnp.bfloat16
I32 = jnp.int32
U32 = jnp.uint32

EPS = 1e-6
GRID_W = 64
N_EXPERTS = 256
TOP_K = 8
N_EXPERT_GROUPS = 8
EXPERTS_PER_GROUP = N_EXPERTS // N_EXPERT_GROUPS
TOP_K_GROUPS = 4
ROUTE_SCALE = 2.5

SUBLANES = 8
LANES = 128
VMEM_LIMIT_BYTES = 48 * 1024 * 1024

TIME_TILE = 64
TOKEN_TILE = 256
SCAN_CHUNK = 128
SCAN_SLAB = 512
EXPERT_CHUNK = 256
EXPERT_RING = 4
SC_CHUNK = 128
COMBINE_PARTS = 4
PACK_SUBLANES = 4


def _dot(a, b):
    return jnp.dot(a.astype(BF16), b.astype(BF16), preferred_element_type=F32)


def _rms(xf, g):
    return xf * lax.rsqrt(jnp.mean(xf * xf, axis=-1, keepdims=True) + EPS) * g


def _params(*sem):
    return pltpu.CompilerParams(dimension_semantics=sem, vmem_limit_bytes=VMEM_LIMIT_BYTES)


def _full(a):
    return pl.BlockSpec(a.shape, lambda *_: (0,) * a.ndim)


def _pack_rows(v):
    half = v.shape[1] // 2
    lo = lax.bitcast_convert_type(v[:, :half].astype(BF16).astype(F32), U32) >> 16
    hi = lax.bitcast_convert_type(v[:, half:].astype(BF16).astype(F32), U32) & jnp.uint32(0xFFFF0000)
    return lo | hi


def _unpack_lo(w):
    return lax.bitcast_convert_type(w << 16, F32)


def _unpack_hi(w):
    return lax.bitcast_convert_type(w & jnp.uint32(0xFFFF0000), F32)


def _store_packed(ref, v):
    t = v.shape[0]
    w = _pack_rows(v)
    for c in range(PACK_SUBLANES):
        ref[pl.ds(c, t, stride=PACK_SUBLANES), :] = w[:, c * LANES:(c + 1) * LANES]


def _load_packed(load, t):
    ws = [load(pl.ds(c, t, stride=PACK_SUBLANES)) for c in range(PACK_SUBLANES)]
    return jnp.concatenate([_unpack_lo(w) for w in ws] + [_unpack_hi(w) for w in ws], axis=1)


def _mod_kernel(c_ref, w_ref, b_ref, o_ref):
    c = c_ref[...]
    o_ref[...] = _dot(c * jax.nn.sigmoid(c), w_ref[...]) + b_ref[...]


def _modulation(c_all, w_mod, b_mod):
    rows, d = c_all.shape
    cols = w_mod.shape[1]
    blk = 1536
    return pl.pallas_call(
        _mod_kernel,
        grid=(cols // blk,),
        in_specs=[pl.BlockSpec((rows, d), lambda j: (0, 0)),
                  pl.BlockSpec((d, blk), lambda j: (0, j)),
                  pl.BlockSpec((1, blk), lambda j: (0, j))],
        out_specs=pl.BlockSpec((rows, blk), lambda j: (0, j)),
        out_shape=jax.ShapeDtypeStruct((rows, cols), F32),
        compiler_params=_params("arbitrary"),
        name="mod",
    )(c_all, w_mod, b_mod.reshape(1, cols))


def _per_row(m3, lo, hi, tt):
    nb = m3.shape[0]
    return jnp.broadcast_to(m3[:, :, lo:hi], (nb, tt, hi - lo)).reshape(nb * tt, hi - lo)


def _to_time_major(val, nb, tt):
    c = val.shape[1]
    return pltpu.einshape("btc->tbc", val.reshape(nb, tt, c)).reshape(nb * tt, c)


def _to_batch_major(val, nb, tt):
    c = val.shape[1]
    return pltpu.einshape("tbc->btc", val.reshape(tt, nb, c)).reshape(nb * tt, c)


def _in_proj_kernel(x_ref, mod_ref, g_ref, w_ref, cw_ref, wco_ref, u_ref, sga_ref, gbt_ref, *, d, sw):
    nb, tt, _ = x_ref.shape
    rows = nb * tt
    m3 = mod_ref[...]
    x = x_ref[...].reshape(rows, d)
    h = _rms(x, g_ref[...]) * (1.0 + _per_row(m3, d, 2 * d, tt)) + _per_row(m3, 0, d, tt)
    hb = h.astype(BF16)
    u_ref[...] = _to_time_major(jnp.dot(hb, w_ref[:, 0:sw], preferred_element_type=F32), nb, tt)
    cb = jnp.dot(hb, w_ref[:, sw:2 * sw], preferred_element_type=F32)
    cc = jnp.dot(hb, w_ref[:, 2 * sw:3 * sw], preferred_element_type=F32)
    cv = jnp.dot(hb, w_ref[:, 3 * sw:4 * sw], preferred_element_type=F32)
    ccv = cc * cv
    col = lax.broadcasted_iota(I32, ccv.shape, 0) % GRID_W
    prev = jnp.where(col == 0, 0.0, pltpu.roll(ccv, 1, axis=0))
    nxt = jnp.where(col == GRID_W - 1, 0.0, pltpu.roll(ccv, rows - 1, axis=0))
    cw = cw_ref[...]
    conv = prev * cw[0:1, :] + ccv * cw[1:2, :] + nxt * cw[2:3, :]
    y_conv = _dot(cb * conv, wco_ref[...])
    ga = jnp.dot(hb, w_ref[:, 4 * sw:4 * sw + d], preferred_element_type=F32)
    gb = jnp.dot(hb, w_ref[:, 4 * sw + d:4 * sw + 2 * d], preferred_element_type=F32)
    sga_ref[...] = jax.nn.sigmoid(ga).reshape(nb, tt, d)
    gbt_ref[...] = (jax.nn.sigmoid(gb) * y_conv).reshape(nb, tt, d)


def _in_proj(x, mod3, g1n, w_in_b, conv_w, w_conv_out_b):
    b, l, d = x.shape
    sw = conv_w.shape[1]
    tt = TIME_TILE
    kern = functools.partial(_in_proj_kernel, d=d, sw=sw)
    tok = pl.BlockSpec((b, tt, d), lambda j: (0, j, 0))
    return pl.pallas_call(
        kern,
        grid=(l // tt,),
        in_specs=[tok,
                  pl.BlockSpec((b, 1, mod3.shape[2]), lambda j: (0, 0, 0)),
                  _full(g1n), _full(w_in_b), _full(conv_w), _full(w_conv_out_b)],
        out_specs=[pl.BlockSpec((tt * b, sw), lambda j: (j, 0)), tok, tok],
        out_shape=[jax.ShapeDtypeStruct((l * b, sw), F32),
                   jax.ShapeDtypeStruct((b, l, d), F32),
                   jax.ShapeDtypeStruct((b, l, d), F32)],
        compiler_params=_params("arbitrary"),
        name="in_proj",
    )(x, mod3, g1n, w_in_b, conv_w, w_conv_out_b)


def _ctx_proj_kernel(x_ref, mod_ref, g_ref, w_ref, u_ref, *, d):
    nb, tt, _ = x_ref.shape
    m = mod_ref[0]
    x = x_ref[...].reshape(nb * tt, d)
    h = _rms(x, g_ref[...]) * (1.0 + m[:, d:2 * d]) + m[:, 0:d]
    u_ref[...] = _to_time_major(_dot(h, w_ref[...]), nb, tt)


def _ctx_proj(ctx, mod3, ctx_row, g1n, w_u_b):
    b, lc, d = ctx.shape
    sw = w_u_b.shape[1]
    tt = TIME_TILE
    kern = functools.partial(_ctx_proj_kernel, d=d)
    return pl.pallas_call(
        kern,
        grid=(lc // tt,),
        in_specs=[pl.BlockSpec((b, tt, d), lambda j: (0, j, 0)),
                  pl.BlockSpec((1, 1, mod3.shape[2]), lambda j: (ctx_row, 0, 0)),
                  _full(g1n), _full(w_u_b)],
        out_specs=pl.BlockSpec((tt * b, sw), lambda j: (j, 0)),
        out_shape=jax.ShapeDtypeStruct((lc * b, sw), F32),
        compiler_params=_params("arbitrary"),
        name="ctx_proj",
    )(ctx, mod3, g1n, w_u_b)


def _scan_kernel(uc_ref, ul_ref, are_ref, aim_ref, bre_ref, bim_ref, cre_ref, cim_ref, y_ref,
                 sre_ref, sim_ref, cre_s, cim_s, *, n_ctx, nb):
    dirn = pl.program_id(0)
    j = pl.program_id(1)
    half_c = bre_ref.shape[2]
    half_s = bre_ref.shape[3]
    n_half = bre_ref.shape[1]
    steps = SCAN_CHUNK

    @pl.when(j == 0)
    def _():
        cre_s[...] = jnp.zeros_like(cre_s)
        cim_s[...] = jnp.zeros_like(cim_s)

    u = jnp.where(j < n_ctx, uc_ref[...], ul_ref[...]).astype(BF16)
    for h in range(n_half):
        uh = u[:, h * half_c:(h + 1) * half_c]
        sre_ref[:, h * half_s:(h + 1) * half_s] = jnp.dot(uh, bre_ref[0, h], preferred_element_type=F32)
        sim_ref[:, h * half_s:(h + 1) * half_s] = jnp.dot(uh, bim_ref[0, h], preferred_element_type=F32)

    lanes = sre_ref.shape[1]
    for q in range(lanes // SCAN_SLAB):
        ls = slice(q * SCAN_SLAB, (q + 1) * SCAN_SLAB)
        a_re = jnp.broadcast_to(are_ref[0, :, ls], (nb, SCAN_SLAB))
        a_im = jnp.broadcast_to(aim_ref[0, :, ls], (nb, SCAN_SLAB))

        def body(i, carry, ls=ls, a_re=a_re, a_im=a_im):
            s_re, s_im = carry
            for k in range(SUBLANES):
                t = i * SUBLANES + k
                t = jnp.where(dirn == 0, t, steps - 1 - t)
                r0 = pl.multiple_of(t * nb, nb)
                b_re = sre_ref[pl.ds(r0, nb), ls]
                b_im = sim_ref[pl.ds(r0, nb), ls]
                n_re = a_re * s_re - a_im * s_im + b_re
                n_im = a_re * s_im + a_im * s_re + b_im
                sre_ref[pl.ds(r0, nb), ls] = n_re
                sim_ref[pl.ds(r0, nb), ls] = n_im
                s_re, s_im = n_re, n_im
            return s_re, s_im

        s_re, s_im = lax.fori_loop(0, steps // SUBLANES, body, (cre_s[:, ls], cim_s[:, ls]))
        cre_s[:, ls] = s_re
        cim_s[:, ls] = s_im

    @pl.when(j >= n_ctx)
    def _():
        half_o = cre_ref.shape[3]
        for h in range(n_half):
            s_r = sre_ref[:, h * half_s:(h + 1) * half_s].astype(BF16)
            s_i = sim_ref[:, h * half_s:(h + 1) * half_s].astype(BF16)
            y_ref[0, :, h * half_o:(h + 1) * half_o] = (
                jnp.dot(s_r, cre_ref[0, h], preferred_element_type=F32)
                + jnp.dot(s_i, cim_ref[0, h], preferred_element_type=F32))


def _s5_scan(u_ctx, u_lat, a_re, a_im, b_re, b_im, c_re, c_imn, nb):
    rows_c, sw = u_ctx.shape
    rows_l = u_lat.shape[0]
    rc = SCAN_CHUNK * nb
    n_ctx = rows_c // rc
    n_lat = rows_l // rc
    lanes = a_re.shape[2]

    def ctx_idx(d, j):
        jj = jnp.minimum(j, n_ctx - 1)
        return (jnp.where(d == 0, jj, n_ctx - 1 - jj), 0)

    def lat_idx(d, j):
        jj = jnp.maximum(j - n_ctx, 0)
        return (jnp.where(d == 0, jj, n_lat - 1 - jj), 0)

    def per_dir4(d, j):
        return (d, 0, 0, 0)

    kern = functools.partial(_scan_kernel, n_ctx=n_ctx, nb=nb)
    return pl.pallas_call(
        kern,
        grid=(2, n_ctx + n_lat),
        in_specs=[pl.BlockSpec((rc, sw), ctx_idx),
                  pl.BlockSpec((rc, sw), lat_idx),
                  pl.BlockSpec((1, 1, lanes), lambda d, j: (d, 0, 0)),
                  pl.BlockSpec((1, 1, lanes), lambda d, j: (d, 0, 0)),
                  pl.BlockSpec((1,) + b_re.shape[1:], per_dir4),
                  pl.BlockSpec((1,) + b_im.shape[1:], per_dir4),
                  pl.BlockSpec((1,) + c_re.shape[1:], per_dir4),
                  pl.BlockSpec((1,) + c_imn.shape[1:], per_dir4)],
        out_specs=pl.BlockSpec((1, rc, sw), lambda d, j: (d,) + lat_idx(d, j)),
        out_shape=jax.ShapeDtypeStruct((2, rows_l, sw), F32),
        scratch_shapes=[pltpu.VMEM((rc, lanes), F32), pltpu.VMEM((rc, lanes), F32),
                        pltpu.VMEM((nb, lanes), F32), pltpu.VMEM((nb, lanes), F32)],
        compiler_params=_params("arbitrary", "arbitrary"),
        name="s5_scan",
    )(u_ctx, u_lat, a_re, a_im, b_re, b_im, c_re, c_imn)


def _ssm_operators(lam_re, lam_im, log_dt, b_re, b_im, c_re, c_im):
    dt = jnp.exp(log_dt)[..., None]
    mag = jnp.exp(lam_re * dt)
    a_re = mag * jnp.cos(lam_im * dt)
    a_im = mag * jnp.sin(lam_im * dt)
    den = lam_re * lam_re + lam_im * lam_im
    k_re = ((a_re - 1.0) * lam_re + a_im * lam_im) / den
    k_im = (a_im * lam_re - (a_re - 1.0) * lam_im) / den
    bb_re = k_re[..., None] * b_re - k_im[..., None] * b_im
    bb_im = k_re[..., None] * b_im + k_im[..., None] * b_re
    nd, g, p, q = bb_re.shape
    halves = 2
    gh = g // halves

    def block_diag(blocks, rows_per, cols_per):
        tall = blocks.reshape(nd, halves, gh * rows_per, cols_per)
        wide = jnp.tile(tall, (1, 1, 1, gh))
        row_g = lax.broadcasted_iota(I32, wide.shape, 2) // rows_per
        col_g = lax.broadcasted_iota(I32, wide.shape, 3) // cols_per
        return jnp.where(row_g == col_g, wide, 0.0).astype(BF16)

    def drive(bb):
        return block_diag(bb.reshape(nd, halves, gh, p, q).transpose(0, 1, 2, 4, 3), q, p)

    def readout(cc):
        return block_diag(cc.reshape(nd, halves, gh, q, p).transpose(0, 1, 2, 4, 3), p, q)

    return (a_re.reshape(nd, 1, g * p), a_im.reshape(nd, 1, g * p),
            drive(bb_re), drive(bb_im), readout(c_re), readout(-c_im))


def _mixer_kernel(y_ref, u_ref, sga_ref, gbt_ref, x_ref, mod_ref, d_ref, wglu_ref, wso_ref, wo_ref, o_ref,
                  *, d, sw):
    nb, tt, _ = x_ref.shape
    rows = nb * tt
    y = _to_batch_major(d_ref[...] * u_ref[...] + y_ref[0] + y_ref[1], nb, tt)
    v = _dot(jax.nn.gelu(y), wglu_ref[...])
    ys = v[:, 0:sw] * jax.nn.sigmoid(v[:, sw:2 * sw])
    y_a = _dot(ys, wso_ref[...])
    merged = sga_ref[...].reshape(rows, d) * y_a + gbt_ref[...].reshape(rows, d)
    o = _dot(merged, wo_ref[...])
    g1 = _per_row(mod_ref[...], 2 * d, 3 * d, tt)
    o_ref[...] = (x_ref[...].reshape(rows, d) + g1 * o).reshape(nb, tt, d)


def _mixer_out(y2, u_lat, sga, gbt, x, mod3, ssm_d, w_glu_b, w_ssm_out_b, w_o_b):
    b, l, d = x.shape
    sw = ssm_d.shape[1]
    tt = TIME_TILE
    kern = functools.partial(_mixer_kernel, d=d, sw=sw)
    tok = pl.BlockSpec((b, tt, d), lambda j: (0, j, 0))
    return pl.pallas_call(
        kern,
        grid=(l // tt,),
        in_specs=[pl.BlockSpec((2, tt * b, sw), lambda j: (0, j, 0)),
                  pl.BlockSpec((tt * b, sw), lambda j: (j, 0)),
                  tok, tok, tok,
                  pl.BlockSpec((b, 1, mod3.shape[2]), lambda j: (0, 0, 0)),
                  _full(ssm_d), _full(w_glu_b), _full(w_ssm_out_b), _full(w_o_b)],
        out_specs=tok,
        out_shape=jax.ShapeDtypeStruct((b, l, d), F32),
        compiler_params=_params("arbitrary"),
        name="mixer_out",
    )(y2, u_lat, sga, gbt, x, mod3, ssm_d, w_glu_b, w_ssm_out_b, w_o_b)


def _split_bf16(a):
    hi = a.astype(BF16)
    lo = (a - hi.astype(F32)).astype(BF16)
    return hi, lo


def _norm2(xl, m, g, d):
    return _rms(xl, g) * (1.0 + m[:, 4 * d:5 * d]) + m[:, 3 * d:4 * d]


def _shared_kernel(xl_ref, mod_ref, g_ref, wsg_ref, wsu_ref, wsd_ref, base_ref, *, d):
    m = mod_ref[0]
    xl = xl_ref[...]
    hb = _norm2(xl, m, g_ref[...], d).astype(BF16)
    sg = jnp.dot(hb, wsg_ref[...], preferred_element_type=F32)
    su = jnp.dot(hb, wsu_ref[...], preferred_element_type=F32)
    shared = _dot(sg * jax.nn.sigmoid(sg) * su, wsd_ref[...])
    base_ref[...] = xl + m[:, 5 * d:6 * d] * shared


def _shared(xl2, mod3, tiles_per_batch, g2n, ws_gate_b, ws_up_b, ws_down_b):
    n, d = xl2.shape
    t = TOKEN_TILE
    tok = pl.BlockSpec((t, d), lambda i: (i, 0))
    return pl.pallas_call(
        functools.partial(_shared_kernel, d=d),
        grid=(n // t,),
        in_specs=[tok,
                  pl.BlockSpec((1, 1, mod3.shape[2]), lambda i: (i // tiles_per_batch, 0, 0)),
                  _full(g2n), _full(ws_gate_b), _full(ws_up_b), _full(ws_down_b)],
        out_specs=tok,
        out_shape=jax.ShapeDtypeStruct((n, d), F32),
        compiler_params=_params("arbitrary"),
        name="shared",
    )(xl2, mod3, g2n, ws_gate_b, ws_up_b, ws_down_b)


def _route_kernel(xl_ref, mod_ref, g_ref, whi_ref, wlo_ref, rb_ref,
                  h2_ref, eidx_ref, wts_ref, rank_ref, cnt_ref, carry_ref, *, d):
    i = pl.program_id(0)
    t = xl_ref.shape[0]
    h2 = _norm2(xl_ref[...], mod_ref[0], g_ref[...], d)
    _store_packed(h2_ref, h2)

    h_hi, h_lo = _split_bf16(h2)
    w_hi, w_lo = whi_ref[...], wlo_ref[...]
    nt = (((1,), (1,)), ((), ()))
    logits = (lax.dot_general(w_hi, h_hi, nt, preferred_element_type=F32)
              + lax.dot_general(w_hi, h_lo, nt, preferred_element_type=F32)
              + lax.dot_general(w_lo, h_hi, nt, preferred_element_type=F32))
    scores = jax.nn.sigmoid(logits)
    choice = scores + rb_ref[...]

    epg = EXPERTS_PER_GROUP
    gi = lax.broadcasted_iota(I32, (epg, t), 0)
    gs = []
    for g in range(N_EXPERT_GROUPS):
        seg = choice[g * epg:(g + 1) * epg, :]
        m1 = jnp.max(seg, axis=0, keepdims=True)
        i1 = jnp.min(jnp.where(seg == m1, gi, epg), axis=0, keepdims=True)
        m2 = jnp.max(jnp.where(gi == i1, -jnp.inf, seg), axis=0, keepdims=True)
        gs.append(m1 + m2)
    masked = []
    for g in range(N_EXPERT_GROUPS):
        beat = jnp.zeros((1, t), I32)
        for g2 in range(N_EXPERT_GROUPS):
            if g2 < g:
                beat = beat + (gs[g2] >= gs[g]).astype(I32)
            elif g2 > g:
                beat = beat + (gs[g2] > gs[g]).astype(I32)
        keep = beat < TOP_K_GROUPS
        masked.append(jnp.where(keep, choice[g * epg:(g + 1) * epg, :], -jnp.inf))
    cur = jnp.concatenate(masked, axis=0)

    ei_all = lax.broadcasted_iota(I32, (N_EXPERTS, t), 0)
    picks, raw = [], []
    onehot = jnp.zeros((N_EXPERTS, t), F32)
    for _ in range(TOP_K):
        mx = jnp.max(cur, axis=0, keepdims=True)
        ei = jnp.min(jnp.where(cur == mx, ei_all, N_EXPERTS), axis=0, keepdims=True)
        hit = ei_all == ei
        raw.append(jnp.sum(jnp.where(hit, scores, 0.0), axis=0, keepdims=True))
        cur = jnp.where(hit, -jnp.inf, cur)
        onehot = jnp.where(hit, 1.0, onehot)
        picks.append(ei)
    tot = raw[0]
    for k in range(1, TOP_K):
        tot = tot + raw[k]

    @pl.when(i == 0)
    def _():
        carry_ref[...] = jnp.zeros_like(carry_ref)

    upper = (lax.broadcasted_iota(I32, (t, t), 0) < lax.broadcasted_iota(I32, (t, t), 1)).astype(BF16)
    before = jnp.dot(onehot.astype(BF16), upper, preferred_element_type=F32) + carry_ref[:, 0:1]
    for k in range(TOP_K):
        eidx_ref[k:k + 1, :] = picks[k]
        wts_ref[k:k + 1, :] = raw[k] / tot * ROUTE_SCALE
        rk = jnp.sum(jnp.where(ei_all == picks[k], before, 0.0), axis=0, keepdims=True)
        rank_ref[k:k + 1, :] = rk.astype(I32)
    carry_ref[...] = carry_ref[...] + jnp.sum(onehot, axis=1, keepdims=True)
    cnt_ref[...] = carry_ref[...]


def _route(xl2, mod3, tiles_per_batch, g2n, w_router_t, router_bias):
    n, d = xl2.shape
    t = TOKEN_TILE
    e = w_router_t.shape[0]
    w_top = lax.bitcast_convert_type(lax.bitcast_convert_type(w_router_t, U32) & jnp.uint32(0xFFFF0000), F32)
    w_hi, w_lo = w_top.astype(BF16), (w_router_t - w_top).astype(BF16)
    kern = functools.partial(_route_kernel, d=d)
    tok = pl.BlockSpec((t, d), lambda i: (i, 0))
    small = pl.BlockSpec((TOP_K, t), lambda i: (0, i))
    return pl.pallas_call(
        kern,
        grid=(n // t,),
        in_specs=[tok,
                  pl.BlockSpec((1, 1, mod3.shape[2]), lambda i: (i // tiles_per_batch, 0, 0)),
                  _full(g2n), _full(w_hi), _full(w_lo), _full(router_bias)],
        out_specs=[pl.BlockSpec((t * PACK_SUBLANES, LANES), lambda i: (i, 0)), small, small, small,
                   pl.BlockSpec((e, LANES), lambda i: (0, 0))],
        out_shape=[jax.ShapeDtypeStruct((n * PACK_SUBLANES, LANES), U32),
                   jax.ShapeDtypeStruct((TOP_K, n), I32),
                   jax.ShapeDtypeStruct((TOP_K, n), F32),
                   jax.ShapeDtypeStruct((TOP_K, n), I32),
                   jax.ShapeDtypeStruct((e, LANES), F32)],
        scratch_shapes=[pltpu.VMEM((e, LANES), F32)],
        compiler_params=_params("arbitrary"),
        name="route",
    )(xl2, mod3, g2n, w_hi, w_lo, router_bias)


def _dest_kernel(eidx_ref, rank_ref, start_ref, o_ref):
    t = eidx_ref.shape[1]
    ei_all = lax.broadcasted_iota(I32, (N_EXPERTS, t), 0)
    st = start_ref[:, 0:1]
    for k in range(TOP_K):
        hit = ei_all == eidx_ref[k:k + 1, :]
        o_ref[k:k + 1, :] = jnp.sum(jnp.where(hit, st, 0), axis=0, keepdims=True) + rank_ref[k:k + 1, :]


def _dest_rows(eidx, rank, starts_b):
    n = eidx.shape[1]
    t = 1024
    small = pl.BlockSpec((TOP_K, t), lambda i: (0, i))
    return pl.pallas_call(
        _dest_kernel,
        grid=(n // t,),
        in_specs=[small, small, _full(starts_b)],
        out_specs=small,
        out_shape=jax.ShapeDtypeStruct((TOP_K, n), I32),
        compiler_params=_params("arbitrary"),
        name="dest",
    )(eidx, rank, starts_b)


def _sc_workers():
    info = plsc.get_sparse_core_info()
    return info.num_cores, info.num_cores * info.num_subcores


def _sc_dispatch(dest_flat, src3, out_rows):
    n = src3.shape[0]
    n_cores, n_workers = _sc_workers()
    per_worker = n // n_workers
    assert per_worker % SC_CHUNK == 0
    mesh = plsc.VectorSubcoreMesh(core_axis_name="c", subcore_axis_name="s")

    @functools.partial(
        pl.kernel, mesh=mesh,
        out_type=jax.ShapeDtypeStruct((out_rows,) + src3.shape[1:], src3.dtype),
        scratch_types=[pltpu.VMEM((SC_CHUNK,) + src3.shape[1:], src3.dtype)]
        + [pltpu.VMEM((SC_CHUNK,), I32)] * TOP_K,
        name="dispatch",
    )
    def run(dest_hbm, src_hbm, out_hbm, rows_v, *idx_v):
        worker = lax.axis_index("s") * n_cores + lax.axis_index("c")

        @pl.loop(0, per_worker // SC_CHUNK)
        def _(j):
            base = worker * per_worker + j * SC_CHUNK
            pltpu.sync_copy(src_hbm.at[pl.ds(base, SC_CHUNK)], rows_v)
            for k in range(TOP_K):
                pltpu.sync_copy(dest_hbm.at[pl.ds(k * n + base, SC_CHUNK)], idx_v[k])
            for k in range(TOP_K):
                pltpu.sync_copy(rows_v, out_hbm.at[idx_v[k]])

    return run(dest_flat, src3)


def _sc_combine(dest_flat, ys3, n):
    n_cores, n_workers = _sc_workers()
    per_worker = n // n_workers
    assert per_worker % SC_CHUNK == 0
    mesh = plsc.VectorSubcoreMesh(core_axis_name="c", subcore_axis_name="s")

    @functools.partial(
        pl.kernel, mesh=mesh,
        out_type=jax.ShapeDtypeStruct((TOP_K * n,) + ys3.shape[1:], ys3.dtype),
        scratch_types=[pltpu.VMEM((SC_CHUNK,) + ys3.shape[1:], ys3.dtype), pltpu.VMEM((SC_CHUNK,), I32)],
        name="combine",
    )
    def run(dest_hbm, ys_hbm, out_hbm, rows_v, idx_v):
        worker = lax.axis_index("s") * n_cores + lax.axis_index("c")

        @pl.loop(0, per_worker // SC_CHUNK)
        def _(j):
            base = worker * per_worker + j * SC_CHUNK
            for k in range(TOP_K):
                pltpu.sync_copy(dest_hbm.at[pl.ds(k * n + base, SC_CHUNK)], idx_v)
                pltpu.sync_copy(ys_hbm.at[idx_v], rows_v)
                pltpu.sync_copy(rows_v, out_hbm.at[pl.ds(k * n + base, SC_CHUNK)])

    return run(dest_flat, ys3)


def _chunk_metadata(counts, n_rows):
    ch = EXPERT_CHUNK
    n_ch = (counts + ch - 1) // ch
    cum = jnp.cumsum(n_ch)
    total = cum[-1]
    max_chunks = n_rows // ch + N_EXPERTS
    i = jnp.arange(max_chunks, dtype=I32)
    e = jnp.sum((cum[None, :] <= i[:, None]).astype(I32), axis=1)
    e_last = jnp.max(jnp.where(counts > 0, jnp.arange(N_EXPERTS, dtype=I32), 0))
    exp = jnp.where(i < total, jnp.minimum(e, N_EXPERTS - 1), e_last).astype(I32)
    newexp = jnp.concatenate([jnp.ones((1,), I32), (exp[1:] != exp[:-1]).astype(I32)])
    change_at = jnp.where(newexp == 1, i, max_chunks)
    nxt_change = lax.cummin(jnp.concatenate([change_at[1:], jnp.array([max_chunks], I32)]), reverse=True)
    nexp = jnp.where(nxt_change < max_chunks, exp[jnp.minimum(nxt_change, max_chunks - 1)], -1).astype(I32)
    starts = (cum - n_ch) * ch
    return starts.astype(I32), (exp, newexp, nexp, total.astype(I32).reshape(1))


def _expert_kernel(exp_ref, newexp_ref, nexp_ref, total_ref, xs_hbm, wg_hbm, wu_hbm, wd_hbm, after_hbm, ys_hbm,
                   xbuf, ybuf, wgf, wuf, wdf, wgb, wub, wdb, sem_x, sem_y, sem_w):
    del after_hbm
    ch = EXPERT_CHUNK
    nbuf = EXPERT_RING
    rows = ch * PACK_SUBLANES
    total = total_ref[0]

    def x_copy(i, slot):
        r0 = pl.multiple_of(i * rows, rows)
        return pltpu.make_async_copy(xs_hbm.at[pl.ds(r0, rows)], xbuf.at[slot], sem_x.at[slot])

    def y_copy(i, slot):
        r0 = pl.multiple_of(i * rows, rows)
        return pltpu.make_async_copy(ybuf.at[slot], ys_hbm.at[pl.ds(r0, rows)], sem_y.at[slot])

    def weight_copies(e):
        return (pltpu.make_async_copy(wg_hbm.at[e], wgf, sem_w.at[0]),
                pltpu.make_async_copy(wu_hbm.at[e], wuf, sem_w.at[1]),
                pltpu.make_async_copy(wd_hbm.at[e], wdf, sem_w.at[2]))

    for cp in weight_copies(exp_ref[0]):
        cp.start()
    for b in range(nbuf - 1):
        @pl.when(b < total)
        def _(b=b):
            x_copy(b, b).start()

    def chunk(i, carry):
        slot = lax.rem(i, nbuf)
        x_copy(i, slot).wait()
        ahead = i + (nbuf - 1)

        @pl.when(ahead < total)
        def _():
            x_copy(ahead, lax.rem(ahead, nbuf)).start()

        @pl.when(newexp_ref[i] == 1)
        def _():
            for cp in weight_copies(exp_ref[i]):
                cp.wait()
            wgb[...] = wgf[...].astype(BF16)
            wub[...] = wuf[...].astype(BF16)
            wdb[...] = wdf[...].astype(BF16)

            @pl.when(nexp_ref[i] >= 0)
            def _():
                for cp in weight_copies(nexp_ref[i]):
                    cp.start()

        @pl.when(i >= nbuf)
        def _():
            y_copy(i, slot).wait()

        xr = xbuf.at[slot]
        yr = ybuf.at[slot]
        x = _load_packed(lambda sl: xr[sl, :], ch).astype(BF16)
        g = jnp.dot(x, wgb[...], preferred_element_type=F32)
        u = jnp.dot(x, wub[...], preferred_element_type=F32)
        y = jnp.dot((g * jax.nn.sigmoid(g) * u).astype(BF16), wdb[...], preferred_element_type=F32)
        _store_packed(yr, y)
        y_copy(i, slot).start()
        return carry

    lax.fori_loop(0, total, chunk, 0)

    for b in range(nbuf):
        @pl.when(b < total)
        def _(b=b):
            y_copy(0, b).wait()


def _experts(meta, xs, w_gate, w_up, w_down, after):
    e, d, f = w_gate.shape
    hbm = pl.BlockSpec(memory_space=pl.ANY)
    rows = EXPERT_CHUNK * PACK_SUBLANES
    grid_spec = pltpu.PrefetchScalarGridSpec(
        num_scalar_prefetch=len(meta), grid=(1,),
        in_specs=[hbm, hbm, hbm, hbm, hbm], out_specs=hbm,
        scratch_shapes=[pltpu.VMEM((EXPERT_RING, rows, LANES), U32), pltpu.VMEM((EXPERT_RING, rows, LANES), U32),
                        pltpu.VMEM((d, f), F32), pltpu.VMEM((d, f), F32), pltpu.VMEM((f, d), F32),
                        pltpu.VMEM((d, f), BF16), pltpu.VMEM((d, f), BF16), pltpu.VMEM((f, d), BF16),
                        pltpu.SemaphoreType.DMA((EXPERT_RING,)), pltpu.SemaphoreType.DMA((EXPERT_RING,)),
                        pltpu.SemaphoreType.DMA((3,))])
    return pl.pallas_call(
        _expert_kernel,
        grid_spec=grid_spec,
        out_shape=jax.ShapeDtypeStruct(xs.shape, xs.dtype),
        compiler_params=_params("arbitrary"),
        name="experts",
    )(*meta, xs, w_gate, w_up, w_down, after)


def _final_kernel(base_ref, yt_ref, wt_ref, mod_ref, g_ref, *rest, d):
    o_ref = rest[-1]
    t = base_ref.shape[0]
    w = wt_ref[...]
    routed = w[:, 0:1] * _load_packed(lambda sl: yt_ref[0, sl, :], t)
    for k in range(1, TOP_K):
        routed = routed + w[:, k:k + 1] * _load_packed(lambda sl, k=k: yt_ref[k, sl, :], t)
    g2 = mod_ref[0][:, 5 * d:6 * d]
    o_ref[...] = _rms(base_ref[...] + g2 * routed, g_ref[...])


def _final(base, ytok3, wts_t, mod3, tiles_per_batch, final_g, part, n_parts, prev_out):
    n, d = base.shape
    t = TOKEN_TILE
    tiles = n // t // n_parts
    off = part * tiles
    kern = functools.partial(_final_kernel, d=d)
    tok = pl.BlockSpec((t, d), lambda i: (off + i, 0))
    in_specs = [tok,
                pl.BlockSpec((TOP_K, t * PACK_SUBLANES, LANES), lambda i: (0, i, 0)),
                pl.BlockSpec((t, TOP_K), lambda i: (off + i, 0)),
                pl.BlockSpec((1, 1, mod3.shape[2]), lambda i: ((off + i) // tiles_per_batch, 0, 0)),
                _full(final_g)]
    args = [base, ytok3, wts_t, mod3, final_g]
    aliases = {}
    if prev_out is not None:
        in_specs.append(pl.BlockSpec(memory_space=pl.ANY))
        args.append(prev_out)
        aliases = {len(args) - 1: 0}
    return pl.pallas_call(
        kern,
        grid=(tiles,),
        in_specs=in_specs,
        out_specs=tok,
        out_shape=jax.ShapeDtypeStruct((n, d), F32),
        input_output_aliases=aliases,
        compiler_params=_params("arbitrary"),
        name="final",
    )(*args)


def kernel(x, c, ctx, c_ctx, w_mod, b_mod, norm1_g, norm2_g, w_in, ssm_lam_re, ssm_lam_im, ssm_log_dt, ssm_b_re, ssm_b_im, ssm_c_re, ssm_c_im, ssm_d, w_glu, w_ssm_out, conv_w, w_conv_out, w_o, w_router, router_bias, w_gate, w_up, w_down, ws_gate, ws_up, ws_down, final_g):
    b, l, d = x.shape
    n = b * l
    sw = ssm_d.shape[1]
    assert w_mod.shape[0] == 1, "single layer"
    assert b == SUBLANES and l % TOKEN_TILE == 0 and l % SCAN_CHUNK == 0 and TIME_TILE % GRID_W == 0

    mod_rows = 2 * SUBLANES
    c_all = jnp.concatenate([c, c_ctx[None, :], jnp.zeros((mod_rows - b - 1, d), F32)], axis=0)
    mod = _modulation(c_all, w_mod[0], b_mod[0])
    mod3 = mod.reshape(mod_rows, 1, mod.shape[1])

    w_in_b = w_in[0].astype(BF16)
    g1n = norm1_g[0].reshape(1, d)
    u_lat, sga, gbt = _in_proj(x, mod3, g1n, w_in_b, conv_w[0], w_conv_out[0].astype(BF16))
    u_ctx = _ctx_proj(ctx, mod3, b, g1n, w_in_b[:, :sw])

    ops = _ssm_operators(ssm_lam_re[0], ssm_lam_im[0], ssm_log_dt[0], ssm_b_re[0], ssm_b_im[0],
                         ssm_c_re[0], ssm_c_im[0])
    y2 = _s5_scan(u_ctx, u_lat, *ops, nb=b)

    xl = _mixer_out(y2, u_lat, sga, gbt, x, mod3, ssm_d[0].reshape(1, sw), w_glu[0].astype(BF16),
                    w_ssm_out[0].astype(BF16), w_o[0].astype(BF16))

    tiles_per_batch = l // TOKEN_TILE
    xl2 = xl.reshape(n, d)
    g2n = norm2_g[0].reshape(1, d)
    h2, eidx, wts, rank, cnt = _route(xl2, mod3, tiles_per_batch, g2n, w_router[0].T,
                                      router_bias[0].reshape(N_EXPERTS, 1))

    counts = cnt[:, 0].astype(I32)
    n_rows = n * TOP_K
    starts, meta = _chunk_metadata(counts, n_rows)
    buf_rows = n_rows + N_EXPERTS * EXPERT_CHUNK
    dest = _dest_rows(eidx, rank, jnp.broadcast_to(starts[:, None], (N_EXPERTS, LANES)))

    xs = _sc_dispatch(dest.reshape(TOP_K * n), h2.reshape(n, PACK_SUBLANES, LANES), buf_rows)
    base = _shared(xl2, mod3, tiles_per_batch, g2n, ws_gate[0].astype(BF16), ws_up[0].astype(BF16),
                   ws_down[0].astype(BF16))
    ys = _experts(meta, xs.reshape(buf_rows * PACK_SUBLANES, LANES), w_gate[0], w_up[0], w_down[0], base)
    ys3 = ys.reshape(buf_rows, PACK_SUBLANES, LANES)

    n_part = n // COMBINE_PARTS
    wts_t = wts.T
    out = None
    for part in range(COMBINE_PARTS):
        dest_part = dest[:, part * n_part:(part + 1) * n_part].reshape(TOP_K * n_part)
        ytok = _sc_combine(dest_part, ys3, n_part)
        out = _final(base, ytok.reshape(TOP_K, n_part * PACK_SUBLANES, LANES), wts_t, mod3, tiles_per_batch,
                     final_g.reshape(1, d), part, COMBINE_PARTS, out)
    return out.reshape(b, l, d)
```

```python
import functools

import jax
import jax.numpy as jnp
from jax import lax
from jax.experimental import pallas as pl
from jax.experimental.pallas import tpu as pltpu
from jax.experimental.pallas import tpu_sc as plsc

F32 = jnp.float32
BF16 = jnp.bfloat16
I32 = jnp.int32
U32 = jnp.uint32

EPS = 1e-6
GRID_W = 64
SSM_GROUP = 16
N_EXPERTS = 256
TOP_K = 8
N_EXPERT_GROUPS = 8
EXPERTS_PER_GROUP = N_EXPERTS // N_EXPERT_GROUPS
TOP_K_GROUPS = 4
ROUTE_SCALE = 2.5

SUBLANES = 8
LANES = 128
VMEM_LIMIT_BYTES = 48 * 1024 * 1024

TIME_TILE = 64
TOKEN_TILE = 256
TIME_BLOCK = 16
S5_CHUNK = 256
EXPERT_CHUNK = 256
EXPERT_RING = 4
SC_CHUNK = 128
COMBINE_PARTS = 4
PACK_SUBLANES = 4


def _dot(a, b):
    return jnp.dot(a.astype(BF16), b.astype(BF16), preferred_element_type=F32)


def _rms(xf, g):
    return xf * lax.rsqrt(jnp.mean(xf * xf, axis=-1, keepdims=True) + EPS) * g


def _params(*sem):
    return pltpu.CompilerParams(dimension_semantics=sem, vmem_limit_bytes=VMEM_LIMIT_BYTES)


def _full(a):
    return pl.BlockSpec(a.shape, lambda *_: (0,) * a.ndim)


def _pack_rows(v):
    half = v.shape[1] // 2
    lo = lax.bitcast_convert_type(v[:, :half].astype(BF16).astype(F32), U32) >> 16
    hi = lax.bitcast_convert_type(v[:, half:].astype(BF16).astype(F32), U32) & jnp.uint32(0xFFFF0000)
    return lo | hi


def _unpack_lo(w):
    return lax.bitcast_convert_type(w << 16, F32)


def _unpack_hi(w):
    return lax.bitcast_convert_type(w & jnp.uint32(0xFFFF0000), F32)


def _store_packed(ref, v):
    t = v.shape[0]
    w = _pack_rows(v)
    for c in range(PACK_SUBLANES):
        ref[pl.ds(c, t, stride=PACK_SUBLANES), :] = w[:, c * LANES:(c + 1) * LANES]


def _load_packed(load, t):
    ws = [load(pl.ds(c, t, stride=PACK_SUBLANES)) for c in range(PACK_SUBLANES)]
    return jnp.concatenate([_unpack_lo(w) for w in ws] + [_unpack_hi(w) for w in ws], axis=1)


def _mod_kernel(c_ref, w_ref, b_ref, o_ref):
    c = c_ref[...]
    o_ref[...] = _dot(c * jax.nn.sigmoid(c), w_ref[...]) + b_ref[...]


def _modulation(c_all, w_mod, b_mod):
    rows, d = c_all.shape
    cols = w_mod.shape[1]
    blk = 1536
    return pl.pallas_call(
        _mod_kernel,
        grid=(cols // blk,),
        in_specs=[pl.BlockSpec((rows, d), lambda j: (0, 0)),
                  pl.BlockSpec((d, blk), lambda j: (0, j)),
                  pl.BlockSpec((1, blk), lambda j: (0, j))],
        out_specs=pl.BlockSpec((rows, blk), lambda j: (0, j)),
        out_shape=jax.ShapeDtypeStruct((rows, cols), F32),
        compiler_params=_params("arbitrary"),
        name="mod",
    )(c_all, w_mod, b_mod.reshape(1, cols))


def _per_row(m3, lo, hi, tt):
    nb = m3.shape[0]
    return jnp.broadcast_to(m3[:, :, lo:hi], (nb, tt, hi - lo)).reshape(nb * tt, hi - lo)


def _to_time_major(val, nb, tt):
    c = val.shape[1]
    return pltpu.einshape("btc->tbc", val.reshape(nb, tt, c)).reshape(nb * tt, c)


def _to_batch_major(val, nb, tt):
    c = val.shape[1]
    return pltpu.einshape("tbc->btc", val.reshape(tt, nb, c)).reshape(nb * tt, c)


def _in_proj_kernel(x_ref, mod_ref, g_ref, w_ref, cw_ref, wco_ref, u_ref, sga_ref, gbt_ref, *, d, sw):
    nb, tt, _ = x_ref.shape
    rows = nb * tt
    m3 = mod_ref[...]
    x = x_ref[...].reshape(rows, d)
    h = _rms(x, g_ref[...]) * (1.0 + _per_row(m3, d, 2 * d, tt)) + _per_row(m3, 0, d, tt)
    hb = h.astype(BF16)
    u_ref[...] = _to_time_major(jnp.dot(hb, w_ref[:, 0:sw], preferred_element_type=F32), nb, tt)
    cb = jnp.dot(hb, w_ref[:, sw:2 * sw], preferred_element_type=F32)
    cc = jnp.dot(hb, w_ref[:, 2 * sw:3 * sw], preferred_element_type=F32)
    cv = jnp.dot(hb, w_ref[:, 3 * sw:4 * sw], preferred_element_type=F32)
    ccv = cc * cv
    col = lax.broadcasted_iota(I32, ccv.shape, 0) % GRID_W
    prev = jnp.where(col == 0, 0.0, pltpu.roll(ccv, 1, axis=0))
    nxt = jnp.where(col == GRID_W - 1, 0.0, pltpu.roll(ccv, rows - 1, axis=0))
    cw = cw_ref[...]
    conv = prev * cw[0:1, :] + ccv * cw[1:2, :] + nxt * cw[2:3, :]
    y_conv = _dot(cb * conv, wco_ref[...])
    ga = jnp.dot(hb, w_ref[:, 4 * sw:4 * sw + d], preferred_element_type=F32)
    gb = jnp.dot(hb, w_ref[:, 4 * sw + d:4 * sw + 2 * d], preferred_element_type=F32)
    sga_ref[...] = jax.nn.sigmoid(ga).reshape(nb, tt, d)
    gbt_ref[...] = (jax.nn.sigmoid(gb) * y_conv).reshape(nb, tt, d)


def _in_proj(x, mod3, g1n, w_in_b, conv_w, w_conv_out_b):
    b, l, d = x.shape
    sw = conv_w.shape[1]
    tt = TIME_TILE
    kern = functools.partial(_in_proj_kernel, d=d, sw=sw)
    tok = pl.BlockSpec((b, tt, d), lambda j: (0, j, 0))
    return pl.pallas_call(
        kern,
        grid=(l // tt,),
        in_specs=[tok,
                  pl.BlockSpec((b, 1, mod3.shape[2]), lambda j: (0, 0, 0)),
                  _full(g1n), _full(w_in_b), _full(conv_w), _full(w_conv_out_b)],
        out_specs=[pl.BlockSpec((tt * b, sw), lambda j: (j, 0)), tok, tok],
        out_shape=[jax.ShapeDtypeStruct((l * b, sw), F32),
                   jax.ShapeDtypeStruct((b, l, d), F32),
                   jax.ShapeDtypeStruct((b, l, d), F32)],
        compiler_params=_params("arbitrary"),
        name="in_proj",
    )(x, mod3, g1n, w_in_b, conv_w, w_conv_out_b)


def _ctx_proj_kernel(x_ref, mod_ref, g_ref, w_ref, u_ref, *, d):
    nb, tt, _ = x_ref.shape
    m = mod_ref[0]
    x = x_ref[...].reshape(nb * tt, d)
    h = _rms(x, g_ref[...]) * (1.0 + m[:, d:2 * d]) + m[:, 0:d]
    u_ref[...] = _to_time_major(_dot(h, w_ref[...]), nb, tt)


def _ctx_proj(ctx, mod3, ctx_row, g1n, w_u_b):
    b, lc, d = ctx.shape
    sw = w_u_b.shape[1]
    tt = TIME_TILE
    kern = functools.partial(_ctx_proj_kernel, d=d)
    return pl.pallas_call(
        kern,
        grid=(lc // tt,),
        in_specs=[pl.BlockSpec((b, tt, d), lambda j: (0, j, 0)),
                  pl.BlockSpec((1, 1, mod3.shape[2]), lambda j: (ctx_row, 0, 0)),
                  _full(g1n), _full(w_u_b)],
        out_specs=pl.BlockSpec((tt * b, sw), lambda j: (j, 0)),
        out_shape=jax.ShapeDtypeStruct((lc * b, sw), F32),
        compiler_params=_params("arbitrary"),
        name="ctx_proj",
    )(ctx, mod3, g1n, w_u_b)


def _blocked_operators(lam_re, lam_im, log_dt, b_re, b_im, c_re, c_im):
    dt = jnp.exp(log_dt)[..., None]
    k = jnp.arange(TIME_BLOCK + 1, dtype=F32)[:, None, None, None]
    mag = jnp.exp(lam_re * dt)[None] ** k
    ang = (lam_im * dt)[None] * k
    ak_re = mag * jnp.cos(ang)
    ak_im = mag * jnp.sin(ang)
    a_re, a_im = ak_re[1], ak_im[1]
    den = lam_re * lam_re + lam_im * lam_im
    k_re = ((a_re - 1.0) * lam_re + a_im * lam_im) / den
    k_im = (a_im * lam_re - (a_re - 1.0) * lam_im) / den
    bb_re = k_re[..., None] * b_re - k_im[..., None] * b_im
    bb_im = k_re[..., None] * b_im + k_im[..., None] * b_re
    nd, g, p, q = bb_re.shape

    ab_re = ak_re[..., None] * bb_re[None] - ak_im[..., None] * bb_im[None]
    ab_im = ak_re[..., None] * bb_im[None] + ak_im[..., None] * bb_re[None]
    m = (jnp.einsum('dgcp,kdgpe->kdgce', c_re, ab_re[:TIME_BLOCK]) - jnp.einsum('dgcp,kdgpe->kdgce', c_im, ab_im[:TIME_BLOCK]))
    s_i = jnp.arange(TIME_BLOCK)[:, None]
    t_i = jnp.arange(TIME_BLOCK)[None, :]
    lag_f = jnp.clip(t_i - s_i, 0, TIME_BLOCK - 1)
    lag_b = jnp.clip(s_i - t_i, 0, TIME_BLOCK - 1)
    tf = jnp.where((t_i >= s_i)[:, :, None, None, None], m[lag_f, 0], 0.0)
    tb_ = jnp.where((s_i >= t_i)[:, :, None, None, None], m[lag_b, 1], 0.0)
    tfb = (tf + tb_).transpose(2, 0, 4, 1, 3).reshape(g, TIME_BLOCK * q, TIME_BLOCK * q)

    inj_f_re = ab_re[TIME_BLOCK - 1 - jnp.arange(TIME_BLOCK), 0]
    inj_f_im = ab_im[TIME_BLOCK - 1 - jnp.arange(TIME_BLOCK), 0]
    inj_b_re = ab_re[jnp.arange(TIME_BLOCK), 1]
    inj_b_im = ab_im[jnp.arange(TIME_BLOCK), 1]

    def inj(x):
        return x.transpose(1, 0, 3, 2).reshape(g, TIME_BLOCK * q, p)
    bst = jnp.concatenate([inj(inj_f_re), inj(inj_f_im), inj(inj_b_re), inj(inj_b_im)], axis=2)

    def ca(c_r, c_i, a_r, a_i):
        return (c_r[None] * a_r[:, :, None, :] - c_i[None] * a_i[:, :, None, :],
                c_r[None] * a_i[:, :, None, :] + c_i[None] * a_r[:, :, None, :])
    kf = 1 + jnp.arange(TIME_BLOCK)
    kb = TIME_BLOCK - jnp.arange(TIME_BLOCK)
    cf_re, cf_im = ca(c_re[0], c_im[0], ak_re[kf, 0], ak_im[kf, 0])
    cb_re, cb_im = ca(c_re[1], c_im[1], ak_re[kb, 1], ak_im[kb, 1])

    def out(x):
        return x.transpose(1, 3, 0, 2).reshape(g, p, TIME_BLOCK * q)
    cst = jnp.concatenate([out(cf_re), -out(cf_im), out(cb_re), -out(cb_im)], axis=1)

    a16_re, a16_im = ak_re[TIME_BLOCK], ak_im[TIME_BLOCK]
    ax = jnp.concatenate([a16_re, a16_re], axis=-1).reshape(nd, 1, g * 2 * p)
    ay = jnp.concatenate([-a16_im, a16_im], axis=-1).reshape(nd, 1, g * 2 * p)
    return tfb.astype(BF16), bst.astype(BF16), cst.astype(BF16), ax, ay


def _lane_chunk(shape):
    return lax.broadcasted_iota(I32, shape, 1) // SSM_GROUP


def _regroup_kernel(uc_ref, ul_ref, bst_ref, xg_ref, e_ref, *, n_ctx, nb, groups, q):
    j = pl.program_id(0)
    rows_in = uc_ref.shape[0]
    nblk = rows_in // (TIME_BLOCK * nb)
    rows = nblk * nb
    u = jnp.where(j < n_ctx, uc_ref[...], ul_ref[...])
    u3 = u.reshape(nblk, TIME_BLOCK * nb, u.shape[1])
    xs = [u3[:, s * nb:(s + 1) * nb, :].reshape(rows, u.shape[1]) for s in range(TIME_BLOCK)]
    per_tile = LANES // q
    chunk = _lane_chunk((rows, LANES))
    for g in range(groups):
        jt, qq = divmod(g, per_tile)
        for h in range(TIME_BLOCK // per_tile):
            acc = jnp.zeros((rows, LANES), F32)
            for s in range(h * per_tile, (h + 1) * per_tile):
                pos = s % per_tile
                a = xs[s][:, jt * LANES:(jt + 1) * LANES]
                shift = ((pos - qq) % per_tile) * q
                r = pltpu.roll(a, shift, axis=1) if shift else a
                acc = jnp.where(chunk == pos, r, acc)
            xg_ref[:, (2 * g + h) * LANES:(2 * g + h + 1) * LANES] = acc.astype(BF16)
    gw = TIME_BLOCK * q
    for g in range(groups):
        e = jnp.dot(xg_ref[:, g * gw:(g + 1) * gw], bst_ref[g], preferred_element_type=F32)
        half = e.shape[1] // 2
        e_ref[0, :, g * half:(g + 1) * half] = e[:, :half]
        e_ref[1, :, g * half:(g + 1) * half] = e[:, half:]


def _s5_regroup(u_ctx, u_lat, bst, nb, tc):
    rows_c, sw = u_ctx.shape
    rows_l = u_lat.shape[0]
    rc = tc * nb
    n_ctx, n_lat = rows_c // rc, rows_l // rc
    groups, gw, p4 = bst.shape
    q = gw // TIME_BLOCK
    rows_out = (tc // TIME_BLOCK) * nb
    total_rows = (n_ctx + n_lat) * rows_out
    kern = functools.partial(_regroup_kernel, n_ctx=n_ctx, nb=nb, groups=groups, q=q)
    return pl.pallas_call(
        kern,
        grid=(n_ctx + n_lat,),
        in_specs=[pl.BlockSpec((rc, sw), lambda j: (jnp.minimum(j, n_ctx - 1), 0)),
                  pl.BlockSpec((rc, sw), lambda j: (jnp.maximum(j - n_ctx, 0), 0)),
                  pl.BlockSpec(bst.shape, lambda j: (0, 0, 0), pipeline_mode=pl.Buffered(1))],
        out_specs=[pl.BlockSpec((rows_out, groups * gw), lambda j: (j, 0)),
                   pl.BlockSpec((2, rows_out, groups * p4 // 2), lambda j: (0, j, 0))],
        out_shape=[jax.ShapeDtypeStruct((total_rows, groups * gw), BF16),
                   jax.ShapeDtypeStruct((2, total_rows, groups * p4 // 2), F32)],
        compiler_params=_params("arbitrary"),
        name="s5_regroup",
    )(u_ctx, u_lat, bst)


def _carry_kernel(e_ref, ax_ref, ay_ref, s_ref, st_ref, *, nb):
    dirn = pl.program_id(0)
    j = pl.program_id(1)
    rows = e_ref.shape[1]
    nblk = rows // nb
    lanes = e_ref.shape[2]

    @pl.when(j == 0)
    def _():
        st_ref[...] = jnp.zeros_like(st_ref)

    ax = jnp.broadcast_to(ax_ref[0], (nb, lanes))
    ay = jnp.broadcast_to(ay_ref[0], (nb, lanes))

    def body(i, s):
        blk = jnp.where(dirn == 0, i, nblk - 1 - i)
        r0 = pl.multiple_of(blk * nb, nb)
        s_ref[0, pl.ds(r0, nb), :] = s.astype(s_ref.dtype)
        swapped = jnp.concatenate([pltpu.roll(s[:, t * LANES:(t + 1) * LANES], LANES // 2, axis=1)
                                   for t in range(lanes // LANES)], axis=1)
        return ax * s + ay * swapped + e_ref[0, pl.ds(r0, nb), :]

    st_ref[...] = lax.fori_loop(0, nblk, body, st_ref[...])


def _s5_carry(e, ax, ay, nb, n_ctx_chunks, rows_per_chunk):
    nd, total_rows, lanes = e.shape
    n_chunks = total_rows // rows_per_chunk
    n_lat = n_chunks - n_ctx_chunks

    def idx(d, j):
        jc = jnp.minimum(j, n_ctx_chunks - 1)
        jl = jnp.maximum(j - n_ctx_chunks, 0)
        fwd = j
        bwd = jnp.where(j < n_ctx_chunks, n_ctx_chunks - 1 - jc, n_chunks - 1 - jl)
        return (d, jnp.where(d == 0, fwd, bwd), 0)

    kern = functools.partial(_carry_kernel, nb=nb)
    return pl.pallas_call(
        kern,
        grid=(nd, n_chunks),
        in_specs=[pl.BlockSpec((1, rows_per_chunk, lanes), idx),
                  pl.BlockSpec((1, 1, lanes), lambda d, j: (d, 0, 0)),
                  pl.BlockSpec((1, 1, lanes), lambda d, j: (d, 0, 0))],
        out_specs=pl.BlockSpec((1, rows_per_chunk, lanes), idx),
        out_shape=jax.ShapeDtypeStruct(e.shape, F32),
        scratch_shapes=[pltpu.VMEM((nb, lanes), F32)],
        compiler_params=_params("arbitrary", "arbitrary"),
        name="s5_carry",
    )(e, ax, ay)


def _output_kernel(xg_ref, s_ref, u_ref, d_ref, tfb_ref, cst_ref, y_ref, yg_ref, *, nb, groups, q):
    rows = xg_ref.shape[0]
    nblk = rows // nb
    gw = TIME_BLOCK * q
    half = s_ref.shape[2] // groups
    for g in range(groups):
        sg = jnp.concatenate([s_ref[0, :, g * half:(g + 1) * half], s_ref[1, :, g * half:(g + 1) * half]], axis=1)
        yg_ref[:, g * gw:(g + 1) * gw] = (
            jnp.dot(xg_ref[:, g * gw:(g + 1) * gw], tfb_ref[g], preferred_element_type=F32)
            + jnp.dot(sg.astype(BF16), cst_ref[g], preferred_element_type=F32))
    per_tile = LANES // q
    chunk = _lane_chunk((rows, LANES))
    sw = groups * q
    steps = []
    for s in range(TIME_BLOCK):
        h, pos = divmod(s, per_tile)
        tiles = []
        for jt in range(sw // LANES):
            acc = jnp.zeros((rows, LANES), F32)
            for qq in range(per_tile):
                g = jt * per_tile + qq
                a = yg_ref[:, (2 * g + h) * LANES:(2 * g + h + 1) * LANES]
                shift = ((qq - pos) % per_tile) * q
                r = pltpu.roll(a, shift, axis=1) if shift else a
                acc = jnp.where(chunk == qq, r, acc)
            tiles.append(acc)
        steps.append(jnp.concatenate(tiles, axis=1).reshape(nblk, nb, sw))
    y = jnp.concatenate(steps, axis=1).reshape(nblk * TIME_BLOCK * nb, sw)
    y_ref[...] = y + d_ref[...] * u_ref[...]


def _s5_outputs(xg, s, u_lat, ssm_d, tfb, cst, nb, tc, n_ctx_chunks):
    rows_l, sw = u_lat.shape
    rc = tc * nb
    n_lat = rows_l // rc
    groups, gw, _ = tfb.shape
    q = gw // TIME_BLOCK
    rows_out = (tc // TIME_BLOCK) * nb
    kern = functools.partial(_output_kernel, nb=nb, groups=groups, q=q)
    return pl.pallas_call(
        kern,
        grid=(n_lat,),
        in_specs=[pl.BlockSpec((rows_out, groups * gw), lambda j: (j + n_ctx_chunks, 0)),
                  pl.BlockSpec((2, rows_out, s.shape[2]), lambda j: (0, j + n_ctx_chunks, 0)),
                  pl.BlockSpec((rc, sw), lambda j: (j, 0)),
                  pl.BlockSpec((1, sw), lambda j: (0, 0)),
                  pl.BlockSpec(tfb.shape, lambda j: (0, 0, 0), pipeline_mode=pl.Buffered(1)),
                  pl.BlockSpec(cst.shape, lambda j: (0, 0, 0), pipeline_mode=pl.Buffered(1))],
        out_specs=pl.BlockSpec((rc, sw), lambda j: (j, 0)),
        out_shape=jax.ShapeDtypeStruct((rows_l, sw), F32),
        scratch_shapes=[pltpu.VMEM((rows_out, groups * gw), F32)],
        compiler_params=_params("arbitrary"),
        name="s5_output",
    )(xg, s, u_lat, ssm_d, tfb, cst)


def _s5_blocked(u_ctx, u_lat, ssm_d, ops, nb, tc):
    tfb, bst, cst, ax, ay = ops
    n_ctx_chunks = u_ctx.shape[0] // (tc * nb)
    xg, e = _s5_regroup(u_ctx, u_lat, bst, nb, tc)
    s = _s5_carry(e, ax, ay, nb, n_ctx_chunks, (tc // TIME_BLOCK) * nb)
    return _s5_outputs(xg, s, u_lat, ssm_d, tfb, cst, nb, tc, n_ctx_chunks)


def _mixer_kernel(y_ref, sga_ref, gbt_ref, x_ref, mod_ref, wglu_ref, wso_ref, wo_ref, o_ref, *, d, sw):
    nb, tt, _ = x_ref.shape
    rows = nb * tt
    y = _to_batch_major(y_ref[...], nb, tt)
    v = _dot(jax.nn.gelu(y), wglu_ref[...])
    ys = v[:, 0:sw] * jax.nn.sigmoid(v[:, sw:2 * sw])
    y_a = _dot(ys, wso_ref[...])
    merged = sga_ref[...].reshape(rows, d) * y_a + gbt_ref[...].reshape(rows, d)
    o = _dot(merged, wo_ref[...])
    g1 = _per_row(mod_ref[...], 2 * d, 3 * d, tt)
    o_ref[...] = (x_ref[...].reshape(rows, d) + g1 * o).reshape(nb, tt, d)


def _mixer_out(y, sga, gbt, x, mod3, w_glu_b, w_ssm_out_b, w_o_b):
    b, l, d = x.shape
    sw = y.shape[1]
    tt = TIME_TILE
    kern = functools.partial(_mixer_kernel, d=d, sw=sw)
    tok = pl.BlockSpec((b, tt, d), lambda j: (0, j, 0))
    return pl.pallas_call(
        kern,
        grid=(l // tt,),
        in_specs=[pl.BlockSpec((tt * b, sw), lambda j: (j, 0)),
                  tok, tok, tok,
                  pl.BlockSpec((b, 1, mod3.shape[2]), lambda j: (0, 0, 0)),
                  _full(w_glu_b), _full(w_ssm_out_b), _full(w_o_b)],
        out_specs=tok,
        out_shape=jax.ShapeDtypeStruct((b, l, d), F32),
        compiler_params=_params("arbitrary"),
        name="mixer_out",
    )(y, sga, gbt, x, mod3, w_glu_b, w_ssm_out_b, w_o_b)


def _split_bf16(a):
    hi = a.astype(BF16)
    lo = (a - hi.astype(F32)).astype(BF16)
    return hi, lo


def _norm2(xl, m, g, d):
    return _rms(xl, g) * (1.0 + m[:, 4 * d:5 * d]) + m[:, 3 * d:4 * d]


def _shared_kernel(xl_ref, mod_ref, g_ref, wsg_ref, wsu_ref, wsd_ref, base_ref, *, d):
    m = mod_ref[0]
    xl = xl_ref[...]
    hb = _norm2(xl, m, g_ref[...], d).astype(BF16)
    sg = jnp.dot(hb, wsg_ref[...], preferred_element_type=F32)
    su = jnp.dot(hb, wsu_ref[...], preferred_element_type=F32)
    shared = _dot(sg * jax.nn.sigmoid(sg) * su, wsd_ref[...])
    base_ref[...] = xl + m[:, 5 * d:6 * d] * shared


def _shared(xl2, mod3, tiles_per_batch, g2n, ws_gate_b, ws_up_b, ws_down_b):
    n, d = xl2.shape
    t = TOKEN_TILE
    tok = pl.BlockSpec((t, d), lambda i: (i, 0))
    return pl.pallas_call(
        functools.partial(_shared_kernel, d=d),
        grid=(n // t,),
        in_specs=[tok,
                  pl.BlockSpec((1, 1, mod3.shape[2]), lambda i: (i // tiles_per_batch, 0, 0)),
                  _full(g2n), _full(ws_gate_b), _full(ws_up_b), _full(ws_down_b)],
        out_specs=tok,
        out_shape=jax.ShapeDtypeStruct((n, d), F32),
        compiler_params=_params("arbitrary"),
        name="shared",
    )(xl2, mod3, g2n, ws_gate_b, ws_up_b, ws_down_b)


def _route_kernel(xl_ref, mod_ref, g_ref, whi_ref, wlo_ref, rb_ref,
                  h2_ref, eidx_ref, wts_ref, rank_ref, cnt_ref, carry_ref, *, d):
    i = pl.program_id(0)
    t = xl_ref.shape[0]
    h2 = _norm2(xl_ref[...], mod_ref[0], g_ref[...], d)
    _store_packed(h2_ref, h2)

    h_hi, h_lo = _split_bf16(h2)
    w_hi, w_lo = whi_ref[...], wlo_ref[...]
    nt = (((1,), (1,)), ((), ()))
    logits = (lax.dot_general(w_hi, h_hi, nt, preferred_element_type=F32)
              + lax.dot_general(w_hi, h_lo, nt, preferred_element_type=F32)
              + lax.dot_general(w_lo, h_hi, nt, preferred_element_type=F32))
    scores = jax.nn.sigmoid(logits)
    choice = scores + rb_ref[...]

    epg = EXPERTS_PER_GROUP
    gi = lax.broadcasted_iota(I32, (epg, t), 0)
    gs = []
    for g in range(N_EXPERT_GROUPS):
        seg = choice[g * epg:(g + 1) * epg, :]
        m1 = jnp.max(seg, axis=0, keepdims=True)
        i1 = jnp.min(jnp.where(seg == m1, gi, epg), axis=0, keepdims=True)
        m2 = jnp.max(jnp.where(gi == i1, -jnp.inf, seg), axis=0, keepdims=True)
        gs.append(m1 + m2)
    masked = []
    for g in range(N_EXPERT_GROUPS):
        beat = jnp.zeros((1, t), I32)
        for g2 in range(N_EXPERT_GROUPS):
            if g2 < g:
                beat = beat + (gs[g2] >= gs[g]).astype(I32)
            elif g2 > g:
                beat = beat + (gs[g2] > gs[g]).astype(I32)
        keep = beat < TOP_K_GROUPS
        masked.append(jnp.where(keep, choice[g * epg:(g + 1) * epg, :], -jnp.inf))
    cur = jnp.concatenate(masked, axis=0)

    ei_all = lax.broadcasted_iota(I32, (N_EXPERTS, t), 0)
    picks, raw = [], []
    onehot = jnp.zeros((N_EXPERTS, t), F32)
    for _ in range(TOP_K):
        mx = jnp.max(cur, axis=0, keepdims=True)
        ei = jnp.min(jnp.where(cur == mx, ei_all, N_EXPERTS), axis=0, keepdims=True)
        hit = ei_all == ei
        raw.append(jnp.sum(jnp.where(hit, scores, 0.0), axis=0, keepdims=True))
        cur = jnp.where(hit, -jnp.inf, cur)
        onehot = jnp.where(hit, 1.0, onehot)
        picks.append(ei)
    tot = raw[0]
    for k in range(1, TOP_K):
        tot = tot + raw[k]

    @pl.when(i == 0)
    def _():
        carry_ref[...] = jnp.zeros_like(carry_ref)

    upper = (lax.broadcasted_iota(I32, (t, t), 0) < lax.broadcasted_iota(I32, (t, t), 1)).astype(BF16)
    before = jnp.dot(onehot.astype(BF16), upper, preferred_element_type=F32) + carry_ref[:, 0:1]
    for k in range(TOP_K):
        eidx_ref[k:k + 1, :] = picks[k]
        wts_ref[k:k + 1, :] = raw[k] / tot * ROUTE_SCALE
        rk = jnp.sum(jnp.where(ei_all == picks[k], before, 0.0), axis=0, keepdims=True)
        rank_ref[k:k + 1, :] = rk.astype(I32)
    carry_ref[...] = carry_ref[...] + jnp.sum(onehot, axis=1, keepdims=True)
    cnt_ref[...] = carry_ref[...]


def _route(xl2, mod3, tiles_per_batch, g2n, w_router_t, router_bias):
    n, d = xl2.shape
    t = TOKEN_TILE
    e = w_router_t.shape[0]
    w_top = lax.bitcast_convert_type(lax.bitcast_convert_type(w_router_t, U32) & jnp.uint32(0xFFFF0000), F32)
    w_hi, w_lo = w_top.astype(BF16), (w_router_t - w_top).astype(BF16)
    kern = functools.partial(_route_kernel, d=d)
    tok = pl.BlockSpec((t, d), lambda i: (i, 0))
    small = pl.BlockSpec((TOP_K, t), lambda i: (0, i))
    return pl.pallas_call(
        kern,
        grid=(n // t,),
        in_specs=[tok,
                  pl.BlockSpec((1, 1, mod3.shape[2]), lambda i: (i // tiles_per_batch, 0, 0)),
                  _full(g2n), _full(w_hi), _full(w_lo), _full(router_bias)],
        out_specs=[pl.BlockSpec((t * PACK_SUBLANES, LANES), lambda i: (i, 0)), small, small, small,
                   pl.BlockSpec((e, LANES), lambda i: (0, 0))],
        out_shape=[jax.ShapeDtypeStruct((n * PACK_SUBLANES, LANES), U32),
                   jax.ShapeDtypeStruct((TOP_K, n), I32),
                   jax.ShapeDtypeStruct((TOP_K, n), F32),
                   jax.ShapeDtypeStruct((TOP_K, n), I32),
                   jax.ShapeDtypeStruct((e, LANES), F32)],
        scratch_shapes=[pltpu.VMEM((e, LANES), F32)],
        compiler_params=_params("arbitrary"),
        name="route",
    )(xl2, mod3, g2n, w_hi, w_lo, router_bias)


def _dest_kernel(eidx_ref, rank_ref, start_ref, o_ref):
    t = eidx_ref.shape[1]
    ei_all = lax.broadcasted_iota(I32, (N_EXPERTS, t), 0)
    st = start_ref[:, 0:1]
    for k in range(TOP_K):
        hit = ei_all == eidx_ref[k:k + 1, :]
        o_ref[k:k + 1, :] = jnp.sum(jnp.where(hit, st, 0), axis=0, keepdims=True) + rank_ref[k:k + 1, :]


def _dest_rows(eidx, rank, starts_b):
    n = eidx.shape[1]
    t = 1024
    small = pl.BlockSpec((TOP_K, t), lambda i: (0, i))
    return pl.pallas_call(
        _dest_kernel,
        grid=(n // t,),
        in_specs=[small, small, _full(starts_b)],
        out_specs=small,
        out_shape=jax.ShapeDtypeStruct((TOP_K, n), I32),
        compiler_params=_params("arbitrary"),
        name="dest",
    )(eidx, rank, starts_b)


def _sc_workers():
    info = plsc.get_sparse_core_info()
    return info.num_cores, info.num_cores * info.num_subcores


def _sc_dispatch(dest_flat, src3, out_rows):
    n = src3.shape[0]
    n_cores, n_workers = _sc_workers()
    per_worker = n // n_workers
    assert per_worker % SC_CHUNK == 0
    mesh = plsc.VectorSubcoreMesh(core_axis_name="c", subcore_axis_name="s")

    @functools.partial(
        pl.kernel, mesh=mesh,
        out_type=jax.ShapeDtypeStruct((out_rows,) + src3.shape[1:], src3.dtype),
        scratch_types=[pltpu.VMEM((SC_CHUNK,) + src3.shape[1:], src3.dtype)]
        + [pltpu.VMEM((SC_CHUNK,), I32)] * TOP_K,
        name="dispatch",
    )
    def run(dest_hbm, src_hbm, out_hbm, rows_v, *idx_v):
        worker = lax.axis_index("s") * n_cores + lax.axis_index("c")

        @pl.loop(0, per_worker // SC_CHUNK)
        def _(j):
            base = worker * per_worker + j * SC_CHUNK
            pltpu.sync_copy(src_hbm.at[pl.ds(base, SC_CHUNK)], rows_v)
            for k in range(TOP_K):
                pltpu.sync_copy(dest_hbm.at[pl.ds(k * n + base, SC_CHUNK)], idx_v[k])
            for k in range(TOP_K):
                pltpu.sync_copy(rows_v, out_hbm.at[idx_v[k]])

    return run(dest_flat, src3)


def _sc_combine(dest_flat, ys3, n):
    n_cores, n_workers = _sc_workers()
    per_worker = n // n_workers
    assert per_worker % SC_CHUNK == 0
    mesh = plsc.VectorSubcoreMesh(core_axis_name="c", subcore_axis_name="s")

    @functools.partial(
        pl.kernel, mesh=mesh,
        out_type=jax.ShapeDtypeStruct((TOP_K * n,) + ys3.shape[1:], ys3.dtype),
        scratch_types=[pltpu.VMEM((SC_CHUNK,) + ys3.shape[1:], ys3.dtype), pltpu.VMEM((SC_CHUNK,), I32)],
        name="combine",
    )
    def run(dest_hbm, ys_hbm, out_hbm, rows_v, idx_v):
        worker = lax.axis_index("s") * n_cores + lax.axis_index("c")

        @pl.loop(0, per_worker // SC_CHUNK)
        def _(j):
            base = worker * per_worker + j * SC_CHUNK
            for k in range(TOP_K):
                pltpu.sync_copy(dest_hbm.at[pl.ds(k * n + base, SC_CHUNK)], idx_v)
                pltpu.sync_copy(ys_hbm.at[idx_v], rows_v)
                pltpu.sync_copy(rows_v, out_hbm.at[pl.ds(k * n + base, SC_CHUNK)])

    return run(dest_flat, ys3)


def _chunk_metadata(counts, n_rows):
    ch = EXPERT_CHUNK
    n_ch = (counts + ch - 1) // ch
    cum = jnp.cumsum(n_ch)
    total = cum[-1]
    max_chunks = n_rows // ch + N_EXPERTS
    i = jnp.arange(max_chunks, dtype=I32)
    e = jnp.sum((cum[None, :] <= i[:, None]).astype(I32), axis=1)
    e_last = jnp.max(jnp.where(counts > 0, jnp.arange(N_EXPERTS, dtype=I32), 0))
    exp = jnp.where(i < total, jnp.minimum(e, N_EXPERTS - 1), e_last).astype(I32)
    newexp = jnp.concatenate([jnp.ones((1,), I32), (exp[1:] != exp[:-1]).astype(I32)])
    change_at = jnp.where(newexp == 1, i, max_chunks)
    nxt_change = lax.cummin(jnp.concatenate([change_at[1:], jnp.array([max_chunks], I32)]), reverse=True)
    nexp = jnp.where(nxt_change < max_chunks, exp[jnp.minimum(nxt_change, max_chunks - 1)], -1).astype(I32)
    starts = (cum - n_ch) * ch
    return starts.astype(I32), (exp, newexp, nexp, total.astype(I32).reshape(1))


def _expert_kernel(exp_ref, newexp_ref, nexp_ref, total_ref, xs_hbm, wg_hbm, wu_hbm, wd_hbm, after_hbm, ys_hbm,
                   xbuf, ybuf, wgf, wuf, wdf, wgb, wub, wdb, sem_x, sem_y, sem_w):
    del after_hbm
    ch = EXPERT_CHUNK
    nbuf = EXPERT_RING
    rows = ch * PACK_SUBLANES
    total = total_ref[0]

    def x_copy(i, slot):
        r0 = pl.multiple_of(i * rows, rows)
        return pltpu.make_async_copy(xs_hbm.at[pl.ds(r0, rows)], xbuf.at[slot], sem_x.at[slot])

    def y_copy(i, slot):
        r0 = pl.multiple_of(i * rows, rows)
        return pltpu.make_async_copy(ybuf.at[slot], ys_hbm.at[pl.ds(r0, rows)], sem_y.at[slot])

    def weight_copies(e):
        return (pltpu.make_async_copy(wg_hbm.at[e], wgf, sem_w.at[0]),
                pltpu.make_async_copy(wu_hbm.at[e], wuf, sem_w.at[1]),
                pltpu.make_async_copy(wd_hbm.at[e], wdf, sem_w.at[2]))

    for cp in weight_copies(exp_ref[0]):
        cp.start()
    for b in range(nbuf - 1):
        @pl.when(b < total)
        def _(b=b):
            x_copy(b, b).start()

    def chunk(i, carry):
        slot = lax.rem(i, nbuf)
        x_copy(i, slot).wait()
        ahead = i + (nbuf - 1)

        @pl.when(ahead < total)
        def _():
            x_copy(ahead, lax.rem(ahead, nbuf)).start()

        @pl.when(newexp_ref[i] == 1)
        def _():
            for cp in weight_copies(exp_ref[i]):
                cp.wait()
            wgb[...] = wgf[...].astype(BF16)
            wub[...] = wuf[...].astype(BF16)
            wdb[...] = wdf[...].astype(BF16)

            @pl.when(nexp_ref[i] >= 0)
            def _():
                for cp in weight_copies(nexp_ref[i]):
                    cp.start()

        @pl.when(i >= nbuf)
        def _():
            y_copy(i, slot).wait()

        xr = xbuf.at[slot]
        yr = ybuf.at[slot]
        x = _load_packed(lambda sl: xr[sl, :], ch).astype(BF16)
        g = jnp.dot(x, wgb[...], preferred_element_type=F32)
        u = jnp.dot(x, wub[...], preferred_element_type=F32)
        y = jnp.dot((g * jax.nn.sigmoid(g) * u).astype(BF16), wdb[...], preferred_element_type=F32)
        _store_packed(yr, y)
        y_copy(i, slot).start()
        return carry

    lax.fori_loop(0, total, chunk, 0)

    for b in range(nbuf):
        @pl.when(b < total)
        def _(b=b):
            y_copy(0, b).wait()


def _experts(meta, xs, w_gate, w_up, w_down, after):
    e, d, f = w_gate.shape
    hbm = pl.BlockSpec(memory_space=pl.ANY)
    rows = EXPERT_CHUNK * PACK_SUBLANES
    grid_spec = pltpu.PrefetchScalarGridSpec(
        num_scalar_prefetch=len(meta), grid=(1,),
        in_specs=[hbm, hbm, hbm, hbm, hbm], out_specs=hbm,
        scratch_shapes=[pltpu.VMEM((EXPERT_RING, rows, LANES), U32), pltpu.VMEM((EXPERT_RING, rows, LANES), U32),
                        pltpu.VMEM((d, f), F32), pltpu.VMEM((d, f), F32), pltpu.VMEM((f, d), F32),
                        pltpu.VMEM((d, f), BF16), pltpu.VMEM((d, f), BF16), pltpu.VMEM((f, d), BF16),
                        pltpu.SemaphoreType.DMA((EXPERT_RING,)), pltpu.SemaphoreType.DMA((EXPERT_RING,)),
                        pltpu.SemaphoreType.DMA((3,))])
    return pl.pallas_call(
        _expert_kernel,
        grid_spec=grid_spec,
        out_shape=jax.ShapeDtypeStruct(xs.shape, xs.dtype),
        compiler_params=_params("arbitrary"),
        name="experts",
    )(*meta, xs, w_gate, w_up, w_down, after)


def _final_kernel(base_ref, yt_ref, wt_ref, mod_ref, g_ref, *rest, d):
    o_ref = rest[-1]
    t = base_ref.shape[0]
    w = wt_ref[...]
    routed = w[:, 0:1] * _load_packed(lambda sl: yt_ref[0, sl, :], t)
    for k in range(1, TOP_K):
        routed = routed + w[:, k:k + 1] * _load_packed(lambda sl, k=k: yt_ref[k, sl, :], t)
    g2 = mod_ref[0][:, 5 * d:6 * d]
    o_ref[...] = _rms(base_ref[...] + g2 * routed, g_ref[...])


def _final(base, ytok3, wts_t, mod3, tiles_per_batch, final_g, part, n_parts, prev_out):
    n, d = base.shape
    t = TOKEN_TILE
    tiles = n // t // n_parts
    off = part * tiles
    kern = functools.partial(_final_kernel, d=d)
    tok = pl.BlockSpec((t, d), lambda i: (off + i, 0))
    in_specs = [tok,
                pl.BlockSpec((TOP_K, t * PACK_SUBLANES, LANES), lambda i: (0, i, 0)),
                pl.BlockSpec((t, TOP_K), lambda i: (off + i, 0)),
                pl.BlockSpec((1, 1, mod3.shape[2]), lambda i: ((off + i) // tiles_per_batch, 0, 0)),
                _full(final_g)]
    args = [base, ytok3, wts_t, mod3, final_g]
    aliases = {}
    if prev_out is not None:
        in_specs.append(pl.BlockSpec(memory_space=pl.ANY))
        args.append(prev_out)
        aliases = {len(args) - 1: 0}
    return pl.pallas_call(
        kern,
        grid=(tiles,),
        in_specs=in_specs,
        out_specs=tok,
        out_shape=jax.ShapeDtypeStruct((n, d), F32),
        input_output_aliases=aliases,
        compiler_params=_params("arbitrary"),
        name="final",
    )(*args)


def kernel(x, c, ctx, c_ctx, w_mod, b_mod, norm1_g, norm2_g, w_in, ssm_lam_re, ssm_lam_im, ssm_log_dt, ssm_b_re, ssm_b_im, ssm_c_re, ssm_c_im, ssm_d, w_glu, w_ssm_out, conv_w, w_conv_out, w_o, w_router, router_bias, w_gate, w_up, w_down, ws_gate, ws_up, ws_down, final_g):
    b, l, d = x.shape
    n = b * l
    sw = ssm_d.shape[1]
    assert w_mod.shape[0] == 1, "single layer"
    assert b == SUBLANES and l % TOKEN_TILE == 0 and l % S5_CHUNK == 0 and ctx.shape[1] % S5_CHUNK == 0 and TIME_TILE % GRID_W == 0

    mod_rows = 2 * SUBLANES
    c_all = jnp.concatenate([c, c_ctx[None, :], jnp.zeros((mod_rows - b - 1, d), F32)], axis=0)
    mod = _modulation(c_all, w_mod[0], b_mod[0])
    mod3 = mod.reshape(mod_rows, 1, mod.shape[1])

    w_in_b = w_in[0].astype(BF16)
    g1n = norm1_g[0].reshape(1, d)
    u_lat, sga, gbt = _in_proj(x, mod3, g1n, w_in_b, conv_w[0], w_conv_out[0].astype(BF16))
    u_ctx = _ctx_proj(ctx, mod3, b, g1n, w_in_b[:, :sw])

    ops = _blocked_operators(ssm_lam_re[0], ssm_lam_im[0], ssm_log_dt[0], ssm_b_re[0], ssm_b_im[0],
                             ssm_c_re[0], ssm_c_im[0])
    y = _s5_blocked(u_ctx, u_lat, ssm_d[0].reshape(1, sw), ops, b, S5_CHUNK)

    xl = _mixer_out(y, sga, gbt, x, mod3, w_glu[0].astype(BF16), w_ssm_out[0].astype(BF16),
                    w_o[0].astype(BF16))

    tiles_per_batch = l // TOKEN_TILE
    xl2 = xl.reshape(n, d)
    g2n = norm2_g[0].reshape(1, d)
    h2, eidx, wts, rank, cnt = _route(xl2, mod3, tiles_per_batch, g2n, w_router[0].T,
                                      router_bias[0].reshape(N_EXPERTS, 1))

    counts = cnt[:, 0].astype(I32)
    n_rows = n * TOP_K
    starts, meta = _chunk_metadata(counts, n_rows)
    buf_rows = n_rows + N_EXPERTS * EXPERT_CHUNK
    dest = _dest_rows(eidx, rank, jnp.broadcast_to(starts[:, None], (N_EXPERTS, LANES)))

    xs = _sc_dispatch(dest.reshape(TOP_K * n), h2.reshape(n, PACK_SUBLANES, LANES), buf_rows)
    base = _shared(xl2, mod3, tiles_per_batch, g2n, ws_gate[0].astype(BF16), ws_up[0].astype(BF16),
                   ws_down[0].astype(BF16))
    ys = _experts(meta, xs.reshape(buf_rows * PACK_SUBLANES, LANES), w_gate[0], w_up[0], w_down[0], base)
    ys3 = ys.reshape(buf_rows, PACK_SUBLANES, LANES)

    n_part = n // COMBINE_PARTS
    wts_t = wts.T
    out = None
    for part in range(COMBINE_PARTS):
        dest_part = dest[:, part * n_part:(part + 1) * n_part].reshape(TOP_K * n_part)
        ytok = _sc_combine(dest_part, ys3, n_part)
        out = _final(base, ytok.reshape(TOP_K, n_part * PACK_SUBLANES, LANES), wts_t, mod3, tiles_per_batch,
                     final_g.reshape(1, d), part, COMBINE_PARTS, out)
    return out.reshape(b, l, d)
```

```python
import functools

import jax
import jax.numpy as jnp
from jax import lax
from jax.experimental import pallas as pl
from jax.experimental.pallas import tpu as pltpu
from jax.experimental.pallas import tpu_sc as plsc

F32 = jnp.float32
BF16 = jnp.bfloat16
I32 = jnp.int32
U32 = jnp.uint32

EPS = 1e-6
GRID_W = 64
SSM_GROUP = 16
N_EXPERTS = 256
TOP_K = 8
N_EXPERT_GROUPS = 8
EXPERTS_PER_GROUP = N_EXPERTS // N_EXPERT_GROUPS
TOP_K_GROUPS = 4
ROUTE_SCALE = 2.5

SUBLANES = 8
LANES = 128
VMEM_LIMIT_BYTES = 48 * 1024 * 1024

TIME_TILE = 64
TOKEN_TILE = 256
TIME_BLOCK = 16
S5_CHUNK = 256
EXPERT_CHUNK = 256
EXPERT_RING = 4
SC_CHUNK = 128
COMBINE_PARTS = 4
PACK_SUBLANES = 4


def _dot(a, b):
    return jnp.dot(a.astype(BF16), b.astype(BF16), preferred_element_type=F32)


def _rms(xf, g):
    return xf * lax.rsqrt(jnp.mean(xf * xf, axis=-1, keepdims=True) + EPS) * g


def _params(*sem):
    return pltpu.CompilerParams(dimension_semantics=sem, vmem_limit_bytes=VMEM_LIMIT_BYTES)


def _full(a):
    return pl.BlockSpec(a.shape, lambda *_: (0,) * a.ndim)


def _pack_rows(v):
    half = v.shape[1] // 2
    lo = lax.bitcast_convert_type(v[:, :half].astype(BF16).astype(F32), U32) >> 16
    hi = lax.bitcast_convert_type(v[:, half:].astype(BF16).astype(F32), U32) & jnp.uint32(0xFFFF0000)
    return lo | hi


def _unpack_lo(w):
    return lax.bitcast_convert_type(w << 16, F32)


def _unpack_hi(w):
    return lax.bitcast_convert_type(w & jnp.uint32(0xFFFF0000), F32)


def _store_packed(ref, v):
    t = v.shape[0]
    w = _pack_rows(v)
    for c in range(PACK_SUBLANES):
        ref[pl.ds(c, t, stride=PACK_SUBLANES), :] = w[:, c * LANES:(c + 1) * LANES]


def _load_packed(load, t):
    ws = [load(pl.ds(c, t, stride=PACK_SUBLANES)) for c in range(PACK_SUBLANES)]
    return jnp.concatenate([_unpack_lo(w) for w in ws] + [_unpack_hi(w) for w in ws], axis=1)


def _mod_kernel(c_ref, w_ref, b_ref, o_ref):
    c = c_ref[...]
    o_ref[...] = _dot(c * jax.nn.sigmoid(c), w_ref[...]) + b_ref[...]


def _modulation(c_all, w_mod, b_mod):
    rows, d = c_all.shape
    cols = w_mod.shape[1]
    blk = 1536
    return pl.pallas_call(
        _mod_kernel,
        grid=(cols // blk,),
        in_specs=[pl.BlockSpec((rows, d), lambda j: (0, 0)),
                  pl.BlockSpec((d, blk), lambda j: (0, j)),
                  pl.BlockSpec((1, blk), lambda j: (0, j))],
        out_specs=pl.BlockSpec((rows, blk), lambda j: (0, j)),
        out_shape=jax.ShapeDtypeStruct((rows, cols), F32),
        compiler_params=_params("arbitrary"),
        name="mod",
    )(c_all, w_mod, b_mod.reshape(1, cols))


def _per_row(m3, lo, hi, tt):
    nb = m3.shape[0]
    return jnp.broadcast_to(m3[:, :, lo:hi], (nb, tt, hi - lo)).reshape(nb * tt, hi - lo)


def _to_time_major(val, nb, tt):
    c = val.shape[1]
    return pltpu.einshape("btc->tbc", val.reshape(nb, tt, c)).reshape(nb * tt, c)


def _to_batch_major(val, nb, tt):
    c = val.shape[1]
    return pltpu.einshape("tbc->btc", val.reshape(tt, nb, c)).reshape(nb * tt, c)


def _in_proj_kernel(x_ref, mod_ref, g_ref, w_ref, cw_ref, wco_ref, u_ref, sga_ref, gbt_ref, *, d, sw):
    nb, tt, _ = x_ref.shape
    rows = nb * tt
    m3 = mod_ref[...]
    x = x_ref[...].reshape(rows, d)
    h = _rms(x, g_ref[...]) * (1.0 + _per_row(m3, d, 2 * d, tt)) + _per_row(m3, 0, d, tt)
    hb = h.astype(BF16)
    u_ref[...] = _to_time_major(jnp.dot(hb, w_ref[:, 0:sw], preferred_element_type=F32), nb, tt)
    cb = jnp.dot(hb, w_ref[:, sw:2 * sw], preferred_element_type=F32)
    cc = jnp.dot(hb, w_ref[:, 2 * sw:3 * sw], preferred_element_type=F32)
    cv = jnp.dot(hb, w_ref[:, 3 * sw:4 * sw], preferred_element_type=F32)
    ccv = cc * cv
    col = lax.broadcasted_iota(I32, ccv.shape, 0) % GRID_W
    prev = jnp.where(col == 0, 0.0, pltpu.roll(ccv, 1, axis=0))
    nxt = jnp.where(col == GRID_W - 1, 0.0, pltpu.roll(ccv, rows - 1, axis=0))
    cw = cw_ref[...]
    conv = prev * cw[0:1, :] + ccv * cw[1:2, :] + nxt * cw[2:3, :]
    y_conv = _dot(cb * conv, wco_ref[...])
    ga = jnp.dot(hb, w_ref[:, 4 * sw:4 * sw + d], preferred_element_type=F32)
    gb = jnp.dot(hb, w_ref[:, 4 * sw + d:4 * sw + 2 * d], preferred_element_type=F32)
    sga_ref[...] = jax.nn.sigmoid(ga).reshape(nb, tt, d)
    gbt_ref[...] = (jax.nn.sigmoid(gb) * y_conv).reshape(nb, tt, d)


def _in_proj(x, mod3, g1n, w_in_b, conv_w, w_conv_out_b):
    b, l, d = x.shape
    sw = conv_w.shape[1]
    tt = TIME_TILE
    kern = functools.partial(_in_proj_kernel, d=d, sw=sw)
    tok = pl.BlockSpec((b, tt, d), lambda j: (0, j, 0))
    return pl.pallas_call(
        kern,
        grid=(l // tt,),
        in_specs=[tok,
                  pl.BlockSpec((b, 1, mod3.shape[2]), lambda j: (0, 0, 0)),
                  _full(g1n), _full(w_in_b), _full(conv_w), _full(w_conv_out_b)],
        out_specs=[pl.BlockSpec((tt * b, sw), lambda j: (j, 0)), tok, tok],
        out_shape=[jax.ShapeDtypeStruct((l * b, sw), F32),
                   jax.ShapeDtypeStruct((b, l, d), F32),
                   jax.ShapeDtypeStruct((b, l, d), F32)],
        compiler_params=_params("arbitrary"),
        name="in_proj",
    )(x, mod3, g1n, w_in_b, conv_w, w_conv_out_b)


def _ctx_proj_kernel(x_ref, mod_ref, g_ref, w_ref, u_ref, *, d):
    nb, tt, _ = x_ref.shape
    m = mod_ref[0]
    x = x_ref[...].reshape(nb * tt, d)
    h = _rms(x, g_ref[...]) * (1.0 + m[:, d:2 * d]) + m[:, 0:d]
    u_ref[...] = _to_time_major(_dot(h, w_ref[...]), nb, tt)


def _ctx_proj(ctx, mod3, ctx_row, g1n, w_u_b):
    b, lc, d = ctx.shape
    sw = w_u_b.shape[1]
    tt = TIME_TILE
    kern = functools.partial(_ctx_proj_kernel, d=d)
    return pl.pallas_call(
        kern,
        grid=(lc // tt,),
        in_specs=[pl.BlockSpec((b, tt, d), lambda j: (0, j, 0)),
                  pl.BlockSpec((1, 1, mod3.shape[2]), lambda j: (ctx_row, 0, 0)),
                  _full(g1n), _full(w_u_b)],
        out_specs=pl.BlockSpec((tt * b, sw), lambda j: (j, 0)),
        out_shape=jax.ShapeDtypeStruct((lc * b, sw), F32),
        compiler_params=_params("arbitrary"),
        name="ctx_proj",
    )(ctx, mod3, g1n, w_u_b)


def _blocked_operators(lam_re, lam_im, log_dt, b_re, b_im, c_re, c_im):
    dt = jnp.exp(log_dt)[..., None]
    k = jnp.arange(TIME_BLOCK + 1, dtype=F32)[:, None, None, None]
    mag = jnp.exp(lam_re * dt)[None] ** k
    ang = (lam_im * dt)[None] * k
    ak_re = mag * jnp.cos(ang)
    ak_im = mag * jnp.sin(ang)
    a_re, a_im = ak_re[1], ak_im[1]
    den = lam_re * lam_re + lam_im * lam_im
    k_re = ((a_re - 1.0) * lam_re + a_im * lam_im) / den
    k_im = (a_im * lam_re - (a_re - 1.0) * lam_im) / den
    bb_re = k_re[..., None] * b_re - k_im[..., None] * b_im
    bb_im = k_re[..., None] * b_im + k_im[..., None] * b_re
    nd, g, p, q = bb_re.shape

    ab_re = ak_re[..., None] * bb_re[None] - ak_im[..., None] * bb_im[None]
    ab_im = ak_re[..., None] * bb_im[None] + ak_im[..., None] * bb_re[None]
    m = (jnp.einsum('dgcp,kdgpe->kdgce', c_re, ab_re[:TIME_BLOCK]) - jnp.einsum('dgcp,kdgpe->kdgce', c_im, ab_im[:TIME_BLOCK]))
    s_i = jnp.arange(TIME_BLOCK)[:, None]
    t_i = jnp.arange(TIME_BLOCK)[None, :]
    lags = jnp.arange(TIME_BLOCK)[None, None, :]
    pick_f = ((t_i - s_i)[:, :, None] == lags).astype(F32)
    pick_b = ((s_i - t_i)[:, :, None] == lags).astype(F32)
    tfb = (jnp.einsum('stk,kgce->gsetc', pick_f, m[:, 0]) + jnp.einsum('stk,kgce->gsetc', pick_b, m[:, 1]))
    tfb = tfb.reshape(g, TIME_BLOCK * q, TIME_BLOCK * q)

    inj_f_re = ab_re[:TIME_BLOCK, 0][::-1]
    inj_f_im = ab_im[:TIME_BLOCK, 0][::-1]
    inj_b_re = ab_re[:TIME_BLOCK, 1]
    inj_b_im = ab_im[:TIME_BLOCK, 1]

    def inj(x):
        return x.transpose(1, 0, 3, 2).reshape(g, TIME_BLOCK * q, p)
    bst = jnp.concatenate([inj(inj_f_re), inj(inj_f_im), inj(inj_b_re), inj(inj_b_im)], axis=2)

    def ca(c_r, c_i, a_r, a_i):
        return (c_r[None] * a_r[:, :, None, :] - c_i[None] * a_i[:, :, None, :],
                c_r[None] * a_i[:, :, None, :] + c_i[None] * a_r[:, :, None, :])
    cf_re, cf_im = ca(c_re[0], c_im[0], ak_re[1:, 0], ak_im[1:, 0])
    cb_re, cb_im = ca(c_re[1], c_im[1], ak_re[1:, 1][::-1], ak_im[1:, 1][::-1])

    def out(x):
        return x.transpose(1, 3, 0, 2).reshape(g, p, TIME_BLOCK * q)
    cst = jnp.concatenate([out(cf_re), -out(cf_im), out(cb_re), -out(cb_im)], axis=1)

    a16_re, a16_im = ak_re[TIME_BLOCK], ak_im[TIME_BLOCK]
    ax = jnp.concatenate([a16_re, a16_re], axis=-1).reshape(nd, 1, g * 2 * p)
    ay = jnp.concatenate([-a16_im, a16_im], axis=-1).reshape(nd, 1, g * 2 * p)
    return tfb.astype(BF16), bst.astype(BF16), cst.astype(BF16), ax, ay


def _lane_chunk(shape):
    return lax.broadcasted_iota(I32, shape, 1) // SSM_GROUP


def _regroup_kernel(uc_ref, ul_ref, bst_ref, xg_ref, e_ref, *, n_ctx, nb, groups, q):
    j = pl.program_id(0)
    rows_in = uc_ref.shape[0]
    nblk = rows_in // (TIME_BLOCK * nb)
    rows = nblk * nb
    u = jnp.where(j < n_ctx, uc_ref[...], ul_ref[...])
    u3 = u.reshape(nblk, TIME_BLOCK * nb, u.shape[1])
    xs = [u3[:, s * nb:(s + 1) * nb, :].reshape(rows, u.shape[1]) for s in range(TIME_BLOCK)]
    per_tile = LANES // q
    chunk = _lane_chunk((rows, LANES))
    for g in range(groups):
        jt, qq = divmod(g, per_tile)
        for h in range(TIME_BLOCK // per_tile):
            acc = jnp.zeros((rows, LANES), F32)
            for s in range(h * per_tile, (h + 1) * per_tile):
                pos = s % per_tile
                a = xs[s][:, jt * LANES:(jt + 1) * LANES]
                shift = ((pos - qq) % per_tile) * q
                r = pltpu.roll(a, shift, axis=1) if shift else a
                acc = jnp.where(chunk == pos, r, acc)
            xg_ref[:, (2 * g + h) * LANES:(2 * g + h + 1) * LANES] = acc.astype(BF16)
    gw = TIME_BLOCK * q
    for g in range(groups):
        e = jnp.dot(xg_ref[:, g * gw:(g + 1) * gw], bst_ref[g], preferred_element_type=F32)
        half = e.shape[1] // 2
        e_ref[0, :, g * half:(g + 1) * half] = e[:, :half]
        e_ref[1, :, g * half:(g + 1) * half] = e[:, half:]


def _s5_regroup(u_ctx, u_lat, bst, nb, tc):
    rows_c, sw = u_ctx.shape
    rows_l = u_lat.shape[0]
    rc = tc * nb
    n_ctx, n_lat = rows_c // rc, rows_l // rc
    groups, gw, p4 = bst.shape
    q = gw // TIME_BLOCK
    rows_out = (tc // TIME_BLOCK) * nb
    total_rows = (n_ctx + n_lat) * rows_out
    kern = functools.partial(_regroup_kernel, n_ctx=n_ctx, nb=nb, groups=groups, q=q)
    return pl.pallas_call(
        kern,
        grid=(n_ctx + n_lat,),
        in_specs=[pl.BlockSpec((rc, sw), lambda j: (jnp.minimum(j, n_ctx - 1), 0)),
                  pl.BlockSpec((rc, sw), lambda j: (jnp.maximum(j - n_ctx, 0), 0)),
                  pl.BlockSpec(bst.shape, lambda j: (0, 0, 0), pipeline_mode=pl.Buffered(1))],
        out_specs=[pl.BlockSpec((rows_out, groups * gw), lambda j: (j, 0)),
                   pl.BlockSpec((2, rows_out, groups * p4 // 2), lambda j: (0, j, 0))],
        out_shape=[jax.ShapeDtypeStruct((total_rows, groups * gw), BF16),
                   jax.ShapeDtypeStruct((2, total_rows, groups * p4 // 2), F32)],
        compiler_params=_params("arbitrary"),
        name="s5_regroup",
    )(u_ctx, u_lat, bst)


def _carry_kernel(e_ref, ax_ref, ay_ref, s_ref, st_ref, *, nb):
    dirn = pl.program_id(0)
    j = pl.program_id(1)
    rows = e_ref.shape[1]
    nblk = rows // nb
    lanes = e_ref.shape[2]

    @pl.when(j == 0)
    def _():
        st_ref[...] = jnp.zeros_like(st_ref)

    ax = jnp.broadcast_to(ax_ref[0], (nb, lanes))
    ay = jnp.broadcast_to(ay_ref[0], (nb, lanes))

    def body(i, s):
        blk = jnp.where(dirn == 0, i, nblk - 1 - i)
        r0 = pl.multiple_of(blk * nb, nb)
        s_ref[0, pl.ds(r0, nb), :] = s.astype(s_ref.dtype)
        swapped = jnp.concatenate([pltpu.roll(s[:, t * LANES:(t + 1) * LANES], LANES // 2, axis=1)
                                   for t in range(lanes // LANES)], axis=1)
        return ax * s + ay * swapped + e_ref[0, pl.ds(r0, nb), :]

    st_ref[...] = lax.fori_loop(0, nblk, body, st_ref[...])


def _s5_carry(e, ax, ay, nb, n_ctx_chunks, rows_per_chunk):
    nd, total_rows, lanes = e.shape
    n_chunks = total_rows // rows_per_chunk
    n_lat = n_chunks - n_ctx_chunks

    def idx(d, j):
        jc = jnp.minimum(j, n_ctx_chunks - 1)
        jl = jnp.maximum(j - n_ctx_chunks, 0)
        fwd = j
        bwd = jnp.where(j < n_ctx_chunks, n_ctx_chunks - 1 - jc, n_chunks - 1 - jl)
        return (d, jnp.where(d == 0, fwd, bwd), 0)

    kern = functools.partial(_carry_kernel, nb=nb)
    return pl.pallas_call(
        kern,
        grid=(nd, n_chunks),
        in_specs=[pl.BlockSpec((1, rows_per_chunk, lanes), idx),
                  pl.BlockSpec((1, 1, lanes), lambda d, j: (d, 0, 0)),
                  pl.BlockSpec((1, 1, lanes), lambda d, j: (d, 0, 0))],
        out_specs=pl.BlockSpec((1, rows_per_chunk, lanes), idx),
        out_shape=jax.ShapeDtypeStruct(e.shape, F32),
        scratch_shapes=[pltpu.VMEM((nb, lanes), F32)],
        compiler_params=_params("arbitrary", "arbitrary"),
        name="s5_carry",
    )(e, ax, ay)


def _output_kernel(xg_ref, s_ref, u_ref, d_ref, tfb_ref, cst_ref, y_ref, yg_ref, *, nb, groups, q):
    rows = xg_ref.shape[0]
    nblk = rows // nb
    gw = TIME_BLOCK * q
    half = s_ref.shape[2] // groups
    for g in range(groups):
        sg = jnp.concatenate([s_ref[0, :, g * half:(g + 1) * half], s_ref[1, :, g * half:(g + 1) * half]], axis=1)
        yg_ref[:, g * gw:(g + 1) * gw] = (
            jnp.dot(xg_ref[:, g * gw:(g + 1) * gw], tfb_ref[g], preferred_element_type=F32)
            + jnp.dot(sg.astype(BF16), cst_ref[g], preferred_element_type=F32))
    per_tile = LANES // q
    chunk = _lane_chunk((rows, LANES))
    sw = groups * q
    steps = []
    for s in range(TIME_BLOCK):
        h, pos = divmod(s, per_tile)
        tiles = []
        for jt in range(sw // LANES):
            acc = jnp.zeros((rows, LANES), F32)
            for qq in range(per_tile):
                g = jt * per_tile + qq
                a = yg_ref[:, (2 * g + h) * LANES:(2 * g + h + 1) * LANES]
                shift = ((qq - pos) % per_tile) * q
                r = pltpu.roll(a, shift, axis=1) if shift else a
                acc = jnp.where(chunk == qq, r, acc)
            tiles.append(acc)
        steps.append(jnp.concatenate(tiles, axis=1).reshape(nblk, nb, sw))
    y = jnp.concatenate(steps, axis=1).reshape(nblk * TIME_BLOCK * nb, sw)
    y_ref[...] = y + d_ref[...] * u_ref[...]


def _s5_outputs(xg, s, u_lat, ssm_d, tfb, cst, nb, tc, n_ctx_chunks):
    rows_l, sw = u_lat.shape
    rc = tc * nb
    n_lat = rows_l // rc
    groups, gw, _ = tfb.shape
    q = gw // TIME_BLOCK
    rows_out = (tc // TIME_BLOCK) * nb
    kern = functools.partial(_output_kernel, nb=nb, groups=groups, q=q)
    return pl.pallas_call(
        kern,
        grid=(n_lat,),
        in_specs=[pl.BlockSpec((rows_out, groups * gw), lambda j: (j + n_ctx_chunks, 0)),
                  pl.BlockSpec((2, rows_out, s.shape[2]), lambda j: (0, j + n_ctx_chunks, 0)),
                  pl.BlockSpec((rc, sw), lambda j: (j, 0)),
                  pl.BlockSpec((1, sw), lambda j: (0, 0)),
                  pl.BlockSpec(tfb.shape, lambda j: (0, 0, 0), pipeline_mode=pl.Buffered(1)),
                  pl.BlockSpec(cst.shape, lambda j: (0, 0, 0), pipeline_mode=pl.Buffered(1))],
        out_specs=pl.BlockSpec((rc, sw), lambda j: (j, 0)),
        out_shape=jax.ShapeDtypeStruct((rows_l, sw), F32),
        scratch_shapes=[pltpu.VMEM((rows_out, groups * gw), F32)],
        compiler_params=_params("arbitrary"),
        name="s5_output",
    )(xg, s, u_lat, ssm_d, tfb, cst)


def _s5_blocked(u_ctx, u_lat, ssm_d, ops, nb, tc):
    tfb, bst, cst, ax, ay = ops
    n_ctx_chunks = u_ctx.shape[0] // (tc * nb)
    xg, e = _s5_regroup(u_ctx, u_lat, bst, nb, tc)
    s = _s5_carry(e, ax, ay, nb, n_ctx_chunks, (tc // TIME_BLOCK) * nb)
    return _s5_outputs(xg, s, u_lat, ssm_d, tfb, cst, nb, tc, n_ctx_chunks)


def _mixer_kernel(y_ref, sga_ref, gbt_ref, x_ref, mod_ref, wglu_ref, wso_ref, wo_ref, o_ref, *, d, sw):
    nb, tt, _ = x_ref.shape
    rows = nb * tt
    y = _to_batch_major(y_ref[...], nb, tt)
    v = _dot(jax.nn.gelu(y), wglu_ref[...])
    ys = v[:, 0:sw] * jax.nn.sigmoid(v[:, sw:2 * sw])
    y_a = _dot(ys, wso_ref[...])
    merged = sga_ref[...].reshape(rows, d) * y_a + gbt_ref[...].reshape(rows, d)
    o = _dot(merged, wo_ref[...])
    g1 = _per_row(mod_ref[...], 2 * d, 3 * d, tt)
    o_ref[...] = (x_ref[...].reshape(rows, d) + g1 * o).reshape(nb, tt, d)


def _mixer_out(y, sga, gbt, x, mod3, w_glu_b, w_ssm_out_b, w_o_b):
    b, l, d = x.shape
    sw = y.shape[1]
    tt = TIME_TILE
    kern = functools.partial(_mixer_kernel, d=d, sw=sw)
    tok = pl.BlockSpec((b, tt, d), lambda j: (0, j, 0))
    return pl.pallas_call(
        kern,
        grid=(l // tt,),
        in_specs=[pl.BlockSpec((tt * b, sw), lambda j: (j, 0)),
                  tok, tok, tok,
                  pl.BlockSpec((b, 1, mod3.shape[2]), lambda j: (0, 0, 0)),
                  _full(w_glu_b), _full(w_ssm_out_b), _full(w_o_b)],
        out_specs=tok,
        out_shape=jax.ShapeDtypeStruct((b, l, d), F32),
        compiler_params=_params("arbitrary"),
        name="mixer_out",
    )(y, sga, gbt, x, mod3, w_glu_b, w_ssm_out_b, w_o_b)


def _split_bf16(a):
    hi = a.astype(BF16)
    lo = (a - hi.astype(F32)).astype(BF16)
    return hi, lo


def _norm2(xl, m, g, d):
    return _rms(xl, g) * (1.0 + m[:, 4 * d:5 * d]) + m[:, 3 * d:4 * d]


def _shared_kernel(xl_ref, mod_ref, g_ref, wsg_ref, wsu_ref, wsd_ref, base_ref, *, d):
    m = mod_ref[0]
    xl = xl_ref[...]
    hb = _norm2(xl, m, g_ref[...], d).astype(BF16)
    sg = jnp.dot(hb, wsg_ref[...], preferred_element_type=F32)
    su = jnp.dot(hb, wsu_ref[...], preferred_element_type=F32)
    shared = _dot(sg * jax.nn.sigmoid(sg) * su, wsd_ref[...])
    base_ref[...] = xl + m[:, 5 * d:6 * d] * shared


def _shared(xl2, mod3, tiles_per_batch, g2n, ws_gate_b, ws_up_b, ws_down_b):
    n, d = xl2.shape
    t = TOKEN_TILE
    tok = pl.BlockSpec((t, d), lambda i: (i, 0))
    return pl.pallas_call(
        functools.partial(_shared_kernel, d=d),
        grid=(n // t,),
        in_specs=[tok,
                  pl.BlockSpec((1, 1, mod3.shape[2]), lambda i: (i // tiles_per_batch, 0, 0)),
                  _full(g2n), _full(ws_gate_b), _full(ws_up_b), _full(ws_down_b)],
        out_specs=tok,
        out_shape=jax.ShapeDtypeStruct((n, d), F32),
        compiler_params=_params("arbitrary"),
        name="shared",
    )(xl2, mod3, g2n, ws_gate_b, ws_up_b, ws_down_b)


def _route_kernel(xl_ref, mod_ref, g_ref, whi_ref, wlo_ref, rb_ref,
                  h2_ref, eidx_ref, wts_ref, rank_ref, cnt_ref, carry_ref, *, d):
    i = pl.program_id(0)
    t = xl_ref.shape[0]
    h2 = _norm2(xl_ref[...], mod_ref[0], g_ref[...], d)
    _store_packed(h2_ref, h2)

    h_hi, h_lo = _split_bf16(h2)
    w_hi, w_lo = whi_ref[...], wlo_ref[...]
    nt = (((1,), (1,)), ((), ()))
    logits = (lax.dot_general(w_hi, h_hi, nt, preferred_element_type=F32)
              + lax.dot_general(w_hi, h_lo, nt, preferred_element_type=F32)
              + lax.dot_general(w_lo, h_hi, nt, preferred_element_type=F32))
    scores = jax.nn.sigmoid(logits)
    choice = scores + rb_ref[...]

    epg = EXPERTS_PER_GROUP
    gi = lax.broadcasted_iota(I32, (epg, t), 0)
    gs = []
    for g in range(N_EXPERT_GROUPS):
        seg = choice[g * epg:(g + 1) * epg, :]
        m1 = jnp.max(seg, axis=0, keepdims=True)
        i1 = jnp.min(jnp.where(seg == m1, gi, epg), axis=0, keepdims=True)
        m2 = jnp.max(jnp.where(gi == i1, -jnp.inf, seg), axis=0, keepdims=True)
        gs.append(m1 + m2)
    masked = []
    for g in range(N_EXPERT_GROUPS):
        beat = jnp.zeros((1, t), I32)
        for g2 in range(N_EXPERT_GROUPS):
            if g2 < g:
                beat = beat + (gs[g2] >= gs[g]).astype(I32)
            elif g2 > g:
                beat = beat + (gs[g2] > gs[g]).astype(I32)
        keep = beat < TOP_K_GROUPS
        masked.append(jnp.where(keep, choice[g * epg:(g + 1) * epg, :], -jnp.inf))
    cur = jnp.concatenate(masked, axis=0)

    ei_all = lax.broadcasted_iota(I32, (N_EXPERTS, t), 0)
    picks, raw = [], []
    onehot = jnp.zeros((N_EXPERTS, t), F32)
    for _ in range(TOP_K):
        mx = jnp.max(cur, axis=0, keepdims=True)
        ei = jnp.min(jnp.where(cur == mx, ei_all, N_EXPERTS), axis=0, keepdims=True)
        hit = ei_all == ei
        raw.append(jnp.sum(jnp.where(hit, scores, 0.0), axis=0, keepdims=True))
        cur = jnp.where(hit, -jnp.inf, cur)
        onehot = jnp.where(hit, 1.0, onehot)
        picks.append(ei)
    tot = raw[0]
    for k in range(1, TOP_K):
        tot = tot + raw[k]

    @pl.when(i == 0)
    def _():
        carry_ref[...] = jnp.zeros_like(carry_ref)

    upper = (lax.broadcasted_iota(I32, (t, t), 0) < lax.broadcasted_iota(I32, (t, t), 1)).astype(BF16)
    before = jnp.dot(onehot.astype(BF16), upper, preferred_element_type=F32) + carry_ref[:, 0:1]
    for k in range(TOP_K):
        eidx_ref[k:k + 1, :] = picks[k]
        wts_ref[k:k + 1, :] = raw[k] / tot * ROUTE_SCALE
        rk = jnp.sum(jnp.where(ei_all == picks[k], before, 0.0), axis=0, keepdims=True)
        rank_ref[k:k + 1, :] = rk.astype(I32)
    carry_ref[...] = carry_ref[...] + jnp.sum(onehot, axis=1, keepdims=True)
    cnt_ref[...] = carry_ref[...]


def _route(xl2, mod3, tiles_per_batch, g2n, w_router_t, router_bias):
    n, d = xl2.shape
    t = TOKEN_TILE
    e = w_router_t.shape[0]
    w_top = lax.bitcast_convert_type(lax.bitcast_convert_type(w_router_t, U32) & jnp.uint32(0xFFFF0000), F32)
    w_hi, w_lo = w_top.astype(BF16), (w_router_t - w_top).astype(BF16)
    kern = functools.partial(_route_kernel, d=d)
    tok = pl.BlockSpec((t, d), lambda i: (i, 0))
    small = pl.BlockSpec((TOP_K, t), lambda i: (0, i))
    return pl.pallas_call(
        kern,
        grid=(n // t,),
        in_specs=[tok,
                  pl.BlockSpec((1, 1, mod3.shape[2]), lambda i: (i // tiles_per_batch, 0, 0)),
                  _full(g2n), _full(w_hi), _full(w_lo), _full(router_bias)],
        out_specs=[pl.BlockSpec((t * PACK_SUBLANES, LANES), lambda i: (i, 0)), small, small, small,
                   pl.BlockSpec((e, LANES), lambda i: (0, 0))],
        out_shape=[jax.ShapeDtypeStruct((n * PACK_SUBLANES, LANES), U32),
                   jax.ShapeDtypeStruct((TOP_K, n), I32),
                   jax.ShapeDtypeStruct((TOP_K, n), F32),
                   jax.ShapeDtypeStruct((TOP_K, n), I32),
                   jax.ShapeDtypeStruct((e, LANES), F32)],
        scratch_shapes=[pltpu.VMEM((e, LANES), F32)],
        compiler_params=_params("arbitrary"),
        name="route",
    )(xl2, mod3, g2n, w_hi, w_lo, router_bias)


def _dest_kernel(eidx_ref, rank_ref, start_ref, o_ref):
    t = eidx_ref.shape[1]
    ei_all = lax.broadcasted_iota(I32, (N_EXPERTS, t), 0)
    st = start_ref[:, 0:1]
    for k in range(TOP_K):
        hit = ei_all == eidx_ref[k:k + 1, :]
        o_ref[k:k + 1, :] = jnp.sum(jnp.where(hit, st, 0), axis=0, keepdims=True) + rank_ref[k:k + 1, :]


def _dest_rows(eidx, rank, starts_b):
    n = eidx.shape[1]
    t = 1024
    small = pl.BlockSpec((TOP_K, t), lambda i: (0, i))
    return pl.pallas_call(
        _dest_kernel,
        grid=(n // t,),
        in_specs=[small, small, _full(starts_b)],
        out_specs=small,
        out_shape=jax.ShapeDtypeStruct((TOP_K, n), I32),
        compiler_params=_params("arbitrary"),
        name="dest",
    )(eidx, rank, starts_b)


def _sc_workers():
    info = plsc.get_sparse_core_info()
    return info.num_cores, info.num_cores * info.num_subcores


def _sc_dispatch(dest_flat, src3, out_rows):
    n = src3.shape[0]
    n_cores, n_workers = _sc_workers()
    per_worker = n // n_workers
    assert per_worker % SC_CHUNK == 0
    mesh = plsc.VectorSubcoreMesh(core_axis_name="c", subcore_axis_name="s")

    @functools.partial(
        pl.kernel, mesh=mesh,
        out_type=jax.ShapeDtypeStruct((out_rows,) + src3.shape[1:], src3.dtype),
        scratch_types=[pltpu.VMEM((SC_CHUNK,) + src3.shape[1:], src3.dtype)]
        + [pltpu.VMEM((SC_CHUNK,), I32)] * TOP_K,
        name="dispatch",
    )
    def run(dest_hbm, src_hbm, out_hbm, rows_v, *idx_v):
        worker = lax.axis_index("s") * n_cores + lax.axis_index("c")

        @pl.loop(0, per_worker // SC_CHUNK)
        def _(j):
            base = worker * per_worker + j * SC_CHUNK
            pltpu.sync_copy(src_hbm.at[pl.ds(base, SC_CHUNK)], rows_v)
            for k in range(TOP_K):
                pltpu.sync_copy(dest_hbm.at[pl.ds(k * n + base, SC_CHUNK)], idx_v[k])
            for k in range(TOP_K):
                pltpu.sync_copy(rows_v, out_hbm.at[idx_v[k]])

    return run(dest_flat, src3)


def _sc_combine(dest_flat, ys3, n):
    n_cores, n_workers = _sc_workers()
    per_worker = n // n_workers
    assert per_worker % SC_CHUNK == 0
    mesh = plsc.VectorSubcoreMesh(core_axis_name="c", subcore_axis_name="s")

    @functools.partial(
        pl.kernel, mesh=mesh,
        out_type=jax.ShapeDtypeStruct((TOP_K * n,) + ys3.shape[1:], ys3.dtype),
        scratch_types=[pltpu.VMEM((SC_CHUNK,) + ys3.shape[1:], ys3.dtype), pltpu.VMEM((SC_CHUNK,), I32)],
        name="combine",
    )
    def run(dest_hbm, ys_hbm, out_hbm, rows_v, idx_v):
        worker = lax.axis_index("s") * n_cores + lax.axis_index("c")

        @pl.loop(0, per_worker // SC_CHUNK)
        def _(j):
            base = worker * per_worker + j * SC_CHUNK
            for k in range(TOP_K):
                pltpu.sync_copy(dest_hbm.at[pl.ds(k * n + base, SC_CHUNK)], idx_v)
                pltpu.sync_copy(ys_hbm.at[idx_v], rows_v)
                pltpu.sync_copy(rows_v, out_hbm.at[pl.ds(k * n + base, SC_CHUNK)])

    return run(dest_flat, ys3)


def _chunk_metadata(counts, n_rows):
    ch = EXPERT_CHUNK
    n_ch = (counts + ch - 1) // ch
    cum = jnp.cumsum(n_ch)
    total = cum[-1]
    max_chunks = n_rows // ch + N_EXPERTS
    i = jnp.arange(max_chunks, dtype=I32)
    e = jnp.sum((cum[None, :] <= i[:, None]).astype(I32), axis=1)
    e_last = jnp.max(jnp.where(counts > 0, jnp.arange(N_EXPERTS, dtype=I32), 0))
    exp = jnp.where(i < total, jnp.minimum(e, N_EXPERTS - 1), e_last).astype(I32)
    newexp = jnp.concatenate([jnp.ones((1,), I32), (exp[1:] != exp[:-1]).astype(I32)])
    change_at = jnp.where(newexp == 1, i, max_chunks)
    nxt_change = lax.cummin(jnp.concatenate([change_at[1:], jnp.array([max_chunks], I32)]), reverse=True)
    nexp = jnp.where(nxt_change < max_chunks, exp[jnp.minimum(nxt_change, max_chunks - 1)], -1).astype(I32)
    starts = (cum - n_ch) * ch
    return starts.astype(I32), (exp, newexp, nexp, total.astype(I32).reshape(1))


def _expert_kernel(exp_ref, newexp_ref, nexp_ref, total_ref, xs_hbm, wg_hbm, wu_hbm, wd_hbm, after_hbm, ys_hbm,
                   xbuf, ybuf, wgf, wuf, wdf, wgb, wub, wdb, sem_x, sem_y, sem_w):
    del after_hbm
    ch = EXPERT_CHUNK
    nbuf = EXPERT_RING
    rows = ch * PACK_SUBLANES
    total = total_ref[0]

    def x_copy(i, slot):
        r0 = pl.multiple_of(i * rows, rows)
        return pltpu.make_async_copy(xs_hbm.at[pl.ds(r0, rows)], xbuf.at[slot], sem_x.at[slot])

    def y_copy(i, slot):
        r0 = pl.multiple_of(i * rows, rows)
        return pltpu.make_async_copy(ybuf.at[slot], ys_hbm.at[pl.ds(r0, rows)], sem_y.at[slot])

    def weight_copies(e):
        return (pltpu.make_async_copy(wg_hbm.at[e], wgf, sem_w.at[0]),
                pltpu.make_async_copy(wu_hbm.at[e], wuf, sem_w.at[1]),
                pltpu.make_async_copy(wd_hbm.at[e], wdf, sem_w.at[2]))

    for cp in weight_copies(exp_ref[0]):
        cp.start()
    for b in range(nbuf - 1):
        @pl.when(b < total)
        def _(b=b):
            x_copy(b, b).start()

    def chunk(i, carry):
        slot = lax.rem(i, nbuf)
        x_copy(i, slot).wait()
        ahead = i + (nbuf - 1)

        @pl.when(ahead < total)
        def _():
            x_copy(ahead, lax.rem(ahead, nbuf)).start()

        @pl.when(newexp_ref[i] == 1)
        def _():
            for cp in weight_copies(exp_ref[i]):
                cp.wait()
            wgb[...] = wgf[...].astype(BF16)
            wub[...] = wuf[...].astype(BF16)
            wdb[...] = wdf[...].astype(BF16)

            @pl.when(nexp_ref[i] >= 0)
            def _():
                for cp in weight_copies(nexp_ref[i]):
                    cp.start()

        @pl.when(i >= nbuf)
        def _():
            y_copy(i, slot).wait()

        xr = xbuf.at[slot]
        yr = ybuf.at[slot]
        x = _load_packed(lambda sl: xr[sl, :], ch).astype(BF16)
        g = jnp.dot(x, wgb[...], preferred_element_type=F32)
        u = jnp.dot(x, wub[...], preferred_element_type=F32)
        y = jnp.dot((g * jax.nn.sigmoid(g) * u).astype(BF16), wdb[...], preferred_element_type=F32)
        _store_packed(yr, y)
        y_copy(i, slot).start()
        return carry

    lax.fori_loop(0, total, chunk, 0)

    for b in range(nbuf):
        @pl.when(b < total)
        def _(b=b):
            y_copy(0, b).wait()


def _experts(meta, xs, w_gate, w_up, w_down, after):
    e, d, f = w_gate.shape
    hbm = pl.BlockSpec(memory_space=pl.ANY)
    rows = EXPERT_CHUNK * PACK_SUBLANES
    grid_spec = pltpu.PrefetchScalarGridSpec(
        num_scalar_prefetch=len(meta), grid=(1,),
        in_specs=[hbm, hbm, hbm, hbm, hbm], out_specs=hbm,
        scratch_shapes=[pltpu.VMEM((EXPERT_RING, rows, LANES), U32), pltpu.VMEM((EXPERT_RING, rows, LANES), U32),
                        pltpu.VMEM((d, f), F32), pltpu.VMEM((d, f), F32), pltpu.VMEM((f, d), F32),
                        pltpu.VMEM((d, f), BF16), pltpu.VMEM((d, f), BF16), pltpu.VMEM((f, d), BF16),
                        pltpu.SemaphoreType.DMA((EXPERT_RING,)), pltpu.SemaphoreType.DMA((EXPERT_RING,)),
                        pltpu.SemaphoreType.DMA((3,))])
    return pl.pallas_call(
        _expert_kernel,
        grid_spec=grid_spec,
        out_shape=jax.ShapeDtypeStruct(xs.shape, xs.dtype),
        compiler_params=_params("arbitrary"),
        name="experts",
    )(*meta, xs, w_gate, w_up, w_down, after)


def _final_kernel(base_ref, yt_ref, wt_ref, mod_ref, g_ref, *rest, d):
    o_ref = rest[-1]
    t = base_ref.shape[0]
    w = wt_ref[...]
    routed = w[:, 0:1] * _load_packed(lambda sl: yt_ref[0, sl, :], t)
    for k in range(1, TOP_K):
        routed = routed + w[:, k:k + 1] * _load_packed(lambda sl, k=k: yt_ref[k, sl, :], t)
    g2 = mod_ref[0][:, 5 * d:6 * d]
    o_ref[...] = _rms(base_ref[...] + g2 * routed, g_ref[...])


def _final(base, ytok3, wts_t, mod3, tiles_per_batch, final_g, part, n_parts, prev_out):
    n, d = base.shape
    t = TOKEN_TILE
    tiles = n // t // n_parts
    off = part * tiles
    kern = functools.partial(_final_kernel, d=d)
    tok = pl.BlockSpec((t, d), lambda i: (off + i, 0))
    in_specs = [tok,
                pl.BlockSpec((TOP_K, t * PACK_SUBLANES, LANES), lambda i: (0, i, 0)),
                pl.BlockSpec((t, TOP_K), lambda i: (off + i, 0)),
                pl.BlockSpec((1, 1, mod3.shape[2]), lambda i: ((off + i) // tiles_per_batch, 0, 0)),
                _full(final_g)]
    args = [base, ytok3, wts_t, mod3, final_g]
    aliases = {}
    if prev_out is not None:
        in_specs.append(pl.BlockSpec(memory_space=pl.ANY))
        args.append(prev_out)
        aliases = {len(args) - 1: 0}
    return pl.pallas_call(
        kern,
        grid=(tiles,),
        in_specs=in_specs,
        out_specs=tok,
        out_shape=jax.ShapeDtypeStruct((n, d), F32),
        input_output_aliases=aliases,
        compiler_params=_params("arbitrary"),
        name="final",
    )(*args)


def kernel(x, c, ctx, c_ctx, w_mod, b_mod, norm1_g, norm2_g, w_in, ssm_lam_re, ssm_lam_im, ssm_log_dt, ssm_b_re, ssm_b_im, ssm_c_re, ssm_c_im, ssm_d, w_glu, w_ssm_out, conv_w, w_conv_out, w_o, w_router, router_bias, w_gate, w_up, w_down, ws_gate, ws_up, ws_down, final_g):
    b, l, d = x.shape
    n = b * l
    sw = ssm_d.shape[1]
    assert w_mod.shape[0] == 1, "single layer"
    assert b == SUBLANES and l % TOKEN_TILE == 0 and l % S5_CHUNK == 0 and ctx.shape[1] % S5_CHUNK == 0 and TIME_TILE % GRID_W == 0

    mod_rows = 2 * SUBLANES
    c_all = jnp.concatenate([c, c_ctx[None, :], jnp.zeros((mod_rows - b - 1, d), F32)], axis=0)
    mod = _modulation(c_all, w_mod[0], b_mod[0])
    mod3 = mod.reshape(mod_rows, 1, mod.shape[1])

    w_in_b = w_in[0].astype(BF16)
    g1n = norm1_g[0].reshape(1, d)
    u_lat, sga, gbt = _in_proj(x, mod3, g1n, w_in_b, conv_w[0], w_conv_out[0].astype(BF16))
    u_ctx = _ctx_proj(ctx, mod3, b, g1n, w_in_b[:, :sw])

    ops = _blocked_operators(ssm_lam_re[0], ssm_lam_im[0], ssm_log_dt[0], ssm_b_re[0], ssm_b_im[0],
                             ssm_c_re[0], ssm_c_im[0])
    y = _s5_blocked(u_ctx, u_lat, ssm_d[0].reshape(1, sw), ops, b, S5_CHUNK)

    xl = _mixer_out(y, sga, gbt, x, mod3, w_glu[0].astype(BF16), w_ssm_out[0].astype(BF16),
                    w_o[0].astype(BF16))

    tiles_per_batch = l // TOKEN_TILE
    xl2 = xl.reshape(n, d)
    g2n = norm2_g[0].reshape(1, d)
    h2, eidx, wts, rank, cnt = _route(xl2, mod3, tiles_per_batch, g2n, w_router[0].T,
                                      router_bias[0].reshape(N_EXPERTS, 1))

    counts = cnt[:, 0].astype(I32)
    n_rows = n * TOP_K
    starts, meta = _chunk_metadata(counts, n_rows)
    buf_rows = n_rows + N_EXPERTS * EXPERT_CHUNK
    dest = _dest_rows(eidx, rank, jnp.broadcast_to(starts[:, None], (N_EXPERTS, LANES)))

    xs = _sc_dispatch(dest.reshape(TOP_K * n), h2.reshape(n, PACK_SUBLANES, LANES), buf_rows)
    base = _shared(xl2, mod3, tiles_per_batch, g2n, ws_gate[0].astype(BF16), ws_up[0].astype(BF16),
                   ws_down[0].astype(BF16))
    ys = _experts(meta, xs.reshape(buf_rows * PACK_SUBLANES, LANES), w_gate[0], w_up[0], w_down[0], base)
    ys3 = ys.reshape(buf_rows, PACK_SUBLANES, LANES)

    n_part = n // COMBINE_PARTS
    wts_t = wts.T
    out = None
    for part in range(COMBINE_PARTS):
        dest_part = dest[:, part * n_part:(part + 1) * n_part].reshape(TOP_K * n_part)
        ytok = _sc_combine(dest_part, ys3, n_part)
        out = _final(base, ytok.reshape(TOP_K, n_part * PACK_SUBLANES, LANES), wts_t, mod3, tiles_per_batch,
                     final_g.reshape(1, d), part, COMBINE_PARTS, out)
    return out.reshape(b, l, d)
```

```python
import functools

import jax
import jax.numpy as jnp
from jax import lax
from jax.experimental import pallas as pl
from jax.experimental.pallas import tpu as pltpu
from jax.experimental.pallas import tpu_sc as plsc

F32 = jnp.float32
BF16 = jnp.bfloat16
I32 = jnp.int32
U32 = jnp.uint32

EPS = 1e-6
GRID_W = 64
SSM_GROUP = 16
N_EXPERTS = 256
TOP_K = 8
N_EXPERT_GROUPS = 8
EXPERTS_PER_GROUP = N_EXPERTS // N_EXPERT_GROUPS
TOP_K_GROUPS = 4
ROUTE_SCALE = 2.5

SUBLANES = 8
LANES = 128
VMEM_LIMIT_BYTES = 48 * 1024 * 1024

TIME_TILE = 64
TOKEN_TILE = 256
TIME_BLOCK = 16
S5_CHUNK = 256
EXPERT_CHUNK = 256
EXPERT_RING = 4
SC_CHUNK = 128
COMBINE_PARTS = 4
PACK_SUBLANES = 4


def _dot(a, b):
    return jnp.dot(a.astype(BF16), b.astype(BF16), preferred_element_type=F32)


def _rms(xf, g):
    return xf * lax.rsqrt(jnp.mean(xf * xf, axis=-1, keepdims=True) + EPS) * g


def _params(*sem):
    return pltpu.CompilerParams(dimension_semantics=sem, vmem_limit_bytes=VMEM_LIMIT_BYTES)


def _full(a):
    return pl.BlockSpec(a.shape, lambda *_: (0,) * a.ndim)


def _pack_rows(v):
    half = v.shape[1] // 2
    lo = lax.bitcast_convert_type(v[:, :half].astype(BF16).astype(F32), U32) >> 16
    hi = lax.bitcast_convert_type(v[:, half:].astype(BF16).astype(F32), U32) & jnp.uint32(0xFFFF0000)
    return lo | hi


def _unpack_lo(w):
    return lax.bitcast_convert_type(w << 16, F32)


def _unpack_hi(w):
    return lax.bitcast_convert_type(w & jnp.uint32(0xFFFF0000), F32)


def _store_packed(ref, v):
    t = v.shape[0]
    w = _pack_rows(v)
    for c in range(PACK_SUBLANES):
        ref[pl.ds(c, t, stride=PACK_SUBLANES), :] = w[:, c * LANES:(c + 1) * LANES]


def _load_packed(load, t):
    ws = [load(pl.ds(c, t, stride=PACK_SUBLANES)) for c in range(PACK_SUBLANES)]
    return jnp.concatenate([_unpack_lo(w) for w in ws] + [_unpack_hi(w) for w in ws], axis=1)


def _mod_kernel(c_ref, w_ref, b_ref, o_ref):
    c = c_ref[...]
    o_ref[...] = _dot(c * jax.nn.sigmoid(c), w_ref[...]) + b_ref[...]


def _modulation(c_all, w_mod, b_mod):
    rows, d = c_all.shape
    cols = w_mod.shape[1]
    blk = 1536
    return pl.pallas_call(
        _mod_kernel,
        grid=(cols // blk,),
        in_specs=[pl.BlockSpec((rows, d), lambda j: (0, 0)),
                  pl.BlockSpec((d, blk), lambda j: (0, j)),
                  pl.BlockSpec((1, blk), lambda j: (0, j))],
        out_specs=pl.BlockSpec((rows, blk), lambda j: (0, j)),
        out_shape=jax.ShapeDtypeStruct((rows, cols), F32),
        compiler_params=_params("arbitrary"),
        name="mod",
    )(c_all, w_mod, b_mod.reshape(1, cols))


def _per_row(m3, lo, hi, tt):
    nb = m3.shape[0]
    return jnp.broadcast_to(m3[:, :, lo:hi], (nb, tt, hi - lo)).reshape(nb * tt, hi - lo)


def _to_time_major(val, nb, tt):
    c = val.shape[1]
    return pltpu.einshape("btc->tbc", val.reshape(nb, tt, c)).reshape(nb * tt, c)


def _to_batch_major(val, nb, tt):
    c = val.shape[1]
    return pltpu.einshape("tbc->btc", val.reshape(tt, nb, c)).reshape(nb * tt, c)


def _in_proj_kernel(x_ref, mod_ref, g_ref, w_ref, cw_ref, wco_ref, u_ref, sga_ref, gbt_ref, *, d, sw):
    nb, tt, _ = x_ref.shape
    rows = nb * tt
    m3 = mod_ref[...]
    x = x_ref[...].reshape(rows, d)
    h = _rms(x, g_ref[...]) * (1.0 + _per_row(m3, d, 2 * d, tt)) + _per_row(m3, 0, d, tt)
    hb = h.astype(BF16)
    u_ref[...] = _to_time_major(jnp.dot(hb, w_ref[:, 0:sw], preferred_element_type=F32), nb, tt)
    cb = jnp.dot(hb, w_ref[:, sw:2 * sw], preferred_element_type=F32)
    cc = jnp.dot(hb, w_ref[:, 2 * sw:3 * sw], preferred_element_type=F32)
    cv = jnp.dot(hb, w_ref[:, 3 * sw:4 * sw], preferred_element_type=F32)
    ccv = cc * cv
    col = lax.broadcasted_iota(I32, ccv.shape, 0) % GRID_W
    prev = jnp.where(col == 0, 0.0, pltpu.roll(ccv, 1, axis=0))
    nxt = jnp.where(col == GRID_W - 1, 0.0, pltpu.roll(ccv, rows - 1, axis=0))
    cw = cw_ref[...]
    conv = prev * cw[0:1, :] + ccv * cw[1:2, :] + nxt * cw[2:3, :]
    y_conv = _dot(cb * conv, wco_ref[...])
    ga = jnp.dot(hb, w_ref[:, 4 * sw:4 * sw + d], preferred_element_type=F32)
    gb = jnp.dot(hb, w_ref[:, 4 * sw + d:4 * sw + 2 * d], preferred_element_type=F32)
    sga_ref[...] = jax.nn.sigmoid(ga).reshape(nb, tt, d)
    gbt_ref[...] = (jax.nn.sigmoid(gb) * y_conv).reshape(nb, tt, d)


def _in_proj(x, mod3, g1n, w_in_b, conv_w, w_conv_out_b):
    b, l, d = x.shape
    sw = conv_w.shape[1]
    tt = TIME_TILE
    kern = functools.partial(_in_proj_kernel, d=d, sw=sw)
    tok = pl.BlockSpec((b, tt, d), lambda j: (0, j, 0))
    return pl.pallas_call(
        kern,
        grid=(l // tt,),
        in_specs=[tok,
                  pl.BlockSpec((b, 1, mod3.shape[2]), lambda j: (0, 0, 0)),
                  _full(g1n), _full(w_in_b), _full(conv_w), _full(w_conv_out_b)],
        out_specs=[pl.BlockSpec((tt * b, sw), lambda j: (j, 0)), tok, tok],
        out_shape=[jax.ShapeDtypeStruct((l * b, sw), F32),
                   jax.ShapeDtypeStruct((b, l, d), F32),
                   jax.ShapeDtypeStruct((b, l, d), F32)],
        compiler_params=_params("arbitrary"),
        name="in_proj",
    )(x, mod3, g1n, w_in_b, conv_w, w_conv_out_b)


def _ctx_proj_kernel(x_ref, mod_ref, g_ref, w_ref, u_ref, *, d):
    nb, tt, _ = x_ref.shape
    m = mod_ref[0]
    x = x_ref[...].reshape(nb * tt, d)
    h = _rms(x, g_ref[...]) * (1.0 + m[:, d:2 * d]) + m[:, 0:d]
    u_ref[...] = _to_time_major(_dot(h, w_ref[...]), nb, tt)


def _ctx_proj(ctx, mod3, ctx_row, g1n, w_u_b):
    b, lc, d = ctx.shape
    sw = w_u_b.shape[1]
    tt = TIME_TILE
    kern = functools.partial(_ctx_proj_kernel, d=d)
    return pl.pallas_call(
        kern,
        grid=(lc // tt,),
        in_specs=[pl.BlockSpec((b, tt, d), lambda j: (0, j, 0)),
                  pl.BlockSpec((1, 1, mod3.shape[2]), lambda j: (ctx_row, 0, 0)),
                  _full(g1n), _full(w_u_b)],
        out_specs=pl.BlockSpec((tt * b, sw), lambda j: (j, 0)),
        out_shape=jax.ShapeDtypeStruct((lc * b, sw), F32),
        compiler_params=_params("arbitrary"),
        name="ctx_proj",
    )(ctx, mod3, g1n, w_u_b)


def _blocked_operators(lam_re, lam_im, log_dt, b_re, b_im, c_re, c_im):
    tb = TIME_BLOCK
    dt = jnp.exp(log_dt)[..., None]
    k = jnp.arange(tb + 1, dtype=F32)[None, None, :, None]
    mag = jnp.exp(lam_re * dt)[:, :, None, :] ** k
    ang = (lam_im * dt)[:, :, None, :] * k
    ak_re = mag * jnp.cos(ang)
    ak_im = mag * jnp.sin(ang)
    a_re, a_im = ak_re[:, :, 1], ak_im[:, :, 1]
    den = lam_re * lam_re + lam_im * lam_im
    k_re = ((a_re - 1.0) * lam_re + a_im * lam_im) / den
    k_im = (a_im * lam_re - (a_re - 1.0) * lam_im) / den
    bt_re = jnp.swapaxes(b_re, 2, 3)
    bt_im = jnp.swapaxes(b_im, 2, 3)
    bbt_re = k_re[:, :, None, :] * bt_re - k_im[:, :, None, :] * bt_im
    bbt_im = k_re[:, :, None, :] * bt_im + k_im[:, :, None, :] * bt_re
    nd, g, q, p = bbt_re.shape
    w = tb * q

    akr, aki = ak_re[:, :, :tb, None, :], ak_im[:, :, :tb, None, :]
    ab_re = akr * bbt_re[:, :, None] - aki * bbt_im[:, :, None]
    ab_im = akr * bbt_im[:, :, None] + aki * bbt_re[:, :, None]

    def rows(x):
        return x.reshape(g, w, p)
    bst = jnp.concatenate([rows(ab_re[0][:, ::-1]), rows(ab_im[0][:, ::-1]), rows(ab_re[1]), rows(ab_im[1])], axis=2)

    def lag_matrices(d):
        abq_re = jnp.swapaxes(ab_re[d], 1, 2)
        abq_im = jnp.swapaxes(ab_im[d], 1, 2)
        return (jnp.einsum('gekp,gcp->gekc', abq_re, c_re[d], precision=lax.Precision.HIGHEST)
                - jnp.einsum('gekp,gcp->gekc', abq_im, c_im[d], precision=lax.Precision.HIGHEST))
    mf = lag_matrices(0).reshape(g, q, w)
    mb = lag_matrices(1)[:, :, ::-1].reshape(g, q, w)
    blocks = []
    for s in range(tb):
        fwd = jnp.pad(mf[:, :, :w - s * q], ((0, 0), (0, 0), (s * q, 0)))
        lb = (tb - 1 - s) * q
        bwd = jnp.pad(mb[:, :, lb:], ((0, 0), (0, 0), (0, lb)))
        blocks.append(fwd + bwd)
    tfb = jnp.stack(blocks, axis=1).reshape(g, w, w)

    def read(d, ks):
        ar = ak_re[d][:, ks][:, :, None, :]
        ai = ak_im[d][:, ks][:, :, None, :]
        cr, ci = c_re[d][:, None], c_im[d][:, None]
        return (cr * ar - ci * ai).reshape(g, w, p), (cr * ai + ci * ar).reshape(g, w, p)
    f_re, f_im = read(0, slice(1, tb + 1))
    r_re, r_im = read(1, slice(tb, 0, -1))
    cst = jnp.swapaxes(jnp.concatenate([f_re, -f_im, r_re, -r_im], axis=2), 1, 2)

    ax = jnp.concatenate([ak_re[:, :, tb], ak_re[:, :, tb]], axis=-1).reshape(nd, 1, g * 2 * p)
    ay = jnp.concatenate([-ak_im[:, :, tb], ak_im[:, :, tb]], axis=-1).reshape(nd, 1, g * 2 * p)
    return tfb.astype(BF16), bst.astype(BF16), cst.astype(BF16), ax, ay


def _lane_chunk(shape):
    return lax.broadcasted_iota(I32, shape, 1) // SSM_GROUP


def _regroup_kernel(uc_ref, ul_ref, bst_ref, xg_ref, e_ref, *, n_ctx, nb, groups, q):
    j = pl.program_id(0)
    rows_in = uc_ref.shape[0]
    nblk = rows_in // (TIME_BLOCK * nb)
    rows = nblk * nb
    u = jnp.where(j < n_ctx, uc_ref[...], ul_ref[...])
    u3 = u.reshape(nblk, TIME_BLOCK * nb, u.shape[1])
    xs = [u3[:, s * nb:(s + 1) * nb, :].reshape(rows, u.shape[1]) for s in range(TIME_BLOCK)]
    per_tile = LANES // q
    chunk = _lane_chunk((rows, LANES))
    for g in range(groups):
        jt, qq = divmod(g, per_tile)
        for h in range(TIME_BLOCK // per_tile):
            acc = jnp.zeros((rows, LANES), F32)
            for s in range(h * per_tile, (h + 1) * per_tile):
                pos = s % per_tile
                a = xs[s][:, jt * LANES:(jt + 1) * LANES]
                shift = ((pos - qq) % per_tile) * q
                r = pltpu.roll(a, shift, axis=1) if shift else a
                acc = jnp.where(chunk == pos, r, acc)
            xg_ref[:, (2 * g + h) * LANES:(2 * g + h + 1) * LANES] = acc.astype(BF16)
    gw = TIME_BLOCK * q
    for g in range(groups):
        e = jnp.dot(xg_ref[:, g * gw:(g + 1) * gw], bst_ref[g], preferred_element_type=F32)
        half = e.shape[1] // 2
        e_ref[0, :, g * half:(g + 1) * half] = e[:, :half]
        e_ref[1, :, g * half:(g + 1) * half] = e[:, half:]


def _s5_regroup(u_ctx, u_lat, bst, nb, tc):
    rows_c, sw = u_ctx.shape
    rows_l = u_lat.shape[0]
    rc = tc * nb
    n_ctx, n_lat = rows_c // rc, rows_l // rc
    groups, gw, p4 = bst.shape
    q = gw // TIME_BLOCK
    rows_out = (tc // TIME_BLOCK) * nb
    total_rows = (n_ctx + n_lat) * rows_out
    kern = functools.partial(_regroup_kernel, n_ctx=n_ctx, nb=nb, groups=groups, q=q)
    return pl.pallas_call(
        kern,
        grid=(n_ctx + n_lat,),
        in_specs=[pl.BlockSpec((rc, sw), lambda j: (jnp.minimum(j, n_ctx - 1), 0)),
                  pl.BlockSpec((rc, sw), lambda j: (jnp.maximum(j - n_ctx, 0), 0)),
                  pl.BlockSpec(bst.shape, lambda j: (0, 0, 0), pipeline_mode=pl.Buffered(1))],
        out_specs=[pl.BlockSpec((rows_out, groups * gw), lambda j: (j, 0)),
                   pl.BlockSpec((2, rows_out, groups * p4 // 2), lambda j: (0, j, 0))],
        out_shape=[jax.ShapeDtypeStruct((total_rows, groups * gw), BF16),
                   jax.ShapeDtypeStruct((2, total_rows, groups * p4 // 2), F32)],
        compiler_params=_params("arbitrary"),
        name="s5_regroup",
    )(u_ctx, u_lat, bst)


def _carry_kernel(e_ref, ax_ref, ay_ref, s_ref, st_ref, *, nb):
    dirn = pl.program_id(0)
    j = pl.program_id(1)
    rows = e_ref.shape[1]
    nblk = rows // nb
    lanes = e_ref.shape[2]

    @pl.when(j == 0)
    def _():
        st_ref[...] = jnp.zeros_like(st_ref)

    ax = jnp.broadcast_to(ax_ref[0], (nb, lanes))
    ay = jnp.broadcast_to(ay_ref[0], (nb, lanes))

    def body(i, s):
        blk = jnp.where(dirn == 0, i, nblk - 1 - i)
        r0 = pl.multiple_of(blk * nb, nb)
        s_ref[0, pl.ds(r0, nb), :] = s.astype(s_ref.dtype)
        swapped = jnp.concatenate([pltpu.roll(s[:, t * LANES:(t + 1) * LANES], LANES // 2, axis=1)
                                   for t in range(lanes // LANES)], axis=1)
        return ax * s + ay * swapped + e_ref[0, pl.ds(r0, nb), :]

    st_ref[...] = lax.fori_loop(0, nblk, body, st_ref[...])


def _s5_carry(e, ax, ay, nb, n_ctx_chunks, rows_per_chunk):
    nd, total_rows, lanes = e.shape
    n_chunks = total_rows // rows_per_chunk
    n_lat = n_chunks - n_ctx_chunks

    def idx(d, j):
        jc = jnp.minimum(j, n_ctx_chunks - 1)
        jl = jnp.maximum(j - n_ctx_chunks, 0)
        fwd = j
        bwd = jnp.where(j < n_ctx_chunks, n_ctx_chunks - 1 - jc, n_chunks - 1 - jl)
        return (d, jnp.where(d == 0, fwd, bwd), 0)

    kern = functools.partial(_carry_kernel, nb=nb)
    return pl.pallas_call(
        kern,
        grid=(nd, n_chunks),
        in_specs=[pl.BlockSpec((1, rows_per_chunk, lanes), idx),
                  pl.BlockSpec((1, 1, lanes), lambda d, j: (d, 0, 0)),
                  pl.BlockSpec((1, 1, lanes), lambda d, j: (d, 0, 0))],
        out_specs=pl.BlockSpec((1, rows_per_chunk, lanes), idx),
        out_shape=jax.ShapeDtypeStruct(e.shape, F32),
        scratch_shapes=[pltpu.VMEM((nb, lanes), F32)],
        compiler_params=_params("arbitrary", "arbitrary"),
        name="s5_carry",
    )(e, ax, ay)


def _output_kernel(xg_ref, s_ref, u_ref, d_ref, tfb_ref, cst_ref, y_ref, yg_ref, *, nb, groups, q):
    rows = xg_ref.shape[0]
    nblk = rows // nb
    gw = TIME_BLOCK * q
    half = s_ref.shape[2] // groups
    for g in range(groups):
        sg = jnp.concatenate([s_ref[0, :, g * half:(g + 1) * half], s_ref[1, :, g * half:(g + 1) * half]], axis=1)
        yg_ref[:, g * gw:(g + 1) * gw] = (
            jnp.dot(xg_ref[:, g * gw:(g + 1) * gw], tfb_ref[g], preferred_element_type=F32)
            + jnp.dot(sg.astype(BF16), cst_ref[g], preferred_element_type=F32))
    per_tile = LANES // q
    chunk = _lane_chunk((rows, LANES))
    sw = groups * q
    steps = []
    for s in range(TIME_BLOCK):
        h, pos = divmod(s, per_tile)
        tiles = []
        for jt in range(sw // LANES):
            acc = jnp.zeros((rows, LANES), F32)
            for qq in range(per_tile):
                g = jt * per_tile + qq
                a = yg_ref[:, (2 * g + h) * LANES:(2 * g + h + 1) * LANES]
                shift = ((qq - pos) % per_tile) * q
                r = pltpu.roll(a, shift, axis=1) if shift else a
                acc = jnp.where(chunk == qq, r, acc)
            tiles.append(acc)
        steps.append(jnp.concatenate(tiles, axis=1).reshape(nblk, nb, sw))
    y = jnp.concatenate(steps, axis=1).reshape(nblk * TIME_BLOCK * nb, sw)
    y_ref[...] = y + d_ref[...] * u_ref[...]


def _s5_outputs(xg, s, u_lat, ssm_d, tfb, cst, nb, tc, n_ctx_chunks):
    rows_l, sw = u_lat.shape
    rc = tc * nb
    n_lat = rows_l // rc
    groups, gw, _ = tfb.shape
    q = gw // TIME_BLOCK
    rows_out = (tc // TIME_BLOCK) * nb
    kern = functools.partial(_output_kernel, nb=nb, groups=groups, q=q)
    return pl.pallas_call(
        kern,
        grid=(n_lat,),
        in_specs=[pl.BlockSpec((rows_out, groups * gw), lambda j: (j + n_ctx_chunks, 0)),
                  pl.BlockSpec((2, rows_out, s.shape[2]), lambda j: (0, j + n_ctx_chunks, 0)),
                  pl.BlockSpec((rc, sw), lambda j: (j, 0)),
                  pl.BlockSpec((1, sw), lambda j: (0, 0)),
                  pl.BlockSpec(tfb.shape, lambda j: (0, 0, 0), pipeline_mode=pl.Buffered(1)),
                  pl.BlockSpec(cst.shape, lambda j: (0, 0, 0), pipeline_mode=pl.Buffered(1))],
        out_specs=pl.BlockSpec((rc, sw), lambda j: (j, 0)),
        out_shape=jax.ShapeDtypeStruct((rows_l, sw), F32),
        scratch_shapes=[pltpu.VMEM((rows_out, groups * gw), F32)],
        compiler_params=_params("arbitrary"),
        name="s5_output",
    )(xg, s, u_lat, ssm_d, tfb, cst)


def _s5_blocked(u_ctx, u_lat, ssm_d, ops, nb, tc):
    tfb, bst, cst, ax, ay = ops
    n_ctx_chunks = u_ctx.shape[0] // (tc * nb)
    xg, e = _s5_regroup(u_ctx, u_lat, bst, nb, tc)
    s = _s5_carry(e, ax, ay, nb, n_ctx_chunks, (tc // TIME_BLOCK) * nb)
    return _s5_outputs(xg, s, u_lat, ssm_d, tfb, cst, nb, tc, n_ctx_chunks)


def _mixer_kernel(y_ref, sga_ref, gbt_ref, x_ref, mod_ref, wglu_ref, wso_ref, wo_ref, o_ref, *, d, sw):
    nb, tt, _ = x_ref.shape
    rows = nb * tt
    y = _to_batch_major(y_ref[...], nb, tt)
    v = _dot(jax.nn.gelu(y), wglu_ref[...])
    ys = v[:, 0:sw] * jax.nn.sigmoid(v[:, sw:2 * sw])
    y_a = _dot(ys, wso_ref[...])
    merged = sga_ref[...].reshape(rows, d) * y_a + gbt_ref[...].reshape(rows, d)
    o = _dot(merged, wo_ref[...])
    g1 = _per_row(mod_ref[...], 2 * d, 3 * d, tt)
    o_ref[...] = (x_ref[...].reshape(rows, d) + g1 * o).reshape(nb, tt, d)


def _mixer_out(y, sga, gbt, x, mod3, w_glu_b, w_ssm_out_b, w_o_b):
    b, l, d = x.shape
    sw = y.shape[1]
    tt = TIME_TILE
    kern = functools.partial(_mixer_kernel, d=d, sw=sw)
    tok = pl.BlockSpec((b, tt, d), lambda j: (0, j, 0))
    return pl.pallas_call(
        kern,
        grid=(l // tt,),
        in_specs=[pl.BlockSpec((tt * b, sw), lambda j: (j, 0)),
                  tok, tok, tok,
                  pl.BlockSpec((b, 1, mod3.shape[2]), lambda j: (0, 0, 0)),
                  _full(w_glu_b), _full(w_ssm_out_b), _full(w_o_b)],
        out_specs=tok,
        out_shape=jax.ShapeDtypeStruct((b, l, d), F32),
        compiler_params=_params("arbitrary"),
        name="mixer_out",
    )(y, sga, gbt, x, mod3, w_glu_b, w_ssm_out_b, w_o_b)


def _split_bf16(a):
    hi = a.astype(BF16)
    lo = (a - hi.astype(F32)).astype(BF16)
    return hi, lo


def _norm2(xl, m, g, d):
    return _rms(xl, g) * (1.0 + m[:, 4 * d:5 * d]) + m[:, 3 * d:4 * d]


def _shared_kernel(xl_ref, mod_ref, g_ref, wsg_ref, wsu_ref, wsd_ref, base_ref, *, d):
    m = mod_ref[0]
    xl = xl_ref[...]
    hb = _norm2(xl, m, g_ref[...], d).astype(BF16)
    sg = jnp.dot(hb, wsg_ref[...], preferred_element_type=F32)
    su = jnp.dot(hb, wsu_ref[...], preferred_element_type=F32)
    shared = _dot(sg * jax.nn.sigmoid(sg) * su, wsd_ref[...])
    base_ref[...] = xl + m[:, 5 * d:6 * d] * shared


def _shared(xl2, mod3, tiles_per_batch, g2n, ws_gate_b, ws_up_b, ws_down_b):
    n, d = xl2.shape
    t = TOKEN_TILE
    tok = pl.BlockSpec((t, d), lambda i: (i, 0))
    return pl.pallas_call(
        functools.partial(_shared_kernel, d=d),
        grid=(n // t,),
        in_specs=[tok,
                  pl.BlockSpec((1, 1, mod3.shape[2]), lambda i: (i // tiles_per_batch, 0, 0)),
                  _full(g2n), _full(ws_gate_b), _full(ws_up_b), _full(ws_down_b)],
        out_specs=tok,
        out_shape=jax.ShapeDtypeStruct((n, d), F32),
        compiler_params=_params("arbitrary"),
        name="shared",
    )(xl2, mod3, g2n, ws_gate_b, ws_up_b, ws_down_b)


def _route_kernel(xl_ref, mod_ref, g_ref, whi_ref, wlo_ref, rb_ref,
                  h2_ref, eidx_ref, wts_ref, rank_ref, cnt_ref, carry_ref, *, d):
    i = pl.program_id(0)
    t = xl_ref.shape[0]
    h2 = _norm2(xl_ref[...], mod_ref[0], g_ref[...], d)
    _store_packed(h2_ref, h2)

    h_hi, h_lo = _split_bf16(h2)
    w_hi, w_lo = whi_ref[...], wlo_ref[...]
    nt = (((1,), (1,)), ((), ()))
    logits = (lax.dot_general(w_hi, h_hi, nt, preferred_element_type=F32)
              + lax.dot_general(w_hi, h_lo, nt, preferred_element_type=F32)
              + lax.dot_general(w_lo, h_hi, nt, preferred_element_type=F32))
    scores = jax.nn.sigmoid(logits)
    choice = scores + rb_ref[...]

    epg = EXPERTS_PER_GROUP
    gi = lax.broadcasted_iota(I32, (epg, t), 0)
    gs = []
    for g in range(N_EXPERT_GROUPS):
        seg = choice[g * epg:(g + 1) * epg, :]
        m1 = jnp.max(seg, axis=0, keepdims=True)
        i1 = jnp.min(jnp.where(seg == m1, gi, epg), axis=0, keepdims=True)
        m2 = jnp.max(jnp.where(gi == i1, -jnp.inf, seg), axis=0, keepdims=True)
        gs.append(m1 + m2)
    masked = []
    for g in range(N_EXPERT_GROUPS):
        beat = jnp.zeros((1, t), I32)
        for g2 in range(N_EXPERT_GROUPS):
            if g2 < g:
                beat = beat + (gs[g2] >= gs[g]).astype(I32)
            elif g2 > g:
                beat = beat + (gs[g2] > gs[g]).astype(I32)
        keep = beat < TOP_K_GROUPS
        masked.append(jnp.where(keep, choice[g * epg:(g + 1) * epg, :], -jnp.inf))
    cur = jnp.concatenate(masked, axis=0)

    ei_all = lax.broadcasted_iota(I32, (N_EXPERTS, t), 0)
    picks, raw = [], []
    onehot = jnp.zeros((N_EXPERTS, t), F32)
    for _ in range(TOP_K):
        mx = jnp.max(cur, axis=0, keepdims=True)
        ei = jnp.min(jnp.where(cur == mx, ei_all, N_EXPERTS), axis=0, keepdims=True)
        hit = ei_all == ei
        raw.append(jnp.sum(jnp.where(hit, scores, 0.0), axis=0, keepdims=True))
        cur = jnp.where(hit, -jnp.inf, cur)
        onehot = jnp.where(hit, 1.0, onehot)
        picks.append(ei)
    tot = raw[0]
    for k in range(1, TOP_K):
        tot = tot + raw[k]

    @pl.when(i == 0)
    def _():
        carry_ref[...] = jnp.zeros_like(carry_ref)

    upper = (lax.broadcasted_iota(I32, (t, t), 0) < lax.broadcasted_iota(I32, (t, t), 1)).astype(BF16)
    before = jnp.dot(onehot.astype(BF16), upper, preferred_element_type=F32) + carry_ref[:, 0:1]
    for k in range(TOP_K):
        eidx_ref[k:k + 1, :] = picks[k]
        wts_ref[k:k + 1, :] = raw[k] / tot * ROUTE_SCALE
        rk = jnp.sum(jnp.where(ei_all == picks[k], before, 0.0), axis=0, keepdims=True)
        rank_ref[k:k + 1, :] = rk.astype(I32)
    carry_ref[...] = carry_ref[...] + jnp.sum(onehot, axis=1, keepdims=True)
    cnt_ref[...] = carry_ref[...]


def _route(xl2, mod3, tiles_per_batch, g2n, w_router_t, router_bias):
    n, d = xl2.shape
    t = TOKEN_TILE
    e = w_router_t.shape[0]
    w_top = lax.bitcast_convert_type(lax.bitcast_convert_type(w_router_t, U32) & jnp.uint32(0xFFFF0000), F32)
    w_hi, w_lo = w_top.astype(BF16), (w_router_t - w_top).astype(BF16)
    kern = functools.partial(_route_kernel, d=d)
    tok = pl.BlockSpec((t, d), lambda i: (i, 0))
    small = pl.BlockSpec((TOP_K, t), lambda i: (0, i))
    return pl.pallas_call(
        kern,
        grid=(n // t,),
        in_specs=[tok,
                  pl.BlockSpec((1, 1, mod3.shape[2]), lambda i: (i // tiles_per_batch, 0, 0)),
                  _full(g2n), _full(w_hi), _full(w_lo), _full(router_bias)],
        out_specs=[pl.BlockSpec((t * PACK_SUBLANES, LANES), lambda i: (i, 0)), small, small, small,
                   pl.BlockSpec((e, LANES), lambda i: (0, 0))],
        out_shape=[jax.ShapeDtypeStruct((n * PACK_SUBLANES, LANES), U32),
                   jax.ShapeDtypeStruct((TOP_K, n), I32),
                   jax.ShapeDtypeStruct((TOP_K, n), F32),
                   jax.ShapeDtypeStruct((TOP_K, n), I32),
                   jax.ShapeDtypeStruct((e, LANES), F32)],
        scratch_shapes=[pltpu.VMEM((e, LANES), F32)],
        compiler_params=_params("arbitrary"),
        name="route",
    )(xl2, mod3, g2n, w_hi, w_lo, router_bias)


def _dest_kernel(eidx_ref, rank_ref, start_ref, o_ref):
    t = eidx_ref.shape[1]
    ei_all = lax.broadcasted_iota(I32, (N_EXPERTS, t), 0)
    st = start_ref[:, 0:1]
    for k in range(TOP_K):
        hit = ei_all == eidx_ref[k:k + 1, :]
        o_ref[k:k + 1, :] = jnp.sum(jnp.where(hit, st, 0), axis=0, keepdims=True) + rank_ref[k:k + 1, :]


def _dest_rows(eidx, rank, starts_b):
    n = eidx.shape[1]
    t = 1024
    small = pl.BlockSpec((TOP_K, t), lambda i: (0, i))
    return pl.pallas_call(
        _dest_kernel,
        grid=(n // t,),
        in_specs=[small, small, _full(starts_b)],
        out_specs=small,
        out_shape=jax.ShapeDtypeStruct((TOP_K, n), I32),
        compiler_params=_params("arbitrary"),
        name="dest",
    )(eidx, rank, starts_b)


def _sc_workers():
    info = plsc.get_sparse_core_info()
    return info.num_cores, info.num_cores * info.num_subcores


def _sc_dispatch(dest_flat, src3, out_rows):
    n = src3.shape[0]
    n_cores, n_workers = _sc_workers()
    per_worker = n // n_workers
    assert per_worker % SC_CHUNK == 0
    mesh = plsc.VectorSubcoreMesh(core_axis_name="c", subcore_axis_name="s")

    @functools.partial(
        pl.kernel, mesh=mesh,
        out_type=jax.ShapeDtypeStruct((out_rows,) + src3.shape[1:], src3.dtype),
        scratch_types=[pltpu.VMEM((SC_CHUNK,) + src3.shape[1:], src3.dtype)]
        + [pltpu.VMEM((SC_CHUNK,), I32)] * TOP_K,
        name="dispatch",
    )
    def run(dest_hbm, src_hbm, out_hbm, rows_v, *idx_v):
        worker = lax.axis_index("s") * n_cores + lax.axis_index("c")

        @pl.loop(0, per_worker // SC_CHUNK)
        def _(j):
            base = worker * per_worker + j * SC_CHUNK
            pltpu.sync_copy(src_hbm.at[pl.ds(base, SC_CHUNK)], rows_v)
            for k in range(TOP_K):
                pltpu.sync_copy(dest_hbm.at[pl.ds(k * n + base, SC_CHUNK)], idx_v[k])
            for k in range(TOP_K):
                pltpu.sync_copy(rows_v, out_hbm.at[idx_v[k]])

    return run(dest_flat, src3)


def _sc_combine(dest_flat, ys3, n):
    n_cores, n_workers = _sc_workers()
    per_worker = n // n_workers
    assert per_worker % SC_CHUNK == 0
    mesh = plsc.VectorSubcoreMesh(core_axis_name="c", subcore_axis_name="s")

    @functools.partial(
        pl.kernel, mesh=mesh,
        out_type=jax.ShapeDtypeStruct((TOP_K * n,) + ys3.shape[1:], ys3.dtype),
        scratch_types=[pltpu.VMEM((SC_CHUNK,) + ys3.shape[1:], ys3.dtype), pltpu.VMEM((SC_CHUNK,), I32)],
        name="combine",
    )
    def run(dest_hbm, ys_hbm, out_hbm, rows_v, idx_v):
        worker = lax.axis_index("s") * n_cores + lax.axis_index("c")

        @pl.loop(0, per_worker // SC_CHUNK)
        def _(j):
            base = worker * per_worker + j * SC_CHUNK
            for k in range(TOP_K):
                pltpu.sync_copy(dest_hbm.at[pl.ds(k * n + base, SC_CHUNK)], idx_v)
                pltpu.sync_copy(ys_hbm.at[idx_v], rows_v)
                pltpu.sync_copy(rows_v, out_hbm.at[pl.ds(k * n + base, SC_CHUNK)])

    return run(dest_flat, ys3)


def _chunk_metadata(counts, n_rows):
    ch = EXPERT_CHUNK
    n_ch = (counts + ch - 1) // ch
    cum = jnp.cumsum(n_ch)
    total = cum[-1]
    max_chunks = n_rows // ch + N_EXPERTS
    i = jnp.arange(max_chunks, dtype=I32)
    e = jnp.sum((cum[None, :] <= i[:, None]).astype(I32), axis=1)
    e_last = jnp.max(jnp.where(counts > 0, jnp.arange(N_EXPERTS, dtype=I32), 0))
    exp = jnp.where(i < total, jnp.minimum(e, N_EXPERTS - 1), e_last).astype(I32)
    newexp = jnp.concatenate([jnp.ones((1,), I32), (exp[1:] != exp[:-1]).astype(I32)])
    change_at = jnp.where(newexp == 1, i, max_chunks)
    nxt_change = lax.cummin(jnp.concatenate([change_at[1:], jnp.array([max_chunks], I32)]), reverse=True)
    nexp = jnp.where(nxt_change < max_chunks, exp[jnp.minimum(nxt_change, max_chunks - 1)], -1).astype(I32)
    starts = (cum - n_ch) * ch
    return starts.astype(I32), (exp, newexp, nexp, total.astype(I32).reshape(1))


def _expert_kernel(exp_ref, newexp_ref, nexp_ref, total_ref, xs_hbm, wg_hbm, wu_hbm, wd_hbm, after_hbm, ys_hbm,
                   xbuf, ybuf, wgf, wuf, wdf, wgb, wub, wdb, sem_x, sem_y, sem_w):
    del after_hbm
    ch = EXPERT_CHUNK
    nbuf = EXPERT_RING
    rows = ch * PACK_SUBLANES
    total = total_ref[0]

    def x_copy(i, slot):
        r0 = pl.multiple_of(i * rows, rows)
        return pltpu.make_async_copy(xs_hbm.at[pl.ds(r0, rows)], xbuf.at[slot], sem_x.at[slot])

    def y_copy(i, slot):
        r0 = pl.multiple_of(i * rows, rows)
        return pltpu.make_async_copy(ybuf.at[slot], ys_hbm.at[pl.ds(r0, rows)], sem_y.at[slot])

    def weight_copies(e):
        return (pltpu.make_async_copy(wg_hbm.at[e], wgf, sem_w.at[0]),
                pltpu.make_async_copy(wu_hbm.at[e], wuf, sem_w.at[1]),
                pltpu.make_async_copy(wd_hbm.at[e], wdf, sem_w.at[2]))

    for cp in weight_copies(exp_ref[0]):
        cp.start()
    for b in range(nbuf - 1):
        @pl.when(b < total)
        def _(b=b):
            x_copy(b, b).start()

    def chunk(i, carry):
        slot = lax.rem(i, nbuf)
        x_copy(i, slot).wait()
        ahead = i + (nbuf - 1)

        @pl.when(ahead < total)
        def _():
            x_copy(ahead, lax.rem(ahead, nbuf)).start()

        @pl.when(newexp_ref[i] == 1)
        def _():
            for cp in weight_copies(exp_ref[i]):
                cp.wait()
            wgb[...] = wgf[...].astype(BF16)
            wub[...] = wuf[...].astype(BF16)
            wdb[...] = wdf[...].astype(BF16)

            @pl.when(nexp_ref[i] >= 0)
            def _():
                for cp in weight_copies(nexp_ref[i]):
                    cp.start()

        @pl.when(i >= nbuf)
        def _():
            y_copy(i, slot).wait()

        xr = xbuf.at[slot]
        yr = ybuf.at[slot]
        x = _load_packed(lambda sl: xr[sl, :], ch).astype(BF16)
        g = jnp.dot(x, wgb[...], preferred_element_type=F32)
        u = jnp.dot(x, wub[...], preferred_element_type=F32)
        y = jnp.dot((g * jax.nn.sigmoid(g) * u).astype(BF16), wdb[...], preferred_element_type=F32)
        _store_packed(yr, y)
        y_copy(i, slot).start()
        return carry

    lax.fori_loop(0, total, chunk, 0)

    for b in range(nbuf):
        @pl.when(b < total)
        def _(b=b):
            y_copy(0, b).wait()


def _experts(meta, xs, w_gate, w_up, w_down, after):
    e, d, f = w_gate.shape
    hbm = pl.BlockSpec(memory_space=pl.ANY)
    rows = EXPERT_CHUNK * PACK_SUBLANES
    grid_spec = pltpu.PrefetchScalarGridSpec(
        num_scalar_prefetch=len(meta), grid=(1,),
        in_specs=[hbm, hbm, hbm, hbm, hbm], out_specs=hbm,
        scratch_shapes=[pltpu.VMEM((EXPERT_RING, rows, LANES), U32), pltpu.VMEM((EXPERT_RING, rows, LANES), U32),
                        pltpu.VMEM((d, f), F32), pltpu.VMEM((d, f), F32), pltpu.VMEM((f, d), F32),
                        pltpu.VMEM((d, f), BF16), pltpu.VMEM((d, f), BF16), pltpu.VMEM((f, d), BF16),
                        pltpu.SemaphoreType.DMA((EXPERT_RING,)), pltpu.SemaphoreType.DMA((EXPERT_RING,)),
                        pltpu.SemaphoreType.DMA((3,))])
    return pl.pallas_call(
        _expert_kernel,
        grid_spec=grid_spec,
        out_shape=jax.ShapeDtypeStruct(xs.shape, xs.dtype),
        compiler_params=_params("arbitrary"),
        name="experts",
    )(*meta, xs, w_gate, w_up, w_down, after)


def _final_kernel(base_ref, yt_ref, wt_ref, mod_ref, g_ref, *rest, d):
    o_ref = rest[-1]
    t = base_ref.shape[0]
    w = wt_ref[...]
    routed = w[:, 0:1] * _load_packed(lambda sl: yt_ref[0, sl, :], t)
    for k in range(1, TOP_K):
        routed = routed + w[:, k:k + 1] * _load_packed(lambda sl, k=k: yt_ref[k, sl, :], t)
    g2 = mod_ref[0][:, 5 * d:6 * d]
    o_ref[...] = _rms(base_ref[...] + g2 * routed, g_ref[...])


def _final(base, ytok3, wts_t, mod3, tiles_per_batch, final_g, part, n_parts, prev_out):
    n, d = base.shape
    t = TOKEN_TILE
    tiles = n // t // n_parts
    off = part * tiles
    kern = functools.partial(_final_kernel, d=d)
    tok = pl.BlockSpec((t, d), lambda i: (off + i, 0))
    in_specs = [tok,
                pl.BlockSpec((TOP_K, t * PACK_SUBLANES, LANES), lambda i: (0, i, 0)),
                pl.BlockSpec((t, TOP_K), lambda i: (off + i, 0)),
                pl.BlockSpec((1, 1, mod3.shape[2]), lambda i: ((off + i) // tiles_per_batch, 0, 0)),
                _full(final_g)]
    args = [base, ytok3, wts_t, mod3, final_g]
    aliases = {}
    if prev_out is not None:
        in_specs.append(pl.BlockSpec(memory_space=pl.ANY))
        args.append(prev_out)
        aliases = {len(args) - 1: 0}
    return pl.pallas_call(
        kern,
        grid=(tiles,),
        in_specs=in_specs,
        out_specs=tok,
        out_shape=jax.ShapeDtypeStruct((n, d), F32),
        input_output_aliases=aliases,
        compiler_params=_params("arbitrary"),
        name="final",
    )(*args)


def kernel(x, c, ctx, c_ctx, w_mod, b_mod, norm1_g, norm2_g, w_in, ssm_lam_re, ssm_lam_im, ssm_log_dt, ssm_b_re, ssm_b_im, ssm_c_re, ssm_c_im, ssm_d, w_glu, w_ssm_out, conv_w, w_conv_out, w_o, w_router, router_bias, w_gate, w_up, w_down, ws_gate, ws_up, ws_down, final_g):
    b, l, d = x.shape
    n = b * l
    sw = ssm_d.shape[1]
    assert w_mod.shape[0] == 1, "single layer"
    assert b == SUBLANES and l % TOKEN_TILE == 0 and l % S5_CHUNK == 0 and ctx.shape[1] % S5_CHUNK == 0 and TIME_TILE % GRID_W == 0

    mod_rows = 2 * SUBLANES
    c_all = jnp.concatenate([c, c_ctx[None, :], jnp.zeros((mod_rows - b - 1, d), F32)], axis=0)
    mod = _modulation(c_all, w_mod[0], b_mod[0])
    mod3 = mod.reshape(mod_rows, 1, mod.shape[1])

    w_in_b = w_in[0].astype(BF16)
    g1n = norm1_g[0].reshape(1, d)
    u_lat, sga, gbt = _in_proj(x, mod3, g1n, w_in_b, conv_w[0], w_conv_out[0].astype(BF16))
    u_ctx = _ctx_proj(ctx, mod3, b, g1n, w_in_b[:, :sw])

    ops = _blocked_operators(ssm_lam_re[0], ssm_lam_im[0], ssm_log_dt[0], ssm_b_re[0], ssm_b_im[0],
                             ssm_c_re[0], ssm_c_im[0])
    y = _s5_blocked(u_ctx, u_lat, ssm_d[0].reshape(1, sw), ops, b, S5_CHUNK)

    xl = _mixer_out(y, sga, gbt, x, mod3, w_glu[0].astype(BF16), w_ssm_out[0].astype(BF16),
                    w_o[0].astype(BF16))

    tiles_per_batch = l // TOKEN_TILE
    xl2 = xl.reshape(n, d)
    g2n = norm2_g[0].reshape(1, d)
    h2, eidx, wts, rank, cnt = _route(xl2, mod3, tiles_per_batch, g2n, w_router[0].T,
                                      router_bias[0].reshape(N_EXPERTS, 1))

    counts = cnt[:, 0].astype(I32)
    n_rows = n * TOP_K
    starts, meta = _chunk_metadata(counts, n_rows)
    buf_rows = n_rows + N_EXPERTS * EXPERT_CHUNK
    dest = _dest_rows(eidx, rank, jnp.broadcast_to(starts[:, None], (N_EXPERTS, LANES)))

    xs = _sc_dispatch(dest.reshape(TOP_K * n), h2.reshape(n, PACK_SUBLANES, LANES), buf_rows)
    base = _shared(xl2, mod3, tiles_per_batch, g2n, ws_gate[0].astype(BF16), ws_up[0].astype(BF16),
                   ws_down[0].astype(BF16))
    ys = _experts(meta, xs.reshape(buf_rows * PACK_SUBLANES, LANES), w_gate[0], w_up[0], w_down[0], base)
    ys3 = ys.reshape(buf_rows, PACK_SUBLANES, LANES)

    n_part = n // COMBINE_PARTS
    wts_t = wts.T
    out = None
    for part in range(COMBINE_PARTS):
        dest_part = dest[:, part * n_part:(part + 1) * n_part].reshape(TOP_K * n_part)
        ytok = _sc_combine(dest_part, ys3, n_part)
        out = _final(base, ytok.reshape(TOP_K, n_part * PACK_SUBLANES, LANES), wts_t, mod3, tiles_per_batch,
                     final_g.reshape(1, d), part, COMBINE_PARTS, out)
    return out.reshape(b, l, d)
```

```python
import functools

import jax
import jax.numpy as jnp
from jax import lax
from jax.experimental import pallas as pl
from jax.experimental.pallas import tpu as pltpu
from jax.experimental.pallas import tpu_sc as plsc

F32 = jnp.float32
BF16 = jnp.bfloat16
I32 = jnp.int32
U32 = jnp.uint32

EPS = 1e-6
GRID_W = 64
SSM_GROUP = 16
N_EXPERTS = 256
TOP_K = 8
N_EXPERT_GROUPS = 8
EXPERTS_PER_GROUP = N_EXPERTS // N_EXPERT_GROUPS
TOP_K_GROUPS = 4
ROUTE_SCALE = 2.5

SUBLANES = 8
LANES = 128
VMEM_LIMIT_BYTES = 48 * 1024 * 1024

TIME_TILE = 64
TOKEN_TILE = 256
TIME_BLOCK = 16
S5_CHUNK = 256
EXPERT_CHUNK = 256
EXPERT_RING = 4
SC_CHUNK = 128
COMBINE_PARTS = 4
PACK_SUBLANES = 4


def _dot(a, b):
    return jnp.dot(a.astype(BF16), b.astype(BF16), preferred_element_type=F32)


def _rms(xf, g):
    return xf * lax.rsqrt(jnp.mean(xf * xf, axis=-1, keepdims=True) + EPS) * g


def _params(*sem):
    return pltpu.CompilerParams(dimension_semantics=sem, vmem_limit_bytes=VMEM_LIMIT_BYTES)


def _full(a):
    return pl.BlockSpec(a.shape, lambda *_: (0,) * a.ndim)


def _pack_rows(v):
    half = v.shape[1] // 2
    lo = lax.bitcast_convert_type(v[:, :half].astype(BF16).astype(F32), U32) >> 16
    hi = lax.bitcast_convert_type(v[:, half:].astype(BF16).astype(F32), U32) & jnp.uint32(0xFFFF0000)
    return lo | hi


def _unpack_lo(w):
    return lax.bitcast_convert_type(w << 16, F32)


def _unpack_hi(w):
    return lax.bitcast_convert_type(w & jnp.uint32(0xFFFF0000), F32)


def _store_packed(ref, v):
    t = v.shape[0]
    w = _pack_rows(v)
    for c in range(PACK_SUBLANES):
        ref[pl.ds(c, t, stride=PACK_SUBLANES), :] = w[:, c * LANES:(c + 1) * LANES]


def _load_packed(load, t):
    ws = [load(pl.ds(c, t, stride=PACK_SUBLANES)) for c in range(PACK_SUBLANES)]
    return jnp.concatenate([_unpack_lo(w) for w in ws] + [_unpack_hi(w) for w in ws], axis=1)


def _mod_kernel(c_ref, w_ref, b_ref, o_ref):
    c = c_ref[...]
    o_ref[...] = _dot(c * jax.nn.sigmoid(c), w_ref[...]) + b_ref[...]


def _modulation(c_all, w_mod, b_mod):
    rows, d = c_all.shape
    cols = w_mod.shape[1]
    blk = 1536
    return pl.pallas_call(
        _mod_kernel,
        grid=(cols // blk,),
        in_specs=[pl.BlockSpec((rows, d), lambda j: (0, 0)),
                  pl.BlockSpec((d, blk), lambda j: (0, j)),
                  pl.BlockSpec((1, blk), lambda j: (0, j))],
        out_specs=pl.BlockSpec((rows, blk), lambda j: (0, j)),
        out_shape=jax.ShapeDtypeStruct((rows, cols), F32),
        compiler_params=_params("arbitrary"),
        name="mod",
    )(c_all, w_mod, b_mod.reshape(1, cols))


def _per_row(m3, lo, hi, tt):
    nb = m3.shape[0]
    return jnp.broadcast_to(m3[:, :, lo:hi], (nb, tt, hi - lo)).reshape(nb * tt, hi - lo)


def _to_time_major(val, nb, tt):
    c = val.shape[1]
    return pltpu.einshape("btc->tbc", val.reshape(nb, tt, c)).reshape(nb * tt, c)


def _to_batch_major(val, nb, tt):
    c = val.shape[1]
    return pltpu.einshape("tbc->btc", val.reshape(tt, nb, c)).reshape(nb * tt, c)


def _in_proj_kernel(x_ref, mod_ref, g_ref, w_ref, cw_ref, wco_ref, u_ref, sga_ref, gbt_ref, *, d, sw):
    nb, tt, _ = x_ref.shape
    rows = nb * tt
    m3 = mod_ref[...]
    x = x_ref[...].reshape(rows, d)
    h = _rms(x, g_ref[...]) * (1.0 + _per_row(m3, d, 2 * d, tt)) + _per_row(m3, 0, d, tt)
    hb = h.astype(BF16)
    u_ref[...] = _to_time_major(jnp.dot(hb, w_ref[:, 0:sw], preferred_element_type=F32), nb, tt)
    cb = jnp.dot(hb, w_ref[:, sw:2 * sw], preferred_element_type=F32)
    cc = jnp.dot(hb, w_ref[:, 2 * sw:3 * sw], preferred_element_type=F32)
    cv = jnp.dot(hb, w_ref[:, 3 * sw:4 * sw], preferred_element_type=F32)
    ccv = cc * cv
    col = lax.broadcasted_iota(I32, ccv.shape, 0) % GRID_W
    prev = jnp.where(col == 0, 0.0, pltpu.roll(ccv, 1, axis=0))
    nxt = jnp.where(col == GRID_W - 1, 0.0, pltpu.roll(ccv, rows - 1, axis=0))
    cw = cw_ref[...]
    conv = prev * cw[0:1, :] + ccv * cw[1:2, :] + nxt * cw[2:3, :]
    y_conv = _dot(cb * conv, wco_ref[...])
    ga = jnp.dot(hb, w_ref[:, 4 * sw:4 * sw + d], preferred_element_type=F32)
    gb = jnp.dot(hb, w_ref[:, 4 * sw + d:4 * sw + 2 * d], preferred_element_type=F32)
    sga_ref[...] = jax.nn.sigmoid(ga).reshape(nb, tt, d)
    gbt_ref[...] = (jax.nn.sigmoid(gb) * y_conv).reshape(nb, tt, d)


def _in_proj(x, mod3, g1n, w_in_b, conv_w, w_conv_out_b):
    b, l, d = x.shape
    sw = conv_w.shape[1]
    tt = TIME_TILE
    kern = functools.partial(_in_proj_kernel, d=d, sw=sw)
    tok = pl.BlockSpec((b, tt, d), lambda j: (0, j, 0))
    return pl.pallas_call(
        kern,
        grid=(l // tt,),
        in_specs=[tok,
                  pl.BlockSpec((b, 1, mod3.shape[2]), lambda j: (0, 0, 0)),
                  _full(g1n), _full(w_in_b), _full(conv_w), _full(w_conv_out_b)],
        out_specs=[pl.BlockSpec((tt * b, sw), lambda j: (j, 0)), tok, tok],
        out_shape=[jax.ShapeDtypeStruct((l * b, sw), F32),
                   jax.ShapeDtypeStruct((b, l, d), F32),
                   jax.ShapeDtypeStruct((b, l, d), F32)],
        compiler_params=_params("arbitrary"),
        name="in_proj",
    )(x, mod3, g1n, w_in_b, conv_w, w_conv_out_b)


def _ctx_proj_kernel(x_ref, mod_ref, g_ref, w_ref, u_ref, *, d):
    nb, tt, _ = x_ref.shape
    m = mod_ref[0]
    x = x_ref[...].reshape(nb * tt, d)
    h = _rms(x, g_ref[...]) * (1.0 + m[:, d:2 * d]) + m[:, 0:d]
    u_ref[...] = _to_time_major(_dot(h, w_ref[...]), nb, tt)


def _ctx_proj(ctx, mod3, ctx_row, g1n, w_u_b):
    b, lc, d = ctx.shape
    sw = w_u_b.shape[1]
    tt = TIME_TILE
    kern = functools.partial(_ctx_proj_kernel, d=d)
    return pl.pallas_call(
        kern,
        grid=(lc // tt,),
        in_specs=[pl.BlockSpec((b, tt, d), lambda j: (0, j, 0)),
                  pl.BlockSpec((1, 1, mod3.shape[2]), lambda j: (ctx_row, 0, 0)),
                  _full(g1n), _full(w_u_b)],
        out_specs=pl.BlockSpec((tt * b, sw), lambda j: (j, 0)),
        out_shape=jax.ShapeDtypeStruct((lc * b, sw), F32),
        compiler_params=_params("arbitrary"),
        name="ctx_proj",
    )(ctx, mod3, g1n, w_u_b)


def _toeplitz_kernel(mf_ref, mb_ref, o_ref):
    mf, mb = mf_ref[0], mb_ref[0]
    q, w = mf.shape
    lane = lax.broadcasted_iota(I32, (q, w), 1)
    for s in range(TIME_BLOCK):
        right = s * q
        left = (TIME_BLOCK - 1 - s) * q
        fwd = jnp.where(lane >= right, pltpu.roll(mf, right, axis=1) if right else mf, 0.0)
        bwd = jnp.where(lane < w - left, pltpu.roll(mb, w - left, axis=1) if left else mb, 0.0)
        o_ref[0, s * q:(s + 1) * q, :] = (fwd + bwd).astype(BF16)


def _toeplitz(mf, mb):
    g, q, w = mf.shape
    spec = pl.BlockSpec((1, q, w), lambda i: (i, 0, 0))
    return pl.pallas_call(
        _toeplitz_kernel,
        grid=(g,),
        in_specs=[spec, spec],
        out_specs=pl.BlockSpec((1, w, w), lambda i: (i, 0, 0)),
        out_shape=jax.ShapeDtypeStruct((g, w, w), BF16),
        compiler_params=_params("arbitrary"),
        name="s5_toeplitz",
    )(mf, mb)


def _blocked_operators(lam_re, lam_im, log_dt, b_re, b_im, c_re, c_im):
    tb = TIME_BLOCK
    dt = jnp.exp(log_dt)[..., None]
    k = jnp.arange(tb + 1, dtype=F32)[None, None, :, None]
    mag = jnp.exp(lam_re * dt)[:, :, None, :] ** k
    ang = (lam_im * dt)[:, :, None, :] * k
    ak_re = mag * jnp.cos(ang)
    ak_im = mag * jnp.sin(ang)
    a_re, a_im = ak_re[:, :, 1], ak_im[:, :, 1]
    den = lam_re * lam_re + lam_im * lam_im
    k_re = ((a_re - 1.0) * lam_re + a_im * lam_im) / den
    k_im = (a_im * lam_re - (a_re - 1.0) * lam_im) / den
    bt_re = jnp.swapaxes(b_re, 2, 3)
    bt_im = jnp.swapaxes(b_im, 2, 3)
    bbt_re = k_re[:, :, None, :] * bt_re - k_im[:, :, None, :] * bt_im
    bbt_im = k_re[:, :, None, :] * bt_im + k_im[:, :, None, :] * bt_re
    nd, g, q, p = bbt_re.shape
    w = tb * q

    akr, aki = ak_re[:, :, :tb, None, :], ak_im[:, :, :tb, None, :]
    ab_re = akr * bbt_re[:, :, None] - aki * bbt_im[:, :, None]
    ab_im = akr * bbt_im[:, :, None] + aki * bbt_re[:, :, None]

    def rows(x):
        return x.reshape(g, w, p)
    bst = jnp.concatenate([rows(ab_re[0][:, ::-1]), rows(ab_im[0][:, ::-1]), rows(ab_re[1]), rows(ab_im[1])], axis=2)

    def lag_matrices(d):
        abq_re = jnp.swapaxes(ab_re[d], 1, 2)
        abq_im = jnp.swapaxes(ab_im[d], 1, 2)
        return (jnp.einsum('gekp,gcp->gekc', abq_re, c_re[d], precision=lax.Precision.HIGHEST)
                - jnp.einsum('gekp,gcp->gekc', abq_im, c_im[d], precision=lax.Precision.HIGHEST))
    mf = lag_matrices(0).reshape(g, q, w)
    mb = lag_matrices(1)[:, :, ::-1].reshape(g, q, w)
    tfb = _toeplitz(mf, mb)

    def read(d, ks):
        ar = ak_re[d][:, ks][:, :, None, :]
        ai = ak_im[d][:, ks][:, :, None, :]
        cr, ci = c_re[d][:, None], c_im[d][:, None]
        return (cr * ar - ci * ai).reshape(g, w, p), (cr * ai + ci * ar).reshape(g, w, p)
    f_re, f_im = read(0, slice(1, tb + 1))
    r_re, r_im = read(1, slice(tb, 0, -1))
    cst = jnp.swapaxes(jnp.concatenate([f_re, -f_im, r_re, -r_im], axis=2), 1, 2)

    ax = jnp.concatenate([ak_re[:, :, tb], ak_re[:, :, tb]], axis=-1).reshape(nd, 1, g * 2 * p)
    ay = jnp.concatenate([-ak_im[:, :, tb], ak_im[:, :, tb]], axis=-1).reshape(nd, 1, g * 2 * p)
    return tfb, bst.astype(BF16), cst.astype(BF16), ax, ay


def _lane_chunk(shape):
    return lax.broadcasted_iota(I32, shape, 1) // SSM_GROUP


def _regroup_kernel(uc_ref, ul_ref, bst_ref, xg_ref, e_ref, *, n_ctx, nb, groups, q):
    j = pl.program_id(0)
    rows_in = uc_ref.shape[0]
    nblk = rows_in // (TIME_BLOCK * nb)
    rows = nblk * nb
    u = jnp.where(j < n_ctx, uc_ref[...], ul_ref[...])
    u3 = u.reshape(nblk, TIME_BLOCK * nb, u.shape[1])
    xs = [u3[:, s * nb:(s + 1) * nb, :].reshape(rows, u.shape[1]) for s in range(TIME_BLOCK)]
    per_tile = LANES // q
    chunk = _lane_chunk((rows, LANES))
    for g in range(groups):
        jt, qq = divmod(g, per_tile)
        for h in range(TIME_BLOCK // per_tile):
            acc = jnp.zeros((rows, LANES), F32)
            for s in range(h * per_tile, (h + 1) * per_tile):
                pos = s % per_tile
                a = xs[s][:, jt * LANES:(jt + 1) * LANES]
                shift = ((pos - qq) % per_tile) * q
                r = pltpu.roll(a, shift, axis=1) if shift else a
                acc = jnp.where(chunk == pos, r, acc)
            xg_ref[:, (2 * g + h) * LANES:(2 * g + h + 1) * LANES] = acc.astype(BF16)
    gw = TIME_BLOCK * q
    for g in range(groups):
        e = jnp.dot(xg_ref[:, g * gw:(g + 1) * gw], bst_ref[g], preferred_element_type=F32)
        half = e.shape[1] // 2
        e_ref[0, :, g * half:(g + 1) * half] = e[:, :half]
        e_ref[1, :, g * half:(g + 1) * half] = e[:, half:]


def _s5_regroup(u_ctx, u_lat, bst, nb, tc):
    rows_c, sw = u_ctx.shape
    rows_l = u_lat.shape[0]
    rc = tc * nb
    n_ctx, n_lat = rows_c // rc, rows_l // rc
    groups, gw, p4 = bst.shape
    q = gw // TIME_BLOCK
    rows_out = (tc // TIME_BLOCK) * nb
    total_rows = (n_ctx + n_lat) * rows_out
    kern = functools.partial(_regroup_kernel, n_ctx=n_ctx, nb=nb, groups=groups, q=q)
    return pl.pallas_call(
        kern,
        grid=(n_ctx + n_lat,),
        in_specs=[pl.BlockSpec((rc, sw), lambda j: (jnp.minimum(j, n_ctx - 1), 0)),
                  pl.BlockSpec((rc, sw), lambda j: (jnp.maximum(j - n_ctx, 0), 0)),
                  pl.BlockSpec(bst.shape, lambda j: (0, 0, 0), pipeline_mode=pl.Buffered(1))],
        out_specs=[pl.BlockSpec((rows_out, groups * gw), lambda j: (j, 0)),
                   pl.BlockSpec((2, rows_out, groups * p4 // 2), lambda j: (0, j, 0))],
        out_shape=[jax.ShapeDtypeStruct((total_rows, groups * gw), BF16),
                   jax.ShapeDtypeStruct((2, total_rows, groups * p4 // 2), F32)],
        compiler_params=_params("arbitrary"),
        name="s5_regroup",
    )(u_ctx, u_lat, bst)


def _carry_kernel(e_ref, ax_ref, ay_ref, s_ref, st_ref, *, nb):
    dirn = pl.program_id(0)
    j = pl.program_id(1)
    rows = e_ref.shape[1]
    nblk = rows // nb
    lanes = e_ref.shape[2]

    @pl.when(j == 0)
    def _():
        st_ref[...] = jnp.zeros_like(st_ref)

    ax = jnp.broadcast_to(ax_ref[0], (nb, lanes))
    ay = jnp.broadcast_to(ay_ref[0], (nb, lanes))

    def body(i, s):
        blk = jnp.where(dirn == 0, i, nblk - 1 - i)
        r0 = pl.multiple_of(blk * nb, nb)
        s_ref[0, pl.ds(r0, nb), :] = s.astype(s_ref.dtype)
        swapped = jnp.concatenate([pltpu.roll(s[:, t * LANES:(t + 1) * LANES], LANES // 2, axis=1)
                                   for t in range(lanes // LANES)], axis=1)
        return ax * s + ay * swapped + e_ref[0, pl.ds(r0, nb), :]

    st_ref[...] = lax.fori_loop(0, nblk, body, st_ref[...])


def _s5_carry(e, ax, ay, nb, n_ctx_chunks, rows_per_chunk):
    nd, total_rows, lanes = e.shape
    n_chunks = total_rows // rows_per_chunk
    n_lat = n_chunks - n_ctx_chunks

    def idx(d, j):
        jc = jnp.minimum(j, n_ctx_chunks - 1)
        jl = jnp.maximum(j - n_ctx_chunks, 0)
        fwd = j
        bwd = jnp.where(j < n_ctx_chunks, n_ctx_chunks - 1 - jc, n_chunks - 1 - jl)
        return (d, jnp.where(d == 0, fwd, bwd), 0)

    kern = functools.partial(_carry_kernel, nb=nb)
    return pl.pallas_call(
        kern,
        grid=(nd, n_chunks),
        in_specs=[pl.BlockSpec((1, rows_per_chunk, lanes), idx),
                  pl.BlockSpec((1, 1, lanes), lambda d, j: (d, 0, 0)),
                  pl.BlockSpec((1, 1, lanes), lambda d, j: (d, 0, 0))],
        out_specs=pl.BlockSpec((1, rows_per_chunk, lanes), idx),
        out_shape=jax.ShapeDtypeStruct(e.shape, F32),
        scratch_shapes=[pltpu.VMEM((nb, lanes), F32)],
        compiler_params=_params("arbitrary", "arbitrary"),
        name="s5_carry",
    )(e, ax, ay)


def _output_kernel(xg_ref, s_ref, u_ref, d_ref, tfb_ref, cst_ref, y_ref, yg_ref, *, nb, groups, q):
    rows = xg_ref.shape[0]
    nblk = rows // nb
    gw = TIME_BLOCK * q
    half = s_ref.shape[2] // groups
    for g in range(groups):
        sg = jnp.concatenate([s_ref[0, :, g * half:(g + 1) * half], s_ref[1, :, g * half:(g + 1) * half]], axis=1)
        yg_ref[:, g * gw:(g + 1) * gw] = (
            jnp.dot(xg_ref[:, g * gw:(g + 1) * gw], tfb_ref[g], preferred_element_type=F32)
            + jnp.dot(sg.astype(BF16), cst_ref[g], preferred_element_type=F32))
    per_tile = LANES // q
    chunk = _lane_chunk((rows, LANES))
    sw = groups * q
    steps = []
    for s in range(TIME_BLOCK):
        h, pos = divmod(s, per_tile)
        tiles = []
        for jt in range(sw // LANES):
            acc = jnp.zeros((rows, LANES), F32)
            for qq in range(per_tile):
                g = jt * per_tile + qq
                a = yg_ref[:, (2 * g + h) * LANES:(2 * g + h + 1) * LANES]
                shift = ((qq - pos) % per_tile) * q
                r = pltpu.roll(a, shift, axis=1) if shift else a
                acc = jnp.where(chunk == qq, r, acc)
            tiles.append(acc)
        steps.append(jnp.concatenate(tiles, axis=1).reshape(nblk, nb, sw))
    y = jnp.concatenate(steps, axis=1).reshape(nblk * TIME_BLOCK * nb, sw)
    y_ref[...] = y + d_ref[...] * u_ref[...]


def _s5_outputs(xg, s, u_lat, ssm_d, tfb, cst, nb, tc, n_ctx_chunks):
    rows_l, sw = u_lat.shape
    rc = tc * nb
    n_lat = rows_l // rc
    groups, gw, _ = tfb.shape
    q = gw // TIME_BLOCK
    rows_out = (tc // TIME_BLOCK) * nb
    kern = functools.partial(_output_kernel, nb=nb, groups=groups, q=q)
    return pl.pallas_call(
        kern,
        grid=(n_lat,),
        in_specs=[pl.BlockSpec((rows_out, groups * gw), lambda j: (j + n_ctx_chunks, 0)),
                  pl.BlockSpec((2, rows_out, s.shape[2]), lambda j: (0, j + n_ctx_chunks, 0)),
                  pl.BlockSpec((rc, sw), lambda j: (j, 0)),
                  pl.BlockSpec((1, sw), lambda j: (0, 0)),
                  pl.BlockSpec(tfb.shape, lambda j: (0, 0, 0), pipeline_mode=pl.Buffered(1)),
                  pl.BlockSpec(cst.shape, lambda j: (0, 0, 0), pipeline_mode=pl.Buffered(1))],
        out_specs=pl.BlockSpec((rc, sw), lambda j: (j, 0)),
        out_shape=jax.ShapeDtypeStruct((rows_l, sw), F32),
        scratch_shapes=[pltpu.VMEM((rows_out, groups * gw), F32)],
        compiler_params=_params("arbitrary"),
        name="s5_output",
    )(xg, s, u_lat, ssm_d, tfb, cst)


def _s5_blocked(u_ctx, u_lat, ssm_d, ops, nb, tc):
    tfb, bst, cst, ax, ay = ops
    n_ctx_chunks = u_ctx.shape[0] // (tc * nb)
    xg, e = _s5_regroup(u_ctx, u_lat, bst, nb, tc)
    s = _s5_carry(e, ax, ay, nb, n_ctx_chunks, (tc // TIME_BLOCK) * nb)
    return _s5_outputs(xg, s, u_lat, ssm_d, tfb, cst, nb, tc, n_ctx_chunks)


def _mixer_kernel(y_ref, sga_ref, gbt_ref, x_ref, mod_ref, wglu_ref, wso_ref, wo_ref, o_ref, *, d, sw):
    nb, tt, _ = x_ref.shape
    rows = nb * tt
    y = _to_batch_major(y_ref[...], nb, tt)
    v = _dot(jax.nn.gelu(y), wglu_ref[...])
    ys = v[:, 0:sw] * jax.nn.sigmoid(v[:, sw:2 * sw])
    y_a = _dot(ys, wso_ref[...])
    merged = sga_ref[...].reshape(rows, d) * y_a + gbt_ref[...].reshape(rows, d)
    o = _dot(merged, wo_ref[...])
    g1 = _per_row(mod_ref[...], 2 * d, 3 * d, tt)
    o_ref[...] = (x_ref[...].reshape(rows, d) + g1 * o).reshape(nb, tt, d)


def _mixer_out(y, sga, gbt, x, mod3, w_glu_b, w_ssm_out_b, w_o_b):
    b, l, d = x.shape
    sw = y.shape[1]
    tt = TIME_TILE
    kern = functools.partial(_mixer_kernel, d=d, sw=sw)
    tok = pl.BlockSpec((b, tt, d), lambda j: (0, j, 0))
    return pl.pallas_call(
        kern,
        grid=(l // tt,),
        in_specs=[pl.BlockSpec((tt * b, sw), lambda j: (j, 0)),
                  tok, tok, tok,
                  pl.BlockSpec((b, 1, mod3.shape[2]), lambda j: (0, 0, 0)),
                  _full(w_glu_b), _full(w_ssm_out_b), _full(w_o_b)],
        out_specs=tok,
        out_shape=jax.ShapeDtypeStruct((b, l, d), F32),
        compiler_params=_params("arbitrary"),
        name="mixer_out",
    )(y, sga, gbt, x, mod3, w_glu_b, w_ssm_out_b, w_o_b)


def _split_bf16(a):
    hi = a.astype(BF16)
    lo = (a - hi.astype(F32)).astype(BF16)
    return hi, lo


def _norm2(xl, m, g, d):
    return _rms(xl, g) * (1.0 + m[:, 4 * d:5 * d]) + m[:, 3 * d:4 * d]


def _shared_kernel(xl_ref, mod_ref, g_ref, wsg_ref, wsu_ref, wsd_ref, base_ref, *, d):
    m = mod_ref[0]
    xl = xl_ref[...]
    hb = _norm2(xl, m, g_ref[...], d).astype(BF16)
    sg = jnp.dot(hb, wsg_ref[...], preferred_element_type=F32)
    su = jnp.dot(hb, wsu_ref[...], preferred_element_type=F32)
    shared = _dot(sg * jax.nn.sigmoid(sg) * su, wsd_ref[...])
    base_ref[...] = xl + m[:, 5 * d:6 * d] * shared


def _shared(xl2, mod3, tiles_per_batch, g2n, ws_gate_b, ws_up_b, ws_down_b):
    n, d = xl2.shape
    t = TOKEN_TILE
    tok = pl.BlockSpec((t, d), lambda i: (i, 0))
    return pl.pallas_call(
        functools.partial(_shared_kernel, d=d),
        grid=(n // t,),
        in_specs=[tok,
                  pl.BlockSpec((1, 1, mod3.shape[2]), lambda i: (i // tiles_per_batch, 0, 0)),
                  _full(g2n), _full(ws_gate_b), _full(ws_up_b), _full(ws_down_b)],
        out_specs=tok,
        out_shape=jax.ShapeDtypeStruct((n, d), F32),
        compiler_params=_params("arbitrary"),
        name="shared",
    )(xl2, mod3, g2n, ws_gate_b, ws_up_b, ws_down_b)


def _route_kernel(xl_ref, mod_ref, g_ref, whi_ref, wlo_ref, rb_ref,
                  h2_ref, eidx_ref, wts_ref, rank_ref, cnt_ref, carry_ref, *, d):
    i = pl.program_id(0)
    t = xl_ref.shape[0]
    h2 = _norm2(xl_ref[...], mod_ref[0], g_ref[...], d)
    _store_packed(h2_ref, h2)

    h_hi, h_lo = _split_bf16(h2)
    w_hi, w_lo = whi_ref[...], wlo_ref[...]
    nt = (((1,), (1,)), ((), ()))
    logits = (lax.dot_general(w_hi, h_hi, nt, preferred_element_type=F32)
              + lax.dot_general(w_hi, h_lo, nt, preferred_element_type=F32)
              + lax.dot_general(w_lo, h_hi, nt, preferred_element_type=F32))
    scores = jax.nn.sigmoid(logits)
    choice = scores + rb_ref[...]

    epg = EXPERTS_PER_GROUP
    gi = lax.broadcasted_iota(I32, (epg, t), 0)
    gs = []
    for g in range(N_EXPERT_GROUPS):
        seg = choice[g * epg:(g + 1) * epg, :]
        m1 = jnp.max(seg, axis=0, keepdims=True)
        i1 = jnp.min(jnp.where(seg == m1, gi, epg), axis=0, keepdims=True)
        m2 = jnp.max(jnp.where(gi == i1, -jnp.inf, seg), axis=0, keepdims=True)
        gs.append(m1 + m2)
    masked = []
    for g in range(N_EXPERT_GROUPS):
        beat = jnp.zeros((1, t), I32)
        for g2 in range(N_EXPERT_GROUPS):
            if g2 < g:
                beat = beat + (gs[g2] >= gs[g]).astype(I32)
            elif g2 > g:
                beat = beat + (gs[g2] > gs[g]).astype(I32)
        keep = beat < TOP_K_GROUPS
        masked.append(jnp.where(keep, choice[g * epg:(g + 1) * epg, :], -jnp.inf))
    cur = jnp.concatenate(masked, axis=0)

    ei_all = lax.broadcasted_iota(I32, (N_EXPERTS, t), 0)
    picks, raw = [], []
    onehot = jnp.zeros((N_EXPERTS, t), F32)
    for _ in range(TOP_K):
        mx = jnp.max(cur, axis=0, keepdims=True)
        ei = jnp.min(jnp.where(cur == mx, ei_all, N_EXPERTS), axis=0, keepdims=True)
        hit = ei_all == ei
        raw.append(jnp.sum(jnp.where(hit, scores, 0.0), axis=0, keepdims=True))
        cur = jnp.where(hit, -jnp.inf, cur)
        onehot = jnp.where(hit, 1.0, onehot)
        picks.append(ei)
    tot = raw[0]
    for k in range(1, TOP_K):
        tot = tot + raw[k]

    @pl.when(i == 0)
    def _():
        carry_ref[...] = jnp.zeros_like(carry_ref)

    upper = (lax.broadcasted_iota(I32, (t, t), 0) < lax.broadcasted_iota(I32, (t, t), 1)).astype(BF16)
    before = jnp.dot(onehot.astype(BF16), upper, preferred_element_type=F32) + carry_ref[:, 0:1]
    for k in range(TOP_K):
        eidx_ref[k:k + 1, :] = picks[k]
        wts_ref[k:k + 1, :] = raw[k] / tot * ROUTE_SCALE
        rk = jnp.sum(jnp.where(ei_all == picks[k], before, 0.0), axis=0, keepdims=True)
        rank_ref[k:k + 1, :] = rk.astype(I32)
    carry_ref[...] = carry_ref[...] + jnp.sum(onehot, axis=1, keepdims=True)
    cnt_ref[...] = carry_ref[...]


def _route(xl2, mod3, tiles_per_batch, g2n, w_router_t, router_bias):
    n, d = xl2.shape
    t = TOKEN_TILE
    e = w_router_t.shape[0]
    w_top = lax.bitcast_convert_type(lax.bitcast_convert_type(w_router_t, U32) & jnp.uint32(0xFFFF0000), F32)
    w_hi, w_lo = w_top.astype(BF16), (w_router_t - w_top).astype(BF16)
    kern = functools.partial(_route_kernel, d=d)
    tok = pl.BlockSpec((t, d), lambda i: (i, 0))
    small = pl.BlockSpec((TOP_K, t), lambda i: (0, i))
    return pl.pallas_call(
        kern,
        grid=(n // t,),
        in_specs=[tok,
                  pl.BlockSpec((1, 1, mod3.shape[2]), lambda i: (i // tiles_per_batch, 0, 0)),
                  _full(g2n), _full(w_hi), _full(w_lo), _full(router_bias)],
        out_specs=[pl.BlockSpec((t * PACK_SUBLANES, LANES), lambda i: (i, 0)), small, small, small,
                   pl.BlockSpec((e, LANES), lambda i: (0, 0))],
        out_shape=[jax.ShapeDtypeStruct((n * PACK_SUBLANES, LANES), U32),
                   jax.ShapeDtypeStruct((TOP_K, n), I32),
                   jax.ShapeDtypeStruct((TOP_K, n), F32),
                   jax.ShapeDtypeStruct((TOP_K, n), I32),
                   jax.ShapeDtypeStruct((e, LANES), F32)],
        scratch_shapes=[pltpu.VMEM((e, LANES), F32)],
        compiler_params=_params("arbitrary"),
        name="route",
    )(xl2, mod3, g2n, w_hi, w_lo, router_bias)


def _dest_kernel(eidx_ref, rank_ref, start_ref, o_ref):
    t = eidx_ref.shape[1]
    ei_all = lax.broadcasted_iota(I32, (N_EXPERTS, t), 0)
    st = start_ref[:, 0:1]
    for k in range(TOP_K):
        hit = ei_all == eidx_ref[k:k + 1, :]
        o_ref[k:k + 1, :] = jnp.sum(jnp.where(hit, st, 0), axis=0, keepdims=True) + rank_ref[k:k + 1, :]


def _dest_rows(eidx, rank, starts_b):
    n = eidx.shape[1]
    t = 1024
    small = pl.BlockSpec((TOP_K, t), lambda i: (0, i))
    return pl.pallas_call(
        _dest_kernel,
        grid=(n // t,),
        in_specs=[small, small, _full(starts_b)],
        out_specs=small,
        out_shape=jax.ShapeDtypeStruct((TOP_K, n), I32),
        compiler_params=_params("arbitrary"),
        name="dest",
    )(eidx, rank, starts_b)


def _sc_workers():
    info = plsc.get_sparse_core_info()
    return info.num_cores, info.num_cores * info.num_subcores


def _sc_dispatch(dest_flat, src3, out_rows):
    n = src3.shape[0]
    n_cores, n_workers = _sc_workers()
    per_worker = n // n_workers
    assert per_worker % SC_CHUNK == 0
    mesh = plsc.VectorSubcoreMesh(core_axis_name="c", subcore_axis_name="s")

    @functools.partial(
        pl.kernel, mesh=mesh,
        out_type=jax.ShapeDtypeStruct((out_rows,) + src3.shape[1:], src3.dtype),
        scratch_types=[pltpu.VMEM((SC_CHUNK,) + src3.shape[1:], src3.dtype)]
        + [pltpu.VMEM((SC_CHUNK,), I32)] * TOP_K,
        name="dispatch",
    )
    def run(dest_hbm, src_hbm, out_hbm, rows_v, *idx_v):
        worker = lax.axis_index("s") * n_cores + lax.axis_index("c")

        @pl.loop(0, per_worker // SC_CHUNK)
        def _(j):
            base = worker * per_worker + j * SC_CHUNK
            pltpu.sync_copy(src_hbm.at[pl.ds(base, SC_CHUNK)], rows_v)
            for k in range(TOP_K):
                pltpu.sync_copy(dest_hbm.at[pl.ds(k * n + base, SC_CHUNK)], idx_v[k])
            for k in range(TOP_K):
                pltpu.sync_copy(rows_v, out_hbm.at[idx_v[k]])

    return run(dest_flat, src3)


def _sc_combine(dest_flat, ys3, n):
    n_cores, n_workers = _sc_workers()
    per_worker = n // n_workers
    assert per_worker % SC_CHUNK == 0
    mesh = plsc.VectorSubcoreMesh(core_axis_name="c", subcore_axis_name="s")

    @functools.partial(
        pl.kernel, mesh=mesh,
        out_type=jax.ShapeDtypeStruct((TOP_K * n,) + ys3.shape[1:], ys3.dtype),
        scratch_types=[pltpu.VMEM((SC_CHUNK,) + ys3.shape[1:], ys3.dtype), pltpu.VMEM((SC_CHUNK,), I32)],
        name="combine",
    )
    def run(dest_hbm, ys_hbm, out_hbm, rows_v, idx_v):
        worker = lax.axis_index("s") * n_cores + lax.axis_index("c")

        @pl.loop(0, per_worker // SC_CHUNK)
        def _(j):
            base = worker * per_worker + j * SC_CHUNK
            for k in range(TOP_K):
                pltpu.sync_copy(dest_hbm.at[pl.ds(k * n + base, SC_CHUNK)], idx_v)
                pltpu.sync_copy(ys_hbm.at[idx_v], rows_v)
                pltpu.sync_copy(rows_v, out_hbm.at[pl.ds(k * n + base, SC_CHUNK)])

    return run(dest_flat, ys3)


def _chunk_metadata(counts, n_rows):
    ch = EXPERT_CHUNK
    n_ch = (counts + ch - 1) // ch
    cum = jnp.cumsum(n_ch)
    total = cum[-1]
    max_chunks = n_rows // ch + N_EXPERTS
    i = jnp.arange(max_chunks, dtype=I32)
    e = jnp.sum((cum[None, :] <= i[:, None]).astype(I32), axis=1)
    e_last = jnp.max(jnp.where(counts > 0, jnp.arange(N_EXPERTS, dtype=I32), 0))
    exp = jnp.where(i < total, jnp.minimum(e, N_EXPERTS - 1), e_last).astype(I32)
    newexp = jnp.concatenate([jnp.ones((1,), I32), (exp[1:] != exp[:-1]).astype(I32)])
    change_at = jnp.where(newexp == 1, i, max_chunks)
    nxt_change = lax.cummin(jnp.concatenate([change_at[1:], jnp.array([max_chunks], I32)]), reverse=True)
    nexp = jnp.where(nxt_change < max_chunks, exp[jnp.minimum(nxt_change, max_chunks - 1)], -1).astype(I32)
    starts = (cum - n_ch) * ch
    return starts.astype(I32), (exp, newexp, nexp, total.astype(I32).reshape(1))


def _expert_kernel(exp_ref, newexp_ref, nexp_ref, total_ref, xs_hbm, wg_hbm, wu_hbm, wd_hbm, after_hbm, ys_hbm,
                   xbuf, ybuf, wgf, wuf, wdf, wgb, wub, wdb, sem_x, sem_y, sem_w):
    del after_hbm
    ch = EXPERT_CHUNK
    nbuf = EXPERT_RING
    rows = ch * PACK_SUBLANES
    total = total_ref[0]

    def x_copy(i, slot):
        r0 = pl.multiple_of(i * rows, rows)
        return pltpu.make_async_copy(xs_hbm.at[pl.ds(r0, rows)], xbuf.at[slot], sem_x.at[slot])

    def y_copy(i, slot):
        r0 = pl.multiple_of(i * rows, rows)
        return pltpu.make_async_copy(ybuf.at[slot], ys_hbm.at[pl.ds(r0, rows)], sem_y.at[slot])

    def weight_copies(e):
        return (pltpu.make_async_copy(wg_hbm.at[e], wgf, sem_w.at[0]),
                pltpu.make_async_copy(wu_hbm.at[e], wuf, sem_w.at[1]),
                pltpu.make_async_copy(wd_hbm.at[e], wdf, sem_w.at[2]))

    for cp in weight_copies(exp_ref[0]):
        cp.start()
    for b in range(nbuf - 1):
        @pl.when(b < total)
        def _(b=b):
            x_copy(b, b).start()

    def chunk(i, carry):
        slot = lax.rem(i, nbuf)
        x_copy(i, slot).wait()
        ahead = i + (nbuf - 1)

        @pl.when(ahead < total)
        def _():
            x_copy(ahead, lax.rem(ahead, nbuf)).start()

        @pl.when(newexp_ref[i] == 1)
        def _():
            for cp in weight_copies(exp_ref[i]):
                cp.wait()
            wgb[...] = wgf[...].astype(BF16)
            wub[...] = wuf[...].astype(BF16)
            wdb[...] = wdf[...].astype(BF16)

            @pl.when(nexp_ref[i] >= 0)
            def _():
                for cp in weight_copies(nexp_ref[i]):
                    cp.start()

        @pl.when(i >= nbuf)
        def _():
            y_copy(i, slot).wait()

        xr = xbuf.at[slot]
        yr = ybuf.at[slot]
        x = _load_packed(lambda sl: xr[sl, :], ch).astype(BF16)
        g = jnp.dot(x, wgb[...], preferred_element_type=F32)
        u = jnp.dot(x, wub[...], preferred_element_type=F32)
        y = jnp.dot((g * jax.nn.sigmoid(g) * u).astype(BF16), wdb[...], preferred_element_type=F32)
        _store_packed(yr, y)
        y_copy(i, slot).start()
        return carry

    lax.fori_loop(0, total, chunk, 0)

    for b in range(nbuf):
        @pl.when(b < total)
        def _(b=b):
            y_copy(0, b).wait()


def _experts(meta, xs, w_gate, w_up, w_down, after):
    e, d, f = w_gate.shape
    hbm = pl.BlockSpec(memory_space=pl.ANY)
    rows = EXPERT_CHUNK * PACK_SUBLANES
    grid_spec = pltpu.PrefetchScalarGridSpec(
        num_scalar_prefetch=len(meta), grid=(1,),
        in_specs=[hbm, hbm, hbm, hbm, hbm], out_specs=hbm,
        scratch_shapes=[pltpu.VMEM((EXPERT_RING, rows, LANES), U32), pltpu.VMEM((EXPERT_RING, rows, LANES), U32),
                        pltpu.VMEM((d, f), F32), pltpu.VMEM((d, f), F32), pltpu.VMEM((f, d), F32),
                        pltpu.VMEM((d, f), BF16), pltpu.VMEM((d, f), BF16), pltpu.VMEM((f, d), BF16),
                        pltpu.SemaphoreType.DMA((EXPERT_RING,)), pltpu.SemaphoreType.DMA((EXPERT_RING,)),
                        pltpu.SemaphoreType.DMA((3,))])
    return pl.pallas_call(
        _expert_kernel,
        grid_spec=grid_spec,
        out_shape=jax.ShapeDtypeStruct(xs.shape, xs.dtype),
        compiler_params=_params("arbitrary"),
        name="experts",
    )(*meta, xs, w_gate, w_up, w_down, after)


def _final_kernel(base_ref, yt_ref, wt_ref, mod_ref, g_ref, *rest, d):
    o_ref = rest[-1]
    t = base_ref.shape[0]
    w = wt_ref[...]
    routed = w[:, 0:1] * _load_packed(lambda sl: yt_ref[0, sl, :], t)
    for k in range(1, TOP_K):
        routed = routed + w[:, k:k + 1] * _load_packed(lambda sl, k=k: yt_ref[k, sl, :], t)
    g2 = mod_ref[0][:, 5 * d:6 * d]
    o_ref[...] = _rms(base_ref[...] + g2 * routed, g_ref[...])


def _final(base, ytok3, wts_t, mod3, tiles_per_batch, final_g, part, n_parts, prev_out):
    n, d = base.shape
    t = TOKEN_TILE
    tiles = n // t // n_parts
    off = part * tiles
    kern = functools.partial(_final_kernel, d=d)
    tok = pl.BlockSpec((t, d), lambda i: (off + i, 0))
    in_specs = [tok,
                pl.BlockSpec((TOP_K, t * PACK_SUBLANES, LANES), lambda i: (0, i, 0)),
                pl.BlockSpec((t, TOP_K), lambda i: (off + i, 0)),
                pl.BlockSpec((1, 1, mod3.shape[2]), lambda i: ((off + i) // tiles_per_batch, 0, 0)),
                _full(final_g)]
    args = [base, ytok3, wts_t, mod3, final_g]
    aliases = {}
    if prev_out is not None:
        in_specs.append(pl.BlockSpec(memory_space=pl.ANY))
        args.append(prev_out)
        aliases = {len(args) - 1: 0}
    return pl.pallas_call(
        kern,
        grid=(tiles,),
        in_specs=in_specs,
        out_specs=tok,
        out_shape=jax.ShapeDtypeStruct((n, d), F32),
        input_output_aliases=aliases,
        compiler_params=_params("arbitrary"),
        name="final",
    )(*args)


def kernel(x, c, ctx, c_ctx, w_mod, b_mod, norm1_g, norm2_g, w_in, ssm_lam_re, ssm_lam_im, ssm_log_dt, ssm_b_re, ssm_b_im, ssm_c_re, ssm_c_im, ssm_d, w_glu, w_ssm_out, conv_w, w_conv_out, w_o, w_router, router_bias, w_gate, w_up, w_down, ws_gate, ws_up, ws_down, final_g):
    b, l, d = x.shape
    n = b * l
    sw = ssm_d.shape[1]
    assert w_mod.shape[0] == 1, "single layer"
    assert b == SUBLANES and l % TOKEN_TILE == 0 and l % S5_CHUNK == 0 and ctx.shape[1] % S5_CHUNK == 0 and TIME_TILE % GRID_W == 0

    mod_rows = 2 * SUBLANES
    c_all = jnp.concatenate([c, c_ctx[None, :], jnp.zeros((mod_rows - b - 1, d), F32)], axis=0)
    mod = _modulation(c_all, w_mod[0], b_mod[0])
    mod3 = mod.reshape(mod_rows, 1, mod.shape[1])

    w_in_b = w_in[0].astype(BF16)
    g1n = norm1_g[0].reshape(1, d)
    u_lat, sga, gbt = _in_proj(x, mod3, g1n, w_in_b, conv_w[0], w_conv_out[0].astype(BF16))
    u_ctx = _ctx_proj(ctx, mod3, b, g1n, w_in_b[:, :sw])

    ops = _blocked_operators(ssm_lam_re[0], ssm_lam_im[0], ssm_log_dt[0], ssm_b_re[0], ssm_b_im[0],
                             ssm_c_re[0], ssm_c_im[0])
    y = _s5_blocked(u_ctx, u_lat, ssm_d[0].reshape(1, sw), ops, b, S5_CHUNK)

    xl = _mixer_out(y, sga, gbt, x, mod3, w_glu[0].astype(BF16), w_ssm_out[0].astype(BF16),
                    w_o[0].astype(BF16))

    tiles_per_batch = l // TOKEN_TILE
    xl2 = xl.reshape(n, d)
    g2n = norm2_g[0].reshape(1, d)
    h2, eidx, wts, rank, cnt = _route(xl2, mod3, tiles_per_batch, g2n, w_router[0].T,
                                      router_bias[0].reshape(N_EXPERTS, 1))

    counts = cnt[:, 0].astype(I32)
    n_rows = n * TOP_K
    starts, meta = _chunk_metadata(counts, n_rows)
    buf_rows = n_rows + N_EXPERTS * EXPERT_CHUNK
    dest = _dest_rows(eidx, rank, jnp.broadcast_to(starts[:, None], (N_EXPERTS, LANES)))

    xs = _sc_dispatch(dest.reshape(TOP_K * n), h2.reshape(n, PACK_SUBLANES, LANES), buf_rows)
    base = _shared(xl2, mod3, tiles_per_batch, g2n, ws_gate[0].astype(BF16), ws_up[0].astype(BF16),
                   ws_down[0].astype(BF16))
    ys = _experts(meta, xs.reshape(buf_rows * PACK_SUBLANES, LANES), w_gate[0], w_up[0], w_down[0], base)
    ys3 = ys.reshape(buf_rows, PACK_SUBLANES, LANES)

    n_part = n // COMBINE_PARTS
    wts_t = wts.T
    out = None
    for part in range(COMBINE_PARTS):
        dest_part = dest[:, part * n_part:(part + 1) * n_part].reshape(TOP_K * n_part)
        ytok = _sc_combine(dest_part, ys3, n_part)
        out = _final(base, ytok.reshape(TOP_K, n_part * PACK_SUBLANES, LANES), wts_t, mod3, tiles_per_batch,
                     final_g.reshape(1, d), part, COMBINE_PARTS, out)
    return out.reshape(b, l, d)
```

```python
import functools

import jax
import jax.numpy as jnp
from jax import lax
from jax.experimental import pallas as pl
from jax.experimental.pallas import tpu as pltpu
from jax.experimental.pallas import tpu_sc as plsc

F32 = jnp.float32
BF16 = jnp.bfloat16
I32 = jnp.int32
U32 = jnp.uint32

EPS = 1e-6
GRID_W = 64
SSM_GROUP = 16
N_EXPERTS = 256
TOP_K = 8
N_EXPERT_GROUPS = 8
EXPERTS_PER_GROUP = N_EXPERTS // N_EXPERT_GROUPS
TOP_K_GROUPS = 4
ROUTE_SCALE = 2.5

SUBLANES = 8
LANES = 128
VMEM_LIMIT_BYTES = 48 * 1024 * 1024

TIME_TILE = 64
TOKEN_TILE = 256
TIME_BLOCK = 16
S5_CHUNK = 256
EXPERT_CHUNK = 256
EXPERT_RING = 4
SC_CHUNK = 128
COMBINE_PARTS = 4
PACK_SUBLANES = 4


def _dot(a, b):
    return jnp.dot(a.astype(BF16), b.astype(BF16), preferred_element_type=F32)


def _rms(xf, g):
    return xf * lax.rsqrt(jnp.mean(xf * xf, axis=-1, keepdims=True) + EPS) * g


def _params(*sem):
    return pltpu.CompilerParams(dimension_semantics=sem, vmem_limit_bytes=VMEM_LIMIT_BYTES)


def _full(a):
    return pl.BlockSpec(a.shape, lambda *_: (0,) * a.ndim)


def _pack_rows(v):
    half = v.shape[1] // 2
    lo = lax.bitcast_convert_type(v[:, :half].astype(BF16).astype(F32), U32) >> 16
    hi = lax.bitcast_convert_type(v[:, half:].astype(BF16).astype(F32), U32) & jnp.uint32(0xFFFF0000)
    return lo | hi


def _unpack_lo(w):
    return lax.bitcast_convert_type(w << 16, F32)


def _unpack_hi(w):
    return lax.bitcast_convert_type(w & jnp.uint32(0xFFFF0000), F32)


def _store_packed(ref, v):
    t = v.shape[0]
    w = _pack_rows(v)
    for c in range(PACK_SUBLANES):
        ref[pl.ds(c, t, stride=PACK_SUBLANES), :] = w[:, c * LANES:(c + 1) * LANES]


def _load_packed(load, t):
    ws = [load(pl.ds(c, t, stride=PACK_SUBLANES)) for c in range(PACK_SUBLANES)]
    return jnp.concatenate([_unpack_lo(w) for w in ws] + [_unpack_hi(w) for w in ws], axis=1)


def _mod_kernel(c_ref, w_ref, b_ref, o_ref):
    c = c_ref[...]
    o_ref[...] = _dot(c * jax.nn.sigmoid(c), w_ref[...]) + b_ref[...]


def _modulation(c_all, w_mod, b_mod):
    rows, d = c_all.shape
    cols = w_mod.shape[1]
    blk = 1536
    return pl.pallas_call(
        _mod_kernel,
        grid=(cols // blk,),
        in_specs=[pl.BlockSpec((rows, d), lambda j: (0, 0)),
                  pl.BlockSpec((d, blk), lambda j: (0, j)),
                  pl.BlockSpec((1, blk), lambda j: (0, j))],
        out_specs=pl.BlockSpec((rows, blk), lambda j: (0, j)),
        out_shape=jax.ShapeDtypeStruct((rows, cols), F32),
        compiler_params=_params("arbitrary"),
        name="mod",
    )(c_all, w_mod, b_mod.reshape(1, cols))


def _per_row(m3, lo, hi, tt):
    nb = m3.shape[0]
    return jnp.broadcast_to(m3[:, :, lo:hi], (nb, tt, hi - lo)).reshape(nb * tt, hi - lo)


def _to_time_major(val, nb, tt):
    c = val.shape[1]
    return pltpu.einshape("btc->tbc", val.reshape(nb, tt, c)).reshape(nb * tt, c)


def _to_batch_major(val, nb, tt):
    c = val.shape[1]
    return pltpu.einshape("tbc->btc", val.reshape(tt, nb, c)).reshape(nb * tt, c)


def _in_proj_kernel(x_ref, mod_ref, g_ref, w_ref, cw_ref, wco_ref, u_ref, sga_ref, gbt_ref, *, d, sw):
    nb, tt, _ = x_ref.shape
    rows = nb * tt
    m3 = mod_ref[...]
    x = x_ref[...].reshape(rows, d)
    h = _rms(x, g_ref[...]) * (1.0 + _per_row(m3, d, 2 * d, tt)) + _per_row(m3, 0, d, tt)
    hb = h.astype(BF16)
    u_ref[...] = _to_time_major(jnp.dot(hb, w_ref[:, 0:sw], preferred_element_type=F32), nb, tt)
    cb = jnp.dot(hb, w_ref[:, sw:2 * sw], preferred_element_type=F32)
    cc = jnp.dot(hb, w_ref[:, 2 * sw:3 * sw], preferred_element_type=F32)
    cv = jnp.dot(hb, w_ref[:, 3 * sw:4 * sw], preferred_element_type=F32)
    ccv = cc * cv
    col = lax.broadcasted_iota(I32, ccv.shape, 0) % GRID_W
    prev = jnp.where(col == 0, 0.0, pltpu.roll(ccv, 1, axis=0))
    nxt = jnp.where(col == GRID_W - 1, 0.0, pltpu.roll(ccv, rows - 1, axis=0))
    cw = cw_ref[...]
    conv = prev * cw[0:1, :] + ccv * cw[1:2, :] + nxt * cw[2:3, :]
    y_conv = _dot(cb * conv, wco_ref[...])
    ga = jnp.dot(hb, w_ref[:, 4 * sw:4 * sw + d], preferred_element_type=F32)
    gb = jnp.dot(hb, w_ref[:, 4 * sw + d:4 * sw + 2 * d], preferred_element_type=F32)
    sga_ref[...] = jax.nn.sigmoid(ga).reshape(nb, tt, d)
    gbt_ref[...] = (jax.nn.sigmoid(gb) * y_conv).reshape(nb, tt, d)


def _in_proj(x, mod3, g1n, w_in_b, conv_w, w_conv_out_b):
    b, l, d = x.shape
    sw = conv_w.shape[1]
    tt = TIME_TILE
    kern = functools.partial(_in_proj_kernel, d=d, sw=sw)
    tok = pl.BlockSpec((b, tt, d), lambda j: (0, j, 0))
    return pl.pallas_call(
        kern,
        grid=(l // tt,),
        in_specs=[tok,
                  pl.BlockSpec((b, 1, mod3.shape[2]), lambda j: (0, 0, 0)),
                  _full(g1n), _full(w_in_b), _full(conv_w), _full(w_conv_out_b)],
        out_specs=[pl.BlockSpec((tt * b, sw), lambda j: (j, 0)), tok, tok],
        out_shape=[jax.ShapeDtypeStruct((l * b, sw), F32),
                   jax.ShapeDtypeStruct((b, l, d), F32),
                   jax.ShapeDtypeStruct((b, l, d), F32)],
        compiler_params=_params("arbitrary"),
        name="in_proj",
    )(x, mod3, g1n, w_in_b, conv_w, w_conv_out_b)


def _ctx_proj_kernel(x_ref, mod_ref, g_ref, w_ref, u_ref, *, d):
    nb, tt, _ = x_ref.shape
    m = mod_ref[0]
    x = x_ref[...].reshape(nb * tt, d)
    h = _rms(x, g_ref[...]) * (1.0 + m[:, d:2 * d]) + m[:, 0:d]
    u_ref[...] = _to_time_major(_dot(h, w_ref[...]), nb, tt)


def _ctx_proj(ctx, mod3, ctx_row, g1n, w_u_b):
    b, lc, d = ctx.shape
    sw = w_u_b.shape[1]
    tt = TIME_TILE
    kern = functools.partial(_ctx_proj_kernel, d=d)
    return pl.pallas_call(
        kern,
        grid=(lc // tt,),
        in_specs=[pl.BlockSpec((b, tt, d), lambda j: (0, j, 0)),
                  pl.BlockSpec((1, 1, mod3.shape[2]), lambda j: (ctx_row, 0, 0)),
                  _full(g1n), _full(w_u_b)],
        out_specs=pl.BlockSpec((tt * b, sw), lambda j: (j, 0)),
        out_shape=jax.ShapeDtypeStruct((lc * b, sw), F32),
        compiler_params=_params("arbitrary"),
        name="ctx_proj",
    )(ctx, mod3, g1n, w_u_b)


_NT = (((1,), (1,)), ((), ()))


def _dot3_nt(a, b):
    a_hi, a_lo = _split_bf16(a)
    b_hi, b_lo = _split_bf16(b)
    return (lax.dot_general(a_hi, b_hi, _NT, preferred_element_type=F32)
            + lax.dot_general(a_hi, b_lo, _NT, preferred_element_type=F32)
            + lax.dot_general(a_lo, b_hi, _NT, preferred_element_type=F32))


def _operator_kernel(akr_ref, aki_ref, bbr_ref, bbi_ref, cr_ref, ci_ref, tfb_ref, bst_ref, cst_ref):
    tb = TIME_BLOCK
    q, p = bbr_ref.shape[2], bbr_ref.shape[3]
    w = tb * q

    def times_ak(d, ks, xr, xi):
        re, im = [], []
        for k in ks:
            ar, ai = akr_ref[d, 0, k:k + 1, :], aki_ref[d, 0, k:k + 1, :]
            re.append(ar * xr - ai * xi)
            im.append(ar * xi + ai * xr)
        return jnp.concatenate(re, axis=0), jnp.concatenate(im, axis=0)

    col = lax.broadcasted_iota(I32, (p, w), 1)
    row = lax.broadcasted_iota(I32, (p, w), 0)

    def place(quarter):
        return (col == row + quarter * p).astype(BF16)

    inj, ca = [], []
    for d in range(2):
        inj += list(times_ak(d, range(tb - 1, -1, -1) if d == 0 else range(tb), bbr_ref[d, 0], bbi_ref[d, 0]))
        re, im = times_ak(d, range(1, tb + 1) if d == 0 else range(tb, 0, -1), cr_ref[d, 0], ci_ref[d, 0])
        ca += [re, -im]
    bst = jnp.zeros((w, w), F32)
    cst = jnp.zeros((w, w), F32)
    for quarter in range(4):
        pm = place(quarter)
        bst = bst + jnp.dot(inj[quarter].astype(BF16), pm, preferred_element_type=F32)
        cst = cst + lax.dot_general(pm, ca[quarter].astype(BF16), (((0,), (1,)), ((), ())),
                                    preferred_element_type=F32)
    bst_ref[0] = bst.astype(BF16)
    cst_ref[0] = cst.astype(BF16)

    sub = lax.broadcasted_iota(I32, (q, w), 0)
    lane = lax.broadcasted_iota(I32, (q, w), 1)
    lag_rows = []
    for d in range(2):
        abr, abi = times_ak(d, range(tb), bbr_ref[d, 0], bbi_ref[d, 0])
        m = _dot3_nt(abr, cr_ref[d, 0]) - _dot3_nt(abi, ci_ref[d, 0])
        acc = jnp.zeros((q, w), F32)
        for k in range(tb):
            blk = tb - 1 - k if d == 1 else k
            put = ((lane - blk * q) == sub).astype(BF16)
            acc = acc + jnp.dot(m[k * q:(k + 1) * q, :].astype(BF16), put, preferred_element_type=F32)
        lag_rows.append(acc)
    rows_f, rows_b = lag_rows

    for s in range(tb):
        right = s * q
        left = (tb - 1 - s) * q
        fwd = jnp.where(lane >= right, pltpu.roll(rows_f, right, axis=1) if right else rows_f, 0.0)
        bwd = jnp.where(lane < w - left, pltpu.roll(rows_b, w - left, axis=1) if left else rows_b, 0.0)
        tfb_ref[0, s * q:(s + 1) * q, :] = (fwd + bwd).astype(BF16)


def _blocked_operators(lam_re, lam_im, log_dt, b_re, b_im, c_re, c_im):
    tb = TIME_BLOCK
    dt = jnp.exp(log_dt)[..., None]
    k = jnp.arange(tb + 1, dtype=F32)[None, None, :, None]
    mag = jnp.exp(lam_re * dt)[:, :, None, :] ** k
    ang = (lam_im * dt)[:, :, None, :] * k
    ak_re = mag * jnp.cos(ang)
    ak_im = mag * jnp.sin(ang)
    a_re, a_im = ak_re[:, :, 1], ak_im[:, :, 1]
    den = lam_re * lam_re + lam_im * lam_im
    k_re = ((a_re - 1.0) * lam_re + a_im * lam_im) / den
    k_im = (a_im * lam_re - (a_re - 1.0) * lam_im) / den
    bt_re = jnp.swapaxes(b_re, 2, 3)
    bt_im = jnp.swapaxes(b_im, 2, 3)
    bbt_re = k_re[:, :, None, :] * bt_re - k_im[:, :, None, :] * bt_im
    bbt_im = k_re[:, :, None, :] * bt_im + k_im[:, :, None, :] * bt_re
    nd, g, q, p = bbt_re.shape
    w = tb * q

    per_group = lambda a: pl.BlockSpec((nd, 1) + a.shape[2:], lambda i: (0, i, 0, 0))
    out = pl.BlockSpec((1, w, w), lambda i: (i, 0, 0))
    ins = (ak_re, ak_im, bbt_re, bbt_im, c_re, c_im)
    tfb, bst, cst = pl.pallas_call(
        _operator_kernel,
        grid=(g,),
        in_specs=[per_group(a) for a in ins],
        out_specs=[out, out, out],
        out_shape=[jax.ShapeDtypeStruct((g, w, w), BF16)] * 3,
        compiler_params=_params("arbitrary"),
        name="s5_operators",
    )(*ins)
    ax = jnp.concatenate([ak_re[:, :, tb], ak_re[:, :, tb]], axis=-1).reshape(nd, 1, g * 2 * p)
    ay = jnp.concatenate([-ak_im[:, :, tb], ak_im[:, :, tb]], axis=-1).reshape(nd, 1, g * 2 * p)
    return tfb, bst, cst, ax, ay


def _lane_chunk(shape):
    return lax.broadcasted_iota(I32, shape, 1) // SSM_GROUP


def _regroup_kernel(uc_ref, ul_ref, bst_ref, xg_ref, e_ref, *, n_ctx, nb, groups, q):
    j = pl.program_id(0)
    rows_in = uc_ref.shape[0]
    nblk = rows_in // (TIME_BLOCK * nb)
    rows = nblk * nb
    u = jnp.where(j < n_ctx, uc_ref[...], ul_ref[...])
    u3 = u.reshape(nblk, TIME_BLOCK * nb, u.shape[1])
    xs = [u3[:, s * nb:(s + 1) * nb, :].reshape(rows, u.shape[1]) for s in range(TIME_BLOCK)]
    per_tile = LANES // q
    chunk = _lane_chunk((rows, LANES))
    for g in range(groups):
        jt, qq = divmod(g, per_tile)
        for h in range(TIME_BLOCK // per_tile):
            acc = jnp.zeros((rows, LANES), F32)
            for s in range(h * per_tile, (h + 1) * per_tile):
                pos = s % per_tile
                a = xs[s][:, jt * LANES:(jt + 1) * LANES]
                shift = ((pos - qq) % per_tile) * q
                r = pltpu.roll(a, shift, axis=1) if shift else a
                acc = jnp.where(chunk == pos, r, acc)
            xg_ref[:, (2 * g + h) * LANES:(2 * g + h + 1) * LANES] = acc.astype(BF16)
    gw = TIME_BLOCK * q
    for g in range(groups):
        e = jnp.dot(xg_ref[:, g * gw:(g + 1) * gw], bst_ref[g], preferred_element_type=F32)
        half = e.shape[1] // 2
        e_ref[0, :, g * half:(g + 1) * half] = e[:, :half]
        e_ref[1, :, g * half:(g + 1) * half] = e[:, half:]


def _s5_regroup(u_ctx, u_lat, bst, nb, tc):
    rows_c, sw = u_ctx.shape
    rows_l = u_lat.shape[0]
    rc = tc * nb
    n_ctx, n_lat = rows_c // rc, rows_l // rc
    groups, gw, p4 = bst.shape
    q = gw // TIME_BLOCK
    rows_out = (tc // TIME_BLOCK) * nb
    total_rows = (n_ctx + n_lat) * rows_out
    kern = functools.partial(_regroup_kernel, n_ctx=n_ctx, nb=nb, groups=groups, q=q)
    return pl.pallas_call(
        kern,
        grid=(n_ctx + n_lat,),
        in_specs=[pl.BlockSpec((rc, sw), lambda j: (jnp.minimum(j, n_ctx - 1), 0)),
                  pl.BlockSpec((rc, sw), lambda j: (jnp.maximum(j - n_ctx, 0), 0)),
                  pl.BlockSpec(bst.shape, lambda j: (0, 0, 0), pipeline_mode=pl.Buffered(1))],
        out_specs=[pl.BlockSpec((rows_out, groups * gw), lambda j: (j, 0)),
                   pl.BlockSpec((2, rows_out, groups * p4 // 2), lambda j: (0, j, 0))],
        out_shape=[jax.ShapeDtypeStruct((total_rows, groups * gw), BF16),
                   jax.ShapeDtypeStruct((2, total_rows, groups * p4 // 2), F32)],
        compiler_params=_params("arbitrary"),
        name="s5_regroup",
    )(u_ctx, u_lat, bst)


def _carry_kernel(e_ref, ax_ref, ay_ref, s_ref, st_ref, *, nb):
    dirn = pl.program_id(0)
    j = pl.program_id(1)
    rows = e_ref.shape[1]
    nblk = rows // nb
    lanes = e_ref.shape[2]

    @pl.when(j == 0)
    def _():
        st_ref[...] = jnp.zeros_like(st_ref)

    ax = jnp.broadcast_to(ax_ref[0], (nb, lanes))
    ay = jnp.broadcast_to(ay_ref[0], (nb, lanes))

    def body(i, s):
        blk = jnp.where(dirn == 0, i, nblk - 1 - i)
        r0 = pl.multiple_of(blk * nb, nb)
        s_ref[0, pl.ds(r0, nb), :] = s.astype(s_ref.dtype)
        swapped = jnp.concatenate([pltpu.roll(s[:, t * LANES:(t + 1) * LANES], LANES // 2, axis=1)
                                   for t in range(lanes // LANES)], axis=1)
        return ax * s + ay * swapped + e_ref[0, pl.ds(r0, nb), :]

    st_ref[...] = lax.fori_loop(0, nblk, body, st_ref[...])


def _s5_carry(e, ax, ay, nb, n_ctx_chunks, rows_per_chunk):
    nd, total_rows, lanes = e.shape
    n_chunks = total_rows // rows_per_chunk
    n_lat = n_chunks - n_ctx_chunks

    def idx(d, j):
        jc = jnp.minimum(j, n_ctx_chunks - 1)
        jl = jnp.maximum(j - n_ctx_chunks, 0)
        fwd = j
        bwd = jnp.where(j < n_ctx_chunks, n_ctx_chunks - 1 - jc, n_chunks - 1 - jl)
        return (d, jnp.where(d == 0, fwd, bwd), 0)

    kern = functools.partial(_carry_kernel, nb=nb)
    return pl.pallas_call(
        kern,
        grid=(nd, n_chunks),
        in_specs=[pl.BlockSpec((1, rows_per_chunk, lanes), idx),
                  pl.BlockSpec((1, 1, lanes), lambda d, j: (d, 0, 0)),
                  pl.BlockSpec((1, 1, lanes), lambda d, j: (d, 0, 0))],
        out_specs=pl.BlockSpec((1, rows_per_chunk, lanes), idx),
        out_shape=jax.ShapeDtypeStruct(e.shape, F32),
        scratch_shapes=[pltpu.VMEM((nb, lanes), F32)],
        compiler_params=_params("arbitrary", "arbitrary"),
        name="s5_carry",
    )(e, ax, ay)


def _output_kernel(xg_ref, s_ref, u_ref, d_ref, tfb_ref, cst_ref, y_ref, yg_ref, *, nb, groups, q):
    rows = xg_ref.shape[0]
    nblk = rows // nb
    gw = TIME_BLOCK * q
    half = s_ref.shape[2] // groups
    for g in range(groups):
        sg = jnp.concatenate([s_ref[0, :, g * half:(g + 1) * half], s_ref[1, :, g * half:(g + 1) * half]], axis=1)
        yg_ref[:, g * gw:(g + 1) * gw] = (
            jnp.dot(xg_ref[:, g * gw:(g + 1) * gw], tfb_ref[g], preferred_element_type=F32)
            + jnp.dot(sg.astype(BF16), cst_ref[g], preferred_element_type=F32))
    per_tile = LANES // q
    chunk = _lane_chunk((rows, LANES))
    sw = groups * q
    steps = []
    for s in range(TIME_BLOCK):
        h, pos = divmod(s, per_tile)
        tiles = []
        for jt in range(sw // LANES):
            acc = jnp.zeros((rows, LANES), F32)
            for qq in range(per_tile):
                g = jt * per_tile + qq
                a = yg_ref[:, (2 * g + h) * LANES:(2 * g + h + 1) * LANES]
                shift = ((qq - pos) % per_tile) * q
                r = pltpu.roll(a, shift, axis=1) if shift else a
                acc = jnp.where(chunk == qq, r, acc)
            tiles.append(acc)
        steps.append(jnp.concatenate(tiles, axis=1).reshape(nblk, nb, sw))
    y = jnp.concatenate(steps, axis=1).reshape(nblk * TIME_BLOCK * nb, sw)
    y_ref[...] = y + d_ref[...] * u_ref[...]


def _s5_outputs(xg, s, u_lat, ssm_d, tfb, cst, nb, tc, n_ctx_chunks):
    rows_l, sw = u_lat.shape
    rc = tc * nb
    n_lat = rows_l // rc
    groups, gw, _ = tfb.shape
    q = gw // TIME_BLOCK
    rows_out = (tc // TIME_BLOCK) * nb
    kern = functools.partial(_output_kernel, nb=nb, groups=groups, q=q)
    return pl.pallas_call(
        kern,
        grid=(n_lat,),
        in_specs=[pl.BlockSpec((rows_out, groups * gw), lambda j: (j + n_ctx_chunks, 0)),
                  pl.BlockSpec((2, rows_out, s.shape[2]), lambda j: (0, j + n_ctx_chunks, 0)),
                  pl.BlockSpec((rc, sw), lambda j: (j, 0)),
                  pl.BlockSpec((1, sw), lambda j: (0, 0)),
                  pl.BlockSpec(tfb.shape, lambda j: (0, 0, 0), pipeline_mode=pl.Buffered(1)),
                  pl.BlockSpec(cst.shape, lambda j: (0, 0, 0), pipeline_mode=pl.Buffered(1))],
        out_specs=pl.BlockSpec((rc, sw), lambda j: (j, 0)),
        out_shape=jax.ShapeDtypeStruct((rows_l, sw), F32),
        scratch_shapes=[pltpu.VMEM((rows_out, groups * gw), F32)],
        compiler_params=_params("arbitrary"),
        name="s5_output",
    )(xg, s, u_lat, ssm_d, tfb, cst)


def _s5_blocked(u_ctx, u_lat, ssm_d, ops, nb, tc):
    tfb, bst, cst, ax, ay = ops
    n_ctx_chunks = u_ctx.shape[0] // (tc * nb)
    xg, e = _s5_regroup(u_ctx, u_lat, bst, nb, tc)
    s = _s5_carry(e, ax, ay, nb, n_ctx_chunks, (tc // TIME_BLOCK) * nb)
    return _s5_outputs(xg, s, u_lat, ssm_d, tfb, cst, nb, tc, n_ctx_chunks)


def _mixer_kernel(y_ref, sga_ref, gbt_ref, x_ref, mod_ref, wglu_ref, wso_ref, wo_ref, o_ref, *, d, sw):
    nb, tt, _ = x_ref.shape
    rows = nb * tt
    y = _to_batch_major(y_ref[...], nb, tt)
    v = _dot(jax.nn.gelu(y), wglu_ref[...])
    ys = v[:, 0:sw] * jax.nn.sigmoid(v[:, sw:2 * sw])
    y_a = _dot(ys, wso_ref[...])
    merged = sga_ref[...].reshape(rows, d) * y_a + gbt_ref[...].reshape(rows, d)
    o = _dot(merged, wo_ref[...])
    g1 = _per_row(mod_ref[...], 2 * d, 3 * d, tt)
    o_ref[...] = (x_ref[...].reshape(rows, d) + g1 * o).reshape(nb, tt, d)


def _mixer_out(y, sga, gbt, x, mod3, w_glu_b, w_ssm_out_b, w_o_b):
    b, l, d = x.shape
    sw = y.shape[1]
    tt = TIME_TILE
    kern = functools.partial(_mixer_kernel, d=d, sw=sw)
    tok = pl.BlockSpec((b, tt, d), lambda j: (0, j, 0))
    return pl.pallas_call(
        kern,
        grid=(l // tt,),
        in_specs=[pl.BlockSpec((tt * b, sw), lambda j: (j, 0)),
                  tok, tok, tok,
                  pl.BlockSpec((b, 1, mod3.shape[2]), lambda j: (0, 0, 0)),
                  _full(w_glu_b), _full(w_ssm_out_b), _full(w_o_b)],
        out_specs=tok,
        out_shape=jax.ShapeDtypeStruct((b, l, d), F32),
        compiler_params=_params("arbitrary"),
        name="mixer_out",
    )(y, sga, gbt, x, mod3, w_glu_b, w_ssm_out_b, w_o_b)


def _split_bf16(a):
    hi = a.astype(BF16)
    lo = (a - hi.astype(F32)).astype(BF16)
    return hi, lo


def _norm2(xl, m, g, d):
    return _rms(xl, g) * (1.0 + m[:, 4 * d:5 * d]) + m[:, 3 * d:4 * d]


def _shared_kernel(xl_ref, mod_ref, g_ref, wsg_ref, wsu_ref, wsd_ref, base_ref, *, d):
    m = mod_ref[0]
    xl = xl_ref[...]
    hb = _norm2(xl, m, g_ref[...], d).astype(BF16)
    sg = jnp.dot(hb, wsg_ref[...], preferred_element_type=F32)
    su = jnp.dot(hb, wsu_ref[...], preferred_element_type=F32)
    shared = _dot(sg * jax.nn.sigmoid(sg) * su, wsd_ref[...])
    base_ref[...] = xl + m[:, 5 * d:6 * d] * shared


def _shared(xl2, mod3, tiles_per_batch, g2n, ws_gate_b, ws_up_b, ws_down_b):
    n, d = xl2.shape
    t = TOKEN_TILE
    tok = pl.BlockSpec((t, d), lambda i: (i, 0))
    return pl.pallas_call(
        functools.partial(_shared_kernel, d=d),
        grid=(n // t,),
        in_specs=[tok,
                  pl.BlockSpec((1, 1, mod3.shape[2]), lambda i: (i // tiles_per_batch, 0, 0)),
                  _full(g2n), _full(ws_gate_b), _full(ws_up_b), _full(ws_down_b)],
        out_specs=tok,
        out_shape=jax.ShapeDtypeStruct((n, d), F32),
        compiler_params=_params("arbitrary"),
        name="shared",
    )(xl2, mod3, g2n, ws_gate_b, ws_up_b, ws_down_b)


def _route_kernel(xl_ref, mod_ref, g_ref, whi_ref, wlo_ref, rb_ref,
                  h2_ref, eidx_ref, wts_ref, rank_ref, cnt_ref, carry_ref, *, d):
    i = pl.program_id(0)
    t = xl_ref.shape[0]
    h2 = _norm2(xl_ref[...], mod_ref[0], g_ref[...], d)
    _store_packed(h2_ref, h2)

    h_hi, h_lo = _split_bf16(h2)
    w_hi, w_lo = whi_ref[...], wlo_ref[...]
    nt = (((1,), (1,)), ((), ()))
    logits = (lax.dot_general(w_hi, h_hi, nt, preferred_element_type=F32)
              + lax.dot_general(w_hi, h_lo, nt, preferred_element_type=F32)
              + lax.dot_general(w_lo, h_hi, nt, preferred_element_type=F32))
    scores = jax.nn.sigmoid(logits)
    choice = scores + rb_ref[...]

    epg = EXPERTS_PER_GROUP
    gi = lax.broadcasted_iota(I32, (epg, t), 0)
    gs = []
    for g in range(N_EXPERT_GROUPS):
        seg = choice[g * epg:(g + 1) * epg, :]
        m1 = jnp.max(seg, axis=0, keepdims=True)
        i1 = jnp.min(jnp.where(seg == m1, gi, epg), axis=0, keepdims=True)
        m2 = jnp.max(jnp.where(gi == i1, -jnp.inf, seg), axis=0, keepdims=True)
        gs.append(m1 + m2)
    masked = []
    for g in range(N_EXPERT_GROUPS):
        beat = jnp.zeros((1, t), I32)
        for g2 in range(N_EXPERT_GROUPS):
            if g2 < g:
                beat = beat + (gs[g2] >= gs[g]).astype(I32)
            elif g2 > g:
                beat = beat + (gs[g2] > gs[g]).astype(I32)
        keep = beat < TOP_K_GROUPS
        masked.append(jnp.where(keep, choice[g * epg:(g + 1) * epg, :], -jnp.inf))
    cur = jnp.concatenate(masked, axis=0)

    ei_all = lax.broadcasted_iota(I32, (N_EXPERTS, t), 0)
    picks, raw = [], []
    onehot = jnp.zeros((N_EXPERTS, t), F32)
    for _ in range(TOP_K):
        mx = jnp.max(cur, axis=0, keepdims=True)
        ei = jnp.min(jnp.where(cur == mx, ei_all, N_EXPERTS), axis=0, keepdims=True)
        hit = ei_all == ei
        raw.append(jnp.sum(jnp.where(hit, scores, 0.0), axis=0, keepdims=True))
        cur = jnp.where(hit, -jnp.inf, cur)
        onehot = jnp.where(hit, 1.0, onehot)
        picks.append(ei)
    tot = raw[0]
    for k in range(1, TOP_K):
        tot = tot + raw[k]

    @pl.when(i == 0)
    def _():
        carry_ref[...] = jnp.zeros_like(carry_ref)

    upper = (lax.broadcasted_iota(I32, (t, t), 0) < lax.broadcasted_iota(I32, (t, t), 1)).astype(BF16)
    before = jnp.dot(onehot.astype(BF16), upper, preferred_element_type=F32) + carry_ref[:, 0:1]
    for k in range(TOP_K):
        eidx_ref[k:k + 1, :] = picks[k]
        wts_ref[k:k + 1, :] = raw[k] / tot * ROUTE_SCALE
        rk = jnp.sum(jnp.where(ei_all == picks[k], before, 0.0), axis=0, keepdims=True)
        rank_ref[k:k + 1, :] = rk.astype(I32)
    carry_ref[...] = carry_ref[...] + jnp.sum(onehot, axis=1, keepdims=True)
    cnt_ref[...] = carry_ref[...]


def _route(xl2, mod3, tiles_per_batch, g2n, w_router_t, router_bias):
    n, d = xl2.shape
    t = TOKEN_TILE
    e = w_router_t.shape[0]
    w_top = lax.bitcast_convert_type(lax.bitcast_convert_type(w_router_t, U32) & jnp.uint32(0xFFFF0000), F32)
    w_hi, w_lo = w_top.astype(BF16), (w_router_t - w_top).astype(BF16)
    kern = functools.partial(_route_kernel, d=d)
    tok = pl.BlockSpec((t, d), lambda i: (i, 0))
    small = pl.BlockSpec((TOP_K, t), lambda i: (0, i))
    return pl.pallas_call(
        kern,
        grid=(n // t,),
        in_specs=[tok,
                  pl.BlockSpec((1, 1, mod3.shape[2]), lambda i: (i // tiles_per_batch, 0, 0)),
                  _full(g2n), _full(w_hi), _full(w_lo), _full(router_bias)],
        out_specs=[pl.BlockSpec((t * PACK_SUBLANES, LANES), lambda i: (i, 0)), small, small, small,
                   pl.BlockSpec((e, LANES), lambda i: (0, 0))],
        out_shape=[jax.ShapeDtypeStruct((n * PACK_SUBLANES, LANES), U32),
                   jax.ShapeDtypeStruct((TOP_K, n), I32),
                   jax.ShapeDtypeStruct((TOP_K, n), F32),
                   jax.ShapeDtypeStruct((TOP_K, n), I32),
                   jax.ShapeDtypeStruct((e, LANES), F32)],
        scratch_shapes=[pltpu.VMEM((e, LANES), F32)],
        compiler_params=_params("arbitrary"),
        name="route",
    )(xl2, mod3, g2n, w_hi, w_lo, router_bias)


def _dest_kernel(eidx_ref, rank_ref, start_ref, o_ref):
    t = eidx_ref.shape[1]
    ei_all = lax.broadcasted_iota(I32, (N_EXPERTS, t), 0)
    st = start_ref[:, 0:1]
    for k in range(TOP_K):
        hit = ei_all == eidx_ref[k:k + 1, :]
        o_ref[k:k + 1, :] = jnp.sum(jnp.where(hit, st, 0), axis=0, keepdims=True) + rank_ref[k:k + 1, :]


def _dest_rows(eidx, rank, starts_b):
    n = eidx.shape[1]
    t = 1024
    small = pl.BlockSpec((TOP_K, t), lambda i: (0, i))
    return pl.pallas_call(
        _dest_kernel,
        grid=(n // t,),
        in_specs=[small, small, _full(starts_b)],
        out_specs=small,
        out_shape=jax.ShapeDtypeStruct((TOP_K, n), I32),
        compiler_params=_params("arbitrary"),
        name="dest",
    )(eidx, rank, starts_b)


def _sc_workers():
    info = plsc.get_sparse_core_info()
    return info.num_cores, info.num_cores * info.num_subcores


def _sc_dispatch(dest_flat, src3, out_rows):
    n = src3.shape[0]
    n_cores, n_workers = _sc_workers()
    per_worker = n // n_workers
    assert per_worker % SC_CHUNK == 0
    mesh = plsc.VectorSubcoreMesh(core_axis_name="c", subcore_axis_name="s")

    @functools.partial(
        pl.kernel, mesh=mesh,
        out_type=jax.ShapeDtypeStruct((out_rows,) + src3.shape[1:], src3.dtype),
        scratch_types=[pltpu.VMEM((SC_CHUNK,) + src3.shape[1:], src3.dtype)]
        + [pltpu.VMEM((SC_CHUNK,), I32)] * TOP_K,
        name="dispatch",
    )
    def run(dest_hbm, src_hbm, out_hbm, rows_v, *idx_v):
        worker = lax.axis_index("s") * n_cores + lax.axis_index("c")

        @pl.loop(0, per_worker // SC_CHUNK)
        def _(j):
            base = worker * per_worker + j * SC_CHUNK
            pltpu.sync_copy(src_hbm.at[pl.ds(base, SC_CHUNK)], rows_v)
            for k in range(TOP_K):
                pltpu.sync_copy(dest_hbm.at[pl.ds(k * n + base, SC_CHUNK)], idx_v[k])
            for k in range(TOP_K):
                pltpu.sync_copy(rows_v, out_hbm.at[idx_v[k]])

    return run(dest_flat, src3)


def _sc_combine(dest_flat, ys3, n):
    n_cores, n_workers = _sc_workers()
    per_worker = n // n_workers
    assert per_worker % SC_CHUNK == 0
    mesh = plsc.VectorSubcoreMesh(core_axis_name="c", subcore_axis_name="s")

    @functools.partial(
        pl.kernel, mesh=mesh,
        out_type=jax.ShapeDtypeStruct((TOP_K * n,) + ys3.shape[1:], ys3.dtype),
        scratch_types=[pltpu.VMEM((SC_CHUNK,) + ys3.shape[1:], ys3.dtype), pltpu.VMEM((SC_CHUNK,), I32)],
        name="combine",
    )
    def run(dest_hbm, ys_hbm, out_hbm, rows_v, idx_v):
        worker = lax.axis_index("s") * n_cores + lax.axis_index("c")

        @pl.loop(0, per_worker // SC_CHUNK)
        def _(j):
            base = worker * per_worker + j * SC_CHUNK
            for k in range(TOP_K):
                pltpu.sync_copy(dest_hbm.at[pl.ds(k * n + base, SC_CHUNK)], idx_v)
                pltpu.sync_copy(ys_hbm.at[idx_v], rows_v)
                pltpu.sync_copy(rows_v, out_hbm.at[pl.ds(k * n + base, SC_CHUNK)])

    return run(dest_flat, ys3)


def _chunk_metadata(counts, n_rows):
    ch = EXPERT_CHUNK
    n_ch = (counts + ch - 1) // ch
    cum = jnp.cumsum(n_ch)
    total = cum[-1]
    max_chunks = n_rows // ch + N_EXPERTS
    i = jnp.arange(max_chunks, dtype=I32)
    e = jnp.sum((cum[None, :] <= i[:, None]).astype(I32), axis=1)
    e_last = jnp.max(jnp.where(counts > 0, jnp.arange(N_EXPERTS, dtype=I32), 0))
    exp = jnp.where(i < total, jnp.minimum(e, N_EXPERTS - 1), e_last).astype(I32)
    newexp = jnp.concatenate([jnp.ones((1,), I32), (exp[1:] != exp[:-1]).astype(I32)])
    change_at = jnp.where(newexp == 1, i, max_chunks)
    nxt_change = lax.cummin(jnp.concatenate([change_at[1:], jnp.array([max_chunks], I32)]), reverse=True)
    nexp = jnp.where(nxt_change < max_chunks, exp[jnp.minimum(nxt_change, max_chunks - 1)], -1).astype(I32)
    starts = (cum - n_ch) * ch
    return starts.astype(I32), (exp, newexp, nexp, total.astype(I32).reshape(1))


def _expert_kernel(exp_ref, newexp_ref, nexp_ref, total_ref, xs_hbm, wg_hbm, wu_hbm, wd_hbm, after_hbm, ys_hbm,
                   xbuf, ybuf, wgf, wuf, wdf, wgb, wub, wdb, sem_x, sem_y, sem_w):
    del after_hbm
    ch = EXPERT_CHUNK
    nbuf = EXPERT_RING
    rows = ch * PACK_SUBLANES
    total = total_ref[0]

    def x_copy(i, slot):
        r0 = pl.multiple_of(i * rows, rows)
        return pltpu.make_async_copy(xs_hbm.at[pl.ds(r0, rows)], xbuf.at[slot], sem_x.at[slot])

    def y_copy(i, slot):
        r0 = pl.multiple_of(i * rows, rows)
        return pltpu.make_async_copy(ybuf.at[slot], ys_hbm.at[pl.ds(r0, rows)], sem_y.at[slot])

    def weight_copies(e):
        return (pltpu.make_async_copy(wg_hbm.at[e], wgf, sem_w.at[0]),
                pltpu.make_async_copy(wu_hbm.at[e], wuf, sem_w.at[1]),
                pltpu.make_async_copy(wd_hbm.at[e], wdf, sem_w.at[2]))

    for cp in weight_copies(exp_ref[0]):
        cp.start()
    for b in range(nbuf - 1):
        @pl.when(b < total)
        def _(b=b):
            x_copy(b, b).start()

    def chunk(i, carry):
        slot = lax.rem(i, nbuf)
        x_copy(i, slot).wait()
        ahead = i + (nbuf - 1)

        @pl.when(ahead < total)
        def _():
            x_copy(ahead, lax.rem(ahead, nbuf)).start()

        @pl.when(newexp_ref[i] == 1)
        def _():
            for cp in weight_copies(exp_ref[i]):
                cp.wait()
            wgb[...] = wgf[...].astype(BF16)
            wub[...] = wuf[...].astype(BF16)
            wdb[...] = wdf[...].astype(BF16)

            @pl.when(nexp_ref[i] >= 0)
            def _():
                for cp in weight_copies(nexp_ref[i]):
                    cp.start()

        @pl.when(i >= nbuf)
        def _():
            y_copy(i, slot).wait()

        xr = xbuf.at[slot]
        yr = ybuf.at[slot]
        x = _load_packed(lambda sl: xr[sl, :], ch).astype(BF16)
        g = jnp.dot(x, wgb[...], preferred_element_type=F32)
        u = jnp.dot(x, wub[...], preferred_element_type=F32)
        y = jnp.dot((g * jax.nn.sigmoid(g) * u).astype(BF16), wdb[...], preferred_element_type=F32)
        _store_packed(yr, y)
        y_copy(i, slot).start()
        return carry

    lax.fori_loop(0, total, chunk, 0)

    for b in range(nbuf):
        @pl.when(b < total)
        def _(b=b):
            y_copy(0, b).wait()


def _experts(meta, xs, w_gate, w_up, w_down, after):
    e, d, f = w_gate.shape
    hbm = pl.BlockSpec(memory_space=pl.ANY)
    rows = EXPERT_CHUNK * PACK_SUBLANES
    grid_spec = pltpu.PrefetchScalarGridSpec(
        num_scalar_prefetch=len(meta), grid=(1,),
        in_specs=[hbm, hbm, hbm, hbm, hbm], out_specs=hbm,
        scratch_shapes=[pltpu.VMEM((EXPERT_RING, rows, LANES), U32), pltpu.VMEM((EXPERT_RING, rows, LANES), U32),
                        pltpu.VMEM((d, f), F32), pltpu.VMEM((d, f), F32), pltpu.VMEM((f, d), F32),
                        pltpu.VMEM((d, f), BF16), pltpu.VMEM((d, f), BF16), pltpu.VMEM((f, d), BF16),
                        pltpu.SemaphoreType.DMA((EXPERT_RING,)), pltpu.SemaphoreType.DMA((EXPERT_RING,)),
                        pltpu.SemaphoreType.DMA((3,))])
    return pl.pallas_call(
        _expert_kernel,
        grid_spec=grid_spec,
        out_shape=jax.ShapeDtypeStruct(xs.shape, xs.dtype),
        compiler_params=_params("arbitrary"),
        name="experts",
    )(*meta, xs, w_gate, w_up, w_down, after)


def _final_kernel(base_ref, yt_ref, wt_ref, mod_ref, g_ref, *rest, d):
    o_ref = rest[-1]
    t = base_ref.shape[0]
    w = wt_ref[...]
    routed = w[:, 0:1] * _load_packed(lambda sl: yt_ref[0, sl, :], t)
    for k in range(1, TOP_K):
        routed = routed + w[:, k:k + 1] * _load_packed(lambda sl, k=k: yt_ref[k, sl, :], t)
    g2 = mod_ref[0][:, 5 * d:6 * d]
    o_ref[...] = _rms(base_ref[...] + g2 * routed, g_ref[...])


def _final(base, ytok3, wts_t, mod3, tiles_per_batch, final_g, part, n_parts, prev_out):
    n, d = base.shape
    t = TOKEN_TILE
    tiles = n // t // n_parts
    off = part * tiles
    kern = functools.partial(_final_kernel, d=d)
    tok = pl.BlockSpec((t, d), lambda i: (off + i, 0))
    in_specs = [tok,
                pl.BlockSpec((TOP_K, t * PACK_SUBLANES, LANES), lambda i: (0, i, 0)),
                pl.BlockSpec((t, TOP_K), lambda i: (off + i, 0)),
                pl.BlockSpec((1, 1, mod3.shape[2]), lambda i: ((off + i) // tiles_per_batch, 0, 0)),
                _full(final_g)]
    args = [base, ytok3, wts_t, mod3, final_g]
    aliases = {}
    if prev_out is not None:
        in_specs.append(pl.BlockSpec(memory_space=pl.ANY))
        args.append(prev_out)
        aliases = {len(args) - 1: 0}
    return pl.pallas_call(
        kern,
        grid=(tiles,),
        in_specs=in_specs,
        out_specs=tok,
        out_shape=jax.ShapeDtypeStruct((n, d), F32),
        input_output_aliases=aliases,
        compiler_params=_params("arbitrary"),
        name="final",
    )(*args)


def kernel(x, c, ctx, c_ctx, w_mod, b_mod, norm1_g, norm2_g, w_in, ssm_lam_re, ssm_lam_im, ssm_log_dt, ssm_b_re, ssm_b_im, ssm_c_re, ssm_c_im, ssm_d, w_glu, w_ssm_out, conv_w, w_conv_out, w_o, w_router, router_bias, w_gate, w_up, w_down, ws_gate, ws_up, ws_down, final_g):
    b, l, d = x.shape
    n = b * l
    sw = ssm_d.shape[1]
    assert w_mod.shape[0] == 1, "single layer"
    assert b == SUBLANES and l % TOKEN_TILE == 0 and l % S5_CHUNK == 0 and ctx.shape[1] % S5_CHUNK == 0 and TIME_TILE % GRID_W == 0

    mod_rows = 2 * SUBLANES
    c_all = jnp.concatenate([c, c_ctx[None, :], jnp.zeros((mod_rows - b - 1, d), F32)], axis=0)
    mod = _modulation(c_all, w_mod[0], b_mod[0])
    mod3 = mod.reshape(mod_rows, 1, mod.shape[1])

    w_in_b = w_in[0].astype(BF16)
    g1n = norm1_g[0].reshape(1, d)
    u_lat, sga, gbt = _in_proj(x, mod3, g1n, w_in_b, conv_w[0], w_conv_out[0].astype(BF16))
    u_ctx = _ctx_proj(ctx, mod3, b, g1n, w_in_b[:, :sw])

    ops = _blocked_operators(ssm_lam_re[0], ssm_lam_im[0], ssm_log_dt[0], ssm_b_re[0], ssm_b_im[0],
                             ssm_c_re[0], ssm_c_im[0])
    y = _s5_blocked(u_ctx, u_lat, ssm_d[0].reshape(1, sw), ops, b, S5_CHUNK)

    xl = _mixer_out(y, sga, gbt, x, mod3, w_glu[0].astype(BF16), w_ssm_out[0].astype(BF16),
                    w_o[0].astype(BF16))

    tiles_per_batch = l // TOKEN_TILE
    xl2 = xl.reshape(n, d)
    g2n = norm2_g[0].reshape(1, d)
    h2, eidx, wts, rank, cnt = _route(xl2, mod3, tiles_per_batch, g2n, w_router[0].T,
                                      router_bias[0].reshape(N_EXPERTS, 1))

    counts = cnt[:, 0].astype(I32)
    n_rows = n * TOP_K
    starts, meta = _chunk_metadata(counts, n_rows)
    buf_rows = n_rows + N_EXPERTS * EXPERT_CHUNK
    dest = _dest_rows(eidx, rank, jnp.broadcast_to(starts[:, None], (N_EXPERTS, LANES)))

    xs = _sc_dispatch(dest.reshape(TOP_K * n), h2.reshape(n, PACK_SUBLANES, LANES), buf_rows)
    base = _shared(xl2, mod3, tiles_per_batch, g2n, ws_gate[0].astype(BF16), ws_up[0].astype(BF16),
                   ws_down[0].astype(BF16))
    ys = _experts(meta, xs.reshape(buf_rows * PACK_SUBLANES, LANES), w_gate[0], w_up[0], w_down[0], base)
    ys3 = ys.reshape(buf_rows, PACK_SUBLANES, LANES)

    n_part = n // COMBINE_PARTS
    wts_t = wts.T
    out = None
    for part in range(COMBINE_PARTS):
        dest_part = dest[:, part * n_part:(part + 1) * n_part].reshape(TOP_K * n_part)
        ytok = _sc_combine(dest_part, ys3, n_part)
        out = _final(base, ytok.reshape(TOP_K, n_part * PACK_SUBLANES, LANES), wts_t, mod3, tiles_per_batch,
                     final_g.reshape(1, d), part, COMBINE_PARTS, out)
    return out.reshape(b, l, d)
```

```python
import functools

import jax
import jax.numpy as jnp
from jax import lax
from jax.experimental import pallas as pl
from jax.experimental.pallas import tpu as pltpu
from jax.experimental.pallas import tpu_sc as plsc

F32 = jnp.float32
BF16 = jnp.bfloat16
I32 = jnp.int32
U32 = jnp.uint32

EPS = 1e-6
GRID_W = 64
SSM_GROUP = 16
N_EXPERTS = 256
TOP_K = 8
N_EXPERT_GROUPS = 8
EXPERTS_PER_GROUP = N_EXPERTS // N_EXPERT_GROUPS
TOP_K_GROUPS = 4
ROUTE_SCALE = 2.5

SUBLANES = 8
LANES = 128
VMEM_LIMIT_BYTES = 48 * 1024 * 1024

TIME_TILE = 64
TOKEN_TILE = 256
TIME_BLOCK = 16
S5_CHUNK = 256
OPERATOR_GROUPS = 4
EXPERT_CHUNK = 256
EXPERT_RING = 4
SC_CHUNK = 128
COMBINE_PARTS = 4
PACK_SUBLANES = 4


def _dot(a, b):
    return jnp.dot(a.astype(BF16), b.astype(BF16), preferred_element_type=F32)


def _rms(xf, g):
    return xf * lax.rsqrt(jnp.mean(xf * xf, axis=-1, keepdims=True) + EPS) * g


def _params(*sem):
    return pltpu.CompilerParams(dimension_semantics=sem, vmem_limit_bytes=VMEM_LIMIT_BYTES)


def _full(a):
    return pl.BlockSpec(a.shape, lambda *_: (0,) * a.ndim)


def _pack_rows(v):
    half = v.shape[1] // 2
    lo = lax.bitcast_convert_type(v[:, :half].astype(BF16).astype(F32), U32) >> 16
    hi = lax.bitcast_convert_type(v[:, half:].astype(BF16).astype(F32), U32) & jnp.uint32(0xFFFF0000)
    return lo | hi


def _unpack_lo(w):
    return lax.bitcast_convert_type(w << 16, F32)


def _unpack_hi(w):
    return lax.bitcast_convert_type(w & jnp.uint32(0xFFFF0000), F32)


def _store_packed(ref, v):
    t = v.shape[0]
    w = _pack_rows(v)
    for c in range(PACK_SUBLANES):
        ref[pl.ds(c, t, stride=PACK_SUBLANES), :] = w[:, c * LANES:(c + 1) * LANES]


def _load_packed(load, t):
    ws = [load(pl.ds(c, t, stride=PACK_SUBLANES)) for c in range(PACK_SUBLANES)]
    return jnp.concatenate([_unpack_lo(w) for w in ws] + [_unpack_hi(w) for w in ws], axis=1)


def _mod_kernel(c_ref, w_ref, b_ref, o_ref):
    c = c_ref[...]
    o_ref[...] = _dot(c * jax.nn.sigmoid(c), w_ref[...]) + b_ref[...]


def _modulation(c_all, w_mod, b_mod):
    rows, d = c_all.shape
    cols = w_mod.shape[1]
    blk = 1536
    return pl.pallas_call(
        _mod_kernel,
        grid=(cols // blk,),
        in_specs=[pl.BlockSpec((rows, d), lambda j: (0, 0)),
                  pl.BlockSpec((d, blk), lambda j: (0, j)),
                  pl.BlockSpec((1, blk), lambda j: (0, j))],
        out_specs=pl.BlockSpec((rows, blk), lambda j: (0, j)),
        out_shape=jax.ShapeDtypeStruct((rows, cols), F32),
        compiler_params=_params("arbitrary"),
        name="mod",
    )(c_all, w_mod, b_mod.reshape(1, cols))


def _per_row(m3, lo, hi, tt):
    nb = m3.shape[0]
    return jnp.broadcast_to(m3[:, :, lo:hi], (nb, tt, hi - lo)).reshape(nb * tt, hi - lo)


def _to_time_major(val, nb, tt):
    c = val.shape[1]
    return pltpu.einshape("btc->tbc", val.reshape(nb, tt, c)).reshape(nb * tt, c)


def _to_batch_major(val, nb, tt):
    c = val.shape[1]
    return pltpu.einshape("tbc->btc", val.reshape(tt, nb, c)).reshape(nb * tt, c)


def _in_proj_kernel(x_ref, mod_ref, g_ref, w_ref, cw_ref, wco_ref, u_ref, sga_ref, gbt_ref, *, d, sw):
    nb, tt, _ = x_ref.shape
    rows = nb * tt
    m3 = mod_ref[...]
    x = x_ref[...].reshape(rows, d)
    h = _rms(x, g_ref[...]) * (1.0 + _per_row(m3, d, 2 * d, tt)) + _per_row(m3, 0, d, tt)
    hb = h.astype(BF16)
    u_ref[...] = _to_time_major(jnp.dot(hb, w_ref[:, 0:sw], preferred_element_type=F32), nb, tt)
    cb = jnp.dot(hb, w_ref[:, sw:2 * sw], preferred_element_type=F32)
    cc = jnp.dot(hb, w_ref[:, 2 * sw:3 * sw], preferred_element_type=F32)
    cv = jnp.dot(hb, w_ref[:, 3 * sw:4 * sw], preferred_element_type=F32)
    ccv = cc * cv
    col = lax.broadcasted_iota(I32, ccv.shape, 0) % GRID_W
    prev = jnp.where(col == 0, 0.0, pltpu.roll(ccv, 1, axis=0))
    nxt = jnp.where(col == GRID_W - 1, 0.0, pltpu.roll(ccv, rows - 1, axis=0))
    cw = cw_ref[...]
    conv = prev * cw[0:1, :] + ccv * cw[1:2, :] + nxt * cw[2:3, :]
    y_conv = _dot(cb * conv, wco_ref[...])
    ga = jnp.dot(hb, w_ref[:, 4 * sw:4 * sw + d], preferred_element_type=F32)
    gb = jnp.dot(hb, w_ref[:, 4 * sw + d:4 * sw + 2 * d], preferred_element_type=F32)
    sga_ref[...] = jax.nn.sigmoid(ga).reshape(nb, tt, d)
    gbt_ref[...] = (jax.nn.sigmoid(gb) * y_conv).reshape(nb, tt, d)


def _in_proj(x, mod3, g1n, w_in_b, conv_w, w_conv_out_b):
    b, l, d = x.shape
    sw = conv_w.shape[1]
    tt = TIME_TILE
    kern = functools.partial(_in_proj_kernel, d=d, sw=sw)
    tok = pl.BlockSpec((b, tt, d), lambda j: (0, j, 0))
    return pl.pallas_call(
        kern,
        grid=(l // tt,),
        in_specs=[tok,
                  pl.BlockSpec((b, 1, mod3.shape[2]), lambda j: (0, 0, 0)),
                  _full(g1n), _full(w_in_b), _full(conv_w), _full(w_conv_out_b)],
        out_specs=[pl.BlockSpec((tt * b, sw), lambda j: (j, 0)), tok, tok],
        out_shape=[jax.ShapeDtypeStruct((l * b, sw), F32),
                   jax.ShapeDtypeStruct((b, l, d), F32),
                   jax.ShapeDtypeStruct((b, l, d), F32)],
        compiler_params=_params("arbitrary"),
        name="in_proj",
    )(x, mod3, g1n, w_in_b, conv_w, w_conv_out_b)


def _ctx_proj_kernel(x_ref, mod_ref, g_ref, w_ref, u_ref, *, d):
    nb, tt, _ = x_ref.shape
    m = mod_ref[0]
    x = x_ref[...].reshape(nb * tt, d)
    h = _rms(x, g_ref[...]) * (1.0 + m[:, d:2 * d]) + m[:, 0:d]
    u_ref[...] = _to_time_major(_dot(h, w_ref[...]), nb, tt)


def _ctx_proj(ctx, mod3, ctx_row, g1n, w_u_b):
    b, lc, d = ctx.shape
    sw = w_u_b.shape[1]
    tt = TIME_TILE
    kern = functools.partial(_ctx_proj_kernel, d=d)
    return pl.pallas_call(
        kern,
        grid=(lc // tt,),
        in_specs=[pl.BlockSpec((b, tt, d), lambda j: (0, j, 0)),
                  pl.BlockSpec((1, 1, mod3.shape[2]), lambda j: (ctx_row, 0, 0)),
                  _full(g1n), _full(w_u_b)],
        out_specs=pl.BlockSpec((tt * b, sw), lambda j: (j, 0)),
        out_shape=jax.ShapeDtypeStruct((lc * b, sw), F32),
        compiler_params=_params("arbitrary"),
        name="ctx_proj",
    )(ctx, mod3, g1n, w_u_b)


_NT = (((1,), (1,)), ((), ()))


def _dot3_nt(a, b):
    a_hi, a_lo = _split_bf16(a)
    b_hi, b_lo = _split_bf16(b)
    return (lax.dot_general(a_hi, b_hi, _NT, preferred_element_type=F32)
            + lax.dot_general(a_hi, b_lo, _NT, preferred_element_type=F32)
            + lax.dot_general(a_lo, b_hi, _NT, preferred_element_type=F32))


def _operator_kernel(*refs):
    n_in = 6
    for gi in range(refs[0].shape[1]):
        _operator_group(*[r.at[:, pl.ds(gi, 1)] for r in refs[:n_in]], *[o.at[pl.ds(gi, 1)] for o in refs[n_in:]])


def _operator_group(akr_ref, aki_ref, bbr_ref, bbi_ref, cr_ref, ci_ref, tfb_ref, bst_ref, cst_ref):
    tb = TIME_BLOCK
    q, p = bbr_ref.shape[2], bbr_ref.shape[3]
    w = tb * q

    def times_ak(d, ks, xr, xi):
        re, im = [], []
        for k in ks:
            ar, ai = akr_ref[d, 0, k:k + 1, :], aki_ref[d, 0, k:k + 1, :]
            re.append(ar * xr - ai * xi)
            im.append(ar * xi + ai * xr)
        return jnp.concatenate(re, axis=0), jnp.concatenate(im, axis=0)

    col = lax.broadcasted_iota(I32, (p, w), 1)
    row = lax.broadcasted_iota(I32, (p, w), 0)

    def place(quarter):
        return (col == row + quarter * p).astype(BF16)

    inj, ca = [], []
    for d in range(2):
        inj += list(times_ak(d, range(tb - 1, -1, -1) if d == 0 else range(tb), bbr_ref[d, 0], bbi_ref[d, 0]))
        re, im = times_ak(d, range(1, tb + 1) if d == 0 else range(tb, 0, -1), cr_ref[d, 0], ci_ref[d, 0])
        ca += [re, -im]
    bst = jnp.zeros((w, w), F32)
    cst = jnp.zeros((w, w), F32)
    for quarter in range(4):
        pm = place(quarter)
        bst = bst + jnp.dot(inj[quarter].astype(BF16), pm, preferred_element_type=F32)
        cst = cst + lax.dot_general(pm, ca[quarter].astype(BF16), (((0,), (1,)), ((), ())),
                                    preferred_element_type=F32)
    bst_ref[0] = bst.astype(BF16)
    cst_ref[0] = cst.astype(BF16)

    sub = lax.broadcasted_iota(I32, (q, w), 0)
    lane = lax.broadcasted_iota(I32, (q, w), 1)
    lag_rows = []
    for d in range(2):
        abr, abi = times_ak(d, range(tb), bbr_ref[d, 0], bbi_ref[d, 0])
        m = _dot3_nt(abr, cr_ref[d, 0]) - _dot3_nt(abi, ci_ref[d, 0])
        acc = jnp.zeros((q, w), F32)
        for k in range(tb):
            blk = tb - 1 - k if d == 1 else k
            put = ((lane - blk * q) == sub).astype(BF16)
            acc = acc + jnp.dot(m[k * q:(k + 1) * q, :].astype(BF16), put, preferred_element_type=F32)
        lag_rows.append(acc)
    rows_f, rows_b = lag_rows

    for s in range(tb):
        right = s * q
        left = (tb - 1 - s) * q
        fwd = jnp.where(lane >= right, pltpu.roll(rows_f, right, axis=1) if right else rows_f, 0.0)
        bwd = jnp.where(lane < w - left, pltpu.roll(rows_b, w - left, axis=1) if left else rows_b, 0.0)
        tfb_ref[0, s * q:(s + 1) * q, :] = (fwd + bwd).astype(BF16)


def _blocked_operators(lam_re, lam_im, log_dt, b_re, b_im, c_re, c_im):
    tb = TIME_BLOCK
    dt = jnp.exp(log_dt)[..., None]
    k = jnp.arange(tb + 1, dtype=F32)[None, None, :, None]
    mag = jnp.exp(lam_re * dt)[:, :, None, :] ** k
    ang = (lam_im * dt)[:, :, None, :] * k
    ak_re = mag * jnp.cos(ang)
    ak_im = mag * jnp.sin(ang)
    a_re, a_im = ak_re[:, :, 1], ak_im[:, :, 1]
    den = lam_re * lam_re + lam_im * lam_im
    k_re = ((a_re - 1.0) * lam_re + a_im * lam_im) / den
    k_im = (a_im * lam_re - (a_re - 1.0) * lam_im) / den
    bt_re = jnp.swapaxes(b_re, 2, 3)
    bt_im = jnp.swapaxes(b_im, 2, 3)
    bbt_re = k_re[:, :, None, :] * bt_re - k_im[:, :, None, :] * bt_im
    bbt_im = k_re[:, :, None, :] * bt_im + k_im[:, :, None, :] * bt_re
    nd, g, q, p = bbt_re.shape
    w = tb * q

    gb = OPERATOR_GROUPS
    per_group = lambda a: pl.BlockSpec((nd, gb) + a.shape[2:], lambda i: (0, i, 0, 0))
    out = pl.BlockSpec((gb, w, w), lambda i: (i, 0, 0))
    ins = (ak_re, ak_im, bbt_re, bbt_im, c_re, c_im)
    tfb, bst, cst = pl.pallas_call(
        _operator_kernel,
        grid=(g // gb,),
        in_specs=[per_group(a) for a in ins],
        out_specs=[out, out, out],
        out_shape=[jax.ShapeDtypeStruct((g, w, w), BF16)] * 3,
        compiler_params=_params("arbitrary"),
        name="s5_operators",
    )(*ins)
    ax = jnp.concatenate([ak_re[:, :, tb], ak_re[:, :, tb]], axis=-1).reshape(nd, 1, g * 2 * p)
    ay = jnp.concatenate([-ak_im[:, :, tb], ak_im[:, :, tb]], axis=-1).reshape(nd, 1, g * 2 * p)
    return tfb, bst, cst, ax, ay


def _lane_chunk(shape):
    return lax.broadcasted_iota(I32, shape, 1) // SSM_GROUP


def _regroup_kernel(uc_ref, ul_ref, bst_ref, xg_ref, e_ref, *, n_ctx, nb, groups, q):
    j = pl.program_id(0)
    rows_in = uc_ref.shape[0]
    nblk = rows_in // (TIME_BLOCK * nb)
    rows = nblk * nb
    u = jnp.where(j < n_ctx, uc_ref[...], ul_ref[...])
    u3 = u.reshape(nblk, TIME_BLOCK * nb, u.shape[1])
    xs = [u3[:, s * nb:(s + 1) * nb, :].reshape(rows, u.shape[1]) for s in range(TIME_BLOCK)]
    per_tile = LANES // q
    chunk = _lane_chunk((rows, LANES))
    for g in range(groups):
        jt, qq = divmod(g, per_tile)
        for h in range(TIME_BLOCK // per_tile):
            acc = jnp.zeros((rows, LANES), F32)
            for s in range(h * per_tile, (h + 1) * per_tile):
                pos = s % per_tile
                a = xs[s][:, jt * LANES:(jt + 1) * LANES]
                shift = ((pos - qq) % per_tile) * q
                r = pltpu.roll(a, shift, axis=1) if shift else a
                acc = jnp.where(chunk == pos, r, acc)
            xg_ref[:, (2 * g + h) * LANES:(2 * g + h + 1) * LANES] = acc.astype(BF16)
    gw = TIME_BLOCK * q
    for g in range(groups):
        e = jnp.dot(xg_ref[:, g * gw:(g + 1) * gw], bst_ref[g], preferred_element_type=F32)
        half = e.shape[1] // 2
        e_ref[0, :, g * half:(g + 1) * half] = e[:, :half]
        e_ref[1, :, g * half:(g + 1) * half] = e[:, half:]


def _s5_regroup(u_ctx, u_lat, bst, nb, tc):
    rows_c, sw = u_ctx.shape
    rows_l = u_lat.shape[0]
    rc = tc * nb
    n_ctx, n_lat = rows_c // rc, rows_l // rc
    groups, gw, p4 = bst.shape
    q = gw // TIME_BLOCK
    rows_out = (tc // TIME_BLOCK) * nb
    total_rows = (n_ctx + n_lat) * rows_out
    kern = functools.partial(_regroup_kernel, n_ctx=n_ctx, nb=nb, groups=groups, q=q)
    return pl.pallas_call(
        kern,
        grid=(n_ctx + n_lat,),
        in_specs=[pl.BlockSpec((rc, sw), lambda j: (jnp.minimum(j, n_ctx - 1), 0)),
                  pl.BlockSpec((rc, sw), lambda j: (jnp.maximum(j - n_ctx, 0), 0)),
                  pl.BlockSpec(bst.shape, lambda j: (0, 0, 0), pipeline_mode=pl.Buffered(1))],
        out_specs=[pl.BlockSpec((rows_out, groups * gw), lambda j: (j, 0)),
                   pl.BlockSpec((2, rows_out, groups * p4 // 2), lambda j: (0, j, 0))],
        out_shape=[jax.ShapeDtypeStruct((total_rows, groups * gw), BF16),
                   jax.ShapeDtypeStruct((2, total_rows, groups * p4 // 2), F32)],
        compiler_params=_params("arbitrary"),
        name="s5_regroup",
    )(u_ctx, u_lat, bst)


def _carry_kernel(e_ref, ax_ref, ay_ref, s_ref, st_ref, *, nb):
    dirn = pl.program_id(0)
    j = pl.program_id(1)
    rows = e_ref.shape[1]
    nblk = rows // nb
    lanes = e_ref.shape[2]

    @pl.when(j == 0)
    def _():
        st_ref[...] = jnp.zeros_like(st_ref)

    ax = jnp.broadcast_to(ax_ref[0], (nb, lanes))
    ay = jnp.broadcast_to(ay_ref[0], (nb, lanes))

    def body(i, s):
        blk = jnp.where(dirn == 0, i, nblk - 1 - i)
        r0 = pl.multiple_of(blk * nb, nb)
        s_ref[0, pl.ds(r0, nb), :] = s.astype(s_ref.dtype)
        swapped = jnp.concatenate([pltpu.roll(s[:, t * LANES:(t + 1) * LANES], LANES // 2, axis=1)
                                   for t in range(lanes // LANES)], axis=1)
        return ax * s + ay * swapped + e_ref[0, pl.ds(r0, nb), :]

    st_ref[...] = lax.fori_loop(0, nblk, body, st_ref[...])


def _s5_carry(e, ax, ay, nb, n_ctx_chunks, rows_per_chunk):
    nd, total_rows, lanes = e.shape
    n_chunks = total_rows // rows_per_chunk
    n_lat = n_chunks - n_ctx_chunks

    def idx(d, j):
        jc = jnp.minimum(j, n_ctx_chunks - 1)
        jl = jnp.maximum(j - n_ctx_chunks, 0)
        fwd = j
        bwd = jnp.where(j < n_ctx_chunks, n_ctx_chunks - 1 - jc, n_chunks - 1 - jl)
        return (d, jnp.where(d == 0, fwd, bwd), 0)

    kern = functools.partial(_carry_kernel, nb=nb)
    return pl.pallas_call(
        kern,
        grid=(nd, n_chunks),
        in_specs=[pl.BlockSpec((1, rows_per_chunk, lanes), idx),
                  pl.BlockSpec((1, 1, lanes), lambda d, j: (d, 0, 0)),
                  pl.BlockSpec((1, 1, lanes), lambda d, j: (d, 0, 0))],
        out_specs=pl.BlockSpec((1, rows_per_chunk, lanes), idx),
        out_shape=jax.ShapeDtypeStruct(e.shape, F32),
        scratch_shapes=[pltpu.VMEM((nb, lanes), F32)],
        compiler_params=_params("arbitrary", "arbitrary"),
        name="s5_carry",
    )(e, ax, ay)


def _output_kernel(xg_ref, s_ref, u_ref, d_ref, tfb_ref, cst_ref, y_ref, yg_ref, *, nb, groups, q):
    rows = xg_ref.shape[0]
    nblk = rows // nb
    gw = TIME_BLOCK * q
    half = s_ref.shape[2] // groups
    for g in range(groups):
        sg = jnp.concatenate([s_ref[0, :, g * half:(g + 1) * half], s_ref[1, :, g * half:(g + 1) * half]], axis=1)
        yg_ref[:, g * gw:(g + 1) * gw] = (
            jnp.dot(xg_ref[:, g * gw:(g + 1) * gw], tfb_ref[g], preferred_element_type=F32)
            + jnp.dot(sg.astype(BF16), cst_ref[g], preferred_element_type=F32))
    per_tile = LANES // q
    chunk = _lane_chunk((rows, LANES))
    sw = groups * q
    steps = []
    for s in range(TIME_BLOCK):
        h, pos = divmod(s, per_tile)
        tiles = []
        for jt in range(sw // LANES):
            acc = jnp.zeros((rows, LANES), F32)
            for qq in range(per_tile):
                g = jt * per_tile + qq
                a = yg_ref[:, (2 * g + h) * LANES:(2 * g + h + 1) * LANES]
                shift = ((qq - pos) % per_tile) * q
                r = pltpu.roll(a, shift, axis=1) if shift else a
                acc = jnp.where(chunk == qq, r, acc)
            tiles.append(acc)
        steps.append(jnp.concatenate(tiles, axis=1).reshape(nblk, nb, sw))
    y = jnp.concatenate(steps, axis=1).reshape(nblk * TIME_BLOCK * nb, sw)
    y_ref[...] = y + d_ref[...] * u_ref[...]


def _s5_outputs(xg, s, u_lat, ssm_d, tfb, cst, nb, tc, n_ctx_chunks):
    rows_l, sw = u_lat.shape
    rc = tc * nb
    n_lat = rows_l // rc
    groups, gw, _ = tfb.shape
    q = gw // TIME_BLOCK
    rows_out = (tc // TIME_BLOCK) * nb
    kern = functools.partial(_output_kernel, nb=nb, groups=groups, q=q)
    return pl.pallas_call(
        kern,
        grid=(n_lat,),
        in_specs=[pl.BlockSpec((rows_out, groups * gw), lambda j: (j + n_ctx_chunks, 0)),
                  pl.BlockSpec((2, rows_out, s.shape[2]), lambda j: (0, j + n_ctx_chunks, 0)),
                  pl.BlockSpec((rc, sw), lambda j: (j, 0)),
                  pl.BlockSpec((1, sw), lambda j: (0, 0)),
                  pl.BlockSpec(tfb.shape, lambda j: (0, 0, 0), pipeline_mode=pl.Buffered(1)),
                  pl.BlockSpec(cst.shape, lambda j: (0, 0, 0), pipeline_mode=pl.Buffered(1))],
        out_specs=pl.BlockSpec((rc, sw), lambda j: (j, 0)),
        out_shape=jax.ShapeDtypeStruct((rows_l, sw), F32),
        scratch_shapes=[pltpu.VMEM((rows_out, groups * gw), F32)],
        compiler_params=_params("arbitrary"),
        name="s5_output",
    )(xg, s, u_lat, ssm_d, tfb, cst)


def _s5_blocked(u_ctx, u_lat, ssm_d, ops, nb, tc):
    tfb, bst, cst, ax, ay = ops
    n_ctx_chunks = u_ctx.shape[0] // (tc * nb)
    xg, e = _s5_regroup(u_ctx, u_lat, bst, nb, tc)
    s = _s5_carry(e, ax, ay, nb, n_ctx_chunks, (tc // TIME_BLOCK) * nb)
    return _s5_outputs(xg, s, u_lat, ssm_d, tfb, cst, nb, tc, n_ctx_chunks)


def _mixer_kernel(y_ref, sga_ref, gbt_ref, x_ref, mod_ref, wglu_ref, wso_ref, wo_ref, o_ref, *, d, sw):
    nb, tt, _ = x_ref.shape
    rows = nb * tt
    y = _to_batch_major(y_ref[...], nb, tt)
    v = _dot(jax.nn.gelu(y), wglu_ref[...])
    ys = v[:, 0:sw] * jax.nn.sigmoid(v[:, sw:2 * sw])
    y_a = _dot(ys, wso_ref[...])
    merged = sga_ref[...].reshape(rows, d) * y_a + gbt_ref[...].reshape(rows, d)
    o = _dot(merged, wo_ref[...])
    g1 = _per_row(mod_ref[...], 2 * d, 3 * d, tt)
    o_ref[...] = (x_ref[...].reshape(rows, d) + g1 * o).reshape(nb, tt, d)


def _mixer_out(y, sga, gbt, x, mod3, w_glu_b, w_ssm_out_b, w_o_b):
    b, l, d = x.shape
    sw = y.shape[1]
    tt = TIME_TILE
    kern = functools.partial(_mixer_kernel, d=d, sw=sw)
    tok = pl.BlockSpec((b, tt, d), lambda j: (0, j, 0))
    return pl.pallas_call(
        kern,
        grid=(l // tt,),
        in_specs=[pl.BlockSpec((tt * b, sw), lambda j: (j, 0)),
                  tok, tok, tok,
                  pl.BlockSpec((b, 1, mod3.shape[2]), lambda j: (0, 0, 0)),
                  _full(w_glu_b), _full(w_ssm_out_b), _full(w_o_b)],
        out_specs=tok,
        out_shape=jax.ShapeDtypeStruct((b, l, d), F32),
        compiler_params=_params("arbitrary"),
        name="mixer_out",
    )(y, sga, gbt, x, mod3, w_glu_b, w_ssm_out_b, w_o_b)


def _split_bf16(a):
    hi = a.astype(BF16)
    lo = (a - hi.astype(F32)).astype(BF16)
    return hi, lo


def _norm2(xl, m, g, d):
    return _rms(xl, g) * (1.0 + m[:, 4 * d:5 * d]) + m[:, 3 * d:4 * d]


def _shared_kernel(xl_ref, mod_ref, g_ref, wsg_ref, wsu_ref, wsd_ref, base_ref, *, d):
    m = mod_ref[0]
    xl = xl_ref[...]
    hb = _norm2(xl, m, g_ref[...], d).astype(BF16)
    sg = jnp.dot(hb, wsg_ref[...], preferred_element_type=F32)
    su = jnp.dot(hb, wsu_ref[...], preferred_element_type=F32)
    shared = _dot(sg * jax.nn.sigmoid(sg) * su, wsd_ref[...])
    base_ref[...] = xl + m[:, 5 * d:6 * d] * shared


def _shared(xl2, mod3, tiles_per_batch, g2n, ws_gate_b, ws_up_b, ws_down_b):
    n, d = xl2.shape
    t = TOKEN_TILE
    tok = pl.BlockSpec((t, d), lambda i: (i, 0))
    return pl.pallas_call(
        functools.partial(_shared_kernel, d=d),
        grid=(n // t,),
        in_specs=[tok,
                  pl.BlockSpec((1, 1, mod3.shape[2]), lambda i: (i // tiles_per_batch, 0, 0)),
                  _full(g2n), _full(ws_gate_b), _full(ws_up_b), _full(ws_down_b)],
        out_specs=tok,
        out_shape=jax.ShapeDtypeStruct((n, d), F32),
        compiler_params=_params("arbitrary"),
        name="shared",
    )(xl2, mod3, g2n, ws_gate_b, ws_up_b, ws_down_b)


def _route_kernel(xl_ref, mod_ref, g_ref, whi_ref, wlo_ref, rb_ref,
                  h2_ref, eidx_ref, wts_ref, rank_ref, cnt_ref, carry_ref, *, d):
    i = pl.program_id(0)
    t = xl_ref.shape[0]
    h2 = _norm2(xl_ref[...], mod_ref[0], g_ref[...], d)
    _store_packed(h2_ref, h2)

    h_hi, h_lo = _split_bf16(h2)
    w_hi, w_lo = whi_ref[...], wlo_ref[...]
    nt = (((1,), (1,)), ((), ()))
    logits = (lax.dot_general(w_hi, h_hi, nt, preferred_element_type=F32)
              + lax.dot_general(w_hi, h_lo, nt, preferred_element_type=F32)
              + lax.dot_general(w_lo, h_hi, nt, preferred_element_type=F32))
    scores = jax.nn.sigmoid(logits)
    choice = scores + rb_ref[...]

    epg = EXPERTS_PER_GROUP
    gi = lax.broadcasted_iota(I32, (epg, t), 0)
    gs = []
    for g in range(N_EXPERT_GROUPS):
        seg = choice[g * epg:(g + 1) * epg, :]
        m1 = jnp.max(seg, axis=0, keepdims=True)
        i1 = jnp.min(jnp.where(seg == m1, gi, epg), axis=0, keepdims=True)
        m2 = jnp.max(jnp.where(gi == i1, -jnp.inf, seg), axis=0, keepdims=True)
        gs.append(m1 + m2)
    masked = []
    for g in range(N_EXPERT_GROUPS):
        beat = jnp.zeros((1, t), I32)
        for g2 in range(N_EXPERT_GROUPS):
            if g2 < g:
                beat = beat + (gs[g2] >= gs[g]).astype(I32)
            elif g2 > g:
                beat = beat + (gs[g2] > gs[g]).astype(I32)
        keep = beat < TOP_K_GROUPS
        masked.append(jnp.where(keep, choice[g * epg:(g + 1) * epg, :], -jnp.inf))
    cur = jnp.concatenate(masked, axis=0)

    ei_all = lax.broadcasted_iota(I32, (N_EXPERTS, t), 0)
    picks, raw = [], []
    onehot = jnp.zeros((N_EXPERTS, t), F32)
    for _ in range(TOP_K):
        mx = jnp.max(cur, axis=0, keepdims=True)
        ei = jnp.min(jnp.where(cur == mx, ei_all, N_EXPERTS), axis=0, keepdims=True)
        hit = ei_all == ei
        raw.append(jnp.sum(jnp.where(hit, scores, 0.0), axis=0, keepdims=True))
        cur = jnp.where(hit, -jnp.inf, cur)
        onehot = jnp.where(hit, 1.0, onehot)
        picks.append(ei)
    tot = raw[0]
    for k in range(1, TOP_K):
        tot = tot + raw[k]

    @pl.when(i == 0)
    def _():
        carry_ref[...] = jnp.zeros_like(carry_ref)

    upper = (lax.broadcasted_iota(I32, (t, t), 0) < lax.broadcasted_iota(I32, (t, t), 1)).astype(BF16)
    before = jnp.dot(onehot.astype(BF16), upper, preferred_element_type=F32) + carry_ref[:, 0:1]
    for k in range(TOP_K):
        eidx_ref[k:k + 1, :] = picks[k]
        wts_ref[k:k + 1, :] = raw[k] / tot * ROUTE_SCALE
        rk = jnp.sum(jnp.where(ei_all == picks[k], before, 0.0), axis=0, keepdims=True)
        rank_ref[k:k + 1, :] = rk.astype(I32)
    carry_ref[...] = carry_ref[...] + jnp.sum(onehot, axis=1, keepdims=True)
    cnt_ref[...] = carry_ref[...]


def _route(xl2, mod3, tiles_per_batch, g2n, w_router_t, router_bias):
    n, d = xl2.shape
    t = TOKEN_TILE
    e = w_router_t.shape[0]
    w_top = lax.bitcast_convert_type(lax.bitcast_convert_type(w_router_t, U32) & jnp.uint32(0xFFFF0000), F32)
    w_hi, w_lo = w_top.astype(BF16), (w_router_t - w_top).astype(BF16)
    kern = functools.partial(_route_kernel, d=d)
    tok = pl.BlockSpec((t, d), lambda i: (i, 0))
    small = pl.BlockSpec((TOP_K, t), lambda i: (0, i))
    return pl.pallas_call(
        kern,
        grid=(n // t,),
        in_specs=[tok,
                  pl.BlockSpec((1, 1, mod3.shape[2]), lambda i: (i // tiles_per_batch, 0, 0)),
                  _full(g2n), _full(w_hi), _full(w_lo), _full(router_bias)],
        out_specs=[pl.BlockSpec((t * PACK_SUBLANES, LANES), lambda i: (i, 0)), small, small, small,
                   pl.BlockSpec((e, LANES), lambda i: (0, 0))],
        out_shape=[jax.ShapeDtypeStruct((n * PACK_SUBLANES, LANES), U32),
                   jax.ShapeDtypeStruct((TOP_K, n), I32),
                   jax.ShapeDtypeStruct((TOP_K, n), F32),
                   jax.ShapeDtypeStruct((TOP_K, n), I32),
                   jax.ShapeDtypeStruct((e, LANES), F32)],
        scratch_shapes=[pltpu.VMEM((e, LANES), F32)],
        compiler_params=_params("arbitrary"),
        name="route",
    )(xl2, mod3, g2n, w_hi, w_lo, router_bias)


def _dest_kernel(eidx_ref, rank_ref, start_ref, o_ref):
    t = eidx_ref.shape[1]
    ei_all = lax.broadcasted_iota(I32, (N_EXPERTS, t), 0)
    st = start_ref[:, 0:1]
    for k in range(TOP_K):
        hit = ei_all == eidx_ref[k:k + 1, :]
        o_ref[k:k + 1, :] = jnp.sum(jnp.where(hit, st, 0), axis=0, keepdims=True) + rank_ref[k:k + 1, :]


def _dest_rows(eidx, rank, starts_b):
    n = eidx.shape[1]
    t = 1024
    small = pl.BlockSpec((TOP_K, t), lambda i: (0, i))
    return pl.pallas_call(
        _dest_kernel,
        grid=(n // t,),
        in_specs=[small, small, _full(starts_b)],
        out_specs=small,
        out_shape=jax.ShapeDtypeStruct((TOP_K, n), I32),
        compiler_params=_params("arbitrary"),
        name="dest",
    )(eidx, rank, starts_b)


def _sc_workers():
    info = plsc.get_sparse_core_info()
    return info.num_cores, info.num_cores * info.num_subcores


def _sc_dispatch(dest_flat, src3, out_rows):
    n = src3.shape[0]
    n_cores, n_workers = _sc_workers()
    per_worker = n // n_workers
    assert per_worker % SC_CHUNK == 0
    mesh = plsc.VectorSubcoreMesh(core_axis_name="c", subcore_axis_name="s")

    @functools.partial(
        pl.kernel, mesh=mesh,
        out_type=jax.ShapeDtypeStruct((out_rows,) + src3.shape[1:], src3.dtype),
        scratch_types=[pltpu.VMEM((SC_CHUNK,) + src3.shape[1:], src3.dtype)]
        + [pltpu.VMEM((SC_CHUNK,), I32)] * TOP_K,
        name="dispatch",
    )
    def run(dest_hbm, src_hbm, out_hbm, rows_v, *idx_v):
        worker = lax.axis_index("s") * n_cores + lax.axis_index("c")

        @pl.loop(0, per_worker // SC_CHUNK)
        def _(j):
            base = worker * per_worker + j * SC_CHUNK
            pltpu.sync_copy(src_hbm.at[pl.ds(base, SC_CHUNK)], rows_v)
            for k in range(TOP_K):
                pltpu.sync_copy(dest_hbm.at[pl.ds(k * n + base, SC_CHUNK)], idx_v[k])
            for k in range(TOP_K):
                pltpu.sync_copy(rows_v, out_hbm.at[idx_v[k]])

    return run(dest_flat, src3)


def _sc_combine(dest_flat, ys3, n):
    n_cores, n_workers = _sc_workers()
    per_worker = n // n_workers
    assert per_worker % SC_CHUNK == 0
    mesh = plsc.VectorSubcoreMesh(core_axis_name="c", subcore_axis_name="s")

    @functools.partial(
        pl.kernel, mesh=mesh,
        out_type=jax.ShapeDtypeStruct((TOP_K * n,) + ys3.shape[1:], ys3.dtype),
        scratch_types=[pltpu.VMEM((SC_CHUNK,) + ys3.shape[1:], ys3.dtype), pltpu.VMEM((SC_CHUNK,), I32)],
        name="combine",
    )
    def run(dest_hbm, ys_hbm, out_hbm, rows_v, idx_v):
        worker = lax.axis_index("s") * n_cores + lax.axis_index("c")

        @pl.loop(0, per_worker // SC_CHUNK)
        def _(j):
            base = worker * per_worker + j * SC_CHUNK
            for k in range(TOP_K):
                pltpu.sync_copy(dest_hbm.at[pl.ds(k * n + base, SC_CHUNK)], idx_v)
                pltpu.sync_copy(ys_hbm.at[idx_v], rows_v)
                pltpu.sync_copy(rows_v, out_hbm.at[pl.ds(k * n + base, SC_CHUNK)])

    return run(dest_flat, ys3)


def _chunk_metadata(counts, n_rows):
    ch = EXPERT_CHUNK
    n_ch = (counts + ch - 1) // ch
    cum = jnp.cumsum(n_ch)
    total = cum[-1]
    max_chunks = n_rows // ch + N_EXPERTS
    i = jnp.arange(max_chunks, dtype=I32)
    e = jnp.sum((cum[None, :] <= i[:, None]).astype(I32), axis=1)
    e_last = jnp.max(jnp.where(counts > 0, jnp.arange(N_EXPERTS, dtype=I32), 0))
    exp = jnp.where(i < total, jnp.minimum(e, N_EXPERTS - 1), e_last).astype(I32)
    newexp = jnp.concatenate([jnp.ones((1,), I32), (exp[1:] != exp[:-1]).astype(I32)])
    change_at = jnp.where(newexp == 1, i, max_chunks)
    nxt_change = lax.cummin(jnp.concatenate([change_at[1:], jnp.array([max_chunks], I32)]), reverse=True)
    nexp = jnp.where(nxt_change < max_chunks, exp[jnp.minimum(nxt_change, max_chunks - 1)], -1).astype(I32)
    starts = (cum - n_ch) * ch
    return starts.astype(I32), (exp, newexp, nexp, total.astype(I32).reshape(1))


def _expert_kernel(exp_ref, newexp_ref, nexp_ref, total_ref, xs_hbm, wg_hbm, wu_hbm, wd_hbm, after_hbm, ys_hbm,
                   xbuf, ybuf, wgf, wuf, wdf, wgb, wub, wdb, sem_x, sem_y, sem_w):
    del after_hbm
    ch = EXPERT_CHUNK
    nbuf = EXPERT_RING
    rows = ch * PACK_SUBLANES
    total = total_ref[0]

    def x_copy(i, slot):
        r0 = pl.multiple_of(i * rows, rows)
        return pltpu.make_async_copy(xs_hbm.at[pl.ds(r0, rows)], xbuf.at[slot], sem_x.at[slot])

    def y_copy(i, slot):
        r0 = pl.multiple_of(i * rows, rows)
        return pltpu.make_async_copy(ybuf.at[slot], ys_hbm.at[pl.ds(r0, rows)], sem_y.at[slot])

    def weight_copies(e):
        return (pltpu.make_async_copy(wg_hbm.at[e], wgf, sem_w.at[0]),
                pltpu.make_async_copy(wu_hbm.at[e], wuf, sem_w.at[1]),
                pltpu.make_async_copy(wd_hbm.at[e], wdf, sem_w.at[2]))

    for cp in weight_copies(exp_ref[0]):
        cp.start()
    for b in range(nbuf - 1):
        @pl.when(b < total)
        def _(b=b):
            x_copy(b, b).start()

    def chunk(i, carry):
        slot = lax.rem(i, nbuf)
        x_copy(i, slot).wait()
        ahead = i + (nbuf - 1)

        @pl.when(ahead < total)
        def _():
            x_copy(ahead, lax.rem(ahead, nbuf)).start()

        @pl.when(newexp_ref[i] == 1)
        def _():
            for cp in weight_copies(exp_ref[i]):
                cp.wait()
            wgb[...] = wgf[...].astype(BF16)
            wub[...] = wuf[...].astype(BF16)
            wdb[...] = wdf[...].astype(BF16)

            @pl.when(nexp_ref[i] >= 0)
            def _():
                for cp in weight_copies(nexp_ref[i]):
                    cp.start()

        @pl.when(i >= nbuf)
        def _():
            y_copy(i, slot).wait()

        xr = xbuf.at[slot]
        yr = ybuf.at[slot]
        x = _load_packed(lambda sl: xr[sl, :], ch).astype(BF16)
        g = jnp.dot(x, wgb[...], preferred_element_type=F32)
        u = jnp.dot(x, wub[...], preferred_element_type=F32)
        y = jnp.dot((g * jax.nn.sigmoid(g) * u).astype(BF16), wdb[...], preferred_element_type=F32)
        _store_packed(yr, y)
        y_copy(i, slot).start()
        return carry

    lax.fori_loop(0, total, chunk, 0)

    for b in range(nbuf):
        @pl.when(b < total)
        def _(b=b):
            y_copy(0, b).wait()


def _experts(meta, xs, w_gate, w_up, w_down, after):
    e, d, f = w_gate.shape
    hbm = pl.BlockSpec(memory_space=pl.ANY)
    rows = EXPERT_CHUNK * PACK_SUBLANES
    grid_spec = pltpu.PrefetchScalarGridSpec(
        num_scalar_prefetch=len(meta), grid=(1,),
        in_specs=[hbm, hbm, hbm, hbm, hbm], out_specs=hbm,
        scratch_shapes=[pltpu.VMEM((EXPERT_RING, rows, LANES), U32), pltpu.VMEM((EXPERT_RING, rows, LANES), U32),
                        pltpu.VMEM((d, f), F32), pltpu.VMEM((d, f), F32), pltpu.VMEM((f, d), F32),
                        pltpu.VMEM((d, f), BF16), pltpu.VMEM((d, f), BF16), pltpu.VMEM((f, d), BF16),
                        pltpu.SemaphoreType.DMA((EXPERT_RING,)), pltpu.SemaphoreType.DMA((EXPERT_RING,)),
                        pltpu.SemaphoreType.DMA((3,))])
    return pl.pallas_call(
        _expert_kernel,
        grid_spec=grid_spec,
        out_shape=jax.ShapeDtypeStruct(xs.shape, xs.dtype),
        compiler_params=_params("arbitrary"),
        name="experts",
    )(*meta, xs, w_gate, w_up, w_down, after)


def _final_kernel(base_ref, yt_ref, wt_ref, mod_ref, g_ref, *rest, d):
    o_ref = rest[-1]
    t = base_ref.shape[0]
    w = wt_ref[...]
    routed = w[:, 0:1] * _load_packed(lambda sl: yt_ref[0, sl, :], t)
    for k in range(1, TOP_K):
        routed = routed + w[:, k:k + 1] * _load_packed(lambda sl, k=k: yt_ref[k, sl, :], t)
    g2 = mod_ref[0][:, 5 * d:6 * d]
    o_ref[...] = _rms(base_ref[...] + g2 * routed, g_ref[...])


def _final(base, ytok3, wts_t, mod3, tiles_per_batch, final_g, part, n_parts, prev_out):
    n, d = base.shape
    t = TOKEN_TILE
    tiles = n // t // n_parts
    off = part * tiles
    kern = functools.partial(_final_kernel, d=d)
    tok = pl.BlockSpec((t, d), lambda i: (off + i, 0))
    in_specs = [tok,
                pl.BlockSpec((TOP_K, t * PACK_SUBLANES, LANES), lambda i: (0, i, 0)),
                pl.BlockSpec((t, TOP_K), lambda i: (off + i, 0)),
                pl.BlockSpec((1, 1, mod3.shape[2]), lambda i: ((off + i) // tiles_per_batch, 0, 0)),
                _full(final_g)]
    args = [base, ytok3, wts_t, mod3, final_g]
    aliases = {}
    if prev_out is not None:
        in_specs.append(pl.BlockSpec(memory_space=pl.ANY))
        args.append(prev_out)
        aliases = {len(args) - 1: 0}
    return pl.pallas_call(
        kern,
        grid=(tiles,),
        in_specs=in_specs,
        out_specs=tok,
        out_shape=jax.ShapeDtypeStruct((n, d), F32),
        input_output_aliases=aliases,
        compiler_params=_params("arbitrary"),
        name="final",
    )(*args)


def kernel(x, c, ctx, c_ctx, w_mod, b_mod, norm1_g, norm2_g, w_in, ssm_lam_re, ssm_lam_im, ssm_log_dt, ssm_b_re, ssm_b_im, ssm_c_re, ssm_c_im, ssm_d, w_glu, w_ssm_out, conv_w, w_conv_out, w_o, w_router, router_bias, w_gate, w_up, w_down, ws_gate, ws_up, ws_down, final_g):
    b, l, d = x.shape
    n = b * l
    sw = ssm_d.shape[1]
    assert w_mod.shape[0] == 1, "single layer"
    assert b == SUBLANES and l % TOKEN_TILE == 0 and l % S5_CHUNK == 0 and ctx.shape[1] % S5_CHUNK == 0 and TIME_TILE % GRID_W == 0

    mod_rows = 2 * SUBLANES
    c_all = jnp.concatenate([c, c_ctx[None, :], jnp.zeros((mod_rows - b - 1, d), F32)], axis=0)
    mod = _modulation(c_all, w_mod[0], b_mod[0])
    mod3 = mod.reshape(mod_rows, 1, mod.shape[1])

    w_in_b = w_in[0].astype(BF16)
    g1n = norm1_g[0].reshape(1, d)
    u_lat, sga, gbt = _in_proj(x, mod3, g1n, w_in_b, conv_w[0], w_conv_out[0].astype(BF16))
    u_ctx = _ctx_proj(ctx, mod3, b, g1n, w_in_b[:, :sw])

    ops = _blocked_operators(ssm_lam_re[0], ssm_lam_im[0], ssm_log_dt[0], ssm_b_re[0], ssm_b_im[0],
                             ssm_c_re[0], ssm_c_im[0])
    y = _s5_blocked(u_ctx, u_lat, ssm_d[0].reshape(1, sw), ops, b, S5_CHUNK)

    xl = _mixer_out(y, sga, gbt, x, mod3, w_glu[0].astype(BF16), w_ssm_out[0].astype(BF16),
                    w_o[0].astype(BF16))

    tiles_per_batch = l // TOKEN_TILE
    xl2 = xl.reshape(n, d)
    g2n = norm2_g[0].reshape(1, d)
    h2, eidx, wts, rank, cnt = _route(xl2, mod3, tiles_per_batch, g2n, w_router[0].T,
                                      router_bias[0].reshape(N_EXPERTS, 1))

    counts = cnt[:, 0].astype(I32)
    n_rows = n * TOP_K
    starts, meta = _chunk_metadata(counts, n_rows)
    buf_rows = n_rows + N_EXPERTS * EXPERT_CHUNK
    dest = _dest_rows(eidx, rank, jnp.broadcast_to(starts[:, None], (N_EXPERTS, LANES)))

    xs = _sc_dispatch(dest.reshape(TOP_K * n), h2.reshape(n, PACK_SUBLANES, LANES), buf_rows)
    base = _shared(xl2, mod3, tiles_per_batch, g2n, ws_gate[0].astype(BF16), ws_up[0].astype(BF16),
                   ws_down[0].astype(BF16))
    ys = _experts(meta, xs.reshape(buf_rows * PACK_SUBLANES, LANES), w_gate[0], w_up[0], w_down[0], base)
    ys3 = ys.reshape(buf_rows, PACK_SUBLANES, LANES)

    n_part = n // COMBINE_PARTS
    wts_t = wts.T
    out = None
    for part in range(COMBINE_PARTS):
        dest_part = dest[:, part * n_part:(part + 1) * n_part].reshape(TOP_K * n_part)
        ytok = _sc_combine(dest_part, ys3, n_part)
        out = _final(base, ytok.reshape(TOP_K, n_part * PACK_SUBLANES, LANES), wts_t, mod3, tiles_per_batch,
                     final_g.reshape(1, d), part, COMBINE_PARTS, out)
    return out.reshape(b, l, d)
```

```python
import functools

import jax
import jax.numpy as jnp
from jax import lax
from jax.experimental import pallas as pl
from jax.experimental.pallas import tpu as pltpu
from jax.experimental.pallas import tpu_sc as plsc

F32 = jnp.float32
BF16 = jnp.bfloat16
I32 = jnp.int32
U32 = jnp.uint32

EPS = 1e-6
GRID_W = 64
SSM_GROUP = 16
N_EXPERTS = 256
TOP_K = 8
N_EXPERT_GROUPS = 8
EXPERTS_PER_GROUP = N_EXPERTS // N_EXPERT_GROUPS
TOP_K_GROUPS = 4
ROUTE_SCALE = 2.5

SUBLANES = 8
LANES = 128
VMEM_LIMIT_BYTES = 48 * 1024 * 1024

TIME_TILE = 64
TOKEN_TILE = 256
WIDE_TILE = 512
TIME_BLOCK = 16
S5_CHUNK = 256
OPERATOR_GROUPS = 4
EXPERT_CHUNK = 256
EXPERT_RING = 4
SC_CHUNK = 128
COMBINE_PARTS = 4
PACK_SUBLANES = 4


def _dot(a, b):
    return jnp.dot(a.astype(BF16), b.astype(BF16), preferred_element_type=F32)


def _rms(xf, g):
    return xf * lax.rsqrt(jnp.mean(xf * xf, axis=-1, keepdims=True) + EPS) * g


def _params(*sem):
    return pltpu.CompilerParams(dimension_semantics=sem, vmem_limit_bytes=VMEM_LIMIT_BYTES)


def _full(a):
    return pl.BlockSpec(a.shape, lambda *_: (0,) * a.ndim)


def _pack_rows(v):
    half = v.shape[1] // 2
    lo = lax.bitcast_convert_type(v[:, :half].astype(BF16).astype(F32), U32) >> 16
    hi = lax.bitcast_convert_type(v[:, half:].astype(BF16).astype(F32), U32) & jnp.uint32(0xFFFF0000)
    return lo | hi


def _unpack_lo(w):
    return lax.bitcast_convert_type(w << 16, F32)


def _unpack_hi(w):
    return lax.bitcast_convert_type(w & jnp.uint32(0xFFFF0000), F32)


def _store_packed(ref, v):
    t = v.shape[0]
    w = _pack_rows(v)
    for c in range(PACK_SUBLANES):
        ref[pl.ds(c, t, stride=PACK_SUBLANES), :] = w[:, c * LANES:(c + 1) * LANES]


def _load_packed(load, t):
    ws = [load(pl.ds(c, t, stride=PACK_SUBLANES)) for c in range(PACK_SUBLANES)]
    return jnp.concatenate([_unpack_lo(w) for w in ws] + [_unpack_hi(w) for w in ws], axis=1)


def _mod_kernel(c_ref, w_ref, b_ref, o_ref):
    c = c_ref[...]
    o_ref[...] = _dot(c * jax.nn.sigmoid(c), w_ref[...]) + b_ref[...]


def _modulation(c_all, w_mod, b_mod):
    rows, d = c_all.shape
    cols = w_mod.shape[1]
    blk = 1536
    return pl.pallas_call(
        _mod_kernel,
        grid=(cols // blk,),
        in_specs=[pl.BlockSpec((rows, d), lambda j: (0, 0)),
                  pl.BlockSpec((d, blk), lambda j: (0, j)),
                  pl.BlockSpec((1, blk), lambda j: (0, j))],
        out_specs=pl.BlockSpec((rows, blk), lambda j: (0, j)),
        out_shape=jax.ShapeDtypeStruct((rows, cols), F32),
        compiler_params=_params("arbitrary"),
        name="mod",
    )(c_all, w_mod, b_mod.reshape(1, cols))


def _per_row(m3, lo, hi, tt):
    nb = m3.shape[0]
    return jnp.broadcast_to(m3[:, :, lo:hi], (nb, tt, hi - lo)).reshape(nb * tt, hi - lo)


def _to_time_major(val, nb, tt):
    c = val.shape[1]
    return pltpu.einshape("btc->tbc", val.reshape(nb, tt, c)).reshape(nb * tt, c)


def _to_batch_major(val, nb, tt):
    c = val.shape[1]
    return pltpu.einshape("tbc->btc", val.reshape(tt, nb, c)).reshape(nb * tt, c)


def _in_proj_kernel(x_ref, mod_ref, g_ref, w_ref, cw_ref, wco_ref, u_ref, sga_ref, gbt_ref, *, d, sw):
    nb, tt, _ = x_ref.shape
    rows = nb * tt
    m3 = mod_ref[...]
    x = x_ref[...].reshape(rows, d)
    h = _rms(x, g_ref[...]) * (1.0 + _per_row(m3, d, 2 * d, tt)) + _per_row(m3, 0, d, tt)
    hb = h.astype(BF16)
    u_ref[...] = _to_time_major(jnp.dot(hb, w_ref[:, 0:sw], preferred_element_type=F32), nb, tt)
    cb = jnp.dot(hb, w_ref[:, sw:2 * sw], preferred_element_type=F32)
    cc = jnp.dot(hb, w_ref[:, 2 * sw:3 * sw], preferred_element_type=F32)
    cv = jnp.dot(hb, w_ref[:, 3 * sw:4 * sw], preferred_element_type=F32)
    ccv = cc * cv
    col = lax.broadcasted_iota(I32, ccv.shape, 0) % GRID_W
    prev = jnp.where(col == 0, 0.0, pltpu.roll(ccv, 1, axis=0))
    nxt = jnp.where(col == GRID_W - 1, 0.0, pltpu.roll(ccv, rows - 1, axis=0))
    cw = cw_ref[...]
    conv = prev * cw[0:1, :] + ccv * cw[1:2, :] + nxt * cw[2:3, :]
    y_conv = _dot(cb * conv, wco_ref[...])
    ga = jnp.dot(hb, w_ref[:, 4 * sw:4 * sw + d], preferred_element_type=F32)
    gb = jnp.dot(hb, w_ref[:, 4 * sw + d:4 * sw + 2 * d], preferred_element_type=F32)
    sga_ref[...] = jax.nn.sigmoid(ga).reshape(nb, tt, d)
    gbt_ref[...] = (jax.nn.sigmoid(gb) * y_conv).reshape(nb, tt, d)


def _in_proj(x, mod3, g1n, w_in_b, conv_w, w_conv_out_b):
    b, l, d = x.shape
    sw = conv_w.shape[1]
    tt = TIME_TILE
    kern = functools.partial(_in_proj_kernel, d=d, sw=sw)
    tok = pl.BlockSpec((b, tt, d), lambda j: (0, j, 0))
    return pl.pallas_call(
        kern,
        grid=(l // tt,),
        in_specs=[tok,
                  pl.BlockSpec((b, 1, mod3.shape[2]), lambda j: (0, 0, 0)),
                  _full(g1n), _full(w_in_b), _full(conv_w), _full(w_conv_out_b)],
        out_specs=[pl.BlockSpec((tt * b, sw), lambda j: (j, 0)), tok, tok],
        out_shape=[jax.ShapeDtypeStruct((l * b, sw), F32),
                   jax.ShapeDtypeStruct((b, l, d), F32),
                   jax.ShapeDtypeStruct((b, l, d), F32)],
        compiler_params=_params("arbitrary"),
        name="in_proj",
    )(x, mod3, g1n, w_in_b, conv_w, w_conv_out_b)


def _ctx_proj_kernel(x_ref, mod_ref, g_ref, w_ref, u_ref, *, d):
    nb, tt, _ = x_ref.shape
    m = mod_ref[0]
    x = x_ref[...].reshape(nb * tt, d)
    h = _rms(x, g_ref[...]) * (1.0 + m[:, d:2 * d]) + m[:, 0:d]
    u_ref[...] = _to_time_major(_dot(h, w_ref[...]), nb, tt)


def _ctx_proj(ctx, mod3, ctx_row, g1n, w_u_b):
    b, lc, d = ctx.shape
    sw = w_u_b.shape[1]
    tt = TIME_TILE
    kern = functools.partial(_ctx_proj_kernel, d=d)
    return pl.pallas_call(
        kern,
        grid=(lc // tt,),
        in_specs=[pl.BlockSpec((b, tt, d), lambda j: (0, j, 0)),
                  pl.BlockSpec((1, 1, mod3.shape[2]), lambda j: (ctx_row, 0, 0)),
                  _full(g1n), _full(w_u_b)],
        out_specs=pl.BlockSpec((tt * b, sw), lambda j: (j, 0)),
        out_shape=jax.ShapeDtypeStruct((lc * b, sw), F32),
        compiler_params=_params("arbitrary"),
        name="ctx_proj",
    )(ctx, mod3, g1n, w_u_b)


_NT = (((1,), (1,)), ((), ()))


def _dot3_nt(a, b):
    a_hi, a_lo = _split_bf16(a)
    b_hi, b_lo = _split_bf16(b)
    return (lax.dot_general(a_hi, b_hi, _NT, preferred_element_type=F32)
            + lax.dot_general(a_hi, b_lo, _NT, preferred_element_type=F32)
            + lax.dot_general(a_lo, b_hi, _NT, preferred_element_type=F32))


def _operator_kernel(*refs):
    n_in = 6
    for gi in range(refs[0].shape[1]):
        _operator_group(*[r.at[:, pl.ds(gi, 1)] for r in refs[:n_in]], *[o.at[pl.ds(gi, 1)] for o in refs[n_in:]])


def _operator_group(akr_ref, aki_ref, bbr_ref, bbi_ref, cr_ref, ci_ref, tfb_ref, bst_ref, cst_ref):
    tb = TIME_BLOCK
    q, p = bbr_ref.shape[2], bbr_ref.shape[3]
    w = tb * q

    def times_ak(d, ks, xr, xi):
        re, im = [], []
        for k in ks:
            ar, ai = akr_ref[d, 0, k:k + 1, :], aki_ref[d, 0, k:k + 1, :]
            re.append(ar * xr - ai * xi)
            im.append(ar * xi + ai * xr)
        return jnp.concatenate(re, axis=0), jnp.concatenate(im, axis=0)

    col = lax.broadcasted_iota(I32, (p, w), 1)
    row = lax.broadcasted_iota(I32, (p, w), 0)

    def place(quarter):
        return (col == row + quarter * p).astype(BF16)

    inj, ca = [], []
    for d in range(2):
        inj += list(times_ak(d, range(tb - 1, -1, -1) if d == 0 else range(tb), bbr_ref[d, 0], bbi_ref[d, 0]))
        re, im = times_ak(d, range(1, tb + 1) if d == 0 else range(tb, 0, -1), cr_ref[d, 0], ci_ref[d, 0])
        ca += [re, -im]
    bst = jnp.zeros((w, w), F32)
    cst = jnp.zeros((w, w), F32)
    for quarter in range(4):
        pm = place(quarter)
        bst = bst + jnp.dot(inj[quarter].astype(BF16), pm, preferred_element_type=F32)
        cst = cst + lax.dot_general(pm, ca[quarter].astype(BF16), (((0,), (1,)), ((), ())),
                                    preferred_element_type=F32)
    bst_ref[0] = bst.astype(BF16)
    cst_ref[0] = cst.astype(BF16)

    sub = lax.broadcasted_iota(I32, (q, w), 0)
    lane = lax.broadcasted_iota(I32, (q, w), 1)
    lag_rows = []
    for d in range(2):
        abr, abi = times_ak(d, range(tb), bbr_ref[d, 0], bbi_ref[d, 0])
        m = _dot3_nt(abr, cr_ref[d, 0]) - _dot3_nt(abi, ci_ref[d, 0])
        acc = jnp.zeros((q, w), F32)
        for k in range(tb):
            blk = tb - 1 - k if d == 1 else k
            put = ((lane - blk * q) == sub).astype(BF16)
            acc = acc + jnp.dot(m[k * q:(k + 1) * q, :].astype(BF16), put, preferred_element_type=F32)
        lag_rows.append(acc)
    rows_f, rows_b = lag_rows

    for s in range(tb):
        right = s * q
        left = (tb - 1 - s) * q
        fwd = jnp.where(lane >= right, pltpu.roll(rows_f, right, axis=1) if right else rows_f, 0.0)
        bwd = jnp.where(lane < w - left, pltpu.roll(rows_b, w - left, axis=1) if left else rows_b, 0.0)
        tfb_ref[0, s * q:(s + 1) * q, :] = (fwd + bwd).astype(BF16)


def _blocked_operators(lam_re, lam_im, log_dt, b_re, b_im, c_re, c_im):
    tb = TIME_BLOCK
    dt = jnp.exp(log_dt)[..., None]
    k = jnp.arange(tb + 1, dtype=F32)[None, None, :, None]
    mag = jnp.exp(lam_re * dt)[:, :, None, :] ** k
    ang = (lam_im * dt)[:, :, None, :] * k
    ak_re = mag * jnp.cos(ang)
    ak_im = mag * jnp.sin(ang)
    a_re, a_im = ak_re[:, :, 1], ak_im[:, :, 1]
    den = lam_re * lam_re + lam_im * lam_im
    k_re = ((a_re - 1.0) * lam_re + a_im * lam_im) / den
    k_im = (a_im * lam_re - (a_re - 1.0) * lam_im) / den
    bt_re = jnp.swapaxes(b_re, 2, 3)
    bt_im = jnp.swapaxes(b_im, 2, 3)
    bbt_re = k_re[:, :, None, :] * bt_re - k_im[:, :, None, :] * bt_im
    bbt_im = k_re[:, :, None, :] * bt_im + k_im[:, :, None, :] * bt_re
    nd, g, q, p = bbt_re.shape
    w = tb * q

    gb = OPERATOR_GROUPS
    per_group = lambda a: pl.BlockSpec((nd, gb) + a.shape[2:], lambda i: (0, i, 0, 0))
    out = pl.BlockSpec((gb, w, w), lambda i: (i, 0, 0))
    ins = (ak_re, ak_im, bbt_re, bbt_im, c_re, c_im)
    tfb, bst, cst = pl.pallas_call(
        _operator_kernel,
        grid=(g // gb,),
        in_specs=[per_group(a) for a in ins],
        out_specs=[out, out, out],
        out_shape=[jax.ShapeDtypeStruct((g, w, w), BF16)] * 3,
        compiler_params=_params("arbitrary"),
        name="s5_operators",
    )(*ins)
    ax = jnp.concatenate([ak_re[:, :, tb], ak_re[:, :, tb]], axis=-1).reshape(nd, 1, g * 2 * p)
    ay = jnp.concatenate([-ak_im[:, :, tb], ak_im[:, :, tb]], axis=-1).reshape(nd, 1, g * 2 * p)
    return tfb, bst, cst, ax, ay


def _lane_chunk(shape):
    return lax.broadcasted_iota(I32, shape, 1) // SSM_GROUP


def _regroup_kernel(uc_ref, ul_ref, bst_ref, xg_ref, e_ref, *, n_ctx, nb, groups, q):
    j = pl.program_id(0)
    rows_in = uc_ref.shape[0]
    nblk = rows_in // (TIME_BLOCK * nb)
    rows = nblk * nb
    u = jnp.where(j < n_ctx, uc_ref[...], ul_ref[...])
    u3 = u.reshape(nblk, TIME_BLOCK * nb, u.shape[1])
    xs = [u3[:, s * nb:(s + 1) * nb, :].reshape(rows, u.shape[1]) for s in range(TIME_BLOCK)]
    per_tile = LANES // q
    chunk = _lane_chunk((rows, LANES))
    for g in range(groups):
        jt, qq = divmod(g, per_tile)
        for h in range(TIME_BLOCK // per_tile):
            acc = jnp.zeros((rows, LANES), F32)
            for s in range(h * per_tile, (h + 1) * per_tile):
                pos = s % per_tile
                a = xs[s][:, jt * LANES:(jt + 1) * LANES]
                shift = ((pos - qq) % per_tile) * q
                r = pltpu.roll(a, shift, axis=1) if shift else a
                acc = jnp.where(chunk == pos, r, acc)
            xg_ref[:, (2 * g + h) * LANES:(2 * g + h + 1) * LANES] = acc.astype(BF16)
    gw = TIME_BLOCK * q
    for g in range(groups):
        e = jnp.dot(xg_ref[:, g * gw:(g + 1) * gw], bst_ref[g], preferred_element_type=F32)
        half = e.shape[1] // 2
        e_ref[0, :, g * half:(g + 1) * half] = e[:, :half]
        e_ref[1, :, g * half:(g + 1) * half] = e[:, half:]


def _s5_regroup(u_ctx, u_lat, bst, nb, tc):
    rows_c, sw = u_ctx.shape
    rows_l = u_lat.shape[0]
    rc = tc * nb
    n_ctx, n_lat = rows_c // rc, rows_l // rc
    groups, gw, p4 = bst.shape
    q = gw // TIME_BLOCK
    rows_out = (tc // TIME_BLOCK) * nb
    total_rows = (n_ctx + n_lat) * rows_out
    kern = functools.partial(_regroup_kernel, n_ctx=n_ctx, nb=nb, groups=groups, q=q)
    return pl.pallas_call(
        kern,
        grid=(n_ctx + n_lat,),
        in_specs=[pl.BlockSpec((rc, sw), lambda j: (jnp.minimum(j, n_ctx - 1), 0)),
                  pl.BlockSpec((rc, sw), lambda j: (jnp.maximum(j - n_ctx, 0), 0)),
                  pl.BlockSpec(bst.shape, lambda j: (0, 0, 0), pipeline_mode=pl.Buffered(1))],
        out_specs=[pl.BlockSpec((rows_out, groups * gw), lambda j: (j, 0)),
                   pl.BlockSpec((2, rows_out, groups * p4 // 2), lambda j: (0, j, 0))],
        out_shape=[jax.ShapeDtypeStruct((total_rows, groups * gw), BF16),
                   jax.ShapeDtypeStruct((2, total_rows, groups * p4 // 2), F32)],
        compiler_params=_params("arbitrary"),
        name="s5_regroup",
    )(u_ctx, u_lat, bst)


def _carry_kernel(e_ref, ax_ref, ay_ref, s_ref, st_ref, *, nb):
    dirn = pl.program_id(0)
    j = pl.program_id(1)
    rows = e_ref.shape[1]
    nblk = rows // nb
    lanes = e_ref.shape[2]

    @pl.when(j == 0)
    def _():
        st_ref[...] = jnp.zeros_like(st_ref)

    ax = jnp.broadcast_to(ax_ref[0], (nb, lanes))
    ay = jnp.broadcast_to(ay_ref[0], (nb, lanes))

    def body(i, s):
        blk = jnp.where(dirn == 0, i, nblk - 1 - i)
        r0 = pl.multiple_of(blk * nb, nb)
        s_ref[0, pl.ds(r0, nb), :] = s.astype(s_ref.dtype)
        swapped = jnp.concatenate([pltpu.roll(s[:, t * LANES:(t + 1) * LANES], LANES // 2, axis=1)
                                   for t in range(lanes // LANES)], axis=1)
        return ax * s + ay * swapped + e_ref[0, pl.ds(r0, nb), :]

    st_ref[...] = lax.fori_loop(0, nblk, body, st_ref[...])


def _s5_carry(e, ax, ay, nb, n_ctx_chunks, rows_per_chunk):
    nd, total_rows, lanes = e.shape
    n_chunks = total_rows // rows_per_chunk
    n_lat = n_chunks - n_ctx_chunks

    def idx(d, j):
        jc = jnp.minimum(j, n_ctx_chunks - 1)
        jl = jnp.maximum(j - n_ctx_chunks, 0)
        fwd = j
        bwd = jnp.where(j < n_ctx_chunks, n_ctx_chunks - 1 - jc, n_chunks - 1 - jl)
        return (d, jnp.where(d == 0, fwd, bwd), 0)

    kern = functools.partial(_carry_kernel, nb=nb)
    return pl.pallas_call(
        kern,
        grid=(nd, n_chunks),
        in_specs=[pl.BlockSpec((1, rows_per_chunk, lanes), idx),
                  pl.BlockSpec((1, 1, lanes), lambda d, j: (d, 0, 0)),
                  pl.BlockSpec((1, 1, lanes), lambda d, j: (d, 0, 0))],
        out_specs=pl.BlockSpec((1, rows_per_chunk, lanes), idx),
        out_shape=jax.ShapeDtypeStruct(e.shape, F32),
        scratch_shapes=[pltpu.VMEM((nb, lanes), F32)],
        compiler_params=_params("arbitrary", "arbitrary"),
        name="s5_carry",
    )(e, ax, ay)


def _output_kernel(xg_ref, s_ref, u_ref, d_ref, tfb_ref, cst_ref, y_ref, yg_ref, *, nb, groups, q):
    rows = xg_ref.shape[0]
    nblk = rows // nb
    gw = TIME_BLOCK * q
    half = s_ref.shape[2] // groups
    for g in range(groups):
        sg = jnp.concatenate([s_ref[0, :, g * half:(g + 1) * half], s_ref[1, :, g * half:(g + 1) * half]], axis=1)
        yg_ref[:, g * gw:(g + 1) * gw] = (
            jnp.dot(xg_ref[:, g * gw:(g + 1) * gw], tfb_ref[g], preferred_element_type=F32)
            + jnp.dot(sg.astype(BF16), cst_ref[g], preferred_element_type=F32))
    per_tile = LANES // q
    chunk = _lane_chunk((rows, LANES))
    sw = groups * q
    steps = []
    for s in range(TIME_BLOCK):
        h, pos = divmod(s, per_tile)
        tiles = []
        for jt in range(sw // LANES):
            acc = jnp.zeros((rows, LANES), F32)
            for qq in range(per_tile):
                g = jt * per_tile + qq
                a = yg_ref[:, (2 * g + h) * LANES:(2 * g + h + 1) * LANES]
                shift = ((qq - pos) % per_tile) * q
                r = pltpu.roll(a, shift, axis=1) if shift else a
                acc = jnp.where(chunk == qq, r, acc)
            tiles.append(acc)
        steps.append(jnp.concatenate(tiles, axis=1).reshape(nblk, nb, sw))
    y = jnp.concatenate(steps, axis=1).reshape(nblk * TIME_BLOCK * nb, sw)
    y_ref[...] = y + d_ref[...] * u_ref[...]


def _s5_outputs(xg, s, u_lat, ssm_d, tfb, cst, nb, tc, n_ctx_chunks):
    rows_l, sw = u_lat.shape
    rc = tc * nb
    n_lat = rows_l // rc
    groups, gw, _ = tfb.shape
    q = gw // TIME_BLOCK
    rows_out = (tc // TIME_BLOCK) * nb
    kern = functools.partial(_output_kernel, nb=nb, groups=groups, q=q)
    return pl.pallas_call(
        kern,
        grid=(n_lat,),
        in_specs=[pl.BlockSpec((rows_out, groups * gw), lambda j: (j + n_ctx_chunks, 0)),
                  pl.BlockSpec((2, rows_out, s.shape[2]), lambda j: (0, j + n_ctx_chunks, 0)),
                  pl.BlockSpec((rc, sw), lambda j: (j, 0)),
                  pl.BlockSpec((1, sw), lambda j: (0, 0)),
                  pl.BlockSpec(tfb.shape, lambda j: (0, 0, 0), pipeline_mode=pl.Buffered(1)),
                  pl.BlockSpec(cst.shape, lambda j: (0, 0, 0), pipeline_mode=pl.Buffered(1))],
        out_specs=pl.BlockSpec((rc, sw), lambda j: (j, 0)),
        out_shape=jax.ShapeDtypeStruct((rows_l, sw), F32),
        scratch_shapes=[pltpu.VMEM((rows_out, groups * gw), F32)],
        compiler_params=_params("arbitrary"),
        name="s5_output",
    )(xg, s, u_lat, ssm_d, tfb, cst)


def _s5_blocked(u_ctx, u_lat, ssm_d, ops, nb, tc):
    tfb, bst, cst, ax, ay = ops
    n_ctx_chunks = u_ctx.shape[0] // (tc * nb)
    xg, e = _s5_regroup(u_ctx, u_lat, bst, nb, tc)
    s = _s5_carry(e, ax, ay, nb, n_ctx_chunks, (tc // TIME_BLOCK) * nb)
    return _s5_outputs(xg, s, u_lat, ssm_d, tfb, cst, nb, tc, n_ctx_chunks)


def _mixer_kernel(y_ref, sga_ref, gbt_ref, x_ref, mod_ref, wglu_ref, wso_ref, wo_ref, o_ref, *, d, sw):
    nb, tt, _ = x_ref.shape
    rows = nb * tt
    y = _to_batch_major(y_ref[...], nb, tt)
    v = _dot(jax.nn.gelu(y), wglu_ref[...])
    ys = v[:, 0:sw] * jax.nn.sigmoid(v[:, sw:2 * sw])
    y_a = _dot(ys, wso_ref[...])
    merged = sga_ref[...].reshape(rows, d) * y_a + gbt_ref[...].reshape(rows, d)
    o = _dot(merged, wo_ref[...])
    g1 = _per_row(mod_ref[...], 2 * d, 3 * d, tt)
    o_ref[...] = (x_ref[...].reshape(rows, d) + g1 * o).reshape(nb, tt, d)


def _mixer_out(y, sga, gbt, x, mod3, w_glu_b, w_ssm_out_b, w_o_b):
    b, l, d = x.shape
    sw = y.shape[1]
    tt = TIME_TILE
    kern = functools.partial(_mixer_kernel, d=d, sw=sw)
    tok = pl.BlockSpec((b, tt, d), lambda j: (0, j, 0))
    return pl.pallas_call(
        kern,
        grid=(l // tt,),
        in_specs=[pl.BlockSpec((tt * b, sw), lambda j: (j, 0)),
                  tok, tok, tok,
                  pl.BlockSpec((b, 1, mod3.shape[2]), lambda j: (0, 0, 0)),
                  _full(w_glu_b), _full(w_ssm_out_b), _full(w_o_b)],
        out_specs=tok,
        out_shape=jax.ShapeDtypeStruct((b, l, d), F32),
        compiler_params=_params("arbitrary"),
        name="mixer_out",
    )(y, sga, gbt, x, mod3, w_glu_b, w_ssm_out_b, w_o_b)


def _split_bf16(a):
    hi = a.astype(BF16)
    lo = (a - hi.astype(F32)).astype(BF16)
    return hi, lo


def _norm2(xl, m, g, d):
    return _rms(xl, g) * (1.0 + m[:, 4 * d:5 * d]) + m[:, 3 * d:4 * d]


def _shared_kernel(xl_ref, mod_ref, g_ref, wsg_ref, wsu_ref, wsd_ref, base_ref, *, d):
    m = mod_ref[0]
    xl = xl_ref[...]
    hb = _norm2(xl, m, g_ref[...], d).astype(BF16)
    sg = jnp.dot(hb, wsg_ref[...], preferred_element_type=F32)
    su = jnp.dot(hb, wsu_ref[...], preferred_element_type=F32)
    shared = _dot(sg * jax.nn.sigmoid(sg) * su, wsd_ref[...])
    base_ref[...] = xl + m[:, 5 * d:6 * d] * shared


def _shared(xl2, mod3, tokens_per_batch, g2n, ws_gate_b, ws_up_b, ws_down_b):
    n, d = xl2.shape
    t = WIDE_TILE
    tiles_per_batch = tokens_per_batch // t
    tok = pl.BlockSpec((t, d), lambda i: (i, 0))
    return pl.pallas_call(
        functools.partial(_shared_kernel, d=d),
        grid=(n // t,),
        in_specs=[tok,
                  pl.BlockSpec((1, 1, mod3.shape[2]), lambda i: (i // tiles_per_batch, 0, 0)),
                  _full(g2n), _full(ws_gate_b), _full(ws_up_b), _full(ws_down_b)],
        out_specs=tok,
        out_shape=jax.ShapeDtypeStruct((n, d), F32),
        compiler_params=_params("arbitrary"),
        name="shared",
    )(xl2, mod3, g2n, ws_gate_b, ws_up_b, ws_down_b)


def _route_kernel(xl_ref, mod_ref, g_ref, whi_ref, wlo_ref, rb_ref,
                  h2_ref, eidx_ref, wts_ref, rank_ref, cnt_ref, carry_ref, *, d):
    i = pl.program_id(0)
    t = xl_ref.shape[0]
    h2 = _norm2(xl_ref[...], mod_ref[0], g_ref[...], d)
    _store_packed(h2_ref, h2)

    h_hi, h_lo = _split_bf16(h2)
    w_hi, w_lo = whi_ref[...], wlo_ref[...]
    nt = (((1,), (1,)), ((), ()))
    logits = (lax.dot_general(w_hi, h_hi, nt, preferred_element_type=F32)
              + lax.dot_general(w_hi, h_lo, nt, preferred_element_type=F32)
              + lax.dot_general(w_lo, h_hi, nt, preferred_element_type=F32))
    scores = jax.nn.sigmoid(logits)
    choice = scores + rb_ref[...]

    epg = EXPERTS_PER_GROUP
    gi = lax.broadcasted_iota(I32, (epg, t), 0)
    gs = []
    for g in range(N_EXPERT_GROUPS):
        seg = choice[g * epg:(g + 1) * epg, :]
        m1 = jnp.max(seg, axis=0, keepdims=True)
        i1 = jnp.min(jnp.where(seg == m1, gi, epg), axis=0, keepdims=True)
        m2 = jnp.max(jnp.where(gi == i1, -jnp.inf, seg), axis=0, keepdims=True)
        gs.append(m1 + m2)
    masked = []
    for g in range(N_EXPERT_GROUPS):
        beat = jnp.zeros((1, t), I32)
        for g2 in range(N_EXPERT_GROUPS):
            if g2 < g:
                beat = beat + (gs[g2] >= gs[g]).astype(I32)
            elif g2 > g:
                beat = beat + (gs[g2] > gs[g]).astype(I32)
        keep = beat < TOP_K_GROUPS
        masked.append(jnp.where(keep, choice[g * epg:(g + 1) * epg, :], -jnp.inf))
    cur = jnp.concatenate(masked, axis=0)

    ei_all = lax.broadcasted_iota(I32, (N_EXPERTS, t), 0)
    picks, raw = [], []
    onehot = jnp.zeros((N_EXPERTS, t), F32)
    for _ in range(TOP_K):
        mx = jnp.max(cur, axis=0, keepdims=True)
        ei = jnp.min(jnp.where(cur == mx, ei_all, N_EXPERTS), axis=0, keepdims=True)
        hit = ei_all == ei
        raw.append(jnp.sum(jnp.where(hit, scores, 0.0), axis=0, keepdims=True))
        cur = jnp.where(hit, -jnp.inf, cur)
        onehot = jnp.where(hit, 1.0, onehot)
        picks.append(ei)
    tot = raw[0]
    for k in range(1, TOP_K):
        tot = tot + raw[k]

    @pl.when(i == 0)
    def _():
        carry_ref[...] = jnp.zeros_like(carry_ref)

    upper = (lax.broadcasted_iota(I32, (t, t), 0) < lax.broadcasted_iota(I32, (t, t), 1)).astype(BF16)
    before = jnp.dot(onehot.astype(BF16), upper, preferred_element_type=F32) + carry_ref[:, 0:1]
    for k in range(TOP_K):
        eidx_ref[k:k + 1, :] = picks[k]
        wts_ref[k:k + 1, :] = raw[k] / tot * ROUTE_SCALE
        rk = jnp.sum(jnp.where(ei_all == picks[k], before, 0.0), axis=0, keepdims=True)
        rank_ref[k:k + 1, :] = rk.astype(I32)
    carry_ref[...] = carry_ref[...] + jnp.sum(onehot, axis=1, keepdims=True)
    cnt_ref[...] = carry_ref[...]


def _route(xl2, mod3, tiles_per_batch, g2n, w_router_t, router_bias):
    n, d = xl2.shape
    t = TOKEN_TILE
    e = w_router_t.shape[0]
    w_top = lax.bitcast_convert_type(lax.bitcast_convert_type(w_router_t, U32) & jnp.uint32(0xFFFF0000), F32)
    w_hi, w_lo = w_top.astype(BF16), (w_router_t - w_top).astype(BF16)
    kern = functools.partial(_route_kernel, d=d)
    tok = pl.BlockSpec((t, d), lambda i: (i, 0))
    small = pl.BlockSpec((TOP_K, t), lambda i: (0, i))
    return pl.pallas_call(
        kern,
        grid=(n // t,),
        in_specs=[tok,
                  pl.BlockSpec((1, 1, mod3.shape[2]), lambda i: (i // tiles_per_batch, 0, 0)),
                  _full(g2n), _full(w_hi), _full(w_lo), _full(router_bias)],
        out_specs=[pl.BlockSpec((t * PACK_SUBLANES, LANES), lambda i: (i, 0)), small, small, small,
                   pl.BlockSpec((e, LANES), lambda i: (0, 0))],
        out_shape=[jax.ShapeDtypeStruct((n * PACK_SUBLANES, LANES), U32),
                   jax.ShapeDtypeStruct((TOP_K, n), I32),
                   jax.ShapeDtypeStruct((TOP_K, n), F32),
                   jax.ShapeDtypeStruct((TOP_K, n), I32),
                   jax.ShapeDtypeStruct((e, LANES), F32)],
        scratch_shapes=[pltpu.VMEM((e, LANES), F32)],
        compiler_params=_params("arbitrary"),
        name="route",
    )(xl2, mod3, g2n, w_hi, w_lo, router_bias)


def _dest_kernel(eidx_ref, rank_ref, start_ref, o_ref):
    t = eidx_ref.shape[1]
    ei_all = lax.broadcasted_iota(I32, (N_EXPERTS, t), 0)
    st = start_ref[:, 0:1]
    for k in range(TOP_K):
        hit = ei_all == eidx_ref[k:k + 1, :]
        o_ref[k:k + 1, :] = jnp.sum(jnp.where(hit, st, 0), axis=0, keepdims=True) + rank_ref[k:k + 1, :]


def _dest_rows(eidx, rank, starts_b):
    n = eidx.shape[1]
    t = 1024
    small = pl.BlockSpec((TOP_K, t), lambda i: (0, i))
    return pl.pallas_call(
        _dest_kernel,
        grid=(n // t,),
        in_specs=[small, small, _full(starts_b)],
        out_specs=small,
        out_shape=jax.ShapeDtypeStruct((TOP_K, n), I32),
        compiler_params=_params("arbitrary"),
        name="dest",
    )(eidx, rank, starts_b)


def _sc_workers():
    info = plsc.get_sparse_core_info()
    return info.num_cores, info.num_cores * info.num_subcores


def _sc_dispatch(dest_flat, src3, out_rows):
    n = src3.shape[0]
    n_cores, n_workers = _sc_workers()
    per_worker = n // n_workers
    assert per_worker % SC_CHUNK == 0
    mesh = plsc.VectorSubcoreMesh(core_axis_name="c", subcore_axis_name="s")

    @functools.partial(
        pl.kernel, mesh=mesh,
        out_type=jax.ShapeDtypeStruct((out_rows,) + src3.shape[1:], src3.dtype),
        scratch_types=[pltpu.VMEM((SC_CHUNK,) + src3.shape[1:], src3.dtype)]
        + [pltpu.VMEM((SC_CHUNK,), I32)] * TOP_K,
        name="dispatch",
    )
    def run(dest_hbm, src_hbm, out_hbm, rows_v, *idx_v):
        worker = lax.axis_index("s") * n_cores + lax.axis_index("c")

        @pl.loop(0, per_worker // SC_CHUNK)
        def _(j):
            base = worker * per_worker + j * SC_CHUNK
            pltpu.sync_copy(src_hbm.at[pl.ds(base, SC_CHUNK)], rows_v)
            for k in range(TOP_K):
                pltpu.sync_copy(dest_hbm.at[pl.ds(k * n + base, SC_CHUNK)], idx_v[k])
            for k in range(TOP_K):
                pltpu.sync_copy(rows_v, out_hbm.at[idx_v[k]])

    return run(dest_flat, src3)


def _sc_combine(dest_flat, ys3, n):
    n_cores, n_workers = _sc_workers()
    per_worker = n // n_workers
    assert per_worker % SC_CHUNK == 0
    mesh = plsc.VectorSubcoreMesh(core_axis_name="c", subcore_axis_name="s")

    @functools.partial(
        pl.kernel, mesh=mesh,
        out_type=jax.ShapeDtypeStruct((TOP_K * n,) + ys3.shape[1:], ys3.dtype),
        scratch_types=[pltpu.VMEM((SC_CHUNK,) + ys3.shape[1:], ys3.dtype), pltpu.VMEM((SC_CHUNK,), I32)],
        name="combine",
    )
    def run(dest_hbm, ys_hbm, out_hbm, rows_v, idx_v):
        worker = lax.axis_index("s") * n_cores + lax.axis_index("c")

        @pl.loop(0, per_worker // SC_CHUNK)
        def _(j):
            base = worker * per_worker + j * SC_CHUNK
            for k in range(TOP_K):
                pltpu.sync_copy(dest_hbm.at[pl.ds(k * n + base, SC_CHUNK)], idx_v)
                pltpu.sync_copy(ys_hbm.at[idx_v], rows_v)
                pltpu.sync_copy(rows_v, out_hbm.at[pl.ds(k * n + base, SC_CHUNK)])

    return run(dest_flat, ys3)


def _chunk_metadata(counts, n_rows):
    ch = EXPERT_CHUNK
    n_ch = (counts + ch - 1) // ch
    cum = jnp.cumsum(n_ch)
    total = cum[-1]
    max_chunks = n_rows // ch + N_EXPERTS
    i = jnp.arange(max_chunks, dtype=I32)
    e = jnp.sum((cum[None, :] <= i[:, None]).astype(I32), axis=1)
    e_last = jnp.max(jnp.where(counts > 0, jnp.arange(N_EXPERTS, dtype=I32), 0))
    exp = jnp.where(i < total, jnp.minimum(e, N_EXPERTS - 1), e_last).astype(I32)
    newexp = jnp.concatenate([jnp.ones((1,), I32), (exp[1:] != exp[:-1]).astype(I32)])
    change_at = jnp.where(newexp == 1, i, max_chunks)
    nxt_change = lax.cummin(jnp.concatenate([change_at[1:], jnp.array([max_chunks], I32)]), reverse=True)
    nexp = jnp.where(nxt_change < max_chunks, exp[jnp.minimum(nxt_change, max_chunks - 1)], -1).astype(I32)
    starts = (cum - n_ch) * ch
    return starts.astype(I32), (exp, newexp, nexp, total.astype(I32).reshape(1))


def _expert_kernel(exp_ref, newexp_ref, nexp_ref, total_ref, xs_hbm, wg_hbm, wu_hbm, wd_hbm, after_hbm, ys_hbm,
                   xbuf, ybuf, wgf, wuf, wdf, wgb, wub, wdb, sem_x, sem_y, sem_w):
    del after_hbm
    ch = EXPERT_CHUNK
    nbuf = EXPERT_RING
    rows = ch * PACK_SUBLANES
    total = total_ref[0]

    def x_copy(i, slot):
        r0 = pl.multiple_of(i * rows, rows)
        return pltpu.make_async_copy(xs_hbm.at[pl.ds(r0, rows)], xbuf.at[slot], sem_x.at[slot])

    def y_copy(i, slot):
        r0 = pl.multiple_of(i * rows, rows)
        return pltpu.make_async_copy(ybuf.at[slot], ys_hbm.at[pl.ds(r0, rows)], sem_y.at[slot])

    def weight_copies(e):
        return (pltpu.make_async_copy(wg_hbm.at[e], wgf, sem_w.at[0]),
                pltpu.make_async_copy(wu_hbm.at[e], wuf, sem_w.at[1]),
                pltpu.make_async_copy(wd_hbm.at[e], wdf, sem_w.at[2]))

    for cp in weight_copies(exp_ref[0]):
        cp.start()
    for b in range(nbuf - 1):
        @pl.when(b < total)
        def _(b=b):
            x_copy(b, b).start()

    def chunk(i, carry):
        slot = lax.rem(i, nbuf)
        x_copy(i, slot).wait()
        ahead = i + (nbuf - 1)

        @pl.when(ahead < total)
        def _():
            x_copy(ahead, lax.rem(ahead, nbuf)).start()

        @pl.when(newexp_ref[i] == 1)
        def _():
            for cp in weight_copies(exp_ref[i]):
                cp.wait()
            wgb[...] = wgf[...].astype(BF16)
            wub[...] = wuf[...].astype(BF16)
            wdb[...] = wdf[...].astype(BF16)

            @pl.when(nexp_ref[i] >= 0)
            def _():
                for cp in weight_copies(nexp_ref[i]):
                    cp.start()

        @pl.when(i >= nbuf)
        def _():
            y_copy(i, slot).wait()

        xr = xbuf.at[slot]
        yr = ybuf.at[slot]
        x = _load_packed(lambda sl: xr[sl, :], ch).astype(BF16)
        g = jnp.dot(x, wgb[...], preferred_element_type=F32)
        u = jnp.dot(x, wub[...], preferred_element_type=F32)
        y = jnp.dot((g * jax.nn.sigmoid(g) * u).astype(BF16), wdb[...], preferred_element_type=F32)
        _store_packed(yr, y)
        y_copy(i, slot).start()
        return carry

    lax.fori_loop(0, total, chunk, 0)

    for b in range(nbuf):
        @pl.when(b < total)
        def _(b=b):
            y_copy(0, b).wait()


def _experts(meta, xs, w_gate, w_up, w_down, after):
    e, d, f = w_gate.shape
    hbm = pl.BlockSpec(memory_space=pl.ANY)
    rows = EXPERT_CHUNK * PACK_SUBLANES
    grid_spec = pltpu.PrefetchScalarGridSpec(
        num_scalar_prefetch=len(meta), grid=(1,),
        in_specs=[hbm, hbm, hbm, hbm, hbm], out_specs=hbm,
        scratch_shapes=[pltpu.VMEM((EXPERT_RING, rows, LANES), U32), pltpu.VMEM((EXPERT_RING, rows, LANES), U32),
                        pltpu.VMEM((d, f), F32), pltpu.VMEM((d, f), F32), pltpu.VMEM((f, d), F32),
                        pltpu.VMEM((d, f), BF16), pltpu.VMEM((d, f), BF16), pltpu.VMEM((f, d), BF16),
                        pltpu.SemaphoreType.DMA((EXPERT_RING,)), pltpu.SemaphoreType.DMA((EXPERT_RING,)),
                        pltpu.SemaphoreType.DMA((3,))])
    return pl.pallas_call(
        _expert_kernel,
        grid_spec=grid_spec,
        out_shape=jax.ShapeDtypeStruct(xs.shape, xs.dtype),
        compiler_params=_params("arbitrary"),
        name="experts",
    )(*meta, xs, w_gate, w_up, w_down, after)


def _final_kernel(base_ref, yt_ref, wt_ref, mod_ref, g_ref, *rest, d):
    o_ref = rest[-1]
    t = base_ref.shape[0]
    w = wt_ref[...]
    routed = w[:, 0:1] * _load_packed(lambda sl: yt_ref[0, sl, :], t)
    for k in range(1, TOP_K):
        routed = routed + w[:, k:k + 1] * _load_packed(lambda sl, k=k: yt_ref[k, sl, :], t)
    g2 = mod_ref[0][:, 5 * d:6 * d]
    o_ref[...] = _rms(base_ref[...] + g2 * routed, g_ref[...])


def _final(base, ytok3, wts_t, mod3, tokens_per_batch, final_g, part, n_parts, prev_out):
    n, d = base.shape
    t = WIDE_TILE
    tiles_per_batch = tokens_per_batch // t
    tiles = n // t // n_parts
    off = part * tiles
    kern = functools.partial(_final_kernel, d=d)
    tok = pl.BlockSpec((t, d), lambda i: (off + i, 0))
    in_specs = [tok,
                pl.BlockSpec((TOP_K, t * PACK_SUBLANES, LANES), lambda i: (0, i, 0)),
                pl.BlockSpec((t, TOP_K), lambda i: (off + i, 0)),
                pl.BlockSpec((1, 1, mod3.shape[2]), lambda i: ((off + i) // tiles_per_batch, 0, 0)),
                _full(final_g)]
    args = [base, ytok3, wts_t, mod3, final_g]
    aliases = {}
    if prev_out is not None:
        in_specs.append(pl.BlockSpec(memory_space=pl.ANY))
        args.append(prev_out)
        aliases = {len(args) - 1: 0}
    return pl.pallas_call(
        kern,
        grid=(tiles,),
        in_specs=in_specs,
        out_specs=tok,
        out_shape=jax.ShapeDtypeStruct((n, d), F32),
        input_output_aliases=aliases,
        compiler_params=_params("arbitrary"),
        name="final",
    )(*args)


def kernel(x, c, ctx, c_ctx, w_mod, b_mod, norm1_g, norm2_g, w_in, ssm_lam_re, ssm_lam_im, ssm_log_dt, ssm_b_re, ssm_b_im, ssm_c_re, ssm_c_im, ssm_d, w_glu, w_ssm_out, conv_w, w_conv_out, w_o, w_router, router_bias, w_gate, w_up, w_down, ws_gate, ws_up, ws_down, final_g):
    b, l, d = x.shape
    n = b * l
    sw = ssm_d.shape[1]
    assert w_mod.shape[0] == 1, "single layer"
    assert b == SUBLANES and l % TOKEN_TILE == 0 and l % WIDE_TILE == 0 and (n // WIDE_TILE) % COMBINE_PARTS == 0 and l % S5_CHUNK == 0 and ctx.shape[1] % S5_CHUNK == 0 and TIME_TILE % GRID_W == 0

    mod_rows = 2 * SUBLANES
    c_all = jnp.concatenate([c, c_ctx[None, :], jnp.zeros((mod_rows - b - 1, d), F32)], axis=0)
    mod = _modulation(c_all, w_mod[0], b_mod[0])
    mod3 = mod.reshape(mod_rows, 1, mod.shape[1])

    w_in_b = w_in[0].astype(BF16)
    g1n = norm1_g[0].reshape(1, d)
    u_lat, sga, gbt = _in_proj(x, mod3, g1n, w_in_b, conv_w[0], w_conv_out[0].astype(BF16))
    u_ctx = _ctx_proj(ctx, mod3, b, g1n, w_in_b[:, :sw])

    ops = _blocked_operators(ssm_lam_re[0], ssm_lam_im[0], ssm_log_dt[0], ssm_b_re[0], ssm_b_im[0],
                             ssm_c_re[0], ssm_c_im[0])
    y = _s5_blocked(u_ctx, u_lat, ssm_d[0].reshape(1, sw), ops, b, S5_CHUNK)

    xl = _mixer_out(y, sga, gbt, x, mod3, w_glu[0].astype(BF16), w_ssm_out[0].astype(BF16),
                    w_o[0].astype(BF16))

    tiles_per_batch = l // TOKEN_TILE
    xl2 = xl.reshape(n, d)
    g2n = norm2_g[0].reshape(1, d)
    h2, eidx, wts, rank, cnt = _route(xl2, mod3, tiles_per_batch, g2n, w_router[0].T,
                                      router_bias[0].reshape(N_EXPERTS, 1))

    counts = cnt[:, 0].astype(I32)
    n_rows = n * TOP_K
    starts, meta = _chunk_metadata(counts, n_rows)
    buf_rows = n_rows + N_EXPERTS * EXPERT_CHUNK
    dest = _dest_rows(eidx, rank, jnp.broadcast_to(starts[:, None], (N_EXPERTS, LANES)))

    xs = _sc_dispatch(dest.reshape(TOP_K * n), h2.reshape(n, PACK_SUBLANES, LANES), buf_rows)
    base = _shared(xl2, mod3, l, g2n, ws_gate[0].astype(BF16), ws_up[0].astype(BF16),
                   ws_down[0].astype(BF16))
    ys = _experts(meta, xs.reshape(buf_rows * PACK_SUBLANES, LANES), w_gate[0], w_up[0], w_down[0], base)
    ys3 = ys.reshape(buf_rows, PACK_SUBLANES, LANES)

    n_part = n // COMBINE_PARTS
    wts_t = wts.T
    out = None
    for part in range(COMBINE_PARTS):
        dest_part = dest[:, part * n_part:(part + 1) * n_part].reshape(TOP_K * n_part)
        ytok = _sc_combine(dest_part, ys3, n_part)
        out = _final(base, ytok.reshape(TOP_K, n_part * PACK_SUBLANES, LANES), wts_t, mod3, l,
                     final_g.reshape(1, d), part, COMBINE_PARTS, out)
    return out.reshape(b, l, d)
```

```python
import functools

import jax
import jax.numpy as jnp
from jax import lax
from jax.experimental import pallas as pl
from jax.experimental.pallas import tpu as pltpu
from jax.experimental.pallas import tpu_sc as plsc

F32 = jnp.float32
BF16 = jnp.bfloat16
I32 = jnp.int32
U32 = jnp.uint32

EPS = 1e-6
GRID_W = 64
SSM_GROUP = 16
N_EXPERTS = 256
TOP_K = 8
N_EXPERT_GROUPS = 8
EXPERTS_PER_GROUP = N_EXPERTS // N_EXPERT_GROUPS
TOP_K_GROUPS = 4
ROUTE_SCALE = 2.5

SUBLANES = 8
LANES = 128
VMEM_LIMIT_BYTES = 48 * 1024 * 1024

TIME_TILE = 64
TOKEN_TILE = 256
WIDE_TILE = 512
TIME_BLOCK = 16
S5_CHUNK = 256
OPERATOR_GROUPS = 4
EXPERT_CHUNK = 256
EXPERT_RING = 4
SC_CHUNK = 128
COMBINE_PARTS = 2
PACK_SUBLANES = 4


def _dot(a, b):
    return jnp.dot(a.astype(BF16), b.astype(BF16), preferred_element_type=F32)


def _rms(xf, g):
    return xf * lax.rsqrt(jnp.mean(xf * xf, axis=-1, keepdims=True) + EPS) * g


def _params(*sem):
    return pltpu.CompilerParams(dimension_semantics=sem, vmem_limit_bytes=VMEM_LIMIT_BYTES)


def _full(a):
    return pl.BlockSpec(a.shape, lambda *_: (0,) * a.ndim)


def _pack_rows(v):
    half = v.shape[1] // 2
    lo = lax.bitcast_convert_type(v[:, :half].astype(BF16).astype(F32), U32) >> 16
    hi = lax.bitcast_convert_type(v[:, half:].astype(BF16).astype(F32), U32) & jnp.uint32(0xFFFF0000)
    return lo | hi


def _unpack_lo(w):
    return lax.bitcast_convert_type(w << 16, F32)


def _unpack_hi(w):
    return lax.bitcast_convert_type(w & jnp.uint32(0xFFFF0000), F32)


def _store_packed(ref, v):
    t = v.shape[0]
    w = _pack_rows(v)
    for c in range(PACK_SUBLANES):
        ref[pl.ds(c, t, stride=PACK_SUBLANES), :] = w[:, c * LANES:(c + 1) * LANES]


def _load_packed(load, t):
    ws = [load(pl.ds(c, t, stride=PACK_SUBLANES)) for c in range(PACK_SUBLANES)]
    return jnp.concatenate([_unpack_lo(w) for w in ws] + [_unpack_hi(w) for w in ws], axis=1)


def _mod_kernel(c_ref, w_ref, b_ref, o_ref):
    c = c_ref[...]
    o_ref[...] = _dot(c * jax.nn.sigmoid(c), w_ref[...]) + b_ref[...]


def _modulation(c_all, w_mod, b_mod):
    rows, d = c_all.shape
    cols = w_mod.shape[1]
    blk = 1536
    return pl.pallas_call(
        _mod_kernel,
        grid=(cols // blk,),
        in_specs=[pl.BlockSpec((rows, d), lambda j: (0, 0)),
                  pl.BlockSpec((d, blk), lambda j: (0, j)),
                  pl.BlockSpec((1, blk), lambda j: (0, j))],
        out_specs=pl.BlockSpec((rows, blk), lambda j: (0, j)),
        out_shape=jax.ShapeDtypeStruct((rows, cols), F32),
        compiler_params=_params("arbitrary"),
        name="mod",
    )(c_all, w_mod, b_mod.reshape(1, cols))


def _per_row(m3, lo, hi, tt):
    nb = m3.shape[0]
    return jnp.broadcast_to(m3[:, :, lo:hi], (nb, tt, hi - lo)).reshape(nb * tt, hi - lo)


def _to_time_major(val, nb, tt):
    c = val.shape[1]
    return pltpu.einshape("btc->tbc", val.reshape(nb, tt, c)).reshape(nb * tt, c)


def _to_batch_major(val, nb, tt):
    c = val.shape[1]
    return pltpu.einshape("tbc->btc", val.reshape(tt, nb, c)).reshape(nb * tt, c)


def _in_proj_kernel(x_ref, mod_ref, g_ref, w_ref, cw_ref, wco_ref, u_ref, sga_ref, gbt_ref, *, d, sw):
    nb, tt, _ = x_ref.shape
    rows = nb * tt
    m3 = mod_ref[...]
    x = x_ref[...].reshape(rows, d)
    h = _rms(x, g_ref[...]) * (1.0 + _per_row(m3, d, 2 * d, tt)) + _per_row(m3, 0, d, tt)
    hb = h.astype(BF16)
    u_ref[...] = _to_time_major(jnp.dot(hb, w_ref[:, 0:sw], preferred_element_type=F32), nb, tt)
    cb = jnp.dot(hb, w_ref[:, sw:2 * sw], preferred_element_type=F32)
    cc = jnp.dot(hb, w_ref[:, 2 * sw:3 * sw], preferred_element_type=F32)
    cv = jnp.dot(hb, w_ref[:, 3 * sw:4 * sw], preferred_element_type=F32)
    ccv = cc * cv
    col = lax.broadcasted_iota(I32, ccv.shape, 0) % GRID_W
    prev = jnp.where(col == 0, 0.0, pltpu.roll(ccv, 1, axis=0))
    nxt = jnp.where(col == GRID_W - 1, 0.0, pltpu.roll(ccv, rows - 1, axis=0))
    cw = cw_ref[...]
    conv = prev * cw[0:1, :] + ccv * cw[1:2, :] + nxt * cw[2:3, :]
    y_conv = _dot(cb * conv, wco_ref[...])
    ga = jnp.dot(hb, w_ref[:, 4 * sw:4 * sw + d], preferred_element_type=F32)
    gb = jnp.dot(hb, w_ref[:, 4 * sw + d:4 * sw + 2 * d], preferred_element_type=F32)
    sga_ref[...] = jax.nn.sigmoid(ga).reshape(nb, tt, d)
    gbt_ref[...] = (jax.nn.sigmoid(gb) * y_conv).reshape(nb, tt, d)


def _in_proj(x, mod3, g1n, w_in_b, conv_w, w_conv_out_b):
    b, l, d = x.shape
    sw = conv_w.shape[1]
    tt = TIME_TILE
    kern = functools.partial(_in_proj_kernel, d=d, sw=sw)
    tok = pl.BlockSpec((b, tt, d), lambda j: (0, j, 0))
    return pl.pallas_call(
        kern,
        grid=(l // tt,),
        in_specs=[tok,
                  pl.BlockSpec((b, 1, mod3.shape[2]), lambda j: (0, 0, 0)),
                  _full(g1n), _full(w_in_b), _full(conv_w), _full(w_conv_out_b)],
        out_specs=[pl.BlockSpec((tt * b, sw), lambda j: (j, 0)), tok, tok],
        out_shape=[jax.ShapeDtypeStruct((l * b, sw), F32),
                   jax.ShapeDtypeStruct((b, l, d), F32),
                   jax.ShapeDtypeStruct((b, l, d), F32)],
        compiler_params=_params("arbitrary"),
        name="in_proj",
    )(x, mod3, g1n, w_in_b, conv_w, w_conv_out_b)


def _ctx_proj_kernel(x_ref, mod_ref, g_ref, w_ref, u_ref, *, d):
    nb, tt, _ = x_ref.shape
    m = mod_ref[0]
    x = x_ref[...].reshape(nb * tt, d)
    h = _rms(x, g_ref[...]) * (1.0 + m[:, d:2 * d]) + m[:, 0:d]
    u_ref[...] = _to_time_major(_dot(h, w_ref[...]), nb, tt)


def _ctx_proj(ctx, mod3, ctx_row, g1n, w_u_b):
    b, lc, d = ctx.shape
    sw = w_u_b.shape[1]
    tt = TIME_TILE
    kern = functools.partial(_ctx_proj_kernel, d=d)
    return pl.pallas_call(
        kern,
        grid=(lc // tt,),
        in_specs=[pl.BlockSpec((b, tt, d), lambda j: (0, j, 0)),
                  pl.BlockSpec((1, 1, mod3.shape[2]), lambda j: (ctx_row, 0, 0)),
                  _full(g1n), _full(w_u_b)],
        out_specs=pl.BlockSpec((tt * b, sw), lambda j: (j, 0)),
        out_shape=jax.ShapeDtypeStruct((lc * b, sw), F32),
        compiler_params=_params("arbitrary"),
        name="ctx_proj",
    )(ctx, mod3, g1n, w_u_b)


_NT = (((1,), (1,)), ((), ()))


def _dot3_nt(a, b):
    a_hi, a_lo = _split_bf16(a)
    b_hi, b_lo = _split_bf16(b)
    return (lax.dot_general(a_hi, b_hi, _NT, preferred_element_type=F32)
            + lax.dot_general(a_hi, b_lo, _NT, preferred_element_type=F32)
            + lax.dot_general(a_lo, b_hi, _NT, preferred_element_type=F32))


def _operator_kernel(*refs):
    n_in = 6
    for gi in range(refs[0].shape[1]):
        _operator_group(*[r.at[:, pl.ds(gi, 1)] for r in refs[:n_in]], *[o.at[pl.ds(gi, 1)] for o in refs[n_in:]])


def _operator_group(akr_ref, aki_ref, bbr_ref, bbi_ref, cr_ref, ci_ref, tfb_ref, bst_ref, cst_ref):
    tb = TIME_BLOCK
    q, p = bbr_ref.shape[2], bbr_ref.shape[3]
    w = tb * q

    def times_ak(d, ks, xr, xi):
        re, im = [], []
        for k in ks:
            ar, ai = akr_ref[d, 0, k:k + 1, :], aki_ref[d, 0, k:k + 1, :]
            re.append(ar * xr - ai * xi)
            im.append(ar * xi + ai * xr)
        return jnp.concatenate(re, axis=0), jnp.concatenate(im, axis=0)

    col = lax.broadcasted_iota(I32, (p, w), 1)
    row = lax.broadcasted_iota(I32, (p, w), 0)

    def place(quarter):
        return (col == row + quarter * p).astype(BF16)

    inj, ca = [], []
    for d in range(2):
        inj += list(times_ak(d, range(tb - 1, -1, -1) if d == 0 else range(tb), bbr_ref[d, 0], bbi_ref[d, 0]))
        re, im = times_ak(d, range(1, tb + 1) if d == 0 else range(tb, 0, -1), cr_ref[d, 0], ci_ref[d, 0])
        ca += [re, -im]
    bst = jnp.zeros((w, w), F32)
    cst = jnp.zeros((w, w), F32)
    for quarter in range(4):
        pm = place(quarter)
        bst = bst + jnp.dot(inj[quarter].astype(BF16), pm, preferred_element_type=F32)
        cst = cst + lax.dot_general(pm, ca[quarter].astype(BF16), (((0,), (1,)), ((), ())),
                                    preferred_element_type=F32)
    bst_ref[0] = bst.astype(BF16)
    cst_ref[0] = cst.astype(BF16)

    sub = lax.broadcasted_iota(I32, (q, w), 0)
    lane = lax.broadcasted_iota(I32, (q, w), 1)
    lag_rows = []
    for d in range(2):
        abr, abi = times_ak(d, range(tb), bbr_ref[d, 0], bbi_ref[d, 0])
        m = _dot3_nt(abr, cr_ref[d, 0]) - _dot3_nt(abi, ci_ref[d, 0])
        acc = jnp.zeros((q, w), F32)
        for k in range(tb):
            blk = tb - 1 - k if d == 1 else k
            put = ((lane - blk * q) == sub).astype(BF16)
            acc = acc + jnp.dot(m[k * q:(k + 1) * q, :].astype(BF16), put, preferred_element_type=F32)
        lag_rows.append(acc)
    rows_f, rows_b = lag_rows

    for s in range(tb):
        right = s * q
        left = (tb - 1 - s) * q
        fwd = jnp.where(lane >= right, pltpu.roll(rows_f, right, axis=1) if right else rows_f, 0.0)
        bwd = jnp.where(lane < w - left, pltpu.roll(rows_b, w - left, axis=1) if left else rows_b, 0.0)
        tfb_ref[0, s * q:(s + 1) * q, :] = (fwd + bwd).astype(BF16)


def _blocked_operators(lam_re, lam_im, log_dt, b_re, b_im, c_re, c_im):
    tb = TIME_BLOCK
    dt = jnp.exp(log_dt)[..., None]
    k = jnp.arange(tb + 1, dtype=F32)[None, None, :, None]
    mag = jnp.exp(lam_re * dt)[:, :, None, :] ** k
    ang = (lam_im * dt)[:, :, None, :] * k
    ak_re = mag * jnp.cos(ang)
    ak_im = mag * jnp.sin(ang)
    a_re, a_im = ak_re[:, :, 1], ak_im[:, :, 1]
    den = lam_re * lam_re + lam_im * lam_im
    k_re = ((a_re - 1.0) * lam_re + a_im * lam_im) / den
    k_im = (a_im * lam_re - (a_re - 1.0) * lam_im) / den
    bt_re = jnp.swapaxes(b_re, 2, 3)
    bt_im = jnp.swapaxes(b_im, 2, 3)
    bbt_re = k_re[:, :, None, :] * bt_re - k_im[:, :, None, :] * bt_im
    bbt_im = k_re[:, :, None, :] * bt_im + k_im[:, :, None, :] * bt_re
    nd, g, q, p = bbt_re.shape
    w = tb * q

    gb = OPERATOR_GROUPS
    per_group = lambda a: pl.BlockSpec((nd, gb) + a.shape[2:], lambda i: (0, i, 0, 0))
    out = pl.BlockSpec((gb, w, w), lambda i: (i, 0, 0))
    ins = (ak_re, ak_im, bbt_re, bbt_im, c_re, c_im)
    tfb, bst, cst = pl.pallas_call(
        _operator_kernel,
        grid=(g // gb,),
        in_specs=[per_group(a) for a in ins],
        out_specs=[out, out, out],
        out_shape=[jax.ShapeDtypeStruct((g, w, w), BF16)] * 3,
        compiler_params=_params("arbitrary"),
        name="s5_operators",
    )(*ins)
    ax = jnp.concatenate([ak_re[:, :, tb], ak_re[:, :, tb]], axis=-1).reshape(nd, 1, g * 2 * p)
    ay = jnp.concatenate([-ak_im[:, :, tb], ak_im[:, :, tb]], axis=-1).reshape(nd, 1, g * 2 * p)
    return tfb, bst, cst, ax, ay


def _lane_chunk(shape):
    return lax.broadcasted_iota(I32, shape, 1) // SSM_GROUP


def _regroup_kernel(uc_ref, ul_ref, bst_ref, xg_ref, e_ref, *, n_ctx, nb, groups, q):
    j = pl.program_id(0)
    rows_in = uc_ref.shape[0]
    nblk = rows_in // (TIME_BLOCK * nb)
    rows = nblk * nb
    u = jnp.where(j < n_ctx, uc_ref[...], ul_ref[...])
    u3 = u.reshape(nblk, TIME_BLOCK * nb, u.shape[1])
    xs = [u3[:, s * nb:(s + 1) * nb, :].reshape(rows, u.shape[1]) for s in range(TIME_BLOCK)]
    per_tile = LANES // q
    chunk = _lane_chunk((rows, LANES))
    for g in range(groups):
        jt, qq = divmod(g, per_tile)
        for h in range(TIME_BLOCK // per_tile):
            acc = jnp.zeros((rows, LANES), F32)
            for s in range(h * per_tile, (h + 1) * per_tile):
                pos = s % per_tile
                a = xs[s][:, jt * LANES:(jt + 1) * LANES]
                shift = ((pos - qq) % per_tile) * q
                r = pltpu.roll(a, shift, axis=1) if shift else a
                acc = jnp.where(chunk == pos, r, acc)
            xg_ref[:, (2 * g + h) * LANES:(2 * g + h + 1) * LANES] = acc.astype(BF16)
    gw = TIME_BLOCK * q
    for g in range(groups):
        e = jnp.dot(xg_ref[:, g * gw:(g + 1) * gw], bst_ref[g], preferred_element_type=F32)
        half = e.shape[1] // 2
        e_ref[0, :, g * half:(g + 1) * half] = e[:, :half]
        e_ref[1, :, g * half:(g + 1) * half] = e[:, half:]


def _s5_regroup(u_ctx, u_lat, bst, nb, tc):
    rows_c, sw = u_ctx.shape
    rows_l = u_lat.shape[0]
    rc = tc * nb
    n_ctx, n_lat = rows_c // rc, rows_l // rc
    groups, gw, p4 = bst.shape
    q = gw // TIME_BLOCK
    rows_out = (tc // TIME_BLOCK) * nb
    total_rows = (n_ctx + n_lat) * rows_out
    kern = functools.partial(_regroup_kernel, n_ctx=n_ctx, nb=nb, groups=groups, q=q)
    return pl.pallas_call(
        kern,
        grid=(n_ctx + n_lat,),
        in_specs=[pl.BlockSpec((rc, sw), lambda j: (jnp.minimum(j, n_ctx - 1), 0)),
                  pl.BlockSpec((rc, sw), lambda j: (jnp.maximum(j - n_ctx, 0), 0)),
                  pl.BlockSpec(bst.shape, lambda j: (0, 0, 0), pipeline_mode=pl.Buffered(1))],
        out_specs=[pl.BlockSpec((rows_out, groups * gw), lambda j: (j, 0)),
                   pl.BlockSpec((2, rows_out, groups * p4 // 2), lambda j: (0, j, 0))],
        out_shape=[jax.ShapeDtypeStruct((total_rows, groups * gw), BF16),
                   jax.ShapeDtypeStruct((2, total_rows, groups * p4 // 2), F32)],
        compiler_params=_params("arbitrary"),
        name="s5_regroup",
    )(u_ctx, u_lat, bst)


def _carry_kernel(e_ref, ax_ref, ay_ref, s_ref, st_ref, *, nb):
    dirn = pl.program_id(0)
    j = pl.program_id(1)
    rows = e_ref.shape[1]
    nblk = rows // nb
    lanes = e_ref.shape[2]

    @pl.when(j == 0)
    def _():
        st_ref[...] = jnp.zeros_like(st_ref)

    ax = jnp.broadcast_to(ax_ref[0], (nb, lanes))
    ay = jnp.broadcast_to(ay_ref[0], (nb, lanes))

    def body(i, s):
        blk = jnp.where(dirn == 0, i, nblk - 1 - i)
        r0 = pl.multiple_of(blk * nb, nb)
        s_ref[0, pl.ds(r0, nb), :] = s.astype(s_ref.dtype)
        swapped = jnp.concatenate([pltpu.roll(s[:, t * LANES:(t + 1) * LANES], LANES // 2, axis=1)
                                   for t in range(lanes // LANES)], axis=1)
        return ax * s + ay * swapped + e_ref[0, pl.ds(r0, nb), :]

    st_ref[...] = lax.fori_loop(0, nblk, body, st_ref[...])


def _s5_carry(e, ax, ay, nb, n_ctx_chunks, rows_per_chunk):
    nd, total_rows, lanes = e.shape
    n_chunks = total_rows // rows_per_chunk
    n_lat = n_chunks - n_ctx_chunks

    def idx(d, j):
        jc = jnp.minimum(j, n_ctx_chunks - 1)
        jl = jnp.maximum(j - n_ctx_chunks, 0)
        fwd = j
        bwd = jnp.where(j < n_ctx_chunks, n_ctx_chunks - 1 - jc, n_chunks - 1 - jl)
        return (d, jnp.where(d == 0, fwd, bwd), 0)

    kern = functools.partial(_carry_kernel, nb=nb)
    return pl.pallas_call(
        kern,
        grid=(nd, n_chunks),
        in_specs=[pl.BlockSpec((1, rows_per_chunk, lanes), idx),
                  pl.BlockSpec((1, 1, lanes), lambda d, j: (d, 0, 0)),
                  pl.BlockSpec((1, 1, lanes), lambda d, j: (d, 0, 0))],
        out_specs=pl.BlockSpec((1, rows_per_chunk, lanes), idx),
        out_shape=jax.ShapeDtypeStruct(e.shape, F32),
        scratch_shapes=[pltpu.VMEM((nb, lanes), F32)],
        compiler_params=_params("arbitrary", "arbitrary"),
        name="s5_carry",
    )(e, ax, ay)


def _output_kernel(xg_ref, s_ref, u_ref, d_ref, tfb_ref, cst_ref, y_ref, yg_ref, *, nb, groups, q):
    rows = xg_ref.shape[0]
    nblk = rows // nb
    gw = TIME_BLOCK * q
    half = s_ref.shape[2] // groups
    for g in range(groups):
        sg = jnp.concatenate([s_ref[0, :, g * half:(g + 1) * half], s_ref[1, :, g * half:(g + 1) * half]], axis=1)
        yg_ref[:, g * gw:(g + 1) * gw] = (
            jnp.dot(xg_ref[:, g * gw:(g + 1) * gw], tfb_ref[g], preferred_element_type=F32)
            + jnp.dot(sg.astype(BF16), cst_ref[g], preferred_element_type=F32))
    per_tile = LANES // q
    chunk = _lane_chunk((rows, LANES))
    sw = groups * q
    steps = []
    for s in range(TIME_BLOCK):
        h, pos = divmod(s, per_tile)
        tiles = []
        for jt in range(sw // LANES):
            acc = jnp.zeros((rows, LANES), F32)
            for qq in range(per_tile):
                g = jt * per_tile + qq
                a = yg_ref[:, (2 * g + h) * LANES:(2 * g + h + 1) * LANES]
                shift = ((qq - pos) % per_tile) * q
                r = pltpu.roll(a, shift, axis=1) if shift else a
                acc = jnp.where(chunk == qq, r, acc)
            tiles.append(acc)
        steps.append(jnp.concatenate(tiles, axis=1).reshape(nblk, nb, sw))
    y = jnp.concatenate(steps, axis=1).reshape(nblk * TIME_BLOCK * nb, sw)
    y_ref[...] = y + d_ref[...] * u_ref[...]


def _s5_outputs(xg, s, u_lat, ssm_d, tfb, cst, nb, tc, n_ctx_chunks):
    rows_l, sw = u_lat.shape
    rc = tc * nb
    n_lat = rows_l // rc
    groups, gw, _ = tfb.shape
    q = gw // TIME_BLOCK
    rows_out = (tc // TIME_BLOCK) * nb
    kern = functools.partial(_output_kernel, nb=nb, groups=groups, q=q)
    return pl.pallas_call(
        kern,
        grid=(n_lat,),
        in_specs=[pl.BlockSpec((rows_out, groups * gw), lambda j: (j + n_ctx_chunks, 0)),
                  pl.BlockSpec((2, rows_out, s.shape[2]), lambda j: (0, j + n_ctx_chunks, 0)),
                  pl.BlockSpec((rc, sw), lambda j: (j, 0)),
                  pl.BlockSpec((1, sw), lambda j: (0, 0)),
                  pl.BlockSpec(tfb.shape, lambda j: (0, 0, 0), pipeline_mode=pl.Buffered(1)),
                  pl.BlockSpec(cst.shape, lambda j: (0, 0, 0), pipeline_mode=pl.Buffered(1))],
        out_specs=pl.BlockSpec((rc, sw), lambda j: (j, 0)),
        out_shape=jax.ShapeDtypeStruct((rows_l, sw), F32),
        scratch_shapes=[pltpu.VMEM((rows_out, groups * gw), F32)],
        compiler_params=_params("arbitrary"),
        name="s5_output",
    )(xg, s, u_lat, ssm_d, tfb, cst)


def _s5_blocked(u_ctx, u_lat, ssm_d, ops, nb, tc):
    tfb, bst, cst, ax, ay = ops
    n_ctx_chunks = u_ctx.shape[0] // (tc * nb)
    xg, e = _s5_regroup(u_ctx, u_lat, bst, nb, tc)
    s = _s5_carry(e, ax, ay, nb, n_ctx_chunks, (tc // TIME_BLOCK) * nb)
    return _s5_outputs(xg, s, u_lat, ssm_d, tfb, cst, nb, tc, n_ctx_chunks)


def _mixer_kernel(y_ref, sga_ref, gbt_ref, x_ref, mod_ref, wglu_ref, wso_ref, wo_ref, o_ref, *, d, sw):
    nb, tt, _ = x_ref.shape
    rows = nb * tt
    y = _to_batch_major(y_ref[...], nb, tt)
    v = _dot(jax.nn.gelu(y), wglu_ref[...])
    ys = v[:, 0:sw] * jax.nn.sigmoid(v[:, sw:2 * sw])
    y_a = _dot(ys, wso_ref[...])
    merged = sga_ref[...].reshape(rows, d) * y_a + gbt_ref[...].reshape(rows, d)
    o = _dot(merged, wo_ref[...])
    g1 = _per_row(mod_ref[...], 2 * d, 3 * d, tt)
    o_ref[...] = (x_ref[...].reshape(rows, d) + g1 * o).reshape(nb, tt, d)


def _mixer_out(y, sga, gbt, x, mod3, w_glu_b, w_ssm_out_b, w_o_b):
    b, l, d = x.shape
    sw = y.shape[1]
    tt = TIME_TILE
    kern = functools.partial(_mixer_kernel, d=d, sw=sw)
    tok = pl.BlockSpec((b, tt, d), lambda j: (0, j, 0))
    return pl.pallas_call(
        kern,
        grid=(l // tt,),
        in_specs=[pl.BlockSpec((tt * b, sw), lambda j: (j, 0)),
                  tok, tok, tok,
                  pl.BlockSpec((b, 1, mod3.shape[2]), lambda j: (0, 0, 0)),
                  _full(w_glu_b), _full(w_ssm_out_b), _full(w_o_b)],
        out_specs=tok,
        out_shape=jax.ShapeDtypeStruct((b, l, d), F32),
        compiler_params=_params("arbitrary"),
        name="mixer_out",
    )(y, sga, gbt, x, mod3, w_glu_b, w_ssm_out_b, w_o_b)


def _split_bf16(a):
    hi = a.astype(BF16)
    lo = (a - hi.astype(F32)).astype(BF16)
    return hi, lo


def _norm2(xl, m, g, d):
    return _rms(xl, g) * (1.0 + m[:, 4 * d:5 * d]) + m[:, 3 * d:4 * d]


def _shared_kernel(xl_ref, mod_ref, g_ref, wsg_ref, wsu_ref, wsd_ref, base_ref, *, d):
    m = mod_ref[0]
    xl = xl_ref[...]
    hb = _norm2(xl, m, g_ref[...], d).astype(BF16)
    sg = jnp.dot(hb, wsg_ref[...], preferred_element_type=F32)
    su = jnp.dot(hb, wsu_ref[...], preferred_element_type=F32)
    shared = _dot(sg * jax.nn.sigmoid(sg) * su, wsd_ref[...])
    base_ref[...] = xl + m[:, 5 * d:6 * d] * shared


def _shared(xl2, mod3, tokens_per_batch, g2n, ws_gate_b, ws_up_b, ws_down_b):
    n, d = xl2.shape
    t = WIDE_TILE
    tiles_per_batch = tokens_per_batch // t
    tok = pl.BlockSpec((t, d), lambda i: (i, 0))
    return pl.pallas_call(
        functools.partial(_shared_kernel, d=d),
        grid=(n // t,),
        in_specs=[tok,
                  pl.BlockSpec((1, 1, mod3.shape[2]), lambda i: (i // tiles_per_batch, 0, 0)),
                  _full(g2n), _full(ws_gate_b), _full(ws_up_b), _full(ws_down_b)],
        out_specs=tok,
        out_shape=jax.ShapeDtypeStruct((n, d), F32),
        compiler_params=_params("arbitrary"),
        name="shared",
    )(xl2, mod3, g2n, ws_gate_b, ws_up_b, ws_down_b)


def _route_kernel(xl_ref, mod_ref, g_ref, whi_ref, wlo_ref, rb_ref,
                  h2_ref, eidx_ref, wts_ref, rank_ref, cnt_ref, carry_ref, *, d):
    i = pl.program_id(0)
    t = xl_ref.shape[0]
    h2 = _norm2(xl_ref[...], mod_ref[0], g_ref[...], d)
    _store_packed(h2_ref, h2)

    h_hi, h_lo = _split_bf16(h2)
    w_hi, w_lo = whi_ref[...], wlo_ref[...]
    nt = (((1,), (1,)), ((), ()))
    logits = (lax.dot_general(w_hi, h_hi, nt, preferred_element_type=F32)
              + lax.dot_general(w_hi, h_lo, nt, preferred_element_type=F32)
              + lax.dot_general(w_lo, h_hi, nt, preferred_element_type=F32))
    scores = jax.nn.sigmoid(logits)
    choice = scores + rb_ref[...]

    epg = EXPERTS_PER_GROUP
    gi = lax.broadcasted_iota(I32, (epg, t), 0)
    gs = []
    for g in range(N_EXPERT_GROUPS):
        seg = choice[g * epg:(g + 1) * epg, :]
        m1 = jnp.max(seg, axis=0, keepdims=True)
        i1 = jnp.min(jnp.where(seg == m1, gi, epg), axis=0, keepdims=True)
        m2 = jnp.max(jnp.where(gi == i1, -jnp.inf, seg), axis=0, keepdims=True)
        gs.append(m1 + m2)
    masked = []
    for g in range(N_EXPERT_GROUPS):
        beat = jnp.zeros((1, t), I32)
        for g2 in range(N_EXPERT_GROUPS):
            if g2 < g:
                beat = beat + (gs[g2] >= gs[g]).astype(I32)
            elif g2 > g:
                beat = beat + (gs[g2] > gs[g]).astype(I32)
        keep = beat < TOP_K_GROUPS
        masked.append(jnp.where(keep, choice[g * epg:(g + 1) * epg, :], -jnp.inf))
    cur = jnp.concatenate(masked, axis=0)

    ei_all = lax.broadcasted_iota(I32, (N_EXPERTS, t), 0)
    picks, raw = [], []
    onehot = jnp.zeros((N_EXPERTS, t), F32)
    for _ in range(TOP_K):
        mx = jnp.max(cur, axis=0, keepdims=True)
        ei = jnp.min(jnp.where(cur == mx, ei_all, N_EXPERTS), axis=0, keepdims=True)
        hit = ei_all == ei
        raw.append(jnp.sum(jnp.where(hit, scores, 0.0), axis=0, keepdims=True))
        cur = jnp.where(hit, -jnp.inf, cur)
        onehot = jnp.where(hit, 1.0, onehot)
        picks.append(ei)
    tot = raw[0]
    for k in range(1, TOP_K):
        tot = tot + raw[k]

    @pl.when(i == 0)
    def _():
        carry_ref[...] = jnp.zeros_like(carry_ref)

    upper = (lax.broadcasted_iota(I32, (t, t), 0) < lax.broadcasted_iota(I32, (t, t), 1)).astype(BF16)
    before = jnp.dot(onehot.astype(BF16), upper, preferred_element_type=F32) + carry_ref[:, 0:1]
    for k in range(TOP_K):
        eidx_ref[k:k + 1, :] = picks[k]
        wts_ref[k:k + 1, :] = raw[k] / tot * ROUTE_SCALE
        rk = jnp.sum(jnp.where(ei_all == picks[k], before, 0.0), axis=0, keepdims=True)
        rank_ref[k:k + 1, :] = rk.astype(I32)
    carry_ref[...] = carry_ref[...] + jnp.sum(onehot, axis=1, keepdims=True)
    cnt_ref[...] = carry_ref[...]


def _route(xl2, mod3, tiles_per_batch, g2n, w_router_t, router_bias):
    n, d = xl2.shape
    t = TOKEN_TILE
    e = w_router_t.shape[0]
    w_top = lax.bitcast_convert_type(lax.bitcast_convert_type(w_router_t, U32) & jnp.uint32(0xFFFF0000), F32)
    w_hi, w_lo = w_top.astype(BF16), (w_router_t - w_top).astype(BF16)
    kern = functools.partial(_route_kernel, d=d)
    tok = pl.BlockSpec((t, d), lambda i: (i, 0))
    small = pl.BlockSpec((TOP_K, t), lambda i: (0, i))
    return pl.pallas_call(
        kern,
        grid=(n // t,),
        in_specs=[tok,
                  pl.BlockSpec((1, 1, mod3.shape[2]), lambda i: (i // tiles_per_batch, 0, 0)),
                  _full(g2n), _full(w_hi), _full(w_lo), _full(router_bias)],
        out_specs=[pl.BlockSpec((t * PACK_SUBLANES, LANES), lambda i: (i, 0)), small, small, small,
                   pl.BlockSpec((e, LANES), lambda i: (0, 0))],
        out_shape=[jax.ShapeDtypeStruct((n * PACK_SUBLANES, LANES), U32),
                   jax.ShapeDtypeStruct((TOP_K, n), I32),
                   jax.ShapeDtypeStruct((TOP_K, n), F32),
                   jax.ShapeDtypeStruct((TOP_K, n), I32),
                   jax.ShapeDtypeStruct((e, LANES), F32)],
        scratch_shapes=[pltpu.VMEM((e, LANES), F32)],
        compiler_params=_params("arbitrary"),
        name="route",
    )(xl2, mod3, g2n, w_hi, w_lo, router_bias)


def _dest_kernel(eidx_ref, rank_ref, start_ref, o_ref):
    t = eidx_ref.shape[1]
    ei_all = lax.broadcasted_iota(I32, (N_EXPERTS, t), 0)
    st = start_ref[:, 0:1]
    for k in range(TOP_K):
        hit = ei_all == eidx_ref[k:k + 1, :]
        o_ref[k:k + 1, :] = jnp.sum(jnp.where(hit, st, 0), axis=0, keepdims=True) + rank_ref[k:k + 1, :]


def _dest_rows(eidx, rank, starts_b):
    n = eidx.shape[1]
    t = 1024
    small = pl.BlockSpec((TOP_K, t), lambda i: (0, i))
    return pl.pallas_call(
        _dest_kernel,
        grid=(n // t,),
        in_specs=[small, small, _full(starts_b)],
        out_specs=small,
        out_shape=jax.ShapeDtypeStruct((TOP_K, n), I32),
        compiler_params=_params("arbitrary"),
        name="dest",
    )(eidx, rank, starts_b)


def _sc_workers():
    info = plsc.get_sparse_core_info()
    return info.num_cores, info.num_cores * info.num_subcores


def _sc_dispatch(dest_flat, src3, out_rows):
    n = src3.shape[0]
    n_cores, n_workers = _sc_workers()
    per_worker = n // n_workers
    assert per_worker % SC_CHUNK == 0
    mesh = plsc.VectorSubcoreMesh(core_axis_name="c", subcore_axis_name="s")

    @functools.partial(
        pl.kernel, mesh=mesh,
        out_type=jax.ShapeDtypeStruct((out_rows,) + src3.shape[1:], src3.dtype),
        scratch_types=[pltpu.VMEM((SC_CHUNK,) + src3.shape[1:], src3.dtype)]
        + [pltpu.VMEM((SC_CHUNK,), I32)] * TOP_K,
        name="dispatch",
    )
    def run(dest_hbm, src_hbm, out_hbm, rows_v, *idx_v):
        worker = lax.axis_index("s") * n_cores + lax.axis_index("c")

        @pl.loop(0, per_worker // SC_CHUNK)
        def _(j):
            base = worker * per_worker + j * SC_CHUNK
            pltpu.sync_copy(src_hbm.at[pl.ds(base, SC_CHUNK)], rows_v)
            for k in range(TOP_K):
                pltpu.sync_copy(dest_hbm.at[pl.ds(k * n + base, SC_CHUNK)], idx_v[k])
            for k in range(TOP_K):
                pltpu.sync_copy(rows_v, out_hbm.at[idx_v[k]])

    return run(dest_flat, src3)


def _sc_combine(dest_flat, ys3, n):
    n_cores, n_workers = _sc_workers()
    per_worker = n // n_workers
    assert per_worker % SC_CHUNK == 0
    mesh = plsc.VectorSubcoreMesh(core_axis_name="c", subcore_axis_name="s")

    @functools.partial(
        pl.kernel, mesh=mesh,
        out_type=jax.ShapeDtypeStruct((TOP_K * n,) + ys3.shape[1:], ys3.dtype),
        scratch_types=[pltpu.VMEM((SC_CHUNK,) + ys3.shape[1:], ys3.dtype), pltpu.VMEM((SC_CHUNK,), I32)],
        name="combine",
    )
    def run(dest_hbm, ys_hbm, out_hbm, rows_v, idx_v):
        worker = lax.axis_index("s") * n_cores + lax.axis_index("c")

        @pl.loop(0, per_worker // SC_CHUNK)
        def _(j):
            base = worker * per_worker + j * SC_CHUNK
            for k in range(TOP_K):
                pltpu.sync_copy(dest_hbm.at[pl.ds(k * n + base, SC_CHUNK)], idx_v)
                pltpu.sync_copy(ys_hbm.at[idx_v], rows_v)
                pltpu.sync_copy(rows_v, out_hbm.at[pl.ds(k * n + base, SC_CHUNK)])

    return run(dest_flat, ys3)


def _chunk_metadata(counts, n_rows):
    ch = EXPERT_CHUNK
    n_ch = (counts + ch - 1) // ch
    cum = jnp.cumsum(n_ch)
    total = cum[-1]
    max_chunks = n_rows // ch + N_EXPERTS
    i = jnp.arange(max_chunks, dtype=I32)
    e = jnp.sum((cum[None, :] <= i[:, None]).astype(I32), axis=1)
    e_last = jnp.max(jnp.where(counts > 0, jnp.arange(N_EXPERTS, dtype=I32), 0))
    exp = jnp.where(i < total, jnp.minimum(e, N_EXPERTS - 1), e_last).astype(I32)
    newexp = jnp.concatenate([jnp.ones((1,), I32), (exp[1:] != exp[:-1]).astype(I32)])
    change_at = jnp.where(newexp == 1, i, max_chunks)
    nxt_change = lax.cummin(jnp.concatenate([change_at[1:], jnp.array([max_chunks], I32)]), reverse=True)
    nexp = jnp.where(nxt_change < max_chunks, exp[jnp.minimum(nxt_change, max_chunks - 1)], -1).astype(I32)
    starts = (cum - n_ch) * ch
    return starts.astype(I32), (exp, newexp, nexp, total.astype(I32).reshape(1))


def _expert_kernel(exp_ref, newexp_ref, nexp_ref, total_ref, xs_hbm, wg_hbm, wu_hbm, wd_hbm, after_hbm, ys_hbm,
                   xbuf, ybuf, wgf, wuf, wdf, wgb, wub, wdb, sem_x, sem_y, sem_w):
    del after_hbm
    ch = EXPERT_CHUNK
    nbuf = EXPERT_RING
    rows = ch * PACK_SUBLANES
    total = total_ref[0]

    def x_copy(i, slot):
        r0 = pl.multiple_of(i * rows, rows)
        return pltpu.make_async_copy(xs_hbm.at[pl.ds(r0, rows)], xbuf.at[slot], sem_x.at[slot])

    def y_copy(i, slot):
        r0 = pl.multiple_of(i * rows, rows)
        return pltpu.make_async_copy(ybuf.at[slot], ys_hbm.at[pl.ds(r0, rows)], sem_y.at[slot])

    def weight_copies(e):
        return (pltpu.make_async_copy(wg_hbm.at[e], wgf, sem_w.at[0]),
                pltpu.make_async_copy(wu_hbm.at[e], wuf, sem_w.at[1]),
                pltpu.make_async_copy(wd_hbm.at[e], wdf, sem_w.at[2]))

    for cp in weight_copies(exp_ref[0]):
        cp.start()
    for b in range(nbuf - 1):
        @pl.when(b < total)
        def _(b=b):
            x_copy(b, b).start()

    def chunk(i, carry):
        slot = lax.rem(i, nbuf)
        x_copy(i, slot).wait()
        ahead = i + (nbuf - 1)

        @pl.when(ahead < total)
        def _():
            x_copy(ahead, lax.rem(ahead, nbuf)).start()

        @pl.when(newexp_ref[i] == 1)
        def _():
            for cp in weight_copies(exp_ref[i]):
                cp.wait()
            wgb[...] = wgf[...].astype(BF16)
            wub[...] = wuf[...].astype(BF16)
            wdb[...] = wdf[...].astype(BF16)

            @pl.when(nexp_ref[i] >= 0)
            def _():
                for cp in weight_copies(nexp_ref[i]):
                    cp.start()

        @pl.when(i >= nbuf)
        def _():
            y_copy(i, slot).wait()

        xr = xbuf.at[slot]
        yr = ybuf.at[slot]
        x = _load_packed(lambda sl: xr[sl, :], ch).astype(BF16)
        g = jnp.dot(x, wgb[...], preferred_element_type=F32)
        u = jnp.dot(x, wub[...], preferred_element_type=F32)
        y = jnp.dot((g * jax.nn.sigmoid(g) * u).astype(BF16), wdb[...], preferred_element_type=F32)
        _store_packed(yr, y)
        y_copy(i, slot).start()
        return carry

    lax.fori_loop(0, total, chunk, 0)

    for b in range(nbuf):
        @pl.when(b < total)
        def _(b=b):
            y_copy(0, b).wait()


def _experts(meta, xs, w_gate, w_up, w_down, after):
    e, d, f = w_gate.shape
    hbm = pl.BlockSpec(memory_space=pl.ANY)
    rows = EXPERT_CHUNK * PACK_SUBLANES
    grid_spec = pltpu.PrefetchScalarGridSpec(
        num_scalar_prefetch=len(meta), grid=(1,),
        in_specs=[hbm, hbm, hbm, hbm, hbm], out_specs=hbm,
        scratch_shapes=[pltpu.VMEM((EXPERT_RING, rows, LANES), U32), pltpu.VMEM((EXPERT_RING, rows, LANES), U32),
                        pltpu.VMEM((d, f), F32), pltpu.VMEM((d, f), F32), pltpu.VMEM((f, d), F32),
                        pltpu.VMEM((d, f), BF16), pltpu.VMEM((d, f), BF16), pltpu.VMEM((f, d), BF16),
                        pltpu.SemaphoreType.DMA((EXPERT_RING,)), pltpu.SemaphoreType.DMA((EXPERT_RING,)),
                        pltpu.SemaphoreType.DMA((3,))])
    return pl.pallas_call(
        _expert_kernel,
        grid_spec=grid_spec,
        out_shape=jax.ShapeDtypeStruct(xs.shape, xs.dtype),
        compiler_params=_params("arbitrary"),
        name="experts",
    )(*meta, xs, w_gate, w_up, w_down, after)


def _final_kernel(base_ref, yt_ref, wt_ref, mod_ref, g_ref, *rest, d):
    o_ref = rest[-1]
    t = base_ref.shape[0]
    w = wt_ref[...]
    routed = w[:, 0:1] * _load_packed(lambda sl: yt_ref[0, sl, :], t)
    for k in range(1, TOP_K):
        routed = routed + w[:, k:k + 1] * _load_packed(lambda sl, k=k: yt_ref[k, sl, :], t)
    g2 = mod_ref[0][:, 5 * d:6 * d]
    o_ref[...] = _rms(base_ref[...] + g2 * routed, g_ref[...])


def _final(base, ytok3, wts_t, mod3, tokens_per_batch, final_g, part, n_parts, prev_out):
    n, d = base.shape
    t = WIDE_TILE
    tiles_per_batch = tokens_per_batch // t
    tiles = n // t // n_parts
    off = part * tiles
    kern = functools.partial(_final_kernel, d=d)
    tok = pl.BlockSpec((t, d), lambda i: (off + i, 0))
    in_specs = [tok,
                pl.BlockSpec((TOP_K, t * PACK_SUBLANES, LANES), lambda i: (0, i, 0)),
                pl.BlockSpec((t, TOP_K), lambda i: (off + i, 0)),
                pl.BlockSpec((1, 1, mod3.shape[2]), lambda i: ((off + i) // tiles_per_batch, 0, 0)),
                _full(final_g)]
    args = [base, ytok3, wts_t, mod3, final_g]
    aliases = {}
    if prev_out is not None:
        in_specs.append(pl.BlockSpec(memory_space=pl.ANY))
        args.append(prev_out)
        aliases = {len(args) - 1: 0}
    return pl.pallas_call(
        kern,
        grid=(tiles,),
        in_specs=in_specs,
        out_specs=tok,
        out_shape=jax.ShapeDtypeStruct((n, d), F32),
        input_output_aliases=aliases,
        compiler_params=_params("arbitrary"),
        name="final",
    )(*args)


def kernel(x, c, ctx, c_ctx, w_mod, b_mod, norm1_g, norm2_g, w_in, ssm_lam_re, ssm_lam_im, ssm_log_dt, ssm_b_re, ssm_b_im, ssm_c_re, ssm_c_im, ssm_d, w_glu, w_ssm_out, conv_w, w_conv_out, w_o, w_router, router_bias, w_gate, w_up, w_down, ws_gate, ws_up, ws_down, final_g):
    b, l, d = x.shape
    n = b * l
    sw = ssm_d.shape[1]
    assert w_mod.shape[0] == 1, "single layer"
    assert b == SUBLANES and l % TOKEN_TILE == 0 and l % WIDE_TILE == 0 and (n // WIDE_TILE) % COMBINE_PARTS == 0 and l % S5_CHUNK == 0 and ctx.shape[1] % S5_CHUNK == 0 and TIME_TILE % GRID_W == 0

    mod_rows = 2 * SUBLANES
    c_all = jnp.concatenate([c, c_ctx[None, :], jnp.zeros((mod_rows - b - 1, d), F32)], axis=0)
    mod = _modulation(c_all, w_mod[0], b_mod[0])
    mod3 = mod.reshape(mod_rows, 1, mod.shape[1])

    w_in_b = w_in[0].astype(BF16)
    g1n = norm1_g[0].reshape(1, d)
    u_lat, sga, gbt = _in_proj(x, mod3, g1n, w_in_b, conv_w[0], w_conv_out[0].astype(BF16))
    u_ctx = _ctx_proj(ctx, mod3, b, g1n, w_in_b[:, :sw])

    ops = _blocked_operators(ssm_lam_re[0], ssm_lam_im[0], ssm_log_dt[0], ssm_b_re[0], ssm_b_im[0],
                             ssm_c_re[0], ssm_c_im[0])
    y = _s5_blocked(u_ctx, u_lat, ssm_d[0].reshape(1, sw), ops, b, S5_CHUNK)

    xl = _mixer_out(y, sga, gbt, x, mod3, w_glu[0].astype(BF16), w_ssm_out[0].astype(BF16),
                    w_o[0].astype(BF16))

    tiles_per_batch = l // TOKEN_TILE
    xl2 = xl.reshape(n, d)
    g2n = norm2_g[0].reshape(1, d)
    h2, eidx, wts, rank, cnt = _route(xl2, mod3, tiles_per_batch, g2n, w_router[0].T,
                                      router_bias[0].reshape(N_EXPERTS, 1))

    counts = cnt[:, 0].astype(I32)
    n_rows = n * TOP_K
    starts, meta = _chunk_metadata(counts, n_rows)
    buf_rows = n_rows + N_EXPERTS * EXPERT_CHUNK
    dest = _dest_rows(eidx, rank, jnp.broadcast_to(starts[:, None], (N_EXPERTS, LANES)))

    xs = _sc_dispatch(dest.reshape(TOP_K * n), h2.reshape(n, PACK_SUBLANES, LANES), buf_rows)
    base = _shared(xl2, mod3, l, g2n, ws_gate[0].astype(BF16), ws_up[0].astype(BF16),
                   ws_down[0].astype(BF16))
    ys = _experts(meta, xs.reshape(buf_rows * PACK_SUBLANES, LANES), w_gate[0], w_up[0], w_down[0], base)
    ys3 = ys.reshape(buf_rows, PACK_SUBLANES, LANES)

    n_part = n // COMBINE_PARTS
    wts_t = wts.T
    out = None
    for part in range(COMBINE_PARTS):
        dest_part = dest[:, part * n_part:(part + 1) * n_part].reshape(TOP_K * n_part)
        ytok = _sc_combine(dest_part, ys3, n_part)
        out = _final(base, ytok.reshape(TOP_K, n_part * PACK_SUBLANES, LANES), wts_t, mod3, l,
                     final_g.reshape(1, d), part, COMBINE_PARTS, out)
    return out.reshape(b, l, d)
```
